```python
import math
import jax, jax.numpy as jnp
from jax import lax
import numpy as np

D_MODEL = 1024
BATCH = 4
SEQ = 4096
DEPTH = 2

HEAD_DIM = 64
ROT_DIM = HEAD_DIM // 4
ROPE_THETA = 500000.0
Q_BLOCK = 128
EPS = 1e-6
NEG = -1e30
PLE_DIM = 256

DA_HEADS = 4
DA_QK = DA_HEADS * 2 * HEAD_DIM
DA_VDIM = 2 * HEAD_DIM
DA_WIDTH = DA_HEADS * DA_VDIM

DSA_HEADS = 8
DSA_WIDTH = DSA_HEADS * HEAD_DIM
IDX_HEADS = 8
IDX_DIM = 64
DSA_TOPK_MAX = 256

NSA_GROUPS = 2
NSA_HPG = 4
NSA_HEADS = NSA_GROUPS * NSA_HPG
NSA_WIDTH = NSA_HEADS * HEAD_DIM
KV_C = NSA_GROUPS * HEAD_DIM
CMP_LEN = 32
CMP_STRIDE = 16
CMP_HIDDEN = 128
SLC_BLOCK = 64
SLC_TOPN_MAX = 16
WINDOW = 512
FORCE_SCORE = 1e4

BRANCH_WIDTH = 512
N_BRANCH = 3

IN_SPLITS = (
    DA_QK, DA_QK, DA_WIDTH, BRANCH_WIDTH,
    DSA_WIDTH, HEAD_DIM, HEAD_DIM, BRANCH_WIDTH,
    IDX_HEADS * IDX_DIM, IDX_DIM, IDX_HEADS,
    NSA_WIDTH, KV_C, KV_C, KV_C, KV_C, KV_C, KV_C,
    BRANCH_WIDTH, NSA_HEADS * 3,
    N_BRANCH * D_MODEL,
)
N_IN = sum(IN_SPLITS)

kernel_name = "hybrid_diff_dsa_nsa_trunk"


def rmsnorm(x, g):
    xf = x.astype(jnp.float32)
    y = xf * lax.rsqrt(jnp.mean(xf * xf, axis=-1, keepdims=True) + EPS)
    return (y * g.astype(jnp.float32)).astype(x.dtype)


def masked_softmax(s, mask):
    s = jnp.where(mask, s.astype(jnp.float32), NEG)
    return jnp.where(mask, jax.nn.softmax(s, axis=-1), 0.0)


def rope_tables(positions):
    inv = ROPE_THETA ** (-jnp.arange(0, ROT_DIM, 2, dtype=jnp.float32) / ROT_DIM)
    ang = positions.astype(jnp.float32)[..., None] * inv
    return jnp.cos(ang), jnp.sin(ang)


def apply_rope(x, cos, sin):
    shp = cos.shape[:2] + (1,) * (x.ndim - 3) + cos.shape[2:]
    c = cos.reshape(shp).astype(x.dtype)
    s = sin.reshape(shp).astype(x.dtype)
    half = ROT_DIM // 2
    x1, x2 = x[..., :half], x[..., half:ROT_DIM]
    return jnp.concatenate([x1 * c - x2 * s, x2 * c + x1 * s, x[..., ROT_DIM:]], axis=-1)


def unblock(o):
    nb, b, qb = o.shape[:3]
    return jnp.moveaxis(o, 0, 1).reshape(b, nb * qb, -1)


def diff_attention(q, k, v, lam, subln_g, lam_init):
    S = q.shape[1]
    kpos = jnp.arange(S)
    scale = HEAD_DIM ** -0.5

    def block(i):
        q0 = i * Q_BLOCK
        tpos = q0 + jnp.arange(Q_BLOCK)
        qb = lax.dynamic_slice_in_dim(q, q0, Q_BLOCK, axis=1)
        s = jnp.einsum('bqhcd,bkhcd->bhcqk', qb, k).astype(jnp.float32) * scale
        pr = masked_softmax(s, kpos[None, :] <= tpos[:, None])
        a = pr[:, :, 0] - lam * pr[:, :, 1]
        return jnp.einsum('bhqk,bkhe->bqhe', a.astype(v.dtype), v)

    o = lax.map(block, jnp.arange(S // Q_BLOCK))
    b = q.shape[0]
    o = jnp.moveaxis(o, 0, 1).reshape(b, S, DA_HEADS, DA_VDIM)
    o = rmsnorm(o, subln_g) * (1.0 - lam_init)
    return o.reshape(b, S, DA_WIDTH)


def dsa_attention(q, k, v, iq, ik, iw, k_sel):
    S = q.shape[1]
    kpos = jnp.arange(S)
    scale = HEAD_DIM ** -0.5
    gather = jax.vmap(lambda kk, ii: kk[ii])

    def block(i):
        q0 = i * Q_BLOCK
        tpos = q0 + jnp.arange(Q_BLOCK)
        qb = lax.dynamic_slice_in_dim(q, q0, Q_BLOCK, axis=1)
        iqb = lax.dynamic_slice_in_dim(iq, q0, Q_BLOCK, axis=1)
        iwb = lax.dynamic_slice_in_dim(iw, q0, Q_BLOCK, axis=1).astype(jnp.float32) * IDX_HEADS ** -0.5
        dots = jnp.einsum('bqhd,bkd->bqhk', iqb, ik).astype(jnp.float32) * IDX_DIM ** -0.5
        score = jnp.einsum('bqh,bqhk->bqk', iwb, jax.nn.relu(dots))
        score = jnp.where(kpos[None, :] <= tpos[:, None], score, -jnp.inf)
        _, idx = lax.top_k(score, k_sel)
        valid = idx <= tpos[None, :, None]
        ksel = gather(k, idx)
        vsel = gather(v, idx)
        s = jnp.einsum('bqhd,bqkd->bqhk', qb, ksel).astype(jnp.float32) * scale
        pr = masked_softmax(s, valid[:, :, None, :])
        return jnp.einsum('bqhk,bqkd->bqhd', pr.astype(v.dtype), vsel)

    return unblock(lax.map(block, jnp.arange(S // Q_BLOCK)))


def compress(kv, pe, w1, w2):
    S = kv.shape[1]
    nc = (S - CMP_LEN) // CMP_STRIDE + 1
    idx = np.arange(nc)[:, None] * CMP_STRIDE + np.arange(CMP_LEN)[None, :]
    blocks = kv[:, idx] + pe[None, None, :, None, :]
    hid = jax.nn.gelu(jnp.einsum('bnlgd,ldf->bngf', blocks, w1))
    return jnp.einsum('bngf,fd->bngd', hid, w2)


def nsa_attention(q, q_rot, kc, vc, ks, vs, kw, vw, gates, n_sel):
    B, S = q.shape[:2]
    nc = kc.shape[1]
    n_blk = S // SLC_BLOCK
    scale = HEAD_DIM ** -0.5
    cstart = np.arange(nc) * CMP_STRIDE
    sstart = np.arange(n_blk) * SLC_BLOCK
    overlap = jnp.asarray(((cstart[:, None] < sstart[None, :] + SLC_BLOCK)
                           & (cstart[:, None] + CMP_LEN > sstart[None, :])).astype(np.float32))
    cmp_end = jnp.arange(nc) * CMP_STRIDE + CMP_LEN - 1
    blk = jnp.arange(n_blk)
    ks_blk = ks.reshape(B, n_blk, SLC_BLOCK, NSA_GROUPS, HEAD_DIM).transpose(0, 3, 1, 2, 4)
    vs_blk = vs.reshape(B, n_blk, SLC_BLOCK, NSA_GROUPS, HEAD_DIM).transpose(0, 3, 1, 2, 4)
    kw_pad = jnp.pad(kw, ((0, 0), (WINDOW, 0), (0, 0), (0, 0)))
    vw_pad = jnp.pad(vw, ((0, 0), (WINDOW, 0), (0, 0), (0, 0)))
    gather = jax.vmap(jax.vmap(lambda kb, ii: kb[ii]))

    def block(i):
        q0 = i * Q_BLOCK
        tpos = q0 + jnp.arange(Q_BLOCK)
        qb = lax.dynamic_slice_in_dim(q, q0, Q_BLOCK, axis=1)
        qrb = lax.dynamic_slice_in_dim(q_rot, q0, Q_BLOCK, axis=1)
        gb = jax.nn.sigmoid(lax.dynamic_slice_in_dim(gates, q0, Q_BLOCK, axis=1))
        sc = jnp.einsum('bqghd,bngd->bghqn', qb, kc).astype(jnp.float32) * scale
        pc = masked_softmax(sc, cmp_end[None, :] <= tpos[:, None])
        o_cmp = jnp.einsum('bghqn,bngd->bqghd', pc.astype(vc.dtype), vc)
        imp = jnp.einsum('bghqn,nm->bgqm', pc, overlap)
        forced = (blk[None, :] == (tpos // SLC_BLOCK)[:, None]) | (blk[None, :] == 0)
        imp = jnp.where(forced, FORCE_SCORE, imp)
        imp = jnp.where(blk[None, :] * SLC_BLOCK <= tpos[:, None], imp, -jnp.inf)
        _, sel = lax.top_k(imp, n_sel)
        ksel = gather(ks_blk, sel)
        vsel = gather(vs_blk, sel).reshape(B, NSA_GROUPS, Q_BLOCK, n_sel * SLC_BLOCK, HEAD_DIM)
        spos = sel[..., None] * SLC_BLOCK + jnp.arange(SLC_BLOCK)
        smask = (spos <= tpos[:, None, None]).reshape(B, NSA_GROUPS, Q_BLOCK, n_sel * SLC_BLOCK)
        ss = jnp.einsum('bqghd,bgqnjd->bghqnj', qrb, ksel).astype(jnp.float32) * scale
        ps = masked_softmax(ss.reshape(B, NSA_GROUPS, NSA_HPG, Q_BLOCK, n_sel * SLC_BLOCK), smask[:, :, None])
        o_slc = jnp.einsum('bghqk,bgqkd->bqghd', ps.astype(vsel.dtype), vsel)
        kwb = lax.dynamic_slice_in_dim(kw_pad, q0, Q_BLOCK + WINDOW, axis=1)
        vwb = lax.dynamic_slice_in_dim(vw_pad, q0, Q_BLOCK + WINDOW, axis=1)
        wpos = q0 - WINDOW + jnp.arange(Q_BLOCK + WINDOW)
        wmask = ((wpos[None, :] <= tpos[:, None]) & (wpos[None, :] > tpos[:, None] - WINDOW)
                 & (wpos[None, :] >= 0))
        sw = jnp.einsum('bqghd,bkgd->bghqk', qrb, kwb).astype(jnp.float32) * scale
        pw = masked_softmax(sw, wmask)
        o_win = jnp.einsum('bghqk,bkgd->bqghd', pw.astype(vwb.dtype), vwb)
        return gb[..., 0:1] * o_cmp + gb[..., 1:2] * o_slc + gb[..., 2:3] * o_win

    return unblock(lax.map(block, jnp.arange(S // Q_BLOCK)))


def setup_inputs(seed: int = 0) -> dict:
    key = jax.random.key(seed)
    ks = jax.random.split(key, 20)

    def nrm(k, shape, scale):
        return jax.random.normal(k, shape, jnp.float32) * scale

    L, D = DEPTH, D_MODEL
    return {
        "x": nrm(ks[0], (BATCH, SEQ, D), 1.0),
        "p": nrm(ks[1], (DEPTH, BATCH, SEQ, PLE_DIM), 1.0),
        "positions": jnp.broadcast_to(jnp.arange(SEQ, dtype=jnp.int32)[None, :], (BATCH, SEQ)),
        "norm_g": 1.0 + nrm(ks[2], (L, D), 0.02),
        "w_in": nrm(ks[3], (L, D, N_IN), D ** -0.5),
        "diff_lambda": nrm(ks[4], (L, 4, HEAD_DIM), 0.1),
        "diff_subln_g": 1.0 + nrm(ks[5], (L, DA_VDIM), 0.02),
        "cmp_pe_k": nrm(ks[6], (L, CMP_LEN, HEAD_DIM), 0.1),
        "cmp_w1_k": nrm(ks[7], (L, CMP_LEN, HEAD_DIM, CMP_HIDDEN), (CMP_LEN * HEAD_DIM) ** -0.5),
        "cmp_w2_k": nrm(ks[8], (L, CMP_HIDDEN, HEAD_DIM), CMP_HIDDEN ** -0.5),
        "cmp_pe_v": nrm(ks[9], (L, CMP_LEN, HEAD_DIM), 0.1),
        "cmp_w1_v": nrm(ks[10], (L, CMP_LEN, HEAD_DIM, CMP_HIDDEN), (CMP_LEN * HEAD_DIM) ** -0.5),
        "cmp_w2_v": nrm(ks[11], (L, CMP_HIDDEN, HEAD_DIM), CMP_HIDDEN ** -0.5),
        "w_up": nrm(ks[12], (L, N_BRANCH, BRANCH_WIDTH, D), BRANCH_WIDTH ** -0.5),
        "w_out": nrm(ks[13], (L, D, D), D ** -0.5),
        "ple_norm_g": 1.0 + nrm(ks[14], (L, D), 0.02),
        "w_ple": nrm(ks[15], (L, PLE_DIM, D), PLE_DIM ** -0.5),
        "w_ple_gate": nrm(ks[16], (L, D, D), D ** -0.5),
        "final_norm_g": 1.0 + nrm(ks[17], (D,), 0.02),
    }


def reference(x, p, positions, norm_g, w_in, diff_lambda, diff_subln_g,
              cmp_pe_k, cmp_w1_k, cmp_w2_k, cmp_pe_v, cmp_w1_v, cmp_w2_v,
              w_up, w_out, ple_norm_g, w_ple, w_ple_gate, final_norm_g):
    B, S, D = x.shape
    k_sel = min(DSA_TOPK_MAX, S // 4)
    n_sel = min(SLC_TOPN_MAX, S // SLC_BLOCK)
    cos, sin = rope_tables(positions)
    offsets = np.cumsum(IN_SPLITS)[:-1].tolist()
    for i in range(DEPTH):
        h = rmsnorm(x, norm_g[i])
        z = h @ w_in[i]
        (a_q, a_k, a_v, a_g, b_q, b_k, b_v, b_g, i_q, i_k, i_w,
         c_q, c_kc, c_vc, c_ks, c_vs, c_kw, c_vw, c_g, c_bg, merge) = jnp.split(z, offsets, axis=-1)

        lam_init = 0.8 - 0.6 * math.exp(-0.3 * i)
        lp = diff_lambda[i].astype(jnp.float32)
        lam = jnp.exp(jnp.sum(lp[0] * lp[1])) - jnp.exp(jnp.sum(lp[2] * lp[3])) + lam_init
        qa = apply_rope(a_q.reshape(B, S, DA_HEADS, 2, HEAD_DIM), cos, sin)
        ka = apply_rope(a_k.reshape(B, S, DA_HEADS, 2, HEAD_DIM), cos, sin)
        o_a = diff_attention(qa, ka, a_v.reshape(B, S, DA_HEADS, DA_VDIM), lam, diff_subln_g[i], lam_init)

        o_b = dsa_attention(apply_rope(b_q.reshape(B, S, DSA_HEADS, HEAD_DIM), cos, sin),
                            apply_rope(b_k, cos, sin), b_v,
                            apply_rope(i_q.reshape(B, S, IDX_HEADS, IDX_DIM), cos, sin),
                            apply_rope(i_k, cos, sin), i_w, k_sel)

        grp = (B, S, NSA_GROUPS, HEAD_DIM)
        qc = c_q.reshape(B, S, NSA_GROUPS, NSA_HPG, HEAD_DIM)
        kc = compress(c_kc.reshape(grp), cmp_pe_k[i], cmp_w1_k[i], cmp_w2_k[i])
        vc = compress(c_vc.reshape(grp), cmp_pe_v[i], cmp_w1_v[i], cmp_w2_v[i])
        o_c = nsa_attention(qc, apply_rope(qc, cos, sin), kc, vc,
                            apply_rope(c_ks.reshape(grp), cos, sin), c_vs.reshape(grp),
                            apply_rope(c_kw.reshape(grp), cos, sin), c_vw.reshape(grp),
                            c_bg.reshape(B, S, NSA_GROUPS, NSA_HPG, 3), n_sel)

        branches = jnp.stack([o_a * jax.nn.silu(a_g), o_b * jax.nn.silu(b_g),
                              o_c * jax.nn.silu(c_g)], axis=2)
        up = jnp.einsum('bsnw,nwd->bsnd', branches, w_up[i])
        gates = jax.nn.sigmoid(merge.reshape(B, S, N_BRANCH, D))
        x = x + jnp.sum(gates * up, axis=2) @ w_out[i]

        x = x + (p[i] @ w_ple[i]) * jax.nn.sigmoid(rmsnorm(x, ple_norm_g[i]) @ w_ple_gate[i])
    return rmsnorm(x, final_norm_g)
```

```python
import functools
import math

import numpy as np
import jax
import jax.numpy as jnp
from jax import lax
from jax.experimental import pallas as pl
from jax.experimental.pallas import tpu as pltpu

F32 = jnp.float32
BF16 = jnp.bfloat16

D_MODEL = 1024
HEAD_DIM = 64
ROT_DIM = 16
ROPE_THETA = 500000.0
EPS = 1e-6
PLE_DIM = 256
DA_HEADS = 4
DSA_HEADS = 8
IDX_HEADS = 8
DSA_TOPK_MAX = 256
NSA_GROUPS = 2
NSA_HEADS = 8
CMP_LEN = 32
CMP_STRIDE = 16
CMP_HIDDEN = 128
SLC_BLOCK = 64
SLC_TOPN_MAX = 16
WINDOW = 512
FORCE_SCORE = 1e4
BRANCH_WIDTH = 512
N_BRANCH = 3

IN_SPLITS = (512, 512, 512, 512, 512, 64, 64, 512, 512, 64, 8,
             512, 128, 128, 128, 128, 128, 128, 512, 24, 3 * D_MODEL)
(_A_Q, _A_K, _A_V, _A_G, _B_Q, _B_K, _B_V, _B_G, _I_Q, _I_K, _I_W,
 _C_Q, _C_KC, _C_VC, _C_KS, _C_VS, _C_KW, _C_VW, _C_G, _C_BG, _MERGE) = [
    int(v) for v in np.concatenate([[0], np.cumsum(IN_SPLITS)[:-1]])]

LANES = 128
TQ = 256
KC = 256
TM_PROJ = 512
TM_POST = 256
NEG_INF = float("-inf")
M_INIT = -1e30
N_NAT = 3584
N_TR = 864
VMEM_LIMIT = 56 * 1024 * 1024


def _dot_nt(a, b):
    return lax.dot_general(a, b, (((1,), (1,)), ((), ())), preferred_element_type=F32)


def _dot(a, b):
    return jnp.dot(a, b, preferred_element_type=F32)


def _rms(x, g):
    return x * lax.rsqrt(jnp.mean(x * x, axis=-1, keepdims=True) + EPS) * g


def _sigmoid(x):
    return 1.0 / (1.0 + jnp.exp(-x))


def _half_masked(ref, n_pairs):
    lane = lax.broadcasted_iota(jnp.int32, (ref.shape[0], LANES), 1)
    out = []
    for hp in range(n_pairs):
        x = ref[:, LANES * hp:LANES * (hp + 1)]
        z = jnp.zeros_like(x)
        out.append(jnp.where(lane < HEAD_DIM, x, z))
        out.append(jnp.where(lane >= HEAD_DIM, x, z))
    return out


def _online_update(s, v_t, m, l, acc_ref, rows):
    m_new = jnp.maximum(m, jnp.max(s, axis=0, keepdims=True))
    alpha = jnp.exp(m - m_new)
    p = jnp.exp(s - m_new)
    l_new = alpha * l + jnp.sum(p, axis=0, keepdims=True)
    acc_ref[rows, :] = alpha * acc_ref[rows, :] + _dot(v_t, p.astype(BF16))
    return m_new, l_new


def _proj_kernel(x_ref, g_ref, wn_ref, wt_ref, rc_ref, rs1_ref, rs2_ref,
                 qa_ref, ka_ref, qb_ref, iq_ref, qc_ref, qcr_ref, kb_ref, ik_ref,
                 ks_ref, kw_ref, cmp_ref,
                 vat_ref, vbt_ref, vst_ref, vwt_ref, iwt_ref, bgt_ref):
    h = _rms(x_ref[...], g_ref[...]).astype(BF16)
    rc, rs1, rs2 = rc_ref[...], rs1_ref[...], rs2_ref[...]

    def rope(z):
        return z * rc + pltpu.roll(z, LANES - ROT_DIM // 2, 1) * rs1 + pltpu.roll(z, ROT_DIM // 2, 1) * rs2

    qscale = HEAD_DIM ** -0.5
    segs = (
        (0, 512, ((qa_ref, True, qscale),)),
        (512, 512, ((ka_ref, True, 1.0),)),
        (1024, 512, ((qb_ref, True, qscale),)),
        (1536, 512, ((iq_ref, True, qscale),)),
        (2048, 512, ((qc_ref, False, qscale), (qcr_ref, True, qscale))),
        (2560, 128, ((kb_ref, True, 1.0),)),
        (2688, 128, ((ik_ref, True, 1.0),)),
        (2816, 256, ((ks_ref, True, 1.0),)),
        (3072, 256, ((kw_ref, True, 1.0),)),
        (3328, 256, ((cmp_ref, False, 1.0),)),
    )
    for c0, width, outs in segs:
        z = _dot(h, wn_ref[:, c0:c0 + width])
        for out_ref, rot, scale in outs:
            for j in range(width // LANES):
                zj = z[:, LANES * j:LANES * (j + 1)]
                if rot:
                    zj = rope(zj)
                if scale != 1.0:
                    zj = zj * scale
                out_ref[:, LANES * j:LANES * (j + 1)] = zj.astype(out_ref.dtype)

    zt = _dot_nt(wt_ref[...], h)
    vat_ref[...] = zt[0:512].astype(BF16)
    vbt_ref[...] = zt[512:576].astype(BF16)
    vst_ref[...] = zt[576:704].astype(BF16)
    vwt_ref[...] = zt[704:832].astype(BF16)
    iwt_ref[...] = zt[832:840] * (IDX_HEADS ** -0.5)
    bgt_ref[...] = zt[840:864]


def _proj_call(x, g, wn, wt, rc, rs1, rs2):
    B, S, D = x.shape
    tm = TM_PROJ
    nat = lambda w: pl.BlockSpec((None, tm, w), lambda b, i: (b, i, 0))
    tr = lambda r: pl.BlockSpec((None, r, tm), lambda b, i: (b, 0, i))
    full = lambda a: pl.BlockSpec(a.shape, lambda b, i: (0,) * a.ndim)
    sds = jax.ShapeDtypeStruct
    out_shape = (
        sds((B, S, 512), BF16), sds((B, S, 512), BF16), sds((B, S, 512), BF16), sds((B, S, 512), BF16),
        sds((B, S, 512), BF16), sds((B, S, 512), BF16), sds((B, S, 128), BF16), sds((B, S, 128), BF16),
        sds((B, S, 256), BF16), sds((B, S, 256), BF16), sds((B, S, 256), F32),
        sds((B, 512, S), BF16), sds((B, 64, S), BF16), sds((B, 128, S), BF16), sds((B, 128, S), BF16),
        sds((B, 8, S), F32), sds((B, 24, S), F32),
    )
    out_specs = (nat(512), nat(512), nat(512), nat(512), nat(512), nat(512), nat(128), nat(128),
                 nat(256), nat(256), nat(256),
                 tr(512), tr(64), tr(128), tr(128), tr(8), tr(24))
    return pl.pallas_call(
        _proj_kernel,
        grid=(B, S // tm),
        in_specs=[nat(D), full(g), full(wn), full(wt), nat(LANES), nat(LANES), nat(LANES)],
        out_specs=out_specs,
        out_shape=out_shape,
        compiler_params=pltpu.CompilerParams(
            dimension_semantics=("arbitrary", "arbitrary"), vmem_limit_bytes=VMEM_LIMIT),
        name="in_proj",
    )(x, g, wn, wt, rc, rs1, rs2)


def _compress_kernel(r_ref, pea_ref, peb_ref, w1a_ref, w1b_ref, w2_ref, nat_ref, t_ref):
    r = r_ref[...]
    y1 = _dot((r + pea_ref[...]).astype(BF16), w1a_ref[...])
    y2 = _dot((r + peb_ref[...]).astype(BF16), w1b_ref[...])
    nrow = r.shape[0]
    hid = jax.nn.gelu(y1 + pltpu.roll(y2, nrow - 1, 0))
    o = _dot(hid.astype(BF16), w2_ref[...])
    nat_ref[...] = jnp.concatenate([o, o], axis=1).astype(BF16)
    t_ref[...] = o.T.astype(BF16)


def _compress_call(r, pea, peb, w1a, w1b, w2):
    B, _, nr, kdim = r.shape
    kind = lambda b, j: (j // NSA_GROUPS, 0, 0)
    return pl.pallas_call(
        _compress_kernel,
        grid=(B, 2 * NSA_GROUPS),
        in_specs=[
            pl.BlockSpec((None, None, nr, kdim), lambda b, j: (b, j, 0, 0)),
            pl.BlockSpec((None, 1, kdim), kind),
            pl.BlockSpec((None, 1, kdim), kind),
            pl.BlockSpec((None, kdim, CMP_HIDDEN), kind),
            pl.BlockSpec((None, kdim, CMP_HIDDEN), kind),
            pl.BlockSpec((None, CMP_HIDDEN, HEAD_DIM), kind),
        ],
        out_specs=(
            pl.BlockSpec((None, None, nr, 2 * HEAD_DIM), lambda b, j: (b, j, 0, 0)),
            pl.BlockSpec((None, None, HEAD_DIM, nr), lambda b, j: (b, j, 0, 0)),
        ),
        out_shape=(jax.ShapeDtypeStruct((B, 2 * NSA_GROUPS, nr, 2 * HEAD_DIM), BF16),
                   jax.ShapeDtypeStruct((B, 2 * NSA_GROUPS, HEAD_DIM, nr), BF16)),
        compiler_params=pltpu.CompilerParams(dimension_semantics=("arbitrary", "arbitrary")),
        name="nsa_compress",
    )(r, pea, peb, w1a, w1b, w2)


def _diff_kernel(q_ref, k_ref, vt_ref, lam_ref, sg_ref, o_ref, acc_ref, *, lam_init):
    i = pl.program_id(1)
    lp = lam_ref[...]
    lam = (jnp.exp(jnp.sum(lp[0:1] * lp[1:2], axis=1, keepdims=True))
           - jnp.exp(jnp.sum(lp[2:3] * lp[3:4], axis=1, keepdims=True)) + lam_init)
    row = lax.broadcasted_iota(jnp.int32, (KC, TQ), 0)
    col = lax.broadcasted_iota(jnp.int32, (KC, TQ), 1)
    causal = row <= col
    qz = _half_masked(q_ref, DA_HEADS)
    vdim = 2 * HEAD_DIM

    for h in range(DA_HEADS):
        acc_ref[...] = jnp.zeros_like(acc_ref)

        def step(j, carry, masked, h=h):
            off = pl.multiple_of(j * KC, KC)
            k = k_ref[pl.ds(off, KC), vdim * h:vdim * (h + 1)]
            v_t = vt_ref[vdim * h:vdim * (h + 1), pl.ds(off, KC)]
            new = []
            for c in range(2):
                s = _dot_nt(k, qz[2 * h + c])
                if masked:
                    s = jnp.where(causal, s, NEG_INF)
                m, l = _online_update(s, v_t, carry[2 * c], carry[2 * c + 1],
                                      acc_ref, slice(vdim * c, vdim * (c + 1)))
                new += [m, l]
            return tuple(new)

        init = (jnp.full((1, TQ), M_INIT, F32), jnp.zeros((1, TQ), F32)) * 2
        carry = lax.fori_loop(0, i, functools.partial(step, masked=False), init)
        _, l0, _, l1 = step(i, carry, True)
        o_t = acc_ref[0:vdim, :] / l0 - lam * (acc_ref[vdim:2 * vdim, :] / l1)
        ms = jnp.mean(o_t * o_t, axis=0, keepdims=True)
        y = o_t * lax.rsqrt(ms + EPS) * sg_ref[...] * (1.0 - lam_init)
        o_ref[:, vdim * h:vdim * (h + 1)] = y.T


def _diff_call(qa, ka, vat, lam_p, sg, lam_init):
    B, S, W = qa.shape
    return pl.pallas_call(
        functools.partial(_diff_kernel, lam_init=lam_init),
        grid=(B, S // TQ),
        in_specs=[
            pl.BlockSpec((None, TQ, W), lambda b, i: (b, i, 0)),
            pl.BlockSpec((None, S, W), lambda b, i: (b, 0, 0)),
            pl.BlockSpec((None, W, S), lambda b, i: (b, 0, 0)),
            pl.BlockSpec(lam_p.shape, lambda b, i: (0, 0)),
            pl.BlockSpec(sg.shape, lambda b, i: (0, 0)),
        ],
        out_specs=pl.BlockSpec((None, TQ, W), lambda b, i: (b, i, 0)),
        out_shape=jax.ShapeDtypeStruct((B, S, W), F32),
        scratch_shapes=[pltpu.VMEM((4 * HEAD_DIM, TQ), F32)],
        compiler_params=pltpu.CompilerParams(
            dimension_semantics=("arbitrary", "arbitrary"), vmem_limit_bytes=VMEM_LIMIT),
        name="diff_attn",
    )(qa, ka, vat, lam_p, sg)


def _dsa_kernel(q_ref, iq_ref, kb_ref, ik_ref, vt_ref, iw_ref, o_ref,
                sc_ref, acc_ref, ml_ref, tj_ref, *, k_sel, seq_len):
    i = pl.program_id(1)
    nch = i + 1
    row = lax.broadcasted_iota(jnp.int32, (KC, TQ), 0)
    col = lax.broadcasted_iota(jnp.int32, (KC, TQ), 1)
    causal = row <= col
    rowf = row.astype(F32)
    iqz = _half_masked(iq_ref, IDX_HEADS // 2)
    iw = iw_ref[...]
    kf = float(k_sel)

    def idx_chunk(j, carry, masked):
        mx, mn = carry
        off = pl.multiple_of(j * KC, KC)
        ikc = ik_ref[pl.ds(off, KC), :]
        sc = jnp.zeros((KC, TQ), F32)
        for h in range(IDX_HEADS):
            sc = sc + iw[h:h + 1, :] * jnp.maximum(_dot_nt(ikc, iqz[h]), 0.0)
        lo_src = sc
        if masked:
            lo_src = jnp.where(causal, sc, jnp.inf)
            sc = jnp.where(causal, sc, NEG_INF)
        sc_ref[pl.ds(off, KC), :] = sc
        return (jnp.maximum(mx, jnp.max(sc, axis=0, keepdims=True)),
                jnp.minimum(mn, jnp.min(lo_src, axis=0, keepdims=True)))

    carry = (jnp.full((1, TQ), NEG_INF, F32), jnp.full((1, TQ), jnp.inf, F32))
    carry = lax.fori_loop(0, i, functools.partial(idx_chunk, masked=False), carry)
    row_max, row_min = idx_chunk(i, carry, True)

    def count(pred):
        def body(j, c):
            off = pl.multiple_of(j * KC, KC)
            x = sc_ref[pl.ds(off, KC), :]
            idx = rowf + (j * KC).astype(F32)
            return c + jnp.sum(jnp.where(pred(x, idx), 1.0, 0.0), axis=0, keepdims=True)
        return lax.fori_loop(0, nch, body, jnp.zeros((1, TQ), F32))

    @pl.when(i == 0)
    def _():
        tj_ref[0:1, :] = jnp.full((1, TQ), NEG_INF, F32)
        tj_ref[1:2, :] = jnp.full((1, TQ), -1.0, F32)

    @pl.when(i > 0)
    def _():
        n_causal = (i * TQ + 1 + lax.broadcasted_iota(jnp.int32, (1, TQ), 1)).astype(F32)
        lo0 = row_min
        hi0 = row_max + (row_max - row_min) + 1.0

        def cond(st):
            it, _, _, _, act = st
            return jnp.logical_and(it < 200, act > 0.0)

        def body(st):
            it, lo, hi, clo, _ = st
            mid = lo + (hi - lo) * 0.5
            c = count(lambda x, idx: x >= mid)
            ge = c >= kf
            lo2 = jnp.where(ge, mid, lo)
            hi2 = jnp.where(ge, hi, mid)
            clo2 = jnp.where(ge, c, clo)
            moving = jnp.logical_and(mid > lo, mid < hi)
            active = jnp.where(jnp.logical_and(clo2 > kf, moving), 1.0, 0.0)
            return it + 1, lo2, hi2, clo2, jnp.max(active)

        _, thr, _, clo, _ = lax.while_loop(
            cond, body, (jnp.int32(0), lo0, hi0, n_causal, jnp.float32(1.0)))
        tied = clo > kf
        tj_ref[0:1, :] = thr
        tj_ref[1:2, :] = jnp.full((1, TQ), float(seq_len), F32)

        @pl.when(jnp.max(jnp.where(tied, 1.0, 0.0)) > 0.0)
        def _():
            need = kf - count(lambda x, idx: x > thr)

            def jbody(_, st):
                jlo, jhi = st
                mid = jnp.floor((jlo + jhi) * 0.5)
                c = count(lambda x, idx: jnp.logical_and(x == thr, idx <= mid))
                ok = c >= need
                return jnp.where(ok, jlo, mid), jnp.where(ok, mid, jhi)

            n_it = int(math.ceil(math.log2(seq_len))) + 1
            _, jhi = lax.fori_loop(
                0, n_it, jbody,
                (jnp.full((1, TQ), -1.0, F32), jnp.full((1, TQ), float(seq_len - 1), F32)))
            tj_ref[1:2, :] = jnp.where(tied, jhi, float(seq_len))

    thr = tj_ref[0:1, :]
    jmax = tj_ref[1:2, :]

    qz = _half_masked(q_ref, DSA_HEADS // 2)
    acc_ref[...] = jnp.zeros_like(acc_ref)
    ml_ref[0:8, :] = jnp.full((8, TQ), M_INIT, F32)
    ml_ref[8:16, :] = jnp.zeros((8, TQ), F32)

    def att_chunk(j, carry):
        off = pl.multiple_of(j * KC, KC)
        x = sc_ref[pl.ds(off, KC), :]
        idx = rowf + (j * KC).astype(F32)
        sel = jnp.logical_or(x > thr, jnp.logical_and(x == thr, idx <= jmax))
        bias = jnp.where(sel, 0.0, NEG_INF)
        kc = kb_ref[pl.ds(off, KC), :]
        v_t = vt_ref[:, pl.ds(off, KC)]
        for h in range(DSA_HEADS):
            s = _dot_nt(kc, qz[h]) + bias
            m, l = _online_update(s, v_t, ml_ref[h:h + 1, :], ml_ref[8 + h:9 + h, :],
                                  acc_ref, slice(HEAD_DIM * h, HEAD_DIM * (h + 1)))
            ml_ref[h:h + 1, :] = m
            ml_ref[8 + h:9 + h, :] = l
        return carry

    lax.fori_loop(0, nch, att_chunk, 0)
    for hp in range(DSA_HEADS // 2):
        blk = jnp.concatenate(
            [acc_ref[HEAD_DIM * h:HEAD_DIM * (h + 1), :] / ml_ref[8 + h:9 + h, :]
             for h in (2 * hp, 2 * hp + 1)], axis=0)
        o_ref[:, LANES * hp:LANES * (hp + 1)] = blk.T


def _dsa_call(qb, iq, kb, ik, vbt, iwt, k_sel):
    B, S, W = qb.shape
    return pl.pallas_call(
        functools.partial(_dsa_kernel, k_sel=k_sel, seq_len=S),
        grid=(B, S // TQ),
        in_specs=[
            pl.BlockSpec((None, TQ, W), lambda b, i: (b, i, 0)),
            pl.BlockSpec((None, TQ, W), lambda b, i: (b, i, 0)),
            pl.BlockSpec((None, S, LANES), lambda b, i: (b, 0, 0)),
            pl.BlockSpec((None, S, LANES), lambda b, i: (b, 0, 0)),
            pl.BlockSpec((None, HEAD_DIM, S), lambda b, i: (b, 0, 0)),
            pl.BlockSpec((None, IDX_HEADS, TQ), lambda b, i: (b, 0, i)),
        ],
        out_specs=pl.BlockSpec((None, TQ, W), lambda b, i: (b, i, 0)),
        out_shape=jax.ShapeDtypeStruct((B, S, W), F32),
        scratch_shapes=[pltpu.VMEM((S, TQ), F32),
                        pltpu.VMEM((DSA_HEADS * HEAD_DIM, TQ), F32),
                        pltpu.VMEM((16, TQ), F32),
                        pltpu.VMEM((8, TQ), F32)],
        compiler_params=pltpu.CompilerParams(
            dimension_semantics=("arbitrary", "arbitrary"), vmem_limit_bytes=VMEM_LIMIT),
        name="dsa_attn",
    )(qb, iq, kb, ik, vbt, iwt)


def _nsa_kernel(q_ref, qr_ref, kc_ref, vct_ref, ovt_ref, ks_ref, vst_ref, kw_ref, vwt_ref, bg_ref,
                o_ref, cmp_ref, slc_ref, win_ref, ml_ref, sel_ref, *, n_sel):
    i = pl.program_id(1)
    row = lax.broadcasted_iota(jnp.int32, (KC, TQ), 0)
    col = lax.broadcasted_iota(jnp.int32, (KC, TQ), 1)
    causal = row <= col
    tq = i * TQ + lax.broadcasted_iota(jnp.int32, (1, TQ), 1)
    qz = _half_masked(q_ref, NSA_HEADS // 2)
    qrz = _half_masked(qr_ref, NSA_HEADS // 2)
    hpg = NSA_HEADS // NSA_GROUPS
    nc = kc_ref.shape[1]
    n_blk = sel_ref.shape[1]
    gw = 2 * HEAD_DIM

    cmp_valid = CMP_STRIDE * lax.broadcasted_iota(jnp.int32, (nc, TQ), 0) + (CMP_LEN - 1) <= tq
    blk = lax.broadcasted_iota(jnp.int32, (n_blk, TQ), 0)
    forced = jnp.logical_or(blk == jnp.right_shift(tq, int(math.log2(SLC_BLOCK))), blk == 0)
    blk_causal = blk * SLC_BLOCK <= tq
    for g in range(NSA_GROUPS):
        kcg = kc_ref[g]
        vcg = vct_ref[NSA_GROUPS + g]
        imp = jnp.zeros((n_blk, TQ), F32)
        for hh in range(hpg):
            h = hpg * g + hh
            s = jnp.where(cmp_valid, _dot_nt(kcg, qz[h]), NEG_INF)
            m = jnp.maximum(jnp.max(s, axis=0, keepdims=True), M_INIT)
            p = jnp.exp(s - m)
            l = jnp.sum(p, axis=0, keepdims=True)
            pc = (p * jnp.where(l > 0.0, 1.0 / l, 0.0)).astype(BF16)
            cmp_ref[HEAD_DIM * h:HEAD_DIM * (h + 1), :] = _dot(vcg, pc)
            imp = imp + _dot(ovt_ref[...], pc)
        imp = jnp.where(forced, FORCE_SCORE, imp)
        imp = jnp.where(blk_causal, imp, NEG_INF)
        rank = jnp.zeros((n_blk, TQ), F32)
        for mp in range(n_blk):
            r = imp[mp:mp + 1, :]
            ahead = jnp.logical_or(r > imp, jnp.logical_and(r == imp, blk > mp))
            rank = rank + jnp.where(ahead, 1.0, 0.0)
        sel_ref[g] = jnp.where(rank < float(n_sel), 1.0, 0.0)

    slc_ref[...] = jnp.zeros_like(slc_ref)
    ml_ref[0:8, :] = jnp.full((8, TQ), M_INIT, F32)
    ml_ref[8:16, :] = jnp.zeros((8, TQ), F32)
    bpc = KC // SLC_BLOCK

    def slc_chunk(j, carry, masked):
        off = pl.multiple_of(j * KC, KC)
        for g in range(NSA_GROUPS):
            rows = [jnp.where(sel_ref[g, pl.ds(bpc * j + b, 1), :] > 0.5, 0.0, NEG_INF)
                    for b in range(bpc)]
            bias = jnp.concatenate(
                [jnp.broadcast_to(r, (SLC_BLOCK, TQ)) for r in rows], axis=0)
            if masked:
                bias = jnp.where(causal, bias, NEG_INF)
            kc = ks_ref[pl.ds(off, KC), gw * g:gw * (g + 1)]
            v_t = vst_ref[HEAD_DIM * g:HEAD_DIM * (g + 1), pl.ds(off, KC)]
            for hh in range(hpg):
                h = hpg * g + hh
                s = _dot_nt(kc, qrz[h]) + bias
                m, l = _online_update(s, v_t, ml_ref[h:h + 1, :], ml_ref[8 + h:9 + h, :],
                                      slc_ref, slice(HEAD_DIM * h, HEAD_DIM * (h + 1)))
                ml_ref[h:h + 1, :] = m
                ml_ref[8 + h:9 + h, :] = l
        return carry

    lax.fori_loop(0, i, functools.partial(slc_chunk, masked=False), 0)
    slc_chunk(i, 0, True)
    for h in range(NSA_HEADS):
        rows = slice(HEAD_DIM * h, HEAD_DIM * (h + 1))
        slc_ref[rows, :] = slc_ref[rows, :] / ml_ref[8 + h:9 + h, :]

    win_ref[...] = jnp.zeros_like(win_ref)
    ml_ref[0:8, :] = jnp.full((8, TQ), M_INIT, F32)
    ml_ref[8:16, :] = jnp.zeros((8, TQ), F32)

    def win_chunk(j, mask):
        off = pl.multiple_of(j * KC, KC)
        for g in range(NSA_GROUPS):
            kc = kw_ref[pl.ds(off, KC), gw * g:gw * (g + 1)]
            v_t = vwt_ref[HEAD_DIM * g:HEAD_DIM * (g + 1), pl.ds(off, KC)]
            for hh in range(hpg):
                h = hpg * g + hh
                s = _dot_nt(kc, qrz[h])
                if mask is not None:
                    s = jnp.where(mask, s, NEG_INF)
                m, l = _online_update(s, v_t, ml_ref[h:h + 1, :], ml_ref[8 + h:9 + h, :],
                                      win_ref, slice(HEAD_DIM * h, HEAD_DIM * (h + 1)))
                ml_ref[h:h + 1, :] = m
                ml_ref[8 + h:9 + h, :] = l

    assert WINDOW == 2 * KC and KC == TQ

    @pl.when(i >= 2)
    def _():
        win_chunk(i - 2, row > col)

    @pl.when(i >= 1)
    def _():
        win_chunk(i - 1, None)

    win_chunk(i, causal)

    gb = _sigmoid(bg_ref[...])
    for hp in range(NSA_HEADS // 2):
        parts = []
        for h in (2 * hp, 2 * hp + 1):
            rows = slice(HEAD_DIM * h, HEAD_DIM * (h + 1))
            parts.append(gb[3 * h:3 * h + 1, :] * cmp_ref[rows, :]
                         + gb[3 * h + 1:3 * h + 2, :] * slc_ref[rows, :]
                         + gb[3 * h + 2:3 * h + 3, :] * (win_ref[rows, :] / ml_ref[8 + h:9 + h, :]))
        o_ref[:, LANES * hp:LANES * (hp + 1)] = jnp.concatenate(parts, axis=0).T


def _nsa_call(qc, qcr, kcc, vct, ovt, ks, vst, kw, vwt, bgt, n_sel):
    B, S, W = qc.shape
    nc = kcc.shape[2]
    n_blk = S // SLC_BLOCK
    qspec = pl.BlockSpec((None, TQ, W), lambda b, i: (b, i, 0))
    return pl.pallas_call(
        functools.partial(_nsa_kernel, n_sel=n_sel),
        grid=(B, S // TQ),
        in_specs=[
            qspec, qspec,
            pl.BlockSpec((None,) + kcc.shape[1:], lambda b, i: (b, 0, 0, 0)),
            pl.BlockSpec((None,) + vct.shape[1:], lambda b, i: (b, 0, 0, 0)),
            pl.BlockSpec(ovt.shape, lambda b, i: (0, 0)),
            pl.BlockSpec((None, S, 2 * LANES), lambda b, i: (b, 0, 0)),
            pl.BlockSpec((None, 2 * HEAD_DIM, S), lambda b, i: (b, 0, 0)),
            pl.BlockSpec((None, S, 2 * LANES), lambda b, i: (b, 0, 0)),
            pl.BlockSpec((None, 2 * HEAD_DIM, S), lambda b, i: (b, 0, 0)),
            pl.BlockSpec((None, 3 * NSA_HEADS, TQ), lambda b, i: (b, 0, i)),
        ],
        out_specs=qspec,
        out_shape=jax.ShapeDtypeStruct((B, S, W), F32),
        scratch_shapes=[pltpu.VMEM((W, TQ), F32), pltpu.VMEM((W, TQ), F32), pltpu.VMEM((W, TQ), F32),
                        pltpu.VMEM((16, TQ), F32), pltpu.VMEM((NSA_GROUPS, n_blk, TQ), F32)],
        compiler_params=pltpu.CompilerParams(
            dimension_semantics=("arbitrary", "arbitrary"), vmem_limit_bytes=VMEM_LIMIT),
        name="nsa_attn",
    )(qc, qcr, kcc, vct, ovt, ks, vst, kw, vwt, bgt)


def _post_kernel(x_ref, oa_ref, ob_ref, oc_ref, p_ref, g_ref, wg_ref, wup_ref, wout_ref,
                 pg_ref, wple_ref, wpg_ref, fg_ref, o_ref, *, final):
    x = x_ref[...]
    h = _rms(x, g_ref[...]).astype(BF16)
    mixed = jnp.zeros(x.shape, F32)
    for n, br_ref in enumerate((oa_ref, ob_ref, oc_ref)):
        gate = _dot(h, wg_ref[:, BRANCH_WIDTH * n:BRANCH_WIDTH * (n + 1)])
        og = br_ref[...] * (gate * _sigmoid(gate))
        up = _dot(og.astype(BF16), wup_ref[n])
        c0 = N_BRANCH * BRANCH_WIDTH + D_MODEL * n
        mixed = mixed + _sigmoid(_dot(h, wg_ref[:, c0:c0 + D_MODEL])) * up
    x1 = x + _dot(mixed.astype(BF16), wout_ref[...])
    emb = _dot(p_ref[...].astype(BF16), wple_ref[...])
    hg = _dot(_rms(x1, pg_ref[...]).astype(BF16), wpg_ref[...])
    x2 = x1 + emb * _sigmoid(hg)
    if final:
        x2 = _rms(x2, fg_ref[...])
    o_ref[...] = x2


def _post_call(x, oa, ob, oc, p, g, wg, wup, wout, pg, wple, wpg, fg, final):
    T, D = x.shape
    tm = TM_POST
    tok = lambda w: pl.BlockSpec((tm, w), lambda i: (i, 0))
    full = lambda a: pl.BlockSpec(a.shape, lambda i: (0,) * a.ndim)
    return pl.pallas_call(
        functools.partial(_post_kernel, final=final),
        grid=(T // tm,),
        in_specs=[tok(D), tok(BRANCH_WIDTH), tok(BRANCH_WIDTH), tok(BRANCH_WIDTH), tok(PLE_DIM),
                  full(g), full(wg), full(wup), full(wout), full(pg), full(wple), full(wpg), full(fg)],
        out_specs=tok(D),
        out_shape=jax.ShapeDtypeStruct((T, D), F32),
        compiler_params=pltpu.CompilerParams(
            dimension_semantics=("arbitrary",), vmem_limit_bytes=VMEM_LIMIT),
        name="post_mix",
    )(x, oa, ob, oc, p, g, wg, wup, wout, pg, wple, wpg, fg)


def _rope_tables(positions):
    inv = ROPE_THETA ** (-jnp.arange(0, ROT_DIM, 2, dtype=F32) / ROT_DIM)
    ang = positions.astype(F32)[..., None] * inv
    cos, sin = jnp.cos(ang), jnp.sin(ang)
    half = ROT_DIM // 2
    pad = HEAD_DIM - ROT_DIM
    zeros = jnp.zeros(cos.shape[:-1] + (half,), F32)
    rc = jnp.concatenate([cos, cos, jnp.ones(cos.shape[:-1] + (pad,), F32)], axis=-1)
    rs1 = jnp.concatenate([-sin, zeros, jnp.zeros(cos.shape[:-1] + (pad,), F32)], axis=-1)
    rs2 = jnp.concatenate([zeros, sin, jnp.zeros(cos.shape[:-1] + (pad,), F32)], axis=-1)
    rep = lambda t: jnp.concatenate([t] * (LANES // HEAD_DIM), axis=-1)
    return rep(rc), rep(rs1), rep(rs2)


def _proj_weights(w):
    sl = lambda o, n: w[:, o:o + n]
    dup = lambda o: jnp.concatenate([sl(o, HEAD_DIM)] * 2, axis=1)
    nat = jnp.concatenate([
        sl(_A_Q, 512), sl(_A_K, 512), sl(_B_Q, 512), sl(_I_Q, 512), sl(_C_Q, 512),
        dup(_B_K), dup(_I_K),
        dup(_C_KS), dup(_C_KS + HEAD_DIM), dup(_C_KW), dup(_C_KW + HEAD_DIM),
        sl(_C_KC, 128), sl(_C_VC, 128)], axis=1)
    tr = jnp.concatenate([
        sl(_A_V, 512), sl(_B_V, 64), sl(_C_VS, 128), sl(_C_VW, 128), sl(_I_W, 8), sl(_C_BG, 24)], axis=1)
    gates = jnp.concatenate([sl(_A_G, 512), sl(_B_G, 512), sl(_C_G, 512), sl(_MERGE, 3 * D_MODEL)], axis=1)
    return nat.astype(BF16), tr.T.astype(BF16), gates.astype(BF16)


def _overlap_t(S):
    nc = S // CMP_STRIDE
    n_blk = S // SLC_BLOCK
    cstart = np.arange(nc) * CMP_STRIDE
    sstart = np.arange(n_blk) * SLC_BLOCK
    ov = ((cstart[None, :] < sstart[:, None] + SLC_BLOCK) & (cstart[None, :] + CMP_LEN > sstart[:, None]))
    return jnp.asarray(ov.astype(np.float32), dtype=BF16)


def kernel(x, p, positions, norm_g, w_in, diff_lambda, diff_subln_g, cmp_pe_k, cmp_w1_k, cmp_w2_k,
           cmp_pe_v, cmp_w1_v, cmp_w2_v, w_up, w_out, ple_norm_g, w_ple, w_ple_gate, final_norm_g):
    B, S, D = x.shape
    depth = w_in.shape[0]
    k_sel = min(DSA_TOPK_MAX, S // 4)
    n_sel = min(SLC_TOPN_MAX, S // SLC_BLOCK)
    assert S % TM_PROJ == 0 and S % TQ == 0 and k_sel <= TQ and D == D_MODEL
    rc, rs1, rs2 = _rope_tables(positions)
    ovt = _overlap_t(S)
    half_len = CMP_LEN // 2
    kdim = half_len * HEAD_DIM
    row2 = lambda v: v.reshape(1, -1)

    for i in range(depth):
        lam_init = 0.8 - 0.6 * math.exp(-0.3 * i)
        wn, wt, wg = _proj_weights(w_in[i])
        (qa, ka, qb, iq, qc, qcr, kb, ik, ks, kw, cmp_raw,
         vat, vbt, vst, vwt, iwt, bgt) = _proj_call(x, row2(norm_g[i]), wn, wt, rc, rs1, rs2)

        r = cmp_raw.reshape(B, S // half_len, half_len, 2 * NSA_GROUPS, HEAD_DIM)
        r = r.transpose(0, 3, 1, 2, 4).reshape(B, 2 * NSA_GROUPS, S // half_len, kdim)
        pe = jnp.stack([cmp_pe_k[i], cmp_pe_v[i]])
        w1 = jnp.stack([cmp_w1_k[i], cmp_w1_v[i]])
        w2 = jnp.stack([cmp_w2_k[i], cmp_w2_v[i]]).astype(BF16)
        kcc, vct = _compress_call(
            r, pe[:, :half_len].reshape(2, 1, kdim), pe[:, half_len:].reshape(2, 1, kdim),
            w1[:, :half_len].reshape(2, kdim, CMP_HIDDEN).astype(BF16),
            w1[:, half_len:].reshape(2, kdim, CMP_HIDDEN).astype(BF16), w2)

        oa = _diff_call(qa, ka, vat, diff_lambda[i], diff_subln_g[i].reshape(-1, 1), lam_init)
        ob = _dsa_call(qb, iq, kb, ik, vbt, iwt, k_sel)
        oc = _nsa_call(qc, qcr, kcc, vct, ovt, ks, vst, kw, vwt, bgt, n_sel)

        x = _post_call(
            x.reshape(B * S, D), oa.reshape(B * S, -1), ob.reshape(B * S, -1), oc.reshape(B * S, -1),
            p[i].reshape(B * S, -1), row2(norm_g[i]), wg, w_up[i].astype(BF16), w_out[i].astype(BF16),
            row2(ple_norm_g[i]), w_ple[i].astype(BF16), w_ple_gate[i].astype(BF16), row2(final_norm_g),
            final=(i == depth - 1)).reshape(B, S, D)
    return x
```

```python
import functools
import math

import numpy as np
import jax
import jax.numpy as jnp
from jax import lax
from jax.experimental import pallas as pl
from jax.experimental.pallas import tpu as pltpu

F32 = jnp.float32
BF16 = jnp.bfloat16

D_MODEL = 1024
HEAD_DIM = 64
ROT_DIM = 16
ROPE_THETA = 500000.0
EPS = 1e-6
PLE_DIM = 256
DA_HEADS = 4
DSA_HEADS = 8
IDX_HEADS = 8
DSA_TOPK_MAX = 256
NSA_GROUPS = 2
NSA_HEADS = 8
CMP_LEN = 32
CMP_STRIDE = 16
CMP_HIDDEN = 128
SLC_BLOCK = 64
SLC_TOPN_MAX = 16
WINDOW = 512
FORCE_SCORE = 1e4
BRANCH_WIDTH = 512
N_BRANCH = 3

IN_SPLITS = (512, 512, 512, 512, 512, 64, 64, 512, 512, 64, 8,
             512, 128, 128, 128, 128, 128, 128, 512, 24, 3 * D_MODEL)
(_A_Q, _A_K, _A_V, _A_G, _B_Q, _B_K, _B_V, _B_G, _I_Q, _I_K, _I_W,
 _C_Q, _C_KC, _C_VC, _C_KS, _C_VS, _C_KW, _C_VW, _C_G, _C_BG, _MERGE) = [
    int(v) for v in np.concatenate([[0], np.cumsum(IN_SPLITS)[:-1]])]

LANES = 128
SUBLANES = 8
TQ = 256
KC = 256
TM_PROJ = 512
TM_POST = 256
NEG_INF = float("-inf")
M_INIT = -1e30
N_NAT = 3584
N_TR = 864
BISECT_ROUND = 8
VMEM_LIMIT = 56 * 1024 * 1024


def _dot_nt(a, b):
    return lax.dot_general(a, b, (((1,), (1,)), ((), ())), preferred_element_type=F32)


def _dot(a, b):
    return jnp.dot(a, b, preferred_element_type=F32)


def _rms(x, g):
    return x * lax.rsqrt(jnp.mean(x * x, axis=-1, keepdims=True) + EPS) * g


def _sigmoid(x):
    return 1.0 / (1.0 + jnp.exp(-x))


def _store_half_masked(src_ref, dst_ref, n_pairs):
    lane = lax.broadcasted_iota(jnp.int32, (src_ref.shape[0], LANES), 1)
    for hp in range(n_pairs):
        x = src_ref[:, LANES * hp:LANES * (hp + 1)]
        z = jnp.zeros_like(x)
        dst_ref[2 * hp] = jnp.where(lane < HEAD_DIM, x, z)
        dst_ref[2 * hp + 1] = jnp.where(lane >= HEAD_DIM, x, z)


def _fold_rows(x, op):
    parts = [x[SUBLANES * r:SUBLANES * (r + 1)] for r in range(x.shape[0] // SUBLANES)]
    while len(parts) > 1:
        parts = [op(parts[2 * n], parts[2 * n + 1]) for n in range(len(parts) // 2)]
    return parts[0]


def _attn_pipelined(j0, j1, nh, hd, prep_fn, score_fn, vt_fn,
                    sbuf_ref, m_ref, al_ref, l_ref, acc_ref):
    j0 = jnp.asarray(j0, jnp.int32)
    m_ref[...] = jnp.full(m_ref.shape, M_INIT, F32)
    l_ref[...] = jnp.zeros(l_ref.shape, F32)
    acc_ref[...] = jnp.zeros(acc_ref.shape, F32)

    def produce(j):
        slot = jnp.bitwise_and(j, 1)
        ctx = prep_fn(j)
        for h in range(nh):
            s = score_fn(ctx, h)
            sbuf_ref[slot, h] = s
            m_old = m_ref[1 - slot, h:h + 1, :]
            m_new = jnp.maximum(m_old, jnp.max(s, axis=0, keepdims=True))
            m_ref[slot, h:h + 1, :] = m_new
            al_ref[slot, h:h + 1, :] = jnp.exp(m_old - m_new)

    def consume(j):
        slot = jnp.bitwise_and(j, 1)
        for h in range(nh):
            p = jnp.exp(sbuf_ref[slot, h] - m_ref[slot, h:h + 1, :])
            al = al_ref[slot, h:h + 1, :]
            l_ref[h:h + 1, :] = al * l_ref[h:h + 1, :] + jnp.sum(p, axis=0, keepdims=True)
            rows = slice(hd * h, hd * (h + 1))
            acc_ref[rows, :] = al * acc_ref[rows, :] + _dot(vt_fn(j, h), p.astype(BF16))

    produce(j0)

    def body(j, carry):
        consume(j - 1)
        produce(j)
        return carry

    lax.fori_loop(j0 + 1, j1 + 1, body, 0)
    consume(j1)


def _attn_scratch(nh, hd):
    return [pltpu.VMEM((2, nh, KC, TQ), F32), pltpu.VMEM((2, nh, TQ), F32),
            pltpu.VMEM((2, nh, TQ), F32), pltpu.VMEM((nh, TQ), F32), pltpu.VMEM((nh * hd, TQ), F32)]


def _proj_kernel(x_ref, g_ref, wn_ref, wt_ref, rc_ref, rs1_ref, rs2_ref,
                 qa_ref, ka_ref, qb_ref, iq_ref, qc_ref, qcr_ref, kb_ref, ik_ref,
                 ks_ref, kw_ref, cmp_ref,
                 vat_ref, vbt_ref, vst_ref, vwt_ref, iwt_ref, bgt_ref):
    h = _rms(x_ref[...], g_ref[...]).astype(BF16)
    rc, rs1, rs2 = rc_ref[...], rs1_ref[...], rs2_ref[...]

    def rope(z):
        return z * rc + pltpu.roll(z, LANES - ROT_DIM // 2, 1) * rs1 + pltpu.roll(z, ROT_DIM // 2, 1) * rs2

    qscale = HEAD_DIM ** -0.5
    segs = (
        (0, 512, ((qa_ref, True, qscale),)),
        (512, 512, ((ka_ref, True, 1.0),)),
        (1024, 512, ((qb_ref, True, qscale),)),
        (1536, 512, ((iq_ref, True, qscale),)),
        (2048, 512, ((qc_ref, False, qscale), (qcr_ref, True, qscale))),
        (2560, 128, ((kb_ref, True, 1.0),)),
        (2688, 128, ((ik_ref, True, 1.0),)),
        (2816, 256, ((ks_ref, True, 1.0),)),
        (3072, 256, ((kw_ref, True, 1.0),)),
        (3328, 256, ((cmp_ref, False, 1.0),)),
    )
    for c0, width, outs in segs:
        z = _dot(h, wn_ref[:, c0:c0 + width])
        for out_ref, rot, scale in outs:
            for j in range(width // LANES):
                zj = z[:, LANES * j:LANES * (j + 1)]
                if rot:
                    zj = rope(zj)
                if scale != 1.0:
                    zj = zj * scale
                out_ref[:, LANES * j:LANES * (j + 1)] = zj.astype(out_ref.dtype)

    zt = _dot_nt(wt_ref[...], h)
    vat_ref[...] = zt[0:512].astype(BF16)
    vbt_ref[...] = zt[512:576].astype(BF16)
    vst_ref[...] = zt[576:704].astype(BF16)
    vwt_ref[...] = zt[704:832].astype(BF16)
    iwt_ref[...] = zt[832:840] * (IDX_HEADS ** -0.5)
    bgt_ref[...] = zt[840:864]


def _proj_call(x, g, wn, wt, rc, rs1, rs2):
    B, S, D = x.shape
    tm = TM_PROJ
    nat = lambda w: pl.BlockSpec((None, tm, w), lambda b, i: (b, i, 0))
    tr = lambda r: pl.BlockSpec((None, r, tm), lambda b, i: (b, 0, i))
    full = lambda a: pl.BlockSpec(a.shape, lambda b, i: (0,) * a.ndim)
    sds = jax.ShapeDtypeStruct
    out_shape = (
        sds((B, S, 512), BF16), sds((B, S, 512), BF16), sds((B, S, 512), BF16), sds((B, S, 512), BF16),
        sds((B, S, 512), BF16), sds((B, S, 512), BF16), sds((B, S, 128), BF16), sds((B, S, 128), BF16),
        sds((B, S, 256), BF16), sds((B, S, 256), BF16), sds((B, S, 256), F32),
        sds((B, 512, S), BF16), sds((B, 64, S), BF16), sds((B, 128, S), BF16), sds((B, 128, S), BF16),
        sds((B, 8, S), F32), sds((B, 24, S), F32),
    )
    out_specs = (nat(512), nat(512), nat(512), nat(512), nat(512), nat(512), nat(128), nat(128),
                 nat(256), nat(256), nat(256),
                 tr(512), tr(64), tr(128), tr(128), tr(8), tr(24))
    return pl.pallas_call(
        _proj_kernel,
        grid=(B, S // tm),
        in_specs=[nat(D), full(g), full(wn), full(wt), nat(LANES), nat(LANES), nat(LANES)],
        out_specs=out_specs,
        out_shape=out_shape,
        compiler_params=pltpu.CompilerParams(
            dimension_semantics=("arbitrary", "arbitrary"), vmem_limit_bytes=VMEM_LIMIT),
        name="in_proj",
    )(x, g, wn, wt, rc, rs1, rs2)


def _compress_kernel(r_ref, pea_ref, peb_ref, w1a_ref, w1b_ref, w2_ref, nat_ref, t_ref):
    r = r_ref[...]
    y1 = _dot((r + pea_ref[...]).astype(BF16), w1a_ref[...])
    y2 = _dot((r + peb_ref[...]).astype(BF16), w1b_ref[...])
    nrow = r.shape[0]
    hid = jax.nn.gelu(y1 + pltpu.roll(y2, nrow - 1, 0))
    o = _dot(hid.astype(BF16), w2_ref[...])
    nat_ref[...] = jnp.concatenate([o, o], axis=1).astype(BF16)
    t_ref[...] = o.T.astype(BF16)


def _compress_call(r, pea, peb, w1a, w1b, w2):
    B, _, nr, kdim = r.shape
    kind = lambda b, j: (j // NSA_GROUPS, 0, 0)
    return pl.pallas_call(
        _compress_kernel,
        grid=(B, 2 * NSA_GROUPS),
        in_specs=[
            pl.BlockSpec((None, None, nr, kdim), lambda b, j: (b, j, 0, 0)),
            pl.BlockSpec((None, 1, kdim), kind),
            pl.BlockSpec((None, 1, kdim), kind),
            pl.BlockSpec((None, kdim, CMP_HIDDEN), kind),
            pl.BlockSpec((None, kdim, CMP_HIDDEN), kind),
            pl.BlockSpec((None, CMP_HIDDEN, HEAD_DIM), kind),
        ],
        out_specs=(
            pl.BlockSpec((None, None, nr, 2 * HEAD_DIM), lambda b, j: (b, j, 0, 0)),
            pl.BlockSpec((None, None, HEAD_DIM, nr), lambda b, j: (b, j, 0, 0)),
        ),
        out_shape=(jax.ShapeDtypeStruct((B, 2 * NSA_GROUPS, nr, 2 * HEAD_DIM), BF16),
                   jax.ShapeDtypeStruct((B, 2 * NSA_GROUPS, HEAD_DIM, nr), BF16)),
        compiler_params=pltpu.CompilerParams(dimension_semantics=("arbitrary", "arbitrary")),
        name="nsa_compress",
    )(r, pea, peb, w1a, w1b, w2)


def _diff_kernel(q_ref, k_ref, vt_ref, lam_ref, sg_ref, o_ref,
                 qz_ref, sbuf_ref, m_ref, al_ref, l_ref, acc_ref, *, lam_init):
    i = pl.program_id(1)
    lp = lam_ref[...]
    lam = (jnp.exp(jnp.sum(lp[0:1] * lp[1:2], axis=1, keepdims=True))
           - jnp.exp(jnp.sum(lp[2:3] * lp[3:4], axis=1, keepdims=True)) + lam_init)
    row_minus_col = (lax.broadcasted_iota(jnp.int32, (KC, TQ), 0)
                     - lax.broadcasted_iota(jnp.int32, (KC, TQ), 1))
    _store_half_masked(q_ref, qz_ref, DA_HEADS)
    vdim = 2 * HEAD_DIM
    n_maps = 2 * DA_HEADS

    def prep(j):
        off = pl.multiple_of(j * KC, KC)
        bias = jnp.where(row_minus_col <= (i - j) * KC, 0.0, NEG_INF)
        return off, bias

    def score(ctx, hm):
        off, bias = ctx
        h = hm // 2
        return _dot_nt(k_ref[pl.ds(off, KC), vdim * h:vdim * (h + 1)], qz_ref[hm]) + bias

    def values(j, hm):
        h = hm // 2
        return vt_ref[vdim * h:vdim * (h + 1), pl.ds(pl.multiple_of(j * KC, KC), KC)]

    _attn_pipelined(0, i, n_maps, vdim, prep, score, values, sbuf_ref, m_ref, al_ref, l_ref, acc_ref)

    for h in range(DA_HEADS):
        r0, r1 = vdim * 2 * h, vdim * (2 * h + 1)
        o_t = (acc_ref[r0:r0 + vdim, :] / l_ref[2 * h:2 * h + 1, :]
               - lam * (acc_ref[r1:r1 + vdim, :] / l_ref[2 * h + 1:2 * h + 2, :]))
        ms = jnp.mean(o_t * o_t, axis=0, keepdims=True)
        y = o_t * lax.rsqrt(ms + EPS) * sg_ref[...] * (1.0 - lam_init)
        o_ref[:, vdim * h:vdim * (h + 1)] = y.T


def _diff_call(qa, ka, vat, lam_p, sg, lam_init):
    B, S, W = qa.shape
    n_maps = 2 * DA_HEADS
    return pl.pallas_call(
        functools.partial(_diff_kernel, lam_init=lam_init),
        grid=(B, S // TQ),
        in_specs=[
            pl.BlockSpec((None, TQ, W), lambda b, i: (b, i, 0)),
            pl.BlockSpec((None, S, W), lambda b, i: (b, 0, 0)),
            pl.BlockSpec((None, W, S), lambda b, i: (b, 0, 0)),
            pl.BlockSpec(lam_p.shape, lambda b, i: (0, 0)),
            pl.BlockSpec(sg.shape, lambda b, i: (0, 0)),
        ],
        out_specs=pl.BlockSpec((None, TQ, W), lambda b, i: (b, i, 0)),
        out_shape=jax.ShapeDtypeStruct((B, S, W), F32),
        scratch_shapes=[pltpu.VMEM((n_maps, TQ, LANES), BF16)] + _attn_scratch(n_maps, 2 * HEAD_DIM),
        compiler_params=pltpu.CompilerParams(
            dimension_semantics=("arbitrary", "arbitrary"), vmem_limit_bytes=VMEM_LIMIT),
        name="diff_attn",
    )(qa, ka, vat, lam_p, sg)


def _dsa_kernel(q_ref, iq_ref, kb_ref, ik_ref, vt_ref, iw_ref, o_ref,
                sc_ref, tj_ref, qz_ref, sbuf_ref, m_ref, al_ref, l_ref, acc_ref, *, k_sel, seq_len):
    i = pl.program_id(1)
    nch = i + 1
    row = lax.broadcasted_iota(jnp.int32, (KC, TQ), 0)
    col = lax.broadcasted_iota(jnp.int32, (KC, TQ), 1)
    causal = row <= col
    rowf = row.astype(F32)
    iw = iw_ref[...]
    kf = float(k_sel)
    _store_half_masked(iq_ref, qz_ref, IDX_HEADS // 2)

    def idx_chunk(j, carry, masked):
        mx, mn = carry
        off = pl.multiple_of(j * KC, KC)
        ikc = ik_ref[pl.ds(off, KC), :]
        sc = jnp.zeros((KC, TQ), F32)
        for h in range(IDX_HEADS):
            sc = sc + iw[h:h + 1, :] * jnp.maximum(_dot_nt(ikc, qz_ref[h]), 0.0)
        lo_src = sc
        if masked:
            lo_src = jnp.where(causal, sc, jnp.inf)
            sc = jnp.where(causal, sc, NEG_INF)
        sc_ref[pl.ds(off, KC), :] = sc
        return (jnp.maximum(mx, _fold_rows(sc, jnp.maximum)),
                jnp.minimum(mn, _fold_rows(lo_src, jnp.minimum)))

    carry = (jnp.full((SUBLANES, TQ), NEG_INF, F32), jnp.full((SUBLANES, TQ), jnp.inf, F32))
    carry = lax.fori_loop(0, i, functools.partial(idx_chunk, masked=False), carry)
    mx8, mn8 = idx_chunk(i, carry, True)
    row_max = jnp.max(mx8, axis=0, keepdims=True)
    row_min = jnp.min(mn8, axis=0, keepdims=True)

    def reduce_chunks(fn, op, init):
        def body(j, c):
            off = pl.multiple_of(j * KC, KC)
            x = sc_ref[pl.ds(off, KC), :]
            idx = rowf + (j * KC).astype(F32)
            return op(c, _fold_rows(fn(x, idx), op))
        return lax.fori_loop(0, nch, body, jnp.full((SUBLANES, TQ), init, F32))

    def count(pred):
        c8 = reduce_chunks(lambda x, idx: jnp.where(pred(x, idx), 1.0, 0.0), jnp.add, 0.0)
        return jnp.sum(c8, axis=0, keepdims=True)

    @pl.when(i == 0)
    def _():
        tj_ref[0:1, :] = jnp.full((1, TQ), NEG_INF, F32)
        tj_ref[1:2, :] = jnp.full((1, TQ), -1.0, F32)

    @pl.when(i > 0)
    def _():
        n_causal = (i * TQ + 1 + lax.broadcasted_iota(jnp.int32, (1, TQ), 1)).astype(F32)
        hi0 = row_max + (row_max - row_min) + 1.0

        def cond(st):
            it, _, _, _, act = st
            return jnp.logical_and(it < 64, act > 0.0)

        def body(st):
            it, lo, hi, clo, _ = st

            def bisect(_, c):
                lo, hi, clo = c
                mid = lo + (hi - lo) * 0.5
                cnt = count(lambda x, idx: x >= mid)
                ge = cnt >= kf
                return jnp.where(ge, mid, lo), jnp.where(ge, hi, mid), jnp.where(ge, cnt, clo)

            lo, hi, clo = lax.fori_loop(0, BISECT_ROUND, bisect, (lo, hi, clo))
            emin = jnp.min(reduce_chunks(lambda x, idx: jnp.where(x >= lo, x, jnp.inf),
                                         jnp.minimum, jnp.inf), axis=0, keepdims=True)
            emax = jnp.max(reduce_chunks(lambda x, idx: jnp.where(x < hi, x, NEG_INF),
                                         jnp.maximum, NEG_INF), axis=0, keepdims=True)
            done = jnp.logical_or(clo <= kf, emin >= emax)
            return it + 1, emin, hi, clo, jnp.max(jnp.where(done, 0.0, 1.0))

        _, thr, _, clo, _ = lax.while_loop(
            cond, body, (jnp.int32(0), row_min, hi0, n_causal, jnp.float32(1.0)))
        tied = clo > kf
        tj_ref[0:1, :] = thr
        tj_ref[1:2, :] = jnp.full((1, TQ), float(seq_len), F32)

        @pl.when(jnp.max(jnp.where(tied, 1.0, 0.0)) > 0.0)
        def _():
            need = kf - count(lambda x, idx: x > thr)

            def jbody(_, st):
                jlo, jhi = st
                mid = jnp.floor((jlo + jhi) * 0.5)
                c = count(lambda x, idx: jnp.logical_and(x == thr, idx <= mid))
                ok = c >= need
                return jnp.where(ok, jlo, mid), jnp.where(ok, mid, jhi)

            n_it = int(math.ceil(math.log2(seq_len))) + 1
            _, jhi = lax.fori_loop(
                0, n_it, jbody,
                (jnp.full((1, TQ), -1.0, F32), jnp.full((1, TQ), float(seq_len - 1), F32)))
            tj_ref[1:2, :] = jnp.where(tied, jhi, float(seq_len))

    thr = tj_ref[0:1, :]
    jmax = tj_ref[1:2, :]

    _store_half_masked(q_ref, qz_ref, DSA_HEADS // 2)

    def prep(j):
        off = pl.multiple_of(j * KC, KC)
        x = sc_ref[pl.ds(off, KC), :]
        idx = rowf + (j * KC).astype(F32)
        sel = jnp.logical_or(x > thr, jnp.logical_and(x == thr, idx <= jmax))
        return kb_ref[pl.ds(off, KC), :], jnp.where(sel, 0.0, NEG_INF)

    def score(ctx, h):
        kc, bias = ctx
        return _dot_nt(kc, qz_ref[h]) + bias

    def values(j, h):
        return vt_ref[:, pl.ds(pl.multiple_of(j * KC, KC), KC)]

    _attn_pipelined(0, i, DSA_HEADS, HEAD_DIM, prep, score, values,
                    sbuf_ref, m_ref, al_ref, l_ref, acc_ref)
    for hp in range(DSA_HEADS // 2):
        blk = jnp.concatenate(
            [acc_ref[HEAD_DIM * h:HEAD_DIM * (h + 1), :] / l_ref[h:h + 1, :]
             for h in (2 * hp, 2 * hp + 1)], axis=0)
        o_ref[:, LANES * hp:LANES * (hp + 1)] = blk.T


def _dsa_call(qb, iq, kb, ik, vbt, iwt, k_sel):
    B, S, W = qb.shape
    return pl.pallas_call(
        functools.partial(_dsa_kernel, k_sel=k_sel, seq_len=S),
        grid=(B, S // TQ),
        in_specs=[
            pl.BlockSpec((None, TQ, W), lambda b, i: (b, i, 0)),
            pl.BlockSpec((None, TQ, W), lambda b, i: (b, i, 0)),
            pl.BlockSpec((None, S, LANES), lambda b, i: (b, 0, 0)),
            pl.BlockSpec((None, S, LANES), lambda b, i: (b, 0, 0)),
            pl.BlockSpec((None, HEAD_DIM, S), lambda b, i: (b, 0, 0)),
            pl.BlockSpec((None, IDX_HEADS, TQ), lambda b, i: (b, 0, i)),
        ],
        out_specs=pl.BlockSpec((None, TQ, W), lambda b, i: (b, i, 0)),
        out_shape=jax.ShapeDtypeStruct((B, S, W), F32),
        scratch_shapes=[pltpu.VMEM((S, TQ), F32), pltpu.VMEM((SUBLANES, TQ), F32),
                        pltpu.VMEM((DSA_HEADS, TQ, LANES), BF16)] + _attn_scratch(DSA_HEADS, HEAD_DIM),
        compiler_params=pltpu.CompilerParams(
            dimension_semantics=("arbitrary", "arbitrary"), vmem_limit_bytes=VMEM_LIMIT),
        name="dsa_attn",
    )(qb, iq, kb, ik, vbt, iwt)


def _nsa_kernel(q_ref, qr_ref, kc_ref, vct_ref, ovt_ref, ks_ref, vst_ref, kw_ref, vwt_ref, bg_ref,
                o_ref, cmp_ref, slc_ref, sel_ref, qz_ref, sbuf_ref, m_ref, al_ref, l_ref, acc_ref,
                *, n_sel):
    i = pl.program_id(1)
    row_minus_col = (lax.broadcasted_iota(jnp.int32, (KC, TQ), 0)
                     - lax.broadcasted_iota(jnp.int32, (KC, TQ), 1))
    tq = i * TQ + lax.broadcasted_iota(jnp.int32, (1, TQ), 1)
    hpg = NSA_HEADS // NSA_GROUPS
    nc = kc_ref.shape[1]
    n_blk = sel_ref.shape[1]
    gw = 2 * HEAD_DIM

    _store_half_masked(q_ref, qz_ref, NSA_HEADS // 2)
    cmp_valid = CMP_STRIDE * lax.broadcasted_iota(jnp.int32, (nc, TQ), 0) + (CMP_LEN - 1) <= tq
    blk = lax.broadcasted_iota(jnp.int32, (n_blk, TQ), 0)
    forced = jnp.logical_or(blk == jnp.right_shift(tq, int(math.log2(SLC_BLOCK))), blk == 0)
    blk_causal = blk * SLC_BLOCK <= tq
    for g in range(NSA_GROUPS):
        kcg = kc_ref[g]
        vcg = vct_ref[NSA_GROUPS + g]
        imp = jnp.zeros((n_blk, TQ), F32)
        for hh in range(hpg):
            h = hpg * g + hh
            s = jnp.where(cmp_valid, _dot_nt(kcg, qz_ref[h]), NEG_INF)
            m = jnp.maximum(jnp.max(s, axis=0, keepdims=True), M_INIT)
            p = jnp.exp(s - m)
            l = jnp.sum(p, axis=0, keepdims=True)
            pc = (p * jnp.where(l > 0.0, 1.0 / l, 0.0)).astype(BF16)
            cmp_ref[HEAD_DIM * h:HEAD_DIM * (h + 1), :] = _dot(vcg, pc)
            imp = imp + _dot(ovt_ref[...], pc)
        imp = jnp.where(forced, FORCE_SCORE, imp)
        imp = jnp.where(blk_causal, imp, NEG_INF)
        rank = jnp.zeros((n_blk, TQ), F32)
        for mp in range(n_blk):
            r = imp[mp:mp + 1, :]
            ahead = jnp.logical_or(r > imp, jnp.logical_and(r == imp, blk > mp))
            rank = rank + jnp.where(ahead, 1.0, 0.0)
        sel_ref[g] = jnp.where(rank < float(n_sel), 0.0, NEG_INF)

    _store_half_masked(qr_ref, qz_ref, NSA_HEADS // 2)
    bpc = KC // SLC_BLOCK

    def slc_prep(j):
        off = pl.multiple_of(j * KC, KC)
        visible = row_minus_col <= (i - j) * KC
        biases = []
        for g in range(NSA_GROUPS):
            bias = jnp.concatenate(
                [jnp.broadcast_to(sel_ref[g, pl.ds(bpc * j + b, 1), :], (SLC_BLOCK, TQ))
                 for b in range(bpc)], axis=0)
            biases.append(jnp.where(visible, bias, NEG_INF))
        return off, biases

    def slc_score(ctx, h):
        off, biases = ctx
        g = h // hpg
        return _dot_nt(ks_ref[pl.ds(off, KC), gw * g:gw * (g + 1)], qz_ref[h]) + biases[g]

    def slc_values(j, h):
        g = h // hpg
        return vst_ref[HEAD_DIM * g:HEAD_DIM * (g + 1), pl.ds(pl.multiple_of(j * KC, KC), KC)]

    _attn_pipelined(0, i, NSA_HEADS, HEAD_DIM, slc_prep, slc_score, slc_values,
                    sbuf_ref, m_ref, al_ref, l_ref, acc_ref)
    for h in range(NSA_HEADS):
        rows = slice(HEAD_DIM * h, HEAD_DIM * (h + 1))
        slc_ref[rows, :] = acc_ref[rows, :] / l_ref[h:h + 1, :]

    def win_prep(j):
        off = pl.multiple_of(j * KC, KC)
        d = (i - j) * KC
        inside = jnp.logical_and(row_minus_col <= d, row_minus_col > d - WINDOW)
        return off, jnp.where(inside, 0.0, NEG_INF)

    def win_score(ctx, h):
        off, bias = ctx
        g = h // hpg
        return _dot_nt(kw_ref[pl.ds(off, KC), gw * g:gw * (g + 1)], qz_ref[h]) + bias

    def win_values(j, h):
        g = h // hpg
        return vwt_ref[HEAD_DIM * g:HEAD_DIM * (g + 1), pl.ds(pl.multiple_of(j * KC, KC), KC)]

    _attn_pipelined(jnp.maximum(i - WINDOW // KC, 0), i, NSA_HEADS, HEAD_DIM,
                    win_prep, win_score, win_values, sbuf_ref, m_ref, al_ref, l_ref, acc_ref)

    gb = _sigmoid(bg_ref[...])
    for hp in range(NSA_HEADS // 2):
        parts = []
        for h in (2 * hp, 2 * hp + 1):
            rows = slice(HEAD_DIM * h, HEAD_DIM * (h + 1))
            parts.append(gb[3 * h:3 * h + 1, :] * cmp_ref[rows, :]
                         + gb[3 * h + 1:3 * h + 2, :] * slc_ref[rows, :]
                         + gb[3 * h + 2:3 * h + 3, :] * (acc_ref[rows, :] / l_ref[h:h + 1, :]))
        o_ref[:, LANES * hp:LANES * (hp + 1)] = jnp.concatenate(parts, axis=0).T


def _nsa_call(qc, qcr, kcc, vct, ovt, ks, vst, kw, vwt, bgt, n_sel):
    B, S, W = qc.shape
    n_blk = S // SLC_BLOCK
    qspec = pl.BlockSpec((None, TQ, W), lambda b, i: (b, i, 0))
    return pl.pallas_call(
        functools.partial(_nsa_kernel, n_sel=n_sel),
        grid=(B, S // TQ),
        in_specs=[
            qspec, qspec,
            pl.BlockSpec((None,) + kcc.shape[1:], lambda b, i: (b, 0, 0, 0)),
            pl.BlockSpec((None,) + vct.shape[1:], lambda b, i: (b, 0, 0, 0)),
            pl.BlockSpec(ovt.shape, lambda b, i: (0, 0)),
            pl.BlockSpec((None, S, 2 * LANES), lambda b, i: (b, 0, 0)),
            pl.BlockSpec((None, 2 * HEAD_DIM, S), lambda b, i: (b, 0, 0)),
            pl.BlockSpec((None, S, 2 * LANES), lambda b, i: (b, 0, 0)),
            pl.BlockSpec((None, 2 * HEAD_DIM, S), lambda b, i: (b, 0, 0)),
            pl.BlockSpec((None, 3 * NSA_HEADS, TQ), lambda b, i: (b, 0, i)),
        ],
        out_specs=qspec,
        out_shape=jax.ShapeDtypeStruct((B, S, W), F32),
        scratch_shapes=[pltpu.VMEM((W, TQ), F32), pltpu.VMEM((W, TQ), F32),
                        pltpu.VMEM((NSA_GROUPS, n_blk, TQ), F32),
                        pltpu.VMEM((NSA_HEADS, TQ, LANES), BF16)] + _attn_scratch(NSA_HEADS, HEAD_DIM),
        compiler_params=pltpu.CompilerParams(
            dimension_semantics=("arbitrary", "arbitrary"), vmem_limit_bytes=VMEM_LIMIT),
        name="nsa_attn",
    )(qc, qcr, kcc, vct, ovt, ks, vst, kw, vwt, bgt)


def _post_kernel(x_ref, oa_ref, ob_ref, oc_ref, p_ref, g_ref, wg_ref, wup_ref, wout_ref,
                 pg_ref, wple_ref, wpg_ref, fg_ref, o_ref, *, final):
    x = x_ref[...]
    h = _rms(x, g_ref[...]).astype(BF16)
    mixed = jnp.zeros(x.shape, F32)
    for n, br_ref in enumerate((oa_ref, ob_ref, oc_ref)):
        gate = _dot(h, wg_ref[:, BRANCH_WIDTH * n:BRANCH_WIDTH * (n + 1)])
        og = br_ref[...] * (gate * _sigmoid(gate))
        up = _dot(og.astype(BF16), wup_ref[n])
        c0 = N_BRANCH * BRANCH_WIDTH + D_MODEL * n
        mixed = mixed + _sigmoid(_dot(h, wg_ref[:, c0:c0 + D_MODEL])) * up
    x1 = x + _dot(mixed.astype(BF16), wout_ref[...])
    emb = _dot(p_ref[...].astype(BF16), wple_ref[...])
    hg = _dot(_rms(x1, pg_ref[...]).astype(BF16), wpg_ref[...])
    x2 = x1 + emb * _sigmoid(hg)
    if final:
        x2 = _rms(x2, fg_ref[...])
    o_ref[...] = x2


def _post_call(x, oa, ob, oc, p, g, wg, wup, wout, pg, wple, wpg, fg, final):
    T, D = x.shape
    tm = TM_POST
    tok = lambda w: pl.BlockSpec((tm, w), lambda i: (i, 0))
    full = lambda a: pl.BlockSpec(a.shape, lambda i: (0,) * a.ndim)
    return pl.pallas_call(
        functools.partial(_post_kernel, final=final),
        grid=(T // tm,),
        in_specs=[tok(D), tok(BRANCH_WIDTH), tok(BRANCH_WIDTH), tok(BRANCH_WIDTH), tok(PLE_DIM),
                  full(g), full(wg), full(wup), full(wout), full(pg), full(wple), full(wpg), full(fg)],
        out_specs=tok(D),
        out_shape=jax.ShapeDtypeStruct((T, D), F32),
        compiler_params=pltpu.CompilerParams(
            dimension_semantics=("arbitrary",), vmem_limit_bytes=VMEM_LIMIT),
        name="post_mix",
    )(x, oa, ob, oc, p, g, wg, wup, wout, pg, wple, wpg, fg)


def _rope_tables(positions):
    inv = ROPE_THETA ** (-jnp.arange(0, ROT_DIM, 2, dtype=F32) / ROT_DIM)
    ang = positions.astype(F32)[..., None] * inv
    cos, sin = jnp.cos(ang), jnp.sin(ang)
    half = ROT_DIM // 2
    pad = HEAD_DIM - ROT_DIM
    zeros = jnp.zeros(cos.shape[:-1] + (half,), F32)
    rc = jnp.concatenate([cos, cos, jnp.ones(cos.shape[:-1] + (pad,), F32)], axis=-1)
    rs1 = jnp.concatenate([-sin, zeros, jnp.zeros(cos.shape[:-1] + (pad,), F32)], axis=-1)
    rs2 = jnp.concatenate([zeros, sin, jnp.zeros(cos.shape[:-1] + (pad,), F32)], axis=-1)
    rep = lambda t: jnp.concatenate([t] * (LANES // HEAD_DIM), axis=-1)
    return rep(rc), rep(rs1), rep(rs2)


def _proj_weights(w):
    sl = lambda o, n: w[:, o:o + n]
    dup = lambda o: jnp.concatenate([sl(o, HEAD_DIM)] * 2, axis=1)
    nat = jnp.concatenate([
        sl(_A_Q, 512), sl(_A_K, 512), sl(_B_Q, 512), sl(_I_Q, 512), sl(_C_Q, 512),
        dup(_B_K), dup(_I_K),
        dup(_C_KS), dup(_C_KS + HEAD_DIM), dup(_C_KW), dup(_C_KW + HEAD_DIM),
        sl(_C_KC, 128), sl(_C_VC, 128)], axis=1)
    tr = jnp.concatenate([
        sl(_A_V, 512), sl(_B_V, 64), sl(_C_VS, 128), sl(_C_VW, 128), sl(_I_W, 8), sl(_C_BG, 24)], axis=1)
    gates = jnp.concatenate([sl(_A_G, 512), sl(_B_G, 512), sl(_C_G, 512), sl(_MERGE, 3 * D_MODEL)], axis=1)
    return nat.astype(BF16), tr.T.astype(BF16), gates.astype(BF16)


def _overlap_t(S):
    nc = S // CMP_STRIDE
    n_blk = S // SLC_BLOCK
    cstart = np.arange(nc) * CMP_STRIDE
    sstart = np.arange(n_blk) * SLC_BLOCK
    ov = ((cstart[None, :] < sstart[:, None] + SLC_BLOCK) & (cstart[None, :] + CMP_LEN > sstart[:, None]))
    return jnp.asarray(ov.astype(np.float32), dtype=BF16)


def kernel(x, p, positions, norm_g, w_in, diff_lambda, diff_subln_g, cmp_pe_k, cmp_w1_k, cmp_w2_k,
           cmp_pe_v, cmp_w1_v, cmp_w2_v, w_up, w_out, ple_norm_g, w_ple, w_ple_gate, final_norm_g):
    B, S, D = x.shape
    depth = w_in.shape[0]
    k_sel = min(DSA_TOPK_MAX, S // 4)
    n_sel = min(SLC_TOPN_MAX, S // SLC_BLOCK)
    assert S % TM_PROJ == 0 and S % TQ == 0 and k_sel <= TQ and D == D_MODEL
    assert KC == TQ and WINDOW % KC == 0
    rc, rs1, rs2 = _rope_tables(positions)
    ovt = _overlap_t(S)
    half_len = CMP_LEN // 2
    kdim = half_len * HEAD_DIM
    row2 = lambda v: v.reshape(1, -1)

    for i in range(depth):
        lam_init = 0.8 - 0.6 * math.exp(-0.3 * i)
        wn, wt, wg = _proj_weights(w_in[i])
        (qa, ka, qb, iq, qc, qcr, kb, ik, ks, kw, cmp_raw,
         vat, vbt, vst, vwt, iwt, bgt) = _proj_call(x, row2(norm_g[i]), wn, wt, rc, rs1, rs2)

        r = cmp_raw.reshape(B, S // half_len, half_len, 2 * NSA_GROUPS, HEAD_DIM)
        r = r.transpose(0, 3, 1, 2, 4).reshape(B, 2 * NSA_GROUPS, S // half_len, kdim)
        pe = jnp.stack([cmp_pe_k[i], cmp_pe_v[i]])
        w1 = jnp.stack([cmp_w1_k[i], cmp_w1_v[i]])
        w2 = jnp.stack([cmp_w2_k[i], cmp_w2_v[i]]).astype(BF16)
        kcc, vct = _compress_call(
            r, pe[:, :half_len].reshape(2, 1, kdim), pe[:, half_len:].reshape(2, 1, kdim),
            w1[:, :half_len].reshape(2, kdim, CMP_HIDDEN).astype(BF16),
            w1[:, half_len:].reshape(2, kdim, CMP_HIDDEN).astype(BF16), w2)

        oa = _diff_call(qa, ka, vat, diff_lambda[i], diff_subln_g[i].reshape(-1, 1), lam_init)
        ob = _dsa_call(qb, iq, kb, ik, vbt, iwt, k_sel)
        oc = _nsa_call(qc, qcr, kcc, vct, ovt, ks, vst, kw, vwt, bgt, n_sel)

        x = _post_call(
            x.reshape(B * S, D), oa.reshape(B * S, -1), ob.reshape(B * S, -1), oc.reshape(B * S, -1),
            p[i].reshape(B * S, -1), row2(norm_g[i]), wg, w_up[i].astype(BF16), w_out[i].astype(BF16),
            row2(ple_norm_g[i]), w_ple[i].astype(BF16), w_ple_gate[i].astype(BF16), row2(final_norm_g),
            final=(i == depth - 1)).reshape(B, S, D)
    return x
```

```python
import functools
import math

import numpy as np
import jax
import jax.numpy as jnp
from jax import lax
from jax.experimental import pallas as pl
from jax.experimental.pallas import tpu as pltpu

F32 = jnp.float32
BF16 = jnp.bfloat16

D_MODEL = 1024
HEAD_DIM = 64
ROT_DIM = 16
ROPE_THETA = 500000.0
EPS = 1e-6
PLE_DIM = 256
DA_HEADS = 4
DSA_HEADS = 8
IDX_HEADS = 8
DSA_TOPK_MAX = 256
NSA_GROUPS = 2
NSA_HEADS = 8
CMP_LEN = 32
CMP_STRIDE = 16
CMP_HIDDEN = 128
SLC_BLOCK = 64
SLC_TOPN_MAX = 16
WINDOW = 512
FORCE_SCORE = 1e4
BRANCH_WIDTH = 512
N_BRANCH = 3

IN_SPLITS = (512, 512, 512, 512, 512, 64, 64, 512, 512, 64, 8,
             512, 128, 128, 128, 128, 128, 128, 512, 24, 3 * D_MODEL)
(_A_Q, _A_K, _A_V, _A_G, _B_Q, _B_K, _B_V, _B_G, _I_Q, _I_K, _I_W,
 _C_Q, _C_KC, _C_VC, _C_KS, _C_VS, _C_KW, _C_VW, _C_G, _C_BG, _MERGE) = [
    int(v) for v in np.concatenate([[0], np.cumsum(IN_SPLITS)[:-1]])]

LANES = 128
SUBLANES = 8
TQ = 512
KC = 512
TM_PROJ = 512
TM_POST = 256
NEG_INF = float("-inf")
M_INIT = -1e30
N_NAT = 3584
N_TR = 864
BISECT_ROUND = 8
ONES_ROWS = 16
HV = HEAD_DIM + ONES_ROWS
HV_A = 2 * HEAD_DIM + ONES_ROWS
LOG2E = math.log2(math.e)
VMEM_LIMIT = 56 * 1024 * 1024


def _dot_nt(a, b):
    return lax.dot_general(a, b, (((1,), (1,)), ((), ())), preferred_element_type=F32)


def _dot(a, b):
    return jnp.dot(a, b, preferred_element_type=F32)


def _rms(x, g):
    return x * lax.rsqrt(jnp.mean(x * x, axis=-1, keepdims=True) + EPS) * g


def _sigmoid(x):
    return 1.0 / (1.0 + jnp.exp(-x))


def _store_half_masked(src_ref, dst_ref, n_pairs):
    lane = lax.broadcasted_iota(jnp.int32, (src_ref.shape[0], LANES), 1)
    for hp in range(n_pairs):
        x = src_ref[:, LANES * hp:LANES * (hp + 1)]
        z = jnp.zeros_like(x)
        dst_ref[2 * hp] = jnp.where(lane < HEAD_DIM, x, z)
        dst_ref[2 * hp + 1] = jnp.where(lane >= HEAD_DIM, x, z)


def _fold_rows(x, op):
    n_acc = 4
    accs = [x[SUBLANES * r:SUBLANES * (r + 1)] for r in range(n_acc)]
    for r in range(n_acc, x.shape[0] // SUBLANES):
        accs[r % n_acc] = op(accs[r % n_acc], x[SUBLANES * r:SUBLANES * (r + 1)])
    return op(op(accs[0], accs[1]), op(accs[2], accs[3]))


def _attn_pipelined(j0, j1, nh, hv, prep_fn, score_fn, vt_fn, sbuf_ref, m_ref, al_ref, acc_ref,
                    prep_last_fn=None):
    j0 = jnp.asarray(j0, jnp.int32)
    j1 = jnp.asarray(j1, jnp.int32)
    m_ref[...] = jnp.full(m_ref.shape, M_INIT, F32)
    acc_ref[...] = jnp.zeros(acc_ref.shape, F32)

    def produce(j, prep):
        slot = jnp.bitwise_and(j, 1)
        ctx = prep(j)
        for h in range(nh):
            s = score_fn(ctx, h)
            sbuf_ref[slot, h] = s
            m_old = m_ref[1 - slot, h:h + 1, :]
            m_new = jnp.maximum(m_old, jnp.max(s, axis=0, keepdims=True))
            m_ref[slot, h:h + 1, :] = m_new
            al_ref[slot, h:h + 1, :] = jnp.exp2(m_old - m_new)

    def consume(j):
        slot = jnp.bitwise_and(j, 1)
        for h in range(nh):
            p = jnp.exp2(sbuf_ref[slot, h] - m_ref[slot, h:h + 1, :])
            rows = slice(hv * h, hv * (h + 1))
            acc_ref[rows, :] = (al_ref[slot, h:h + 1, :] * acc_ref[rows, :]
                                + _dot(vt_fn(j, h), p.astype(BF16)))

    def body(j, carry):
        consume(j - 1)
        produce(j, prep_fn)
        return carry

    if prep_last_fn is None:
        produce(j0, prep_fn)
        lax.fori_loop(j0 + 1, j1 + 1, body, 0)
        consume(j1)
        return

    @pl.when(j1 == j0)
    def _():
        produce(j1, prep_last_fn)
        consume(j1)

    @pl.when(j1 > j0)
    def _():
        produce(j0, prep_fn)
        lax.fori_loop(j0 + 1, j1, body, 0)
        consume(j1 - 1)
        produce(j1, prep_last_fn)
        consume(j1)


def _attn_scratch(nh, hv):
    return [pltpu.VMEM((2, nh, KC, TQ), F32), pltpu.VMEM((2, nh, TQ), F32),
            pltpu.VMEM((2, nh, TQ), F32), pltpu.VMEM((nh * hv, TQ), F32)]


def _ones_rows(n_cols):
    r = lax.broadcasted_iota(jnp.int32, (ONES_ROWS, n_cols), 0)
    return jnp.where(r == 0, 1.0, 0.0).astype(BF16)


def _proj_kernel(x_ref, g_ref, wn_ref, wt_ref, rc_ref, rs1_ref, rs2_ref,
                 qa_ref, ka_ref, qb_ref, iq_ref, qc_ref, qcr_ref, kb_ref, ik_ref,
                 ks_ref, kw_ref, cmp_ref,
                 vat_ref, vbt_ref, vst_ref, vwt_ref, iwt_ref, bgt_ref):
    h = _rms(x_ref[...], g_ref[...]).astype(BF16)
    rc, rs1, rs2 = rc_ref[...], rs1_ref[...], rs2_ref[...]

    def rope(z):
        return z * rc + pltpu.roll(z, LANES - ROT_DIM // 2, 1) * rs1 + pltpu.roll(z, ROT_DIM // 2, 1) * rs2

    qscale = HEAD_DIM ** -0.5
    qscale2 = qscale * LOG2E
    segs = (
        (0, 512, ((qa_ref, True, qscale2),)),
        (512, 512, ((ka_ref, True, 1.0),)),
        (1024, 512, ((qb_ref, True, qscale2),)),
        (1536, 512, ((iq_ref, True, qscale),)),
        (2048, 512, ((qc_ref, False, qscale), (qcr_ref, True, qscale2))),
        (2560, 128, ((kb_ref, True, 1.0),)),
        (2688, 128, ((ik_ref, True, 1.0),)),
        (2816, 256, ((ks_ref, True, 1.0),)),
        (3072, 256, ((kw_ref, True, 1.0),)),
        (3328, 256, ((cmp_ref, False, 1.0),)),
    )
    for c0, width, outs in segs:
        z = _dot(h, wn_ref[:, c0:c0 + width])
        for out_ref, rot, scale in outs:
            for j in range(width // LANES):
                zj = z[:, LANES * j:LANES * (j + 1)]
                if rot:
                    zj = rope(zj)
                if scale != 1.0:
                    zj = zj * scale
                out_ref[:, LANES * j:LANES * (j + 1)] = zj.astype(out_ref.dtype)

    zt = _dot_nt(wt_ref[...], h)
    ones = _ones_rows(zt.shape[1])
    vd = 2 * HEAD_DIM
    for h in range(DA_HEADS):
        vat_ref[HV_A * h:HV_A * h + vd, :] = zt[vd * h:vd * (h + 1)].astype(BF16)
        vat_ref[HV_A * h + vd:HV_A * (h + 1), :] = ones
    vbt_ref[0:HEAD_DIM, :] = zt[512:576].astype(BF16)
    vbt_ref[HEAD_DIM:HV, :] = ones
    for g in range(NSA_GROUPS):
        for ref, base in ((vst_ref, 576), (vwt_ref, 704)):
            ref[HV * g:HV * g + HEAD_DIM, :] = zt[base + HEAD_DIM * g:base + HEAD_DIM * (g + 1)].astype(BF16)
            ref[HV * g + HEAD_DIM:HV * (g + 1), :] = ones
    iwt_ref[...] = zt[832:840] * (IDX_HEADS ** -0.5)
    bgt_ref[...] = zt[840:864]


def _proj_call(x, g, wn, wt, rc, rs1, rs2):
    B, S, D = x.shape
    tm = TM_PROJ
    nat = lambda w: pl.BlockSpec((None, tm, w), lambda b, i: (b, i, 0))
    tr = lambda r: pl.BlockSpec((None, r, tm), lambda b, i: (b, 0, i))
    full = lambda a: pl.BlockSpec(a.shape, lambda b, i: (0,) * a.ndim)
    sds = jax.ShapeDtypeStruct
    out_shape = (
        sds((B, S, 512), BF16), sds((B, S, 512), BF16), sds((B, S, 512), BF16), sds((B, S, 512), BF16),
        sds((B, S, 512), BF16), sds((B, S, 512), BF16), sds((B, S, 128), BF16), sds((B, S, 128), BF16),
        sds((B, S, 256), BF16), sds((B, S, 256), BF16), sds((B, S, 256), F32),
        sds((B, DA_HEADS * HV_A, S), BF16), sds((B, HV, S), BF16),
        sds((B, NSA_GROUPS * HV, S), BF16), sds((B, NSA_GROUPS * HV, S), BF16),
        sds((B, 8, S), F32), sds((B, 24, S), F32),
    )
    out_specs = (nat(512), nat(512), nat(512), nat(512), nat(512), nat(512), nat(128), nat(128),
                 nat(256), nat(256), nat(256),
                 tr(DA_HEADS * HV_A), tr(HV), tr(NSA_GROUPS * HV), tr(NSA_GROUPS * HV), tr(8), tr(24))
    return pl.pallas_call(
        _proj_kernel,
        grid=(B, S // tm),
        in_specs=[nat(D), full(g), full(wn), full(wt), nat(LANES), nat(LANES), nat(LANES)],
        out_specs=out_specs,
        out_shape=out_shape,
        compiler_params=pltpu.CompilerParams(
            dimension_semantics=("arbitrary", "arbitrary"), vmem_limit_bytes=VMEM_LIMIT),
        name="in_proj",
    )(x, g, wn, wt, rc, rs1, rs2)


def _compress_kernel(r_ref, pea_ref, peb_ref, w1a_ref, w1b_ref, w2_ref, nat_ref, t_ref):
    r = r_ref[...]
    y1 = _dot((r + pea_ref[...]).astype(BF16), w1a_ref[...])
    y2 = _dot((r + peb_ref[...]).astype(BF16), w1b_ref[...])
    nrow = r.shape[0]
    hid = jax.nn.gelu(y1 + pltpu.roll(y2, nrow - 1, 0))
    o = _dot(hid.astype(BF16), w2_ref[...])
    nat_ref[...] = jnp.concatenate([o, o], axis=1).astype(BF16)
    t_ref[...] = o.T.astype(BF16)


def _compress_call(r, pea, peb, w1a, w1b, w2):
    B, _, nr, kdim = r.shape
    kind = lambda b, j: (j // NSA_GROUPS, 0, 0)
    return pl.pallas_call(
        _compress_kernel,
        grid=(B, 2 * NSA_GROUPS),
        in_specs=[
            pl.BlockSpec((None, None, nr, kdim), lambda b, j: (b, j, 0, 0)),
            pl.BlockSpec((None, 1, kdim), kind),
            pl.BlockSpec((None, 1, kdim), kind),
            pl.BlockSpec((None, kdim, CMP_HIDDEN), kind),
            pl.BlockSpec((None, kdim, CMP_HIDDEN), kind),
            pl.BlockSpec((None, CMP_HIDDEN, HEAD_DIM), kind),
        ],
        out_specs=(
            pl.BlockSpec((None, None, nr, 2 * HEAD_DIM), lambda b, j: (b, j, 0, 0)),
            pl.BlockSpec((None, None, HEAD_DIM, nr), lambda b, j: (b, j, 0, 0)),
        ),
        out_shape=(jax.ShapeDtypeStruct((B, 2 * NSA_GROUPS, nr, 2 * HEAD_DIM), BF16),
                   jax.ShapeDtypeStruct((B, 2 * NSA_GROUPS, HEAD_DIM, nr), BF16)),
        compiler_params=pltpu.CompilerParams(dimension_semantics=("arbitrary", "arbitrary")),
        name="nsa_compress",
    )(r, pea, peb, w1a, w1b, w2)


def _diff_kernel(q_ref, k_ref, vt_ref, lam_ref, sg_ref, o_ref,
                 qz_ref, sbuf_ref, m_ref, al_ref, acc_ref, *, lam_init):
    i = pl.program_id(1)
    lp = lam_ref[...]
    lam = (jnp.exp(jnp.sum(lp[0:1] * lp[1:2], axis=1, keepdims=True))
           - jnp.exp(jnp.sum(lp[2:3] * lp[3:4], axis=1, keepdims=True)) + lam_init)
    row_minus_col = (lax.broadcasted_iota(jnp.int32, (KC, TQ), 0)
                     - lax.broadcasted_iota(jnp.int32, (KC, TQ), 1))
    _store_half_masked(q_ref, qz_ref, DA_HEADS)
    vdim = 2 * HEAD_DIM
    n_maps = 2 * DA_HEADS

    def prep(j):
        return pl.multiple_of(j * KC, KC), None

    def prep_diag(j):
        return pl.multiple_of(j * KC, KC), jnp.where(row_minus_col <= 0, 0.0, NEG_INF)

    def score(ctx, hm):
        off, bias = ctx
        h = hm // 2
        s = _dot_nt(k_ref[pl.ds(off, KC), vdim * h:vdim * (h + 1)], qz_ref[hm])
        return s if bias is None else s + bias

    def values(j, hm):
        h = hm // 2
        return vt_ref[HV_A * h:HV_A * (h + 1), pl.ds(pl.multiple_of(j * KC, KC), KC)]

    _attn_pipelined(0, i, n_maps, HV_A, prep, score, values, sbuf_ref, m_ref, al_ref, acc_ref,
                    prep_last_fn=prep_diag)

    for h in range(DA_HEADS):
        r0, r1 = HV_A * 2 * h, HV_A * (2 * h + 1)
        o_t = (acc_ref[r0:r0 + vdim, :] / acc_ref[r0 + vdim:r0 + vdim + 1, :]
               - lam * (acc_ref[r1:r1 + vdim, :] / acc_ref[r1 + vdim:r1 + vdim + 1, :]))
        ms = jnp.mean(o_t * o_t, axis=0, keepdims=True)
        y = o_t * lax.rsqrt(ms + EPS) * sg_ref[...] * (1.0 - lam_init)
        o_ref[:, vdim * h:vdim * (h + 1)] = y.T


def _diff_call(qa, ka, vat, lam_p, sg, lam_init):
    B, S, W = qa.shape
    n_maps = 2 * DA_HEADS
    return pl.pallas_call(
        functools.partial(_diff_kernel, lam_init=lam_init),
        grid=(B, S // TQ),
        in_specs=[
            pl.BlockSpec((None, TQ, W), lambda b, i: (b, i, 0)),
            pl.BlockSpec((None, S, W), lambda b, i: (b, 0, 0)),
            pl.BlockSpec((None, DA_HEADS * HV_A, S), lambda b, i: (b, 0, 0)),
            pl.BlockSpec(lam_p.shape, lambda b, i: (0, 0)),
            pl.BlockSpec(sg.shape, lambda b, i: (0, 0)),
        ],
        out_specs=pl.BlockSpec((None, TQ, W), lambda b, i: (b, i, 0)),
        out_shape=jax.ShapeDtypeStruct((B, S, W), F32),
        scratch_shapes=[pltpu.VMEM((n_maps, TQ, LANES), BF16)] + _attn_scratch(n_maps, HV_A),
        compiler_params=pltpu.CompilerParams(
            dimension_semantics=("arbitrary", "arbitrary"), vmem_limit_bytes=VMEM_LIMIT),
        name="diff_attn",
    )(qa, ka, vat, lam_p, sg)


def _dsa_kernel(q_ref, iq_ref, kb_ref, ik_ref, vt_ref, iw_ref, o_ref,
                sc_ref, tj_ref, qz_ref, sbuf_ref, m_ref, al_ref, acc_ref, *, k_sel, seq_len):
    i = pl.program_id(1)
    nch = i + 1
    row = lax.broadcasted_iota(jnp.int32, (KC, TQ), 0)
    col = lax.broadcasted_iota(jnp.int32, (KC, TQ), 1)
    causal = row <= col
    rowf = row.astype(F32)
    iw = iw_ref[...]
    kf = float(k_sel)
    _store_half_masked(iq_ref, qz_ref, IDX_HEADS // 2)

    def idx_chunk(j, carry, masked):
        mx, mn = carry
        off = pl.multiple_of(j * KC, KC)
        ikc = ik_ref[pl.ds(off, KC), :]
        sc = jnp.zeros((KC, TQ), F32)
        for h in range(IDX_HEADS):
            sc = sc + iw[h:h + 1, :] * jnp.maximum(_dot_nt(ikc, qz_ref[h]), 0.0)
        lo_src = sc
        if masked:
            lo_src = jnp.where(causal, sc, jnp.inf)
            sc = jnp.where(causal, sc, NEG_INF)
        sc_ref[pl.ds(off, KC), :] = sc
        return (jnp.maximum(mx, _fold_rows(sc, jnp.maximum)),
                jnp.minimum(mn, _fold_rows(lo_src, jnp.minimum)))

    carry = (jnp.full((SUBLANES, TQ), NEG_INF, F32), jnp.full((SUBLANES, TQ), jnp.inf, F32))
    carry = lax.fori_loop(0, i, functools.partial(idx_chunk, masked=False), carry)
    mx8, mn8 = idx_chunk(i, carry, True)
    row_max = jnp.max(mx8, axis=0, keepdims=True)
    row_min = jnp.min(mn8, axis=0, keepdims=True)

    def reduce_chunks(fn, op, init):
        def body(j, c):
            off = pl.multiple_of(j * KC, KC)
            x = sc_ref[pl.ds(off, KC), :]
            idx = rowf + (j * KC).astype(F32)
            return op(c, _fold_rows(fn(x, idx), op))
        return lax.fori_loop(0, nch, body, jnp.full((SUBLANES, TQ), init, F32))

    def count(pred):
        c8 = reduce_chunks(lambda x, idx: jnp.where(pred(x, idx), 1.0, 0.0), jnp.add, 0.0)
        return jnp.sum(c8, axis=0, keepdims=True)

    n_causal = (i * TQ + 1 + lax.broadcasted_iota(jnp.int32, (1, TQ), 1)).astype(F32)
    keep_all = n_causal <= kf

    hi0 = row_max + (row_max - row_min) + 1.0

    def cond(st):
        it, _, _, _, act = st
        return jnp.logical_and(it < 64, act > 0.0)

    def body(st):
        it, lo, hi, clo, _ = st

        def bisect(_, c):
            lo, hi, clo = c
            mid = lo + (hi - lo) * 0.5
            cnt = count(lambda x, idx: x >= mid)
            ge = cnt >= kf
            return jnp.where(ge, mid, lo), jnp.where(ge, hi, mid), jnp.where(ge, cnt, clo)

        lo, hi, clo = lax.fori_loop(0, BISECT_ROUND, bisect, (lo, hi, clo))
        emin = jnp.min(reduce_chunks(lambda x, idx: jnp.where(x >= lo, x, jnp.inf),
                                     jnp.minimum, jnp.inf), axis=0, keepdims=True)
        emax = jnp.max(reduce_chunks(lambda x, idx: jnp.where(x < hi, x, NEG_INF),
                                     jnp.maximum, NEG_INF), axis=0, keepdims=True)
        done = jnp.logical_or(clo <= kf, emin >= emax)
        return it + 1, emin, hi, clo, jnp.max(jnp.where(done, 0.0, 1.0))

    _, thr, _, clo, _ = lax.while_loop(
        cond, body, (jnp.int32(0), row_min, hi0, n_causal, jnp.float32(1.0)))
    tied = clo > kf
    tj_ref[0:1, :] = jnp.where(keep_all, NEG_INF, thr)
    tj_ref[1:2, :] = jnp.where(keep_all, -1.0, float(seq_len))

    @pl.when(jnp.max(jnp.where(tied, 1.0, 0.0)) > 0.0)
    def _():
        need = kf - count(lambda x, idx: x > thr)

        def jbody(_, st):
            jlo, jhi = st
            mid = jnp.floor((jlo + jhi) * 0.5)
            c = count(lambda x, idx: jnp.logical_and(x == thr, idx <= mid))
            ok = c >= need
            return jnp.where(ok, jlo, mid), jnp.where(ok, mid, jhi)

        n_it = int(math.ceil(math.log2(seq_len))) + 1
        _, jhi = lax.fori_loop(
            0, n_it, jbody,
            (jnp.full((1, TQ), -1.0, F32), jnp.full((1, TQ), float(seq_len - 1), F32)))
        tj_ref[1:2, :] = jnp.where(tied, jhi, tj_ref[1:2, :])

    thr = tj_ref[0:1, :]
    jmax = tj_ref[1:2, :]

    _store_half_masked(q_ref, qz_ref, DSA_HEADS // 2)

    def prep(j):
        off = pl.multiple_of(j * KC, KC)
        x = sc_ref[pl.ds(off, KC), :]
        idx = rowf + (j * KC).astype(F32)
        sel = jnp.logical_or(x > thr, jnp.logical_and(x == thr, idx <= jmax))
        return kb_ref[pl.ds(off, KC), :], jnp.where(sel, 0.0, NEG_INF)

    def score(ctx, h):
        kc, bias = ctx
        return _dot_nt(kc, qz_ref[h]) + bias

    def values(j, h):
        return vt_ref[:, pl.ds(pl.multiple_of(j * KC, KC), KC)]

    _attn_pipelined(0, i, DSA_HEADS, HV, prep, score, values, sbuf_ref, m_ref, al_ref, acc_ref)
    for hp in range(DSA_HEADS // 2):
        blk = jnp.concatenate(
            [acc_ref[HV * h:HV * h + HEAD_DIM, :] / acc_ref[HV * h + HEAD_DIM:HV * h + HEAD_DIM + 1, :]
             for h in (2 * hp, 2 * hp + 1)], axis=0)
        o_ref[:, LANES * hp:LANES * (hp + 1)] = blk.T


def _dsa_call(qb, iq, kb, ik, vbt, iwt, k_sel):
    B, S, W = qb.shape
    return pl.pallas_call(
        functools.partial(_dsa_kernel, k_sel=k_sel, seq_len=S),
        grid=(B, S // TQ),
        in_specs=[
            pl.BlockSpec((None, TQ, W), lambda b, i: (b, i, 0)),
            pl.BlockSpec((None, TQ, W), lambda b, i: (b, i, 0)),
            pl.BlockSpec((None, S, LANES), lambda b, i: (b, 0, 0)),
            pl.BlockSpec((None, S, LANES), lambda b, i: (b, 0, 0)),
            pl.BlockSpec((None, HV, S), lambda b, i: (b, 0, 0)),
            pl.BlockSpec((None, IDX_HEADS, TQ), lambda b, i: (b, 0, i)),
        ],
        out_specs=pl.BlockSpec((None, TQ, W), lambda b, i: (b, i, 0)),
        out_shape=jax.ShapeDtypeStruct((B, S, W), F32),
        scratch_shapes=[pltpu.VMEM((S, TQ), F32), pltpu.VMEM((SUBLANES, TQ), F32),
                        pltpu.VMEM((DSA_HEADS, TQ, LANES), BF16)] + _attn_scratch(DSA_HEADS, HV),
        compiler_params=pltpu.CompilerParams(
            dimension_semantics=("arbitrary", "arbitrary"), vmem_limit_bytes=VMEM_LIMIT),
        name="dsa_attn",
    )(qb, iq, kb, ik, vbt, iwt)


def _nsa_kernel(q_ref, qr_ref, kc_ref, vct_ref, ovt_ref, ks_ref, vst_ref, kw_ref, vwt_ref, bg_ref,
                o_ref, cmp_ref, slc_ref, sel_ref, qz_ref, sbuf_ref, m_ref, al_ref, acc_ref, *, n_sel):
    i = pl.program_id(1)
    row_minus_col = (lax.broadcasted_iota(jnp.int32, (KC, TQ), 0)
                     - lax.broadcasted_iota(jnp.int32, (KC, TQ), 1))
    tq = i * TQ + lax.broadcasted_iota(jnp.int32, (1, TQ), 1)
    hpg = NSA_HEADS // NSA_GROUPS
    nc = kc_ref.shape[1]
    n_blk = sel_ref.shape[1]
    gw = 2 * HEAD_DIM

    _store_half_masked(q_ref, qz_ref, NSA_HEADS // 2)
    cmp_valid = CMP_STRIDE * lax.broadcasted_iota(jnp.int32, (nc, TQ), 0) + (CMP_LEN - 1) <= tq
    blk = lax.broadcasted_iota(jnp.int32, (n_blk, TQ), 0)
    forced = jnp.logical_or(blk == jnp.right_shift(tq, int(math.log2(SLC_BLOCK))), blk == 0)
    blk_causal = blk * SLC_BLOCK <= tq
    for g in range(NSA_GROUPS):
        kcg = kc_ref[g]
        vcg = vct_ref[NSA_GROUPS + g]
        imp = jnp.zeros((n_blk, TQ), F32)
        for hh in range(hpg):
            h = hpg * g + hh
            s = jnp.where(cmp_valid, _dot_nt(kcg, qz_ref[h]), NEG_INF)
            m = jnp.maximum(jnp.max(s, axis=0, keepdims=True), M_INIT)
            p = jnp.exp(s - m)
            l = jnp.sum(p, axis=0, keepdims=True)
            pc = (p * jnp.where(l > 0.0, 1.0 / l, 0.0)).astype(BF16)
            cmp_ref[HEAD_DIM * h:HEAD_DIM * (h + 1), :] = _dot(vcg, pc)
            imp = imp + _dot(ovt_ref[...], pc)
        imp = jnp.where(forced, FORCE_SCORE, imp)
        imp = jnp.where(blk_causal, imp, NEG_INF)
        rank = jnp.zeros((n_blk, TQ), F32)
        for mp in range(n_blk):
            r = imp[mp:mp + 1, :]
            ahead = jnp.logical_or(r > imp, jnp.logical_and(r == imp, blk > mp))
            rank = rank + jnp.where(ahead, 1.0, 0.0)
        sel_ref[g] = jnp.where(rank < float(n_sel), 0.0, NEG_INF)

    _store_half_masked(qr_ref, qz_ref, NSA_HEADS // 2)
    bpc = KC // SLC_BLOCK

    def slc_prep(j):
        off = pl.multiple_of(j * KC, KC)
        visible = row_minus_col <= (i - j) * KC
        biases = []
        for g in range(NSA_GROUPS):
            bias = jnp.concatenate(
                [jnp.broadcast_to(sel_ref[g, pl.ds(bpc * j + b, 1), :], (SLC_BLOCK, TQ))
                 for b in range(bpc)], axis=0)
            biases.append(jnp.where(visible, bias, NEG_INF))
        return off, biases

    def slc_score(ctx, h):
        off, biases = ctx
        g = h // hpg
        return _dot_nt(ks_ref[pl.ds(off, KC), gw * g:gw * (g + 1)], qz_ref[h]) + biases[g]

    def slc_values(j, h):
        g = h // hpg
        return vst_ref[HV * g:HV * (g + 1), pl.ds(pl.multiple_of(j * KC, KC), KC)]

    _attn_pipelined(0, i, NSA_HEADS, HV, slc_prep, slc_score, slc_values,
                    sbuf_ref, m_ref, al_ref, acc_ref)

    def head_out(h):
        return acc_ref[HV * h:HV * h + HEAD_DIM, :] / acc_ref[HV * h + HEAD_DIM:HV * h + HEAD_DIM + 1, :]

    for h in range(NSA_HEADS):
        slc_ref[HEAD_DIM * h:HEAD_DIM * (h + 1), :] = head_out(h)

    def win_prep(j):
        off = pl.multiple_of(j * KC, KC)
        d = (i - j) * KC
        inside = jnp.logical_and(row_minus_col <= d, row_minus_col > d - WINDOW)
        return off, jnp.where(inside, 0.0, NEG_INF)

    def win_score(ctx, h):
        off, bias = ctx
        g = h // hpg
        return _dot_nt(kw_ref[pl.ds(off, KC), gw * g:gw * (g + 1)], qz_ref[h]) + bias

    def win_values(j, h):
        g = h // hpg
        return vwt_ref[HV * g:HV * (g + 1), pl.ds(pl.multiple_of(j * KC, KC), KC)]

    _attn_pipelined(jnp.maximum(i - WINDOW // KC, 0), i, NSA_HEADS, HV,
                    win_prep, win_score, win_values, sbuf_ref, m_ref, al_ref, acc_ref)

    gb = _sigmoid(bg_ref[...])
    for hp in range(NSA_HEADS // 2):
        parts = []
        for h in (2 * hp, 2 * hp + 1):
            rows = slice(HEAD_DIM * h, HEAD_DIM * (h + 1))
            parts.append(gb[3 * h:3 * h + 1, :] * cmp_ref[rows, :]
                         + gb[3 * h + 1:3 * h + 2, :] * slc_ref[rows, :]
                         + gb[3 * h + 2:3 * h + 3, :] * head_out(h))
        o_ref[:, LANES * hp:LANES * (hp + 1)] = jnp.concatenate(parts, axis=0).T


def _nsa_call(qc, qcr, kcc, vct, ovt, ks, vst, kw, vwt, bgt, n_sel):
    B, S, W = qc.shape
    n_blk = S // SLC_BLOCK
    qspec = pl.BlockSpec((None, TQ, W), lambda b, i: (b, i, 0))
    return pl.pallas_call(
        functools.partial(_nsa_kernel, n_sel=n_sel),
        grid=(B, S // TQ),
        in_specs=[
            qspec, qspec,
            pl.BlockSpec((None,) + kcc.shape[1:], lambda b, i: (b, 0, 0, 0)),
            pl.BlockSpec((None,) + vct.shape[1:], lambda b, i: (b, 0, 0, 0)),
            pl.BlockSpec(ovt.shape, lambda b, i: (0, 0)),
            pl.BlockSpec((None, S, 2 * LANES), lambda b, i: (b, 0, 0)),
            pl.BlockSpec((None, NSA_GROUPS * HV, S), lambda b, i: (b, 0, 0)),
            pl.BlockSpec((None, S, 2 * LANES), lambda b, i: (b, 0, 0)),
            pl.BlockSpec((None, NSA_GROUPS * HV, S), lambda b, i: (b, 0, 0)),
            pl.BlockSpec((None, 3 * NSA_HEADS, TQ), lambda b, i: (b, 0, i)),
        ],
        out_specs=qspec,
        out_shape=jax.ShapeDtypeStruct((B, S, W), F32),
        scratch_shapes=[pltpu.VMEM((W, TQ), F32), pltpu.VMEM((W, TQ), F32),
                        pltpu.VMEM((NSA_GROUPS, n_blk, TQ), F32),
                        pltpu.VMEM((NSA_HEADS, TQ, LANES), BF16)] + _attn_scratch(NSA_HEADS, HV),
        compiler_params=pltpu.CompilerParams(
            dimension_semantics=("arbitrary", "arbitrary"), vmem_limit_bytes=VMEM_LIMIT),
        name="nsa_attn",
    )(qc, qcr, kcc, vct, ovt, ks, vst, kw, vwt, bgt)


def _post_kernel(x_ref, oa_ref, ob_ref, oc_ref, p_ref, g_ref, wg_ref, wup_ref, wout_ref,
                 pg_ref, wple_ref, wpg_ref, fg_ref, o_ref, *, final):
    x = x_ref[...]
    h = _rms(x, g_ref[...]).astype(BF16)
    mixed = jnp.zeros(x.shape, F32)
    for n, br_ref in enumerate((oa_ref, ob_ref, oc_ref)):
        gate = _dot(h, wg_ref[:, BRANCH_WIDTH * n:BRANCH_WIDTH * (n + 1)])
        og = br_ref[...] * (gate * _sigmoid(gate))
        up = _dot(og.astype(BF16), wup_ref[n])
        c0 = N_BRANCH * BRANCH_WIDTH + D_MODEL * n
        mixed = mixed + _sigmoid(_dot(h, wg_ref[:, c0:c0 + D_MODEL])) * up
    x1 = x + _dot(mixed.astype(BF16), wout_ref[...])
    emb = _dot(p_ref[...].astype(BF16), wple_ref[...])
    hg = _dot(_rms(x1, pg_ref[...]).astype(BF16), wpg_ref[...])
    x2 = x1 + emb * _sigmoid(hg)
    if final:
        x2 = _rms(x2, fg_ref[...])
    o_ref[...] = x2


def _post_call(x, oa, ob, oc, p, g, wg, wup, wout, pg, wple, wpg, fg, final):
    T, D = x.shape
    tm = TM_POST
    tok = lambda w: pl.BlockSpec((tm, w), lambda i: (i, 0))
    full = lambda a: pl.BlockSpec(a.shape, lambda i: (0,) * a.ndim)
    return pl.pallas_call(
        functools.partial(_post_kernel, final=final),
        grid=(T // tm,),
        in_specs=[tok(D), tok(BRANCH_WIDTH), tok(BRANCH_WIDTH), tok(BRANCH_WIDTH), tok(PLE_DIM),
                  full(g), full(wg), full(wup), full(wout), full(pg), full(wple), full(wpg), full(fg)],
        out_specs=tok(D),
        out_shape=jax.ShapeDtypeStruct((T, D), F32),
        compiler_params=pltpu.CompilerParams(
            dimension_semantics=("arbitrary",), vmem_limit_bytes=VMEM_LIMIT),
        name="post_mix",
    )(x, oa, ob, oc, p, g, wg, wup, wout, pg, wple, wpg, fg)


def _rope_tables(positions):
    inv = ROPE_THETA ** (-jnp.arange(0, ROT_DIM, 2, dtype=F32) / ROT_DIM)
    ang = positions.astype(F32)[..., None] * inv
    cos, sin = jnp.cos(ang), jnp.sin(ang)
    half = ROT_DIM // 2
    pad = HEAD_DIM - ROT_DIM
    zeros = jnp.zeros(cos.shape[:-1] + (half,), F32)
    rc = jnp.concatenate([cos, cos, jnp.ones(cos.shape[:-1] + (pad,), F32)], axis=-1)
    rs1 = jnp.concatenate([-sin, zeros, jnp.zeros(cos.shape[:-1] + (pad,), F32)], axis=-1)
    rs2 = jnp.concatenate([zeros, sin, jnp.zeros(cos.shape[:-1] + (pad,), F32)], axis=-1)
    rep = lambda t: jnp.concatenate([t] * (LANES // HEAD_DIM), axis=-1)
    return rep(rc), rep(rs1), rep(rs2)


def _proj_weights(w):
    sl = lambda o, n: w[:, o:o + n]
    dup = lambda o: jnp.concatenate([sl(o, HEAD_DIM)] * 2, axis=1)
    nat = jnp.concatenate([
        sl(_A_Q, 512), sl(_A_K, 512), sl(_B_Q, 512), sl(_I_Q, 512), sl(_C_Q, 512),
        dup(_B_K), dup(_I_K),
        dup(_C_KS), dup(_C_KS + HEAD_DIM), dup(_C_KW), dup(_C_KW + HEAD_DIM),
        sl(_C_KC, 128), sl(_C_VC, 128)], axis=1)
    tr = jnp.concatenate([
        sl(_A_V, 512), sl(_B_V, 64), sl(_C_VS, 128), sl(_C_VW, 128), sl(_I_W, 8), sl(_C_BG, 24)], axis=1)
    gates = jnp.concatenate([sl(_A_G, 512), sl(_B_G, 512), sl(_C_G, 512), sl(_MERGE, 3 * D_MODEL)], axis=1)
    return nat.astype(BF16), tr.T.astype(BF16), gates.astype(BF16)


def _overlap_t(S):
    nc = S // CMP_STRIDE
    n_blk = S // SLC_BLOCK
    cstart = np.arange(nc) * CMP_STRIDE
    sstart = np.arange(n_blk) * SLC_BLOCK
    ov = ((cstart[None, :] < sstart[:, None] + SLC_BLOCK) & (cstart[None, :] + CMP_LEN > sstart[:, None]))
    return jnp.asarray(ov.astype(np.float32), dtype=BF16)


def kernel(x, p, positions, norm_g, w_in, diff_lambda, diff_subln_g, cmp_pe_k, cmp_w1_k, cmp_w2_k,
           cmp_pe_v, cmp_w1_v, cmp_w2_v, w_up, w_out, ple_norm_g, w_ple, w_ple_gate, final_norm_g):
    B, S, D = x.shape
    depth = w_in.shape[0]
    k_sel = min(DSA_TOPK_MAX, S // 4)
    n_sel = min(SLC_TOPN_MAX, S // SLC_BLOCK)
    assert S % TM_PROJ == 0 and S % TQ == 0 and k_sel <= TQ and D == D_MODEL
    assert KC == TQ and WINDOW % KC == 0
    rc, rs1, rs2 = _rope_tables(positions)
    ovt = _overlap_t(S)
    half_len = CMP_LEN // 2
    kdim = half_len * HEAD_DIM
    row2 = lambda v: v.reshape(1, -1)

    for i in range(depth):
        lam_init = 0.8 - 0.6 * math.exp(-0.3 * i)
        wn, wt, wg = _proj_weights(w_in[i])
        (qa, ka, qb, iq, qc, qcr, kb, ik, ks, kw, cmp_raw,
         vat, vbt, vst, vwt, iwt, bgt) = _proj_call(x, row2(norm_g[i]), wn, wt, rc, rs1, rs2)

        r = cmp_raw.reshape(B, S // half_len, half_len, 2 * NSA_GROUPS, HEAD_DIM)
        r = r.transpose(0, 3, 1, 2, 4).reshape(B, 2 * NSA_GROUPS, S // half_len, kdim)
        pe = jnp.stack([cmp_pe_k[i], cmp_pe_v[i]])
        w1 = jnp.stack([cmp_w1_k[i], cmp_w1_v[i]])
        w2 = jnp.stack([cmp_w2_k[i], cmp_w2_v[i]]).astype(BF16)
        kcc, vct = _compress_call(
            r, pe[:, :half_len].reshape(2, 1, kdim), pe[:, half_len:].reshape(2, 1, kdim),
            w1[:, :half_len].reshape(2, kdim, CMP_HIDDEN).astype(BF16),
            w1[:, half_len:].reshape(2, kdim, CMP_HIDDEN).astype(BF16), w2)

        oa = _diff_call(qa, ka, vat, diff_lambda[i], diff_subln_g[i].reshape(-1, 1), lam_init)
        ob = _dsa_call(qb, iq, kb, ik, vbt, iwt, k_sel)
        oc = _nsa_call(qc, qcr, kcc, vct, ovt, ks, vst, kw, vwt, bgt, n_sel)

        x = _post_call(
            x.reshape(B * S, D), oa.reshape(B * S, -1), ob.reshape(B * S, -1), oc.reshape(B * S, -1),
            p[i].reshape(B * S, -1), row2(norm_g[i]), wg, w_up[i].astype(BF16), w_out[i].astype(BF16),
            row2(ple_norm_g[i]), w_ple[i].astype(BF16), w_ple_gate[i].astype(BF16), row2(final_norm_g),
            final=(i == depth - 1)).reshape(B, S, D)
    return x
```

```python
import functools
import math

import numpy as np
import jax
import jax.numpy as jnp
from jax import lax
from jax.experimental import pallas as pl
from jax.experimental.pallas import tpu as pltpu

F32 = jnp.float32
BF16 = jnp.bfloat16

D_MODEL = 1024
HEAD_DIM = 64
ROT_DIM = 16
ROPE_THETA = 500000.0
EPS = 1e-6
PLE_DIM = 256
DA_HEADS = 4
DSA_HEADS = 8
IDX_HEADS = 8
DSA_TOPK_MAX = 256
NSA_GROUPS = 2
NSA_HEADS = 8
CMP_LEN = 32
CMP_STRIDE = 16
CMP_HIDDEN = 128
SLC_BLOCK = 64
SLC_TOPN_MAX = 16
WINDOW = 512
FORCE_SCORE = 1e4
BRANCH_WIDTH = 512
N_BRANCH = 3

IN_SPLITS = (512, 512, 512, 512, 512, 64, 64, 512, 512, 64, 8,
             512, 128, 128, 128, 128, 128, 128, 512, 24, 3 * D_MODEL)
(_A_Q, _A_K, _A_V, _A_G, _B_Q, _B_K, _B_V, _B_G, _I_Q, _I_K, _I_W,
 _C_Q, _C_KC, _C_VC, _C_KS, _C_VS, _C_KW, _C_VW, _C_G, _C_BG, _MERGE) = [
    int(v) for v in np.concatenate([[0], np.cumsum(IN_SPLITS)[:-1]])]

LANES = 128
SUBLANES = 8
TQ = 512
KC = 512
TM_PROJ = 512
TM_POST = 256
NEG_INF = float("-inf")
M_INIT = -1e30
N_NAT = 3584
N_TR = 864
BISECT_PLAIN = 10
ONES_ROWS = 16
HV = HEAD_DIM + ONES_ROWS
HV_A = 2 * HEAD_DIM + ONES_ROWS
LOG2E = math.log2(math.e)
VMEM_LIMIT = 56 * 1024 * 1024


def _dot_nt(a, b):
    return lax.dot_general(a, b, (((1,), (1,)), ((), ())), preferred_element_type=F32)


def _dot(a, b):
    return jnp.dot(a, b, preferred_element_type=F32)


def _rms(x, g):
    return x * lax.rsqrt(jnp.mean(x * x, axis=-1, keepdims=True) + EPS) * g


def _sigmoid(x):
    return 1.0 / (1.0 + jnp.exp(-x))


def _store_half_masked(src_ref, dst_ref, n_pairs):
    lane = lax.broadcasted_iota(jnp.int32, (src_ref.shape[0], LANES), 1)
    for hp in range(n_pairs):
        x = src_ref[:, LANES * hp:LANES * (hp + 1)]
        z = jnp.zeros_like(x)
        dst_ref[2 * hp] = jnp.where(lane < HEAD_DIM, x, z)
        dst_ref[2 * hp + 1] = jnp.where(lane >= HEAD_DIM, x, z)


def _fold_rows(x, op):
    n_acc = 2
    accs = [x[SUBLANES * r:SUBLANES * (r + 1)] for r in range(n_acc)]
    for r in range(n_acc, x.shape[0] // SUBLANES):
        accs[r % n_acc] = op(accs[r % n_acc], x[SUBLANES * r:SUBLANES * (r + 1)])
    return op(accs[0], accs[1])


def _attn_pipelined(j0, j1, nh, hv, prep_fn, score_fn, vt_fn, sbuf_ref, m_ref, al_ref, acc_ref,
                    prep_last_fn=None):
    j0 = jnp.asarray(j0, jnp.int32)
    j1 = jnp.asarray(j1, jnp.int32)
    m_ref[...] = jnp.full(m_ref.shape, M_INIT, F32)
    acc_ref[...] = jnp.zeros(acc_ref.shape, F32)

    def produce(j, prep):
        slot = jnp.bitwise_and(j, 1)
        ctx = prep(j)
        for h in range(nh):
            s = score_fn(ctx, h)
            sbuf_ref[slot, h] = s
            m_old = m_ref[1 - slot, h:h + 1, :]
            m_new = jnp.maximum(m_old, jnp.max(s, axis=0, keepdims=True))
            m_ref[slot, h:h + 1, :] = m_new
            al_ref[slot, h:h + 1, :] = jnp.exp2(m_old - m_new)

    def consume(j):
        slot = jnp.bitwise_and(j, 1)
        for h in range(nh):
            p = jnp.exp2(sbuf_ref[slot, h] - m_ref[slot, h:h + 1, :])
            rows = slice(hv * h, hv * (h + 1))
            acc_ref[rows, :] = (al_ref[slot, h:h + 1, :] * acc_ref[rows, :]
                                + _dot(vt_fn(j, h), p.astype(BF16)))

    def body(j, carry):
        consume(j - 1)
        produce(j, prep_fn)
        return carry

    if prep_last_fn is None:
        produce(j0, prep_fn)
        lax.fori_loop(j0 + 1, j1 + 1, body, 0)
        consume(j1)
        return

    @pl.when(j1 == j0)
    def _():
        produce(j1, prep_last_fn)
        consume(j1)

    @pl.when(j1 > j0)
    def _():
        produce(j0, prep_fn)
        lax.fori_loop(j0 + 1, j1, body, 0)
        consume(j1 - 1)
        produce(j1, prep_last_fn)
        consume(j1)


def _attn_scratch(nh, hv):
    return [pltpu.VMEM((2, nh, KC, TQ), F32), pltpu.VMEM((2, nh, TQ), F32),
            pltpu.VMEM((2, nh, TQ), F32), pltpu.VMEM((nh * hv, TQ), F32)]


def _ones_rows(n_cols):
    r = lax.broadcasted_iota(jnp.int32, (ONES_ROWS, n_cols), 0)
    return jnp.where(r == 0, 1.0, 0.0).astype(BF16)


def _proj_kernel(x_ref, g_ref, wn_ref, wt_ref, rc_ref, rs1_ref, rs2_ref,
                 qa_ref, ka_ref, qb_ref, iq_ref, qc_ref, qcr_ref, kb_ref, ik_ref,
                 ks_ref, kw_ref, cmp_ref,
                 vat_ref, vbt_ref, vst_ref, vwt_ref, iwt_ref, bgt_ref):
    h = _rms(x_ref[...], g_ref[...]).astype(BF16)
    rc, rs1, rs2 = rc_ref[...], rs1_ref[...], rs2_ref[...]

    def rope(z):
        return z * rc + pltpu.roll(z, LANES - ROT_DIM // 2, 1) * rs1 + pltpu.roll(z, ROT_DIM // 2, 1) * rs2

    qscale = HEAD_DIM ** -0.5
    qscale2 = qscale * LOG2E
    segs = (
        (0, 512, ((qa_ref, True, qscale2),)),
        (512, 512, ((ka_ref, True, 1.0),)),
        (1024, 512, ((qb_ref, True, qscale2),)),
        (1536, 512, ((iq_ref, True, qscale),)),
        (2048, 512, ((qc_ref, False, qscale), (qcr_ref, True, qscale2))),
        (2560, 128, ((kb_ref, True, 1.0),)),
        (2688, 128, ((ik_ref, True, 1.0),)),
        (2816, 256, ((ks_ref, True, 1.0),)),
        (3072, 256, ((kw_ref, True, 1.0),)),
        (3328, 256, ((cmp_ref, False, 1.0),)),
    )
    for c0, width, outs in segs:
        z = _dot(h, wn_ref[:, c0:c0 + width])
        for out_ref, rot, scale in outs:
            for j in range(width // LANES):
                zj = z[:, LANES * j:LANES * (j + 1)]
                if rot:
                    zj = rope(zj)
                if scale != 1.0:
                    zj = zj * scale
                out_ref[:, LANES * j:LANES * (j + 1)] = zj.astype(out_ref.dtype)

    zt = _dot_nt(wt_ref[...], h)
    ones = _ones_rows(zt.shape[1])
    vd = 2 * HEAD_DIM
    for h in range(DA_HEADS):
        vat_ref[HV_A * h:HV_A * h + vd, :] = zt[vd * h:vd * (h + 1)].astype(BF16)
        vat_ref[HV_A * h + vd:HV_A * (h + 1), :] = ones
    vbt_ref[0:HEAD_DIM, :] = zt[512:576].astype(BF16)
    vbt_ref[HEAD_DIM:HV, :] = ones
    for g in range(NSA_GROUPS):
        for ref, base in ((vst_ref, 576), (vwt_ref, 704)):
            ref[HV * g:HV * g + HEAD_DIM, :] = zt[base + HEAD_DIM * g:base + HEAD_DIM * (g + 1)].astype(BF16)
            ref[HV * g + HEAD_DIM:HV * (g + 1), :] = ones
    iwt_ref[...] = zt[832:840] * (IDX_HEADS ** -0.5)
    bgt_ref[...] = zt[840:864]


def _proj_call(x, g, wn, wt, rc, rs1, rs2):
    B, S, D = x.shape
    tm = TM_PROJ
    nat = lambda w: pl.BlockSpec((None, tm, w), lambda b, i: (b, i, 0))
    tr = lambda r: pl.BlockSpec((None, r, tm), lambda b, i: (b, 0, i))
    full = lambda a: pl.BlockSpec(a.shape, lambda b, i: (0,) * a.ndim)
    sds = jax.ShapeDtypeStruct
    out_shape = (
        sds((B, S, 512), BF16), sds((B, S, 512), BF16), sds((B, S, 512), BF16), sds((B, S, 512), BF16),
        sds((B, S, 512), BF16), sds((B, S, 512), BF16), sds((B, S, 128), BF16), sds((B, S, 128), BF16),
        sds((B, S, 256), BF16), sds((B, S, 256), BF16), sds((B, S, 256), F32),
        sds((B, DA_HEADS * HV_A, S), BF16), sds((B, HV, S), BF16),
        sds((B, NSA_GROUPS * HV, S), BF16), sds((B, NSA_GROUPS * HV, S), BF16),
        sds((B, 8, S), F32), sds((B, 24, S), F32),
    )
    out_specs = (nat(512), nat(512), nat(512), nat(512), nat(512), nat(512), nat(128), nat(128),
                 nat(256), nat(256), nat(256),
                 tr(DA_HEADS * HV_A), tr(HV), tr(NSA_GROUPS * HV), tr(NSA_GROUPS * HV), tr(8), tr(24))
    return pl.pallas_call(
        _proj_kernel,
        grid=(B, S // tm),
        in_specs=[nat(D), full(g), full(wn), full(wt), nat(LANES), nat(LANES), nat(LANES)],
        out_specs=out_specs,
        out_shape=out_shape,
        compiler_params=pltpu.CompilerParams(
            dimension_semantics=("arbitrary", "arbitrary"), vmem_limit_bytes=VMEM_LIMIT),
        name="in_proj",
    )(x, g, wn, wt, rc, rs1, rs2)


def _compress_kernel(r_ref, pea_ref, peb_ref, w1a_ref, w1b_ref, w2_ref, nat_ref, t_ref):
    r = r_ref[...]
    y1 = _dot((r + pea_ref[...]).astype(BF16), w1a_ref[...])
    y2 = _dot((r + peb_ref[...]).astype(BF16), w1b_ref[...])
    nrow = r.shape[0]
    hid = jax.nn.gelu(y1 + pltpu.roll(y2, nrow - 1, 0))
    o = _dot(hid.astype(BF16), w2_ref[...])
    nat_ref[...] = jnp.concatenate([o, o], axis=1).astype(BF16)
    t_ref[...] = o.T.astype(BF16)


def _compress_call(r, pea, peb, w1a, w1b, w2):
    B, _, nr, kdim = r.shape
    kind = lambda b, j: (j // NSA_GROUPS, 0, 0)
    return pl.pallas_call(
        _compress_kernel,
        grid=(B, 2 * NSA_GROUPS),
        in_specs=[
            pl.BlockSpec((None, None, nr, kdim), lambda b, j: (b, j, 0, 0)),
            pl.BlockSpec((None, 1, kdim), kind),
            pl.BlockSpec((None, 1, kdim), kind),
            pl.BlockSpec((None, kdim, CMP_HIDDEN), kind),
            pl.BlockSpec((None, kdim, CMP_HIDDEN), kind),
            pl.BlockSpec((None, CMP_HIDDEN, HEAD_DIM), kind),
        ],
        out_specs=(
            pl.BlockSpec((None, None, nr, 2 * HEAD_DIM), lambda b, j: (b, j, 0, 0)),
            pl.BlockSpec((None, None, HEAD_DIM, nr), lambda b, j: (b, j, 0, 0)),
        ),
        out_shape=(jax.ShapeDtypeStruct((B, 2 * NSA_GROUPS, nr, 2 * HEAD_DIM), BF16),
                   jax.ShapeDtypeStruct((B, 2 * NSA_GROUPS, HEAD_DIM, nr), BF16)),
        compiler_params=pltpu.CompilerParams(dimension_semantics=("arbitrary", "arbitrary")),
        name="nsa_compress",
    )(r, pea, peb, w1a, w1b, w2)


def _diff_kernel(q_ref, k_ref, vt_ref, lam_ref, sg_ref, o_ref,
                 qz_ref, sbuf_ref, m_ref, al_ref, acc_ref, *, lam_init):
    i = pl.program_id(1)
    lp = lam_ref[...]
    lam = (jnp.exp(jnp.sum(lp[0:1] * lp[1:2], axis=1, keepdims=True))
           - jnp.exp(jnp.sum(lp[2:3] * lp[3:4], axis=1, keepdims=True)) + lam_init)
    row_minus_col = (lax.broadcasted_iota(jnp.int32, (KC, TQ), 0)
                     - lax.broadcasted_iota(jnp.int32, (KC, TQ), 1))
    _store_half_masked(q_ref, qz_ref, DA_HEADS)
    vdim = 2 * HEAD_DIM
    n_maps = 2 * DA_HEADS

    def prep(j):
        return pl.multiple_of(j * KC, KC), None

    def prep_diag(j):
        return pl.multiple_of(j * KC, KC), jnp.where(row_minus_col <= 0, 0.0, NEG_INF)

    def score(ctx, hm):
        off, bias = ctx
        h = hm // 2
        s = _dot_nt(k_ref[pl.ds(off, KC), vdim * h:vdim * (h + 1)], qz_ref[hm])
        return s if bias is None else s + bias

    def values(j, hm):
        h = hm // 2
        return vt_ref[HV_A * h:HV_A * (h + 1), pl.ds(pl.multiple_of(j * KC, KC), KC)]

    _attn_pipelined(0, i, n_maps, HV_A, prep, score, values, sbuf_ref, m_ref, al_ref, acc_ref,
                    prep_last_fn=prep_diag)

    for h in range(DA_HEADS):
        r0, r1 = HV_A * 2 * h, HV_A * (2 * h + 1)
        o_t = (acc_ref[r0:r0 + vdim, :] / acc_ref[r0 + vdim:r0 + vdim + 1, :]
               - lam * (acc_ref[r1:r1 + vdim, :] / acc_ref[r1 + vdim:r1 + vdim + 1, :]))
        ms = jnp.mean(o_t * o_t, axis=0, keepdims=True)
        y = o_t * lax.rsqrt(ms + EPS) * sg_ref[...] * (1.0 - lam_init)
        o_ref[:, vdim * h:vdim * (h + 1)] = y.T


def _diff_call(qa, ka, vat, lam_p, sg, lam_init):
    B, S, W = qa.shape
    n_maps = 2 * DA_HEADS
    return pl.pallas_call(
        functools.partial(_diff_kernel, lam_init=lam_init),
        grid=(B, S // TQ),
        in_specs=[
            pl.BlockSpec((None, TQ, W), lambda b, i: (b, i, 0)),
            pl.BlockSpec((None, S, W), lambda b, i: (b, 0, 0)),
            pl.BlockSpec((None, DA_HEADS * HV_A, S), lambda b, i: (b, 0, 0)),
            pl.BlockSpec(lam_p.shape, lambda b, i: (0, 0)),
            pl.BlockSpec(sg.shape, lambda b, i: (0, 0)),
        ],
        out_specs=pl.BlockSpec((None, TQ, W), lambda b, i: (b, i, 0)),
        out_shape=jax.ShapeDtypeStruct((B, S, W), F32),
        scratch_shapes=[pltpu.VMEM((n_maps, TQ, LANES), BF16)] + _attn_scratch(n_maps, HV_A),
        compiler_params=pltpu.CompilerParams(
            dimension_semantics=("arbitrary", "arbitrary"), vmem_limit_bytes=VMEM_LIMIT),
        name="diff_attn",
    )(qa, ka, vat, lam_p, sg)


def _dsa_kernel(q_ref, iq_ref, kb_ref, ik_ref, vt_ref, iw_ref, o_ref,
                sc_ref, tj_ref, tri_ref, qz_ref, sbuf_ref, m_ref, al_ref, acc_ref, *, k_sel, seq_len):
    i = pl.program_id(1)
    nch = i + 1
    row = lax.broadcasted_iota(jnp.int32, (KC, TQ), 0)
    col = lax.broadcasted_iota(jnp.int32, (KC, TQ), 1)
    causal = row <= col
    iw = iw_ref[...]
    kf = float(k_sel)
    _store_half_masked(iq_ref, qz_ref, IDX_HEADS // 2)

    def idx_chunk(j, carry, masked):
        mx, mn = carry
        off = pl.multiple_of(j * KC, KC)
        ikc = ik_ref[pl.ds(off, KC), :]
        sc = jnp.zeros((KC, TQ), F32)
        for h in range(IDX_HEADS):
            sc = sc + iw[h:h + 1, :] * jnp.maximum(_dot_nt(ikc, qz_ref[h]), 0.0)
        lo_src = sc
        if masked:
            lo_src = jnp.where(causal, sc, jnp.inf)
            sc = jnp.where(causal, sc, NEG_INF)
        sc_ref[pl.ds(off, KC), :] = sc
        return (jnp.maximum(mx, _fold_rows(sc, jnp.maximum)),
                jnp.minimum(mn, _fold_rows(lo_src, jnp.minimum)))

    carry = (jnp.full((SUBLANES, TQ), NEG_INF, F32), jnp.full((SUBLANES, TQ), jnp.inf, F32))
    carry = lax.fori_loop(0, i, functools.partial(idx_chunk, masked=False), carry)
    mx8, mn8 = idx_chunk(i, carry, True)
    row_max = jnp.max(mx8, axis=0, keepdims=True)
    row_min = jnp.min(mn8, axis=0, keepdims=True)

    def probe_pass(th, snap):
        thb = jnp.broadcast_to(th, (SUBLANES, TQ))

        def body(j, c):
            off = pl.multiple_of(j * KC, KC)
            c = list(c)
            for r in range(KC // SUBLANES):
                x = sc_ref[pl.ds(off + SUBLANES * r, SUBLANES), :]
                ge = x >= thb
                c[0] = c[0] + jnp.where(ge, 1.0, 0.0)
                if snap:
                    c[1] = jnp.maximum(c[1], jnp.where(ge, NEG_INF, x))
            return tuple(c)
        init = (jnp.zeros((SUBLANES, TQ), F32),)
        if snap:
            init += (jnp.full((SUBLANES, TQ), NEG_INF, F32),)
        out = lax.fori_loop(0, nch, body, init)
        cnt = jnp.sum(out[0], axis=0, keepdims=True)
        return (cnt, jnp.max(out[1], axis=0, keepdims=True)) if snap else cnt

    n_causal = (i * TQ + 1 + lax.broadcasted_iota(jnp.int32, (1, TQ), 1)).astype(F32)
    keep_all = n_causal <= kf

    def bisect(_, c):
        lo, hi, clo, chi = c
        mid = lo + (hi - lo) * 0.5
        cnt = probe_pass(mid, False)
        ge = cnt >= kf
        return (jnp.where(ge, mid, lo), jnp.where(ge, hi, mid),
                jnp.where(ge, cnt, clo), jnp.where(ge, chi, cnt))

    hi0 = row_max + (row_max - row_min) + 1.0
    lo, hi, clo, chi = lax.fori_loop(
        0, BISECT_PLAIN, bisect, (row_min, hi0, n_causal, jnp.zeros((1, TQ), F32)))

    def cond(st):
        return jnp.logical_and(st[0] < 128, st[-1] > 0.0)

    def snap_step(st):
        it, lo, hi, clo, chi, hie, known, stuck, done, _ = st
        top = jnp.where(known > 0.0, hie, hi)
        mid = lo + (top - lo) * 0.5
        inside = jnp.logical_and(mid > lo, mid < top)
        near_top = jnp.logical_or(kf - chi <= 2.0, stuck > 0.0)
        use_top = jnp.logical_or(jnp.logical_and(known > 0.0, near_top), jnp.logical_not(inside))
        th = jnp.where(use_top, top, mid)
        cnt, edn = probe_pass(th, True)
        live = done <= 0.0
        ge = jnp.logical_and(live, cnt >= kf)
        lt = jnp.logical_and(live, cnt < kf)
        stuck = jnp.where(jnp.logical_and(ge, cnt == clo), 1.0, 0.0)
        lo = jnp.where(ge, th, lo)
        clo = jnp.where(ge, cnt, clo)
        hi = jnp.where(lt, th, hi)
        chi = jnp.where(lt, cnt, chi)
        hie = jnp.where(lt, edn, hie)
        known = jnp.where(lt, 1.0, known)
        fin = jnp.logical_or(clo == kf, jnp.logical_and(known > 0.0, lo >= hie))
        done = jnp.where(fin, 1.0, done)
        return it + 1, lo, hi, clo, chi, hie, known, stuck, done, jnp.max(1.0 - done)

    zero = jnp.zeros((1, TQ), F32)
    done0 = jnp.where(jnp.logical_or(clo == kf, keep_all), 1.0, 0.0)
    st = lax.while_loop(cond, snap_step, (jnp.int32(0), lo, hi, clo, chi, jnp.full((1, TQ), NEG_INF, F32),
                                          zero, zero, done0, jnp.max(1.0 - done0)))
    thr, clo, chi = st[1], st[3], st[4]
    need = jnp.where(clo == kf, float(seq_len), kf - chi)
    thr = jnp.where(keep_all, NEG_INF, thr)
    need = jnp.where(keep_all, 0.0, need)
    tj_ref[2:3, :] = jnp.zeros((1, TQ), F32)

    _store_half_masked(q_ref, qz_ref, DSA_HEADS // 2)
    half = KC // 2
    tri_ref[...] = jnp.where(lax.broadcasted_iota(jnp.int32, (half, half), 0)
                             >= lax.broadcasted_iota(jnp.int32, (half, half), 1), 1.0, 0.0).astype(BF16)

    def prep(j):
        off = pl.multiple_of(j * KC, KC)
        x = sc_ref[pl.ds(off, KC), :]
        eq = x == thr
        e = jnp.where(eq, 1.0, 0.0).astype(BF16)
        r_top = _dot(tri_ref[...], e[:half]) + tj_ref[2:3, :]
        r_bot = _dot(tri_ref[...], e[half:]) + r_top[half - 1:half, :]
        tj_ref[2:3, :] = r_bot[half - 1:half, :]
        rank = jnp.concatenate([r_top, r_bot], axis=0)
        tie_bias = jnp.where(jnp.logical_and(eq, rank <= need), 0.0, NEG_INF)
        return kb_ref[pl.ds(off, KC), :], jnp.where(x > thr, 0.0, tie_bias)

    def score(ctx, h):
        kc, bias = ctx
        return _dot_nt(kc, qz_ref[h]) + bias

    def values(j, h):
        return vt_ref[:, pl.ds(pl.multiple_of(j * KC, KC), KC)]

    _attn_pipelined(0, i, DSA_HEADS, HV, prep, score, values, sbuf_ref, m_ref, al_ref, acc_ref)
    for hp in range(DSA_HEADS // 2):
        blk = jnp.concatenate(
            [acc_ref[HV * h:HV * h + HEAD_DIM, :] / acc_ref[HV * h + HEAD_DIM:HV * h + HEAD_DIM + 1, :]
             for h in (2 * hp, 2 * hp + 1)], axis=0)
        o_ref[:, LANES * hp:LANES * (hp + 1)] = blk.T


def _dsa_call(qb, iq, kb, ik, vbt, iwt, k_sel):
    B, S, W = qb.shape
    return pl.pallas_call(
        functools.partial(_dsa_kernel, k_sel=k_sel, seq_len=S),
        grid=(B, S // TQ),
        in_specs=[
            pl.BlockSpec((None, TQ, W), lambda b, i: (b, i, 0)),
            pl.BlockSpec((None, TQ, W), lambda b, i: (b, i, 0)),
            pl.BlockSpec((None, S, LANES), lambda b, i: (b, 0, 0)),
            pl.BlockSpec((None, S, LANES), lambda b, i: (b, 0, 0)),
            pl.BlockSpec((None, HV, S), lambda b, i: (b, 0, 0)),
            pl.BlockSpec((None, IDX_HEADS, TQ), lambda b, i: (b, 0, i)),
        ],
        out_specs=pl.BlockSpec((None, TQ, W), lambda b, i: (b, i, 0)),
        out_shape=jax.ShapeDtypeStruct((B, S, W), F32),
        scratch_shapes=[pltpu.VMEM((S, TQ), F32), pltpu.VMEM((SUBLANES, TQ), F32),
                        pltpu.VMEM((KC // 2, KC // 2), BF16),
                        pltpu.VMEM((DSA_HEADS, TQ, LANES), BF16)] + _attn_scratch(DSA_HEADS, HV),
        compiler_params=pltpu.CompilerParams(
            dimension_semantics=("arbitrary", "arbitrary"), vmem_limit_bytes=VMEM_LIMIT),
        name="dsa_attn",
    )(qb, iq, kb, ik, vbt, iwt)


def _nsa_kernel(q_ref, qr_ref, kc_ref, vct_ref, ovt_ref, ks_ref, vst_ref, kw_ref, vwt_ref, bg_ref,
                o_ref, cmp_ref, slc_ref, sel_ref, qz_ref, sbuf_ref, m_ref, al_ref, acc_ref, *, n_sel):
    i = pl.program_id(1)
    row_minus_col = (lax.broadcasted_iota(jnp.int32, (KC, TQ), 0)
                     - lax.broadcasted_iota(jnp.int32, (KC, TQ), 1))
    tq = i * TQ + lax.broadcasted_iota(jnp.int32, (1, TQ), 1)
    hpg = NSA_HEADS // NSA_GROUPS
    nc = kc_ref.shape[1]
    n_blk = sel_ref.shape[1]
    gw = 2 * HEAD_DIM

    _store_half_masked(q_ref, qz_ref, NSA_HEADS // 2)
    cmp_valid = CMP_STRIDE * lax.broadcasted_iota(jnp.int32, (nc, TQ), 0) + (CMP_LEN - 1) <= tq
    blk = lax.broadcasted_iota(jnp.int32, (n_blk, TQ), 0)
    forced = jnp.logical_or(blk == jnp.right_shift(tq, int(math.log2(SLC_BLOCK))), blk == 0)
    blk_causal = blk * SLC_BLOCK <= tq
    for g in range(NSA_GROUPS):
        kcg = kc_ref[g]
        vcg = vct_ref[NSA_GROUPS + g]
        imp = jnp.zeros((n_blk, TQ), F32)
        for hh in range(hpg):
            h = hpg * g + hh
            s = jnp.where(cmp_valid, _dot_nt(kcg, qz_ref[h]), NEG_INF)
            m = jnp.maximum(jnp.max(s, axis=0, keepdims=True), M_INIT)
            p = jnp.exp(s - m)
            l = jnp.sum(p, axis=0, keepdims=True)
            pc = (p * jnp.where(l > 0.0, 1.0 / l, 0.0)).astype(BF16)
            cmp_ref[HEAD_DIM * h:HEAD_DIM * (h + 1), :] = _dot(vcg, pc)
            imp = imp + _dot(ovt_ref[...], pc)
        imp = jnp.where(forced, FORCE_SCORE, imp)
        imp = jnp.where(blk_causal, imp, NEG_INF)
        rank = jnp.zeros((n_blk, TQ), F32)
        for mp in range(n_blk):
            r = imp[mp:mp + 1, :]
            ahead = jnp.logical_or(r > imp, jnp.logical_and(r == imp, blk > mp))
            rank = rank + jnp.where(ahead, 1.0, 0.0)
        sel_ref[g] = jnp.where(rank < float(n_sel), 0.0, NEG_INF)

    _store_half_masked(qr_ref, qz_ref, NSA_HEADS // 2)
    bpc = KC // SLC_BLOCK

    def slc_prep(j):
        off = pl.multiple_of(j * KC, KC)
        visible = row_minus_col <= (i - j) * KC
        biases = []
        for g in range(NSA_GROUPS):
            bias = jnp.concatenate(
                [jnp.broadcast_to(sel_ref[g, pl.ds(bpc * j + b, 1), :], (SLC_BLOCK, TQ))
                 for b in range(bpc)], axis=0)
            biases.append(jnp.where(visible, bias, NEG_INF))
        return off, biases

    def slc_score(ctx, h):
        off, biases = ctx
        g = h // hpg
        return _dot_nt(ks_ref[pl.ds(off, KC), gw * g:gw * (g + 1)], qz_ref[h]) + biases[g]

    def slc_values(j, h):
        g = h // hpg
        return vst_ref[HV * g:HV * (g + 1), pl.ds(pl.multiple_of(j * KC, KC), KC)]

    _attn_pipelined(0, i, NSA_HEADS, HV, slc_prep, slc_score, slc_values,
                    sbuf_ref, m_ref, al_ref, acc_ref)

    def head_out(h):
        return acc_ref[HV * h:HV * h + HEAD_DIM, :] / acc_ref[HV * h + HEAD_DIM:HV * h + HEAD_DIM + 1, :]

    for h in range(NSA_HEADS):
        slc_ref[HEAD_DIM * h:HEAD_DIM * (h + 1), :] = head_out(h)

    def win_prep(j):
        off = pl.multiple_of(j * KC, KC)
        d = (i - j) * KC
        inside = jnp.logical_and(row_minus_col <= d, row_minus_col > d - WINDOW)
        return off, jnp.where(inside, 0.0, NEG_INF)

    def win_score(ctx, h):
        off, bias = ctx
        g = h // hpg
        return _dot_nt(kw_ref[pl.ds(off, KC), gw * g:gw * (g + 1)], qz_ref[h]) + bias

    def win_values(j, h):
        g = h // hpg
        return vwt_ref[HV * g:HV * (g + 1), pl.ds(pl.multiple_of(j * KC, KC), KC)]

    _attn_pipelined(jnp.maximum(i - WINDOW // KC, 0), i, NSA_HEADS, HV,
                    win_prep, win_score, win_values, sbuf_ref, m_ref, al_ref, acc_ref)

    gb = _sigmoid(bg_ref[...])
    for hp in range(NSA_HEADS // 2):
        parts = []
        for h in (2 * hp, 2 * hp + 1):
            rows = slice(HEAD_DIM * h, HEAD_DIM * (h + 1))
            parts.append(gb[3 * h:3 * h + 1, :] * cmp_ref[rows, :]
                         + gb[3 * h + 1:3 * h + 2, :] * slc_ref[rows, :]
                         + gb[3 * h + 2:3 * h + 3, :] * head_out(h))
        o_ref[:, LANES * hp:LANES * (hp + 1)] = jnp.concatenate(parts, axis=0).T


def _nsa_call(qc, qcr, kcc, vct, ovt, ks, vst, kw, vwt, bgt, n_sel):
    B, S, W = qc.shape
    n_blk = S // SLC_BLOCK
    qspec = pl.BlockSpec((None, TQ, W), lambda b, i: (b, i, 0))
    return pl.pallas_call(
        functools.partial(_nsa_kernel, n_sel=n_sel),
        grid=(B, S // TQ),
        in_specs=[
            qspec, qspec,
            pl.BlockSpec((None,) + kcc.shape[1:], lambda b, i: (b, 0, 0, 0)),
            pl.BlockSpec((None,) + vct.shape[1:], lambda b, i: (b, 0, 0, 0)),
            pl.BlockSpec(ovt.shape, lambda b, i: (0, 0)),
            pl.BlockSpec((None, S, 2 * LANES), lambda b, i: (b, 0, 0)),
            pl.BlockSpec((None, NSA_GROUPS * HV, S), lambda b, i: (b, 0, 0)),
            pl.BlockSpec((None, S, 2 * LANES), lambda b, i: (b, 0, 0)),
            pl.BlockSpec((None, NSA_GROUPS * HV, S), lambda b, i: (b, 0, 0)),
            pl.BlockSpec((None, 3 * NSA_HEADS, TQ), lambda b, i: (b, 0, i)),
        ],
        out_specs=qspec,
        out_shape=jax.ShapeDtypeStruct((B, S, W), F32),
        scratch_shapes=[pltpu.VMEM((W, TQ), F32), pltpu.VMEM((W, TQ), F32),
                        pltpu.VMEM((NSA_GROUPS, n_blk, TQ), F32),
                        pltpu.VMEM((NSA_HEADS, TQ, LANES), BF16)] + _attn_scratch(NSA_HEADS, HV),
        compiler_params=pltpu.CompilerParams(
            dimension_semantics=("arbitrary", "arbitrary"), vmem_limit_bytes=VMEM_LIMIT),
        name="nsa_attn",
    )(qc, qcr, kcc, vct, ovt, ks, vst, kw, vwt, bgt)


def _post_kernel(x_ref, oa_ref, ob_ref, oc_ref, p_ref, g_ref, wg_ref, wup_ref, wout_ref,
                 pg_ref, wple_ref, wpg_ref, fg_ref, o_ref, *, final):
    x = x_ref[...]
    h = _rms(x, g_ref[...]).astype(BF16)
    mixed = jnp.zeros(x.shape, F32)
    for n, br_ref in enumerate((oa_ref, ob_ref, oc_ref)):
        gate = _dot(h, wg_ref[:, BRANCH_WIDTH * n:BRANCH_WIDTH * (n + 1)])
        og = br_ref[...] * (gate * _sigmoid(gate))
        up = _dot(og.astype(BF16), wup_ref[n])
        c0 = N_BRANCH * BRANCH_WIDTH + D_MODEL * n
        mixed = mixed + _sigmoid(_dot(h, wg_ref[:, c0:c0 + D_MODEL])) * up
    x1 = x + _dot(mixed.astype(BF16), wout_ref[...])
    emb = _dot(p_ref[...].astype(BF16), wple_ref[...])
    hg = _dot(_rms(x1, pg_ref[...]).astype(BF16), wpg_ref[...])
    x2 = x1 + emb * _sigmoid(hg)
    if final:
        x2 = _rms(x2, fg_ref[...])
    o_ref[...] = x2


def _post_call(x, oa, ob, oc, p, g, wg, wup, wout, pg, wple, wpg, fg, final):
    T, D = x.shape
    tm = TM_POST
    tok = lambda w: pl.BlockSpec((tm, w), lambda i: (i, 0))
    full = lambda a: pl.BlockSpec(a.shape, lambda i: (0,) * a.ndim)
    return pl.pallas_call(
        functools.partial(_post_kernel, final=final),
        grid=(T // tm,),
        in_specs=[tok(D), tok(BRANCH_WIDTH), tok(BRANCH_WIDTH), tok(BRANCH_WIDTH), tok(PLE_DIM),
                  full(g), full(wg), full(wup), full(wout), full(pg), full(wple), full(wpg), full(fg)],
        out_specs=tok(D),
        out_shape=jax.ShapeDtypeStruct((T, D), F32),
        compiler_params=pltpu.CompilerParams(
            dimension_semantics=("arbitrary",), vmem_limit_bytes=VMEM_LIMIT),
        name="post_mix",
    )(x, oa, ob, oc, p, g, wg, wup, wout, pg, wple, wpg, fg)


def _rope_tables(positions):
    inv = ROPE_THETA ** (-jnp.arange(0, ROT_DIM, 2, dtype=F32) / ROT_DIM)
    ang = positions.astype(F32)[..., None] * inv
    cos, sin = jnp.cos(ang), jnp.sin(ang)
    half = ROT_DIM // 2
    pad = HEAD_DIM - ROT_DIM
    zeros = jnp.zeros(cos.shape[:-1] + (half,), F32)
    rc = jnp.concatenate([cos, cos, jnp.ones(cos.shape[:-1] + (pad,), F32)], axis=-1)
    rs1 = jnp.concatenate([-sin, zeros, jnp.zeros(cos.shape[:-1] + (pad,), F32)], axis=-1)
    rs2 = jnp.concatenate([zeros, sin, jnp.zeros(cos.shape[:-1] + (pad,), F32)], axis=-1)
    rep = lambda t: jnp.concatenate([t] * (LANES // HEAD_DIM), axis=-1)
    return rep(rc), rep(rs1), rep(rs2)


def _proj_weights(w):
    sl = lambda o, n: w[:, o:o + n]
    dup = lambda o: jnp.concatenate([sl(o, HEAD_DIM)] * 2, axis=1)
    nat = jnp.concatenate([
        sl(_A_Q, 512), sl(_A_K, 512), sl(_B_Q, 512), sl(_I_Q, 512), sl(_C_Q, 512),
        dup(_B_K), dup(_I_K),
        dup(_C_KS), dup(_C_KS + HEAD_DIM), dup(_C_KW), dup(_C_KW + HEAD_DIM),
        sl(_C_KC, 128), sl(_C_VC, 128)], axis=1)
    tr = jnp.concatenate([
        sl(_A_V, 512), sl(_B_V, 64), sl(_C_VS, 128), sl(_C_VW, 128), sl(_I_W, 8), sl(_C_BG, 24)], axis=1)
    gates = jnp.concatenate([sl(_A_G, 512), sl(_B_G, 512), sl(_C_G, 512), sl(_MERGE, 3 * D_MODEL)], axis=1)
    return nat.astype(BF16), tr.T.astype(BF16), gates.astype(BF16)


def _overlap_t(S):
    nc = S // CMP_STRIDE
    n_blk = S // SLC_BLOCK
    cstart = np.arange(nc) * CMP_STRIDE
    sstart = np.arange(n_blk) * SLC_BLOCK
    ov = ((cstart[None, :] < sstart[:, None] + SLC_BLOCK) & (cstart[None, :] + CMP_LEN > sstart[:, None]))
    return jnp.asarray(ov.astype(np.float32), dtype=BF16)


def kernel(x, p, positions, norm_g, w_in, diff_lambda, diff_subln_g, cmp_pe_k, cmp_w1_k, cmp_w2_k,
           cmp_pe_v, cmp_w1_v, cmp_w2_v, w_up, w_out, ple_norm_g, w_ple, w_ple_gate, final_norm_g):
    B, S, D = x.shape
    depth = w_in.shape[0]
    k_sel = min(DSA_TOPK_MAX, S // 4)
    n_sel = min(SLC_TOPN_MAX, S // SLC_BLOCK)
    assert S % TM_PROJ == 0 and S % TQ == 0 and k_sel <= TQ and D == D_MODEL
    assert KC == TQ and WINDOW % KC == 0
    rc, rs1, rs2 = _rope_tables(positions)
    ovt = _overlap_t(S)
    half_len = CMP_LEN // 2
    kdim = half_len * HEAD_DIM
    row2 = lambda v: v.reshape(1, -1)

    for i in range(depth):
        lam_init = 0.8 - 0.6 * math.exp(-0.3 * i)
        wn, wt, wg = _proj_weights(w_in[i])
        (qa, ka, qb, iq, qc, qcr, kb, ik, ks, kw, cmp_raw,
         vat, vbt, vst, vwt, iwt, bgt) = _proj_call(x, row2(norm_g[i]), wn, wt, rc, rs1, rs2)

        r = cmp_raw.reshape(B, S // half_len, half_len, 2 * NSA_GROUPS, HEAD_DIM)
        r = r.transpose(0, 3, 1, 2, 4).reshape(B, 2 * NSA_GROUPS, S // half_len, kdim)
        pe = jnp.stack([cmp_pe_k[i], cmp_pe_v[i]])
        w1 = jnp.stack([cmp_w1_k[i], cmp_w1_v[i]])
        w2 = jnp.stack([cmp_w2_k[i], cmp_w2_v[i]]).astype(BF16)
        kcc, vct = _compress_call(
            r, pe[:, :half_len].reshape(2, 1, kdim), pe[:, half_len:].reshape(2, 1, kdim),
            w1[:, :half_len].reshape(2, kdim, CMP_HIDDEN).astype(BF16),
            w1[:, half_len:].reshape(2, kdim, CMP_HIDDEN).astype(BF16), w2)

        oa = _diff_call(qa, ka, vat, diff_lambda[i], diff_subln_g[i].reshape(-1, 1), lam_init)
        ob = _dsa_call(qb, iq, kb, ik, vbt, iwt, k_sel)
        oc = _nsa_call(qc, qcr, kcc, vct, ovt, ks, vst, kw, vwt, bgt, n_sel)

        x = _post_call(
            x.reshape(B * S, D), oa.reshape(B * S, -1), ob.reshape(B * S, -1), oc.reshape(B * S, -1),
            p[i].reshape(B * S, -1), row2(norm_g[i]), wg, w_up[i].astype(BF16), w_out[i].astype(BF16),
            row2(ple_norm_g[i]), w_ple[i].astype(BF16), w_ple_gate[i].astype(BF16), row2(final_norm_g),
            final=(i == depth - 1)).reshape(B, S, D)
    return x
```

```python
import functools
import math

import numpy as np
import jax
import jax.numpy as jnp
from jax import lax
from jax.experimental import pallas as pl
from jax.experimental.pallas import tpu as pltpu

F32 = jnp.float32
BF16 = jnp.bfloat16

D_MODEL = 1024
HEAD_DIM = 64
ROT_DIM = 16
ROPE_THETA = 500000.0
EPS = 1e-6
PLE_DIM = 256
DA_HEADS = 4
DSA_HEADS = 8
IDX_HEADS = 8
DSA_TOPK_MAX = 256
NSA_GROUPS = 2
NSA_HEADS = 8
CMP_LEN = 32
CMP_STRIDE = 16
CMP_HIDDEN = 128
SLC_BLOCK = 64
SLC_TOPN_MAX = 16
WINDOW = 512
FORCE_SCORE = 1e4
BRANCH_WIDTH = 512
N_BRANCH = 3

IN_SPLITS = (512, 512, 512, 512, 512, 64, 64, 512, 512, 64, 8,
             512, 128, 128, 128, 128, 128, 128, 512, 24, 3 * D_MODEL)
(_A_Q, _A_K, _A_V, _A_G, _B_Q, _B_K, _B_V, _B_G, _I_Q, _I_K, _I_W,
 _C_Q, _C_KC, _C_VC, _C_KS, _C_VS, _C_KW, _C_VW, _C_G, _C_BG, _MERGE) = [
    int(v) for v in np.concatenate([[0], np.cumsum(IN_SPLITS)[:-1]])]

LANES = 128
SUBLANES = 8
TQ = 512
KC = 512
TM_PROJ = 512
TM_POST = 512
NEG_INF = float("-inf")
M_INIT = -1e30
N_NAT = 3584
N_TR = 864
BISECT_PLAIN = 10
ONES_ROWS = 16
HV = HEAD_DIM + ONES_ROWS
HV_A = 2 * HEAD_DIM + ONES_ROWS
LOG2E = math.log2(math.e)
VMEM_LIMIT = 56 * 1024 * 1024


def _dot_nt(a, b):
    return lax.dot_general(a, b, (((1,), (1,)), ((), ())), preferred_element_type=F32)


def _dot(a, b):
    return jnp.dot(a, b, preferred_element_type=F32)


def _rms(x, g):
    return x * lax.rsqrt(jnp.mean(x * x, axis=-1, keepdims=True) + EPS) * g


def _sigmoid(x):
    return 1.0 / (1.0 + jnp.exp(-x))


def _store_half_masked(src_ref, dst_ref, n_pairs):
    lane = lax.broadcasted_iota(jnp.int32, (src_ref.shape[0], LANES), 1)
    for hp in range(n_pairs):
        x = src_ref[:, LANES * hp:LANES * (hp + 1)]
        z = jnp.zeros_like(x)
        dst_ref[2 * hp] = jnp.where(lane < HEAD_DIM, x, z)
        dst_ref[2 * hp + 1] = jnp.where(lane >= HEAD_DIM, x, z)


def _fold_rows(x, op):
    n_acc = 2
    accs = [x[SUBLANES * r:SUBLANES * (r + 1)] for r in range(n_acc)]
    for r in range(n_acc, x.shape[0] // SUBLANES):
        accs[r % n_acc] = op(accs[r % n_acc], x[SUBLANES * r:SUBLANES * (r + 1)])
    return op(accs[0], accs[1])


def _attn_pipelined(j0, j1, nh, hv, prep_fn, score_fn, vt_fn, sbuf_ref, m_ref, al_ref, acc_ref,
                    prep_last_fn=None):
    j0 = jnp.asarray(j0, jnp.int32)
    j1 = jnp.asarray(j1, jnp.int32)
    m_ref[...] = jnp.full(m_ref.shape, M_INIT, F32)
    acc_ref[...] = jnp.zeros(acc_ref.shape, F32)

    def produce(j, prep):
        slot = jnp.bitwise_and(j, 1)
        ctx = prep(j)
        for h in range(nh):
            s = score_fn(ctx, h)
            sbuf_ref[slot, h] = s
            m_old = m_ref[1 - slot, h:h + 1, :]
            m_new = jnp.maximum(m_old, jnp.max(s, axis=0, keepdims=True))
            m_ref[slot, h:h + 1, :] = m_new
            al_ref[slot, h:h + 1, :] = jnp.exp2(m_old - m_new)

    def consume(j):
        slot = jnp.bitwise_and(j, 1)
        for h in range(nh):
            p = jnp.exp2(sbuf_ref[slot, h] - m_ref[slot, h:h + 1, :])
            rows = slice(hv * h, hv * (h + 1))
            acc_ref[rows, :] = (al_ref[slot, h:h + 1, :] * acc_ref[rows, :]
                                + _dot(vt_fn(j, h), p.astype(BF16)))

    def body(j, carry):
        consume(j - 1)
        produce(j, prep_fn)
        return carry

    if prep_last_fn is None:
        produce(j0, prep_fn)
        lax.fori_loop(j0 + 1, j1 + 1, body, 0)
        consume(j1)
        return

    @pl.when(j1 == j0)
    def _():
        produce(j1, prep_last_fn)
        consume(j1)

    @pl.when(j1 > j0)
    def _():
        produce(j0, prep_fn)
        lax.fori_loop(j0 + 1, j1, body, 0)
        consume(j1 - 1)
        produce(j1, prep_last_fn)
        consume(j1)


def _attn_scratch(nh, hv):
    return [pltpu.VMEM((2, nh, KC, TQ), F32), pltpu.VMEM((2, nh, TQ), F32),
            pltpu.VMEM((2, nh, TQ), F32), pltpu.VMEM((nh * hv, TQ), F32)]


def _ones_rows(n_cols):
    r = lax.broadcasted_iota(jnp.int32, (ONES_ROWS, n_cols), 0)
    return jnp.where(r == 0, 1.0, 0.0).astype(BF16)


def _proj_kernel(x_ref, g_ref, wn_ref, wt_ref, rc_ref, rs1_ref, rs2_ref,
                 qa_ref, ka_ref, qb_ref, iq_ref, qc_ref, qcr_ref, kb_ref, ik_ref,
                 ks_ref, kw_ref, ck_ref, cv_ref,
                 vat_ref, vbt_ref, vst_ref, vwt_ref, iwt_ref, bgt_ref):
    h = _rms(x_ref[...], g_ref[...]).astype(BF16)
    rc, rs1, rs2 = rc_ref[...], rs1_ref[...], rs2_ref[...]

    def rope(z):
        return z * rc + pltpu.roll(z, LANES - ROT_DIM // 2, 1) * rs1 + pltpu.roll(z, ROT_DIM // 2, 1) * rs2

    qscale = HEAD_DIM ** -0.5
    qscale2 = qscale * LOG2E
    segs = (
        (0, 512, ((qa_ref, True, qscale2),)),
        (512, 512, ((ka_ref, True, 1.0),)),
        (1024, 512, ((qb_ref, True, qscale2),)),
        (1536, 512, ((iq_ref, True, qscale),)),
        (2048, 512, ((qc_ref, False, qscale), (qcr_ref, True, qscale2))),
        (2560, 128, ((kb_ref, True, 1.0),)),
        (2688, 128, ((ik_ref, True, 1.0),)),
        (2816, 256, ((ks_ref, True, 1.0),)),
        (3072, 256, ((kw_ref, True, 1.0),)),
    )
    zc = _dot(h, wn_ref[:, 3328:3328 + 2 * LANES])
    ck_ref[...] = zc[:, 0:LANES]
    cv_ref[...] = zc[:, LANES:2 * LANES]
    for c0, width, outs in segs:
        z = _dot(h, wn_ref[:, c0:c0 + width])
        for out_ref, rot, scale in outs:
            for j in range(width // LANES):
                zj = z[:, LANES * j:LANES * (j + 1)]
                if rot:
                    zj = rope(zj)
                if scale != 1.0:
                    zj = zj * scale
                out_ref[:, LANES * j:LANES * (j + 1)] = zj.astype(out_ref.dtype)

    zt = _dot_nt(wt_ref[...], h)
    ones = _ones_rows(zt.shape[1])
    vd = 2 * HEAD_DIM
    for h in range(DA_HEADS):
        vat_ref[HV_A * h:HV_A * h + vd, :] = zt[vd * h:vd * (h + 1)].astype(BF16)
        vat_ref[HV_A * h + vd:HV_A * (h + 1), :] = ones
    vbt_ref[0:HEAD_DIM, :] = zt[512:576].astype(BF16)
    vbt_ref[HEAD_DIM:HV, :] = ones
    for g in range(NSA_GROUPS):
        for ref, base in ((vst_ref, 576), (vwt_ref, 704)):
            ref[HV * g:HV * g + HEAD_DIM, :] = zt[base + HEAD_DIM * g:base + HEAD_DIM * (g + 1)].astype(BF16)
            ref[HV * g + HEAD_DIM:HV * (g + 1), :] = ones
    iwt_ref[...] = zt[832:840] * (IDX_HEADS ** -0.5)
    bgt_ref[...] = zt[840:864]


def _proj_call(x, g, wn, wt, rc, rs1, rs2):
    B, S, D = x.shape
    tm = TM_PROJ
    nat = lambda w: pl.BlockSpec((None, tm, w), lambda b, i: (b, i, 0))
    tr = lambda r: pl.BlockSpec((None, r, tm), lambda b, i: (b, 0, i))
    full = lambda a: pl.BlockSpec(a.shape, lambda b, i: (0,) * a.ndim)
    sds = jax.ShapeDtypeStruct
    out_shape = (
        sds((B, S, 512), BF16), sds((B, S, 512), BF16), sds((B, S, 512), BF16), sds((B, S, 512), BF16),
        sds((B, S, 512), BF16), sds((B, S, 512), BF16), sds((B, S, 128), BF16), sds((B, S, 128), BF16),
        sds((B, S, 256), BF16), sds((B, S, 256), BF16),
        sds((B, S, LANES), F32), sds((B, S, LANES), F32),
        sds((B, DA_HEADS * HV_A, S), BF16), sds((B, HV, S), BF16),
        sds((B, NSA_GROUPS * HV, S), BF16), sds((B, NSA_GROUPS * HV, S), BF16),
        sds((B, 8, S), F32), sds((B, 24, S), F32),
    )
    out_specs = (nat(512), nat(512), nat(512), nat(512), nat(512), nat(512), nat(128), nat(128),
                 nat(256), nat(256), nat(LANES), nat(LANES),
                 tr(DA_HEADS * HV_A), tr(HV), tr(NSA_GROUPS * HV), tr(NSA_GROUPS * HV), tr(8), tr(24))
    return pl.pallas_call(
        _proj_kernel,
        grid=(B, S // tm),
        in_specs=[nat(D), full(g), full(wn), full(wt), nat(LANES), nat(LANES), nat(LANES)],
        out_specs=out_specs,
        out_shape=out_shape,
        compiler_params=pltpu.CompilerParams(
            dimension_semantics=("arbitrary", "arbitrary"), vmem_limit_bytes=VMEM_LIMIT),
        name="in_proj",
    )(x, g, wn, wt, rc, rs1, rs2)


def _compress_kernel(rk_ref, rv_ref, pea_ref, peb_ref, w1a_ref, w1b_ref, w2_ref, nat_ref, t_ref):
    for kind, r_ref in enumerate((rk_ref, rv_ref)):
        r = r_ref[...]
        xa = (r + pea_ref[kind]).astype(BF16)
        xb = (r + peb_ref[kind]).astype(BF16)
        nrow = r.shape[0]
        for g in range(NSA_GROUPS):
            n = NSA_GROUPS * kind + g
            hid = jax.nn.gelu(_dot(xa, w1a_ref[n]) + pltpu.roll(_dot(xb, w1b_ref[n]), nrow - 1, 0))
            o = _dot(hid.astype(BF16), w2_ref[kind])
            nat_ref[n] = jnp.concatenate([o, o], axis=1).astype(BF16)
            t_ref[n] = o.T.astype(BF16)


def _compress_call(rk, rv, pea, peb, w1a, w1b, w2):
    B, nr, kdim = rk.shape
    full = lambda a: pl.BlockSpec(a.shape, lambda b: (0,) * a.ndim)
    rspec = pl.BlockSpec((None, nr, kdim), lambda b: (b, 0, 0))
    return pl.pallas_call(
        _compress_kernel,
        grid=(B,),
        in_specs=[rspec, rspec, full(pea), full(peb), full(w1a), full(w1b), full(w2)],
        out_specs=(
            pl.BlockSpec((None, 2 * NSA_GROUPS, nr, 2 * HEAD_DIM), lambda b: (b, 0, 0, 0)),
            pl.BlockSpec((None, 2 * NSA_GROUPS, HEAD_DIM, nr), lambda b: (b, 0, 0, 0)),
        ),
        out_shape=(jax.ShapeDtypeStruct((B, 2 * NSA_GROUPS, nr, 2 * HEAD_DIM), BF16),
                   jax.ShapeDtypeStruct((B, 2 * NSA_GROUPS, HEAD_DIM, nr), BF16)),
        compiler_params=pltpu.CompilerParams(dimension_semantics=("arbitrary",)),
        name="nsa_compress",
    )(rk, rv, pea, peb, w1a, w1b, w2)


def _diff_kernel(q_ref, k_ref, vt_ref, lam_ref, sg_ref, o_ref,
                 qz_ref, sbuf_ref, m_ref, al_ref, acc_ref, *, lam_init):
    i = pl.program_id(1)
    lp = lam_ref[...]
    lam = (jnp.exp(jnp.sum(lp[0:1] * lp[1:2], axis=1, keepdims=True))
           - jnp.exp(jnp.sum(lp[2:3] * lp[3:4], axis=1, keepdims=True)) + lam_init)
    row_minus_col = (lax.broadcasted_iota(jnp.int32, (KC, TQ), 0)
                     - lax.broadcasted_iota(jnp.int32, (KC, TQ), 1))
    _store_half_masked(q_ref, qz_ref, DA_HEADS)
    vdim = 2 * HEAD_DIM
    n_maps = 2 * DA_HEADS

    def prep(j):
        return pl.multiple_of(j * KC, KC), None

    def prep_diag(j):
        return pl.multiple_of(j * KC, KC), jnp.where(row_minus_col <= 0, 0.0, NEG_INF)

    def score(ctx, hm):
        off, bias = ctx
        h = hm // 2
        s = _dot_nt(k_ref[pl.ds(off, KC), vdim * h:vdim * (h + 1)], qz_ref[hm])
        return s if bias is None else s + bias

    def values(j, hm):
        h = hm // 2
        return vt_ref[HV_A * h:HV_A * (h + 1), pl.ds(pl.multiple_of(j * KC, KC), KC)]

    _attn_pipelined(0, i, n_maps, HV_A, prep, score, values, sbuf_ref, m_ref, al_ref, acc_ref,
                    prep_last_fn=prep_diag)

    for h in range(DA_HEADS):
        r0, r1 = HV_A * 2 * h, HV_A * (2 * h + 1)
        o_t = (acc_ref[r0:r0 + vdim, :] / acc_ref[r0 + vdim:r0 + vdim + 1, :]
               - lam * (acc_ref[r1:r1 + vdim, :] / acc_ref[r1 + vdim:r1 + vdim + 1, :]))
        ms = jnp.mean(o_t * o_t, axis=0, keepdims=True)
        y = o_t * lax.rsqrt(ms + EPS) * sg_ref[...] * (1.0 - lam_init)
        o_ref[:, vdim * h:vdim * (h + 1)] = y.T


def _diff_call(qa, ka, vat, lam_p, sg, lam_init):
    B, S, W = qa.shape
    n_maps = 2 * DA_HEADS
    return pl.pallas_call(
        functools.partial(_diff_kernel, lam_init=lam_init),
        grid=(B, S // TQ),
        in_specs=[
            pl.BlockSpec((None, TQ, W), lambda b, i: (b, i, 0)),
            pl.BlockSpec((None, S, W), lambda b, i: (b, 0, 0)),
            pl.BlockSpec((None, DA_HEADS * HV_A, S), lambda b, i: (b, 0, 0)),
            pl.BlockSpec(lam_p.shape, lambda b, i: (0, 0)),
            pl.BlockSpec(sg.shape, lambda b, i: (0, 0)),
        ],
        out_specs=pl.BlockSpec((None, TQ, W), lambda b, i: (b, i, 0)),
        out_shape=jax.ShapeDtypeStruct((B, S, W), F32),
        scratch_shapes=[pltpu.VMEM((n_maps, TQ, LANES), BF16)] + _attn_scratch(n_maps, HV_A),
        compiler_params=pltpu.CompilerParams(
            dimension_semantics=("arbitrary", "arbitrary"), vmem_limit_bytes=VMEM_LIMIT),
        name="diff_attn",
    )(qa, ka, vat, lam_p, sg)


def _dsa_kernel(q_ref, iq_ref, kb_ref, ik_ref, vt_ref, iw_ref, o_ref,
                sc_ref, tj_ref, tri_ref, qz_ref, sbuf_ref, m_ref, al_ref, acc_ref, *, k_sel, seq_len):
    i = pl.program_id(1)
    nch = i + 1
    row = lax.broadcasted_iota(jnp.int32, (KC, TQ), 0)
    col = lax.broadcasted_iota(jnp.int32, (KC, TQ), 1)
    causal = row <= col
    iw = iw_ref[...]
    kf = float(k_sel)
    _store_half_masked(iq_ref, qz_ref, IDX_HEADS // 2)

    def idx_chunk(j, carry, masked):
        mx, mn = carry
        off = pl.multiple_of(j * KC, KC)
        ikc = ik_ref[pl.ds(off, KC), :]
        sc = jnp.zeros((KC, TQ), F32)
        for h in range(IDX_HEADS):
            sc = sc + iw[h:h + 1, :] * jnp.maximum(_dot_nt(ikc, qz_ref[h]), 0.0)
        lo_src = sc
        if masked:
            lo_src = jnp.where(causal, sc, jnp.inf)
            sc = jnp.where(causal, sc, NEG_INF)
        sc_ref[pl.ds(off, KC), :] = sc
        return (jnp.maximum(mx, _fold_rows(sc, jnp.maximum)),
                jnp.minimum(mn, _fold_rows(lo_src, jnp.minimum)))

    carry = (jnp.full((SUBLANES, TQ), NEG_INF, F32), jnp.full((SUBLANES, TQ), jnp.inf, F32))
    carry = lax.fori_loop(0, i, functools.partial(idx_chunk, masked=False), carry)
    mx8, mn8 = idx_chunk(i, carry, True)
    row_max = jnp.max(mx8, axis=0, keepdims=True)
    row_min = jnp.min(mn8, axis=0, keepdims=True)

    def probe_pass(th, snap):
        thb = jnp.broadcast_to(th, (SUBLANES, TQ))

        def body(j, c):
            off = pl.multiple_of(j * KC, KC)
            c = list(c)
            for r in range(KC // SUBLANES):
                x = sc_ref[pl.ds(off + SUBLANES * r, SUBLANES), :]
                ge = x >= thb
                c[0] = c[0] + jnp.where(ge, 1.0, 0.0)
                if snap:
                    c[1] = jnp.maximum(c[1], jnp.where(ge, NEG_INF, x))
            return tuple(c)
        init = (jnp.zeros((SUBLANES, TQ), F32),)
        if snap:
            init += (jnp.full((SUBLANES, TQ), NEG_INF, F32),)
        out = lax.fori_loop(0, nch, body, init)
        cnt = jnp.sum(out[0], axis=0, keepdims=True)
        return (cnt, jnp.max(out[1], axis=0, keepdims=True)) if snap else cnt

    n_causal = (i * TQ + 1 + lax.broadcasted_iota(jnp.int32, (1, TQ), 1)).astype(F32)
    keep_all = n_causal <= kf

    def bisect(_, c):
        lo, hi, clo, chi = c
        mid = lo + (hi - lo) * 0.5
        cnt = probe_pass(mid, False)
        ge = cnt >= kf
        return (jnp.where(ge, mid, lo), jnp.where(ge, hi, mid),
                jnp.where(ge, cnt, clo), jnp.where(ge, chi, cnt))

    hi0 = row_max + (row_max - row_min) + 1.0
    lo, hi, clo, chi = lax.fori_loop(
        0, BISECT_PLAIN, bisect, (row_min, hi0, n_causal, jnp.zeros((1, TQ), F32)))

    def cond(st):
        return jnp.logical_and(st[0] < 128, st[-1] > 0.0)

    def snap_step(st):
        it, lo, hi, clo, chi, hie, known, stuck, done, _ = st
        top = jnp.where(known > 0.0, hie, hi)
        mid = lo + (top - lo) * 0.5
        inside = jnp.logical_and(mid > lo, mid < top)
        near_top = jnp.logical_or(kf - chi <= 2.0, stuck > 0.0)
        use_top = jnp.logical_or(jnp.logical_and(known > 0.0, near_top), jnp.logical_not(inside))
        th = jnp.where(use_top, top, mid)
        cnt, edn = probe_pass(th, True)
        live = done <= 0.0
        ge = jnp.logical_and(live, cnt >= kf)
        lt = jnp.logical_and(live, cnt < kf)
        stuck = jnp.where(jnp.logical_and(ge, cnt == clo), 1.0, 0.0)
        lo = jnp.where(ge, th, lo)
        clo = jnp.where(ge, cnt, clo)
        hi = jnp.where(lt, th, hi)
        chi = jnp.where(lt, cnt, chi)
        hie = jnp.where(lt, edn, hie)
        known = jnp.where(lt, 1.0, known)
        fin = jnp.logical_or(clo == kf, jnp.logical_and(known > 0.0, lo >= hie))
        done = jnp.where(fin, 1.0, done)
        return it + 1, lo, hi, clo, chi, hie, known, stuck, done, jnp.max(1.0 - done)

    zero = jnp.zeros((1, TQ), F32)
    done0 = jnp.where(jnp.logical_or(clo == kf, keep_all), 1.0, 0.0)
    st = lax.while_loop(cond, snap_step, (jnp.int32(0), lo, hi, clo, chi, jnp.full((1, TQ), NEG_INF, F32),
                                          zero, zero, done0, jnp.max(1.0 - done0)))
    thr, clo, chi = st[1], st[3], st[4]
    need = jnp.where(clo == kf, float(seq_len), kf - chi)
    thr = jnp.where(keep_all, NEG_INF, thr)
    need = jnp.where(keep_all, 0.0, need)
    tj_ref[2:3, :] = jnp.zeros((1, TQ), F32)

    _store_half_masked(q_ref, qz_ref, DSA_HEADS // 2)
    half = KC // 2
    tri_ref[...] = jnp.where(lax.broadcasted_iota(jnp.int32, (half, half), 0)
                             >= lax.broadcasted_iota(jnp.int32, (half, half), 1), 1.0, 0.0).astype(BF16)

    def prep(j):
        off = pl.multiple_of(j * KC, KC)
        x = sc_ref[pl.ds(off, KC), :]
        eq = x == thr
        e = jnp.where(eq, 1.0, 0.0).astype(BF16)
        r_top = _dot(tri_ref[...], e[:half]) + tj_ref[2:3, :]
        r_bot = _dot(tri_ref[...], e[half:]) + r_top[half - 1:half, :]
        tj_ref[2:3, :] = r_bot[half - 1:half, :]
        rank = jnp.concatenate([r_top, r_bot], axis=0)
        tie_bias = jnp.where(jnp.logical_and(eq, rank <= need), 0.0, NEG_INF)
        return kb_ref[pl.ds(off, KC), :], jnp.where(x > thr, 0.0, tie_bias)

    def score(ctx, h):
        kc, bias = ctx
        return _dot_nt(kc, qz_ref[h]) + bias

    def values(j, h):
        return vt_ref[:, pl.ds(pl.multiple_of(j * KC, KC), KC)]

    _attn_pipelined(0, i, DSA_HEADS, HV, prep, score, values, sbuf_ref, m_ref, al_ref, acc_ref)
    for hp in range(DSA_HEADS // 2):
        blk = jnp.concatenate(
            [acc_ref[HV * h:HV * h + HEAD_DIM, :] / acc_ref[HV * h + HEAD_DIM:HV * h + HEAD_DIM + 1, :]
             for h in (2 * hp, 2 * hp + 1)], axis=0)
        o_ref[:, LANES * hp:LANES * (hp + 1)] = blk.T


def _dsa_call(qb, iq, kb, ik, vbt, iwt, k_sel):
    B, S, W = qb.shape
    return pl.pallas_call(
        functools.partial(_dsa_kernel, k_sel=k_sel, seq_len=S),
        grid=(B, S // TQ),
        in_specs=[
            pl.BlockSpec((None, TQ, W), lambda b, i: (b, i, 0)),
            pl.BlockSpec((None, TQ, W), lambda b, i: (b, i, 0)),
            pl.BlockSpec((None, S, LANES), lambda b, i: (b, 0, 0)),
            pl.BlockSpec((None, S, LANES), lambda b, i: (b, 0, 0)),
            pl.BlockSpec((None, HV, S), lambda b, i: (b, 0, 0)),
            pl.BlockSpec((None, IDX_HEADS, TQ), lambda b, i: (b, 0, i)),
        ],
        out_specs=pl.BlockSpec((None, TQ, W), lambda b, i: (b, i, 0)),
        out_shape=jax.ShapeDtypeStruct((B, S, W), F32),
        scratch_shapes=[pltpu.VMEM((S, TQ), F32), pltpu.VMEM((SUBLANES, TQ), F32),
                        pltpu.VMEM((KC // 2, KC // 2), BF16),
                        pltpu.VMEM((DSA_HEADS, TQ, LANES), BF16)] + _attn_scratch(DSA_HEADS, HV),
        compiler_params=pltpu.CompilerParams(
            dimension_semantics=("arbitrary", "arbitrary"), vmem_limit_bytes=VMEM_LIMIT),
        name="dsa_attn",
    )(qb, iq, kb, ik, vbt, iwt)


def _nsa_kernel(q_ref, qr_ref, kc_ref, vct_ref, ovt_ref, ks_ref, vst_ref, kw_ref, vwt_ref, bg_ref,
                o_ref, cmp_ref, slc_ref, sel_ref, qz_ref, sbuf_ref, m_ref, al_ref, acc_ref, *, n_sel):
    i = pl.program_id(1)
    row_minus_col = (lax.broadcasted_iota(jnp.int32, (KC, TQ), 0)
                     - lax.broadcasted_iota(jnp.int32, (KC, TQ), 1))
    tq = i * TQ + lax.broadcasted_iota(jnp.int32, (1, TQ), 1)
    hpg = NSA_HEADS // NSA_GROUPS
    nc = kc_ref.shape[1]
    n_blk = sel_ref.shape[1]
    gw = 2 * HEAD_DIM

    _store_half_masked(q_ref, qz_ref, NSA_HEADS // 2)
    cmp_valid = CMP_STRIDE * lax.broadcasted_iota(jnp.int32, (nc, TQ), 0) + (CMP_LEN - 1) <= tq
    blk = lax.broadcasted_iota(jnp.int32, (n_blk, TQ), 0)
    forced = jnp.logical_or(blk == jnp.right_shift(tq, int(math.log2(SLC_BLOCK))), blk == 0)
    blk_causal = blk * SLC_BLOCK <= tq
    for g in range(NSA_GROUPS):
        kcg = kc_ref[g]
        vcg = vct_ref[NSA_GROUPS + g]
        imp = jnp.zeros((n_blk, TQ), F32)
        for hh in range(hpg):
            h = hpg * g + hh
            s = jnp.where(cmp_valid, _dot_nt(kcg, qz_ref[h]), NEG_INF)
            m = jnp.maximum(jnp.max(s, axis=0, keepdims=True), M_INIT)
            p = jnp.exp(s - m)
            l = jnp.sum(p, axis=0, keepdims=True)
            pc = (p * jnp.where(l > 0.0, 1.0 / l, 0.0)).astype(BF16)
            cmp_ref[HEAD_DIM * h:HEAD_DIM * (h + 1), :] = _dot(vcg, pc)
            imp = imp + _dot(ovt_ref[...], pc)
        imp = jnp.where(forced, FORCE_SCORE, imp)
        imp = jnp.where(blk_causal, imp, NEG_INF)
        n_tiles = n_blk // SUBLANES
        sub = lax.broadcasted_iota(jnp.int32, (SUBLANES, TQ), 0)
        tiles = [imp[SUBLANES * t:SUBLANES * (t + 1)] for t in range(n_tiles)]
        ranks = [jnp.zeros((SUBLANES, TQ), F32)] * n_tiles
        for mp in range(n_blk):
            r = jnp.broadcast_to(imp[mp:mp + 1, :], (SUBLANES, TQ))
            for t in range(n_tiles):
                if SUBLANES * t > mp:
                    ahead = r >= tiles[t]
                elif SUBLANES * (t + 1) <= mp:
                    ahead = r > tiles[t]
                else:
                    later = sub > mp - SUBLANES * t
                    ahead = jnp.logical_or(r > tiles[t], jnp.logical_and(r == tiles[t], later))
                ranks[t] = ranks[t] + jnp.where(ahead, 1.0, 0.0)
        rank = jnp.concatenate(ranks, axis=0)
        sel_ref[g] = jnp.where(rank < float(n_sel), 0.0, NEG_INF)

    _store_half_masked(qr_ref, qz_ref, NSA_HEADS // 2)
    bpc = KC // SLC_BLOCK

    def slc_prep(j):
        off = pl.multiple_of(j * KC, KC)
        visible = row_minus_col <= (i - j) * KC
        biases = []
        for g in range(NSA_GROUPS):
            bias = jnp.concatenate(
                [jnp.broadcast_to(sel_ref[g, pl.ds(bpc * j + b, 1), :], (SLC_BLOCK, TQ))
                 for b in range(bpc)], axis=0)
            biases.append(jnp.where(visible, bias, NEG_INF))
        return off, biases

    def slc_score(ctx, h):
        off, biases = ctx
        g = h // hpg
        return _dot_nt(ks_ref[pl.ds(off, KC), gw * g:gw * (g + 1)], qz_ref[h]) + biases[g]

    def slc_values(j, h):
        g = h // hpg
        return vst_ref[HV * g:HV * (g + 1), pl.ds(pl.multiple_of(j * KC, KC), KC)]

    _attn_pipelined(0, i, NSA_HEADS, HV, slc_prep, slc_score, slc_values,
                    sbuf_ref, m_ref, al_ref, acc_ref)

    def head_out(h):
        return acc_ref[HV * h:HV * h + HEAD_DIM, :] / acc_ref[HV * h + HEAD_DIM:HV * h + HEAD_DIM + 1, :]

    for h in range(NSA_HEADS):
        slc_ref[HEAD_DIM * h:HEAD_DIM * (h + 1), :] = head_out(h)

    def win_prep(j):
        off = pl.multiple_of(j * KC, KC)
        d = (i - j) * KC
        inside = jnp.logical_and(row_minus_col <= d, row_minus_col > d - WINDOW)
        return off, jnp.where(inside, 0.0, NEG_INF)

    def win_score(ctx, h):
        off, bias = ctx
        g = h // hpg
        return _dot_nt(kw_ref[pl.ds(off, KC), gw * g:gw * (g + 1)], qz_ref[h]) + bias

    def win_values(j, h):
        g = h // hpg
        return vwt_ref[HV * g:HV * (g + 1), pl.ds(pl.multiple_of(j * KC, KC), KC)]

    _attn_pipelined(jnp.maximum(i - WINDOW // KC, 0), i, NSA_HEADS, HV,
                    win_prep, win_score, win_values, sbuf_ref, m_ref, al_ref, acc_ref)

    gb = _sigmoid(bg_ref[...])
    for hp in range(NSA_HEADS // 2):
        parts = []
        for h in (2 * hp, 2 * hp + 1):
            rows = slice(HEAD_DIM * h, HEAD_DIM * (h + 1))
            parts.append(gb[3 * h:3 * h + 1, :] * cmp_ref[rows, :]
                         + gb[3 * h + 1:3 * h + 2, :] * slc_ref[rows, :]
                         + gb[3 * h + 2:3 * h + 3, :] * head_out(h))
        o_ref[:, LANES * hp:LANES * (hp + 1)] = jnp.concatenate(parts, axis=0).T


def _nsa_call(qc, qcr, kcc, vct, ovt, ks, vst, kw, vwt, bgt, n_sel):
    B, S, W = qc.shape
    n_blk = S // SLC_BLOCK
    qspec = pl.BlockSpec((None, TQ, W), lambda b, i: (b, i, 0))
    return pl.pallas_call(
        functools.partial(_nsa_kernel, n_sel=n_sel),
        grid=(B, S // TQ),
        in_specs=[
            qspec, qspec,
            pl.BlockSpec((None,) + kcc.shape[1:], lambda b, i: (b, 0, 0, 0)),
            pl.BlockSpec((None,) + vct.shape[1:], lambda b, i: (b, 0, 0, 0)),
            pl.BlockSpec(ovt.shape, lambda b, i: (0, 0)),
            pl.BlockSpec((None, S, 2 * LANES), lambda b, i: (b, 0, 0)),
            pl.BlockSpec((None, NSA_GROUPS * HV, S), lambda b, i: (b, 0, 0)),
            pl.BlockSpec((None, S, 2 * LANES), lambda b, i: (b, 0, 0)),
            pl.BlockSpec((None, NSA_GROUPS * HV, S), lambda b, i: (b, 0, 0)),
            pl.BlockSpec((None, 3 * NSA_HEADS, TQ), lambda b, i: (b, 0, i)),
        ],
        out_specs=qspec,
        out_shape=jax.ShapeDtypeStruct((B, S, W), F32),
        scratch_shapes=[pltpu.VMEM((W, TQ), F32), pltpu.VMEM((W, TQ), F32),
                        pltpu.VMEM((NSA_GROUPS, n_blk, TQ), F32),
                        pltpu.VMEM((NSA_HEADS, TQ, LANES), BF16)] + _attn_scratch(NSA_HEADS, HV),
        compiler_params=pltpu.CompilerParams(
            dimension_semantics=("arbitrary", "arbitrary"), vmem_limit_bytes=VMEM_LIMIT),
        name="nsa_attn",
    )(qc, qcr, kcc, vct, ovt, ks, vst, kw, vwt, bgt)


def _post_kernel(x_ref, oa_ref, ob_ref, oc_ref, p_ref, g_ref, wg_ref, wup_ref, wout_ref,
                 pg_ref, wple_ref, wpg_ref, fg_ref, o_ref, *, final):
    x = x_ref[...]
    h = _rms(x, g_ref[...]).astype(BF16)
    mixed = jnp.zeros(x.shape, F32)
    for n, br_ref in enumerate((oa_ref, ob_ref, oc_ref)):
        gate = _dot(h, wg_ref[:, BRANCH_WIDTH * n:BRANCH_WIDTH * (n + 1)])
        og = br_ref[...] * (gate * _sigmoid(gate))
        up = _dot(og.astype(BF16), wup_ref[n])
        c0 = N_BRANCH * BRANCH_WIDTH + D_MODEL * n
        mixed = mixed + _sigmoid(_dot(h, wg_ref[:, c0:c0 + D_MODEL])) * up
    x1 = x + _dot(mixed.astype(BF16), wout_ref[...])
    emb = _dot(p_ref[...].astype(BF16), wple_ref[...])
    hg = _dot(_rms(x1, pg_ref[...]).astype(BF16), wpg_ref[...])
    x2 = x1 + emb * _sigmoid(hg)
    if final:
        x2 = _rms(x2, fg_ref[...])
    o_ref[...] = x2


def _post_call(x, oa, ob, oc, p, g, wg, wup, wout, pg, wple, wpg, fg, final):
    T, D = x.shape
    tm = TM_POST
    tok = lambda w: pl.BlockSpec((tm, w), lambda i: (i, 0))
    full = lambda a: pl.BlockSpec(a.shape, lambda i: (0,) * a.ndim, pipeline_mode=pl.Buffered(1))
    return pl.pallas_call(
        functools.partial(_post_kernel, final=final),
        grid=(T // tm,),
        in_specs=[tok(D), tok(BRANCH_WIDTH), tok(BRANCH_WIDTH), tok(BRANCH_WIDTH), tok(PLE_DIM),
                  full(g), full(wg), full(wup), full(wout), full(pg), full(wple), full(wpg), full(fg)],
        out_specs=tok(D),
        out_shape=jax.ShapeDtypeStruct((T, D), F32),
        compiler_params=pltpu.CompilerParams(
            dimension_semantics=("arbitrary",), vmem_limit_bytes=VMEM_LIMIT),
        name="post_mix",
    )(x, oa, ob, oc, p, g, wg, wup, wout, pg, wple, wpg, fg)


def _rope_tables(positions):
    inv = ROPE_THETA ** (-jnp.arange(0, ROT_DIM, 2, dtype=F32) / ROT_DIM)
    ang = positions.astype(F32)[..., None] * inv
    cos, sin = jnp.cos(ang), jnp.sin(ang)
    half = ROT_DIM // 2
    pad = HEAD_DIM - ROT_DIM
    zeros = jnp.zeros(cos.shape[:-1] + (half,), F32)
    rc = jnp.concatenate([cos, cos, jnp.ones(cos.shape[:-1] + (pad,), F32)], axis=-1)
    rs1 = jnp.concatenate([-sin, zeros, jnp.zeros(cos.shape[:-1] + (pad,), F32)], axis=-1)
    rs2 = jnp.concatenate([zeros, sin, jnp.zeros(cos.shape[:-1] + (pad,), F32)], axis=-1)
    rep = lambda t: jnp.concatenate([t] * (LANES // HEAD_DIM), axis=-1)
    return rep(rc), rep(rs1), rep(rs2)


def _proj_weights(w):
    sl = lambda o, n: w[:, o:o + n]
    dup = lambda o: jnp.concatenate([sl(o, HEAD_DIM)] * 2, axis=1)
    nat = jnp.concatenate([
        sl(_A_Q, 512), sl(_A_K, 512), sl(_B_Q, 512), sl(_I_Q, 512), sl(_C_Q, 512),
        dup(_B_K), dup(_I_K),
        dup(_C_KS), dup(_C_KS + HEAD_DIM), dup(_C_KW), dup(_C_KW + HEAD_DIM),
        sl(_C_KC, 128), sl(_C_VC, 128)], axis=1)
    tr = jnp.concatenate([
        sl(_A_V, 512), sl(_B_V, 64), sl(_C_VS, 128), sl(_C_VW, 128), sl(_I_W, 8), sl(_C_BG, 24)], axis=1)
    gates = jnp.concatenate([sl(_A_G, 512), sl(_B_G, 512), sl(_C_G, 512), sl(_MERGE, 3 * D_MODEL)], axis=1)
    return nat.astype(BF16), tr.T.astype(BF16), gates.astype(BF16)


def _overlap_t(S):
    nc = S // CMP_STRIDE
    n_blk = S // SLC_BLOCK
    cstart = np.arange(nc) * CMP_STRIDE
    sstart = np.arange(n_blk) * SLC_BLOCK
    ov = ((cstart[None, :] < sstart[:, None] + SLC_BLOCK) & (cstart[None, :] + CMP_LEN > sstart[:, None]))
    return jnp.asarray(ov.astype(np.float32), dtype=BF16)


def kernel(x, p, positions, norm_g, w_in, diff_lambda, diff_subln_g, cmp_pe_k, cmp_w1_k, cmp_w2_k,
           cmp_pe_v, cmp_w1_v, cmp_w2_v, w_up, w_out, ple_norm_g, w_ple, w_ple_gate, final_norm_g):
    B, S, D = x.shape
    depth = w_in.shape[0]
    k_sel = min(DSA_TOPK_MAX, S // 4)
    n_sel = min(SLC_TOPN_MAX, S // SLC_BLOCK)
    assert S % TM_PROJ == 0 and S % TQ == 0 and k_sel <= TQ and D == D_MODEL
    assert KC == TQ and WINDOW % KC == 0
    rc, rs1, rs2 = _rope_tables(positions)
    ovt = _overlap_t(S)
    half_len = CMP_LEN // 2
    kdim = half_len * NSA_GROUPS * HEAD_DIM
    row2 = lambda v: v.reshape(1, -1)

    for i in range(depth):
        lam_init = 0.8 - 0.6 * math.exp(-0.3 * i)
        wn, wt, wg = _proj_weights(w_in[i])
        (qa, ka, qb, iq, qc, qcr, kb, ik, ks, kw, ck, cv,
         vat, vbt, vst, vwt, iwt, bgt) = _proj_call(x, row2(norm_g[i]), wn, wt, rc, rs1, rs2)

        rk = ck.reshape(B, S // half_len, kdim)
        rv = cv.reshape(B, S // half_len, kdim)
        pe = jnp.stack([cmp_pe_k[i], cmp_pe_v[i]])
        pe = jnp.broadcast_to(pe[:, :, None, :], (2, CMP_LEN, NSA_GROUPS, HEAD_DIM))
        w1 = jnp.stack([cmp_w1_k[i], cmp_w1_v[i]]).astype(BF16)
        zero = jnp.zeros_like(w1)
        w1g = jnp.stack([jnp.stack([w1 if h == g else zero for h in range(NSA_GROUPS)], axis=2)
                         for g in range(NSA_GROUPS)], axis=1)
        w1g = w1g.reshape(2 * NSA_GROUPS, CMP_LEN, NSA_GROUPS * HEAD_DIM, CMP_HIDDEN)
        w2 = jnp.stack([cmp_w2_k[i], cmp_w2_v[i]]).astype(BF16)
        kcc, vct = _compress_call(
            rk, rv, pe[:, :half_len].reshape(2, 1, kdim), pe[:, half_len:].reshape(2, 1, kdim),
            w1g[:, :half_len].reshape(2 * NSA_GROUPS, kdim, CMP_HIDDEN),
            w1g[:, half_len:].reshape(2 * NSA_GROUPS, kdim, CMP_HIDDEN), w2)

        oa = _diff_call(qa, ka, vat, diff_lambda[i], diff_subln_g[i].reshape(-1, 1), lam_init)
        ob = _dsa_call(qb, iq, kb, ik, vbt, iwt, k_sel)
        oc = _nsa_call(qc, qcr, kcc, vct, ovt, ks, vst, kw, vwt, bgt, n_sel)

        x = _post_call(
            x.reshape(B * S, D), oa.reshape(B * S, -1), ob.reshape(B * S, -1), oc.reshape(B * S, -1),
            p[i].reshape(B * S, -1), row2(norm_g[i]), wg, w_up[i].astype(BF16), w_out[i].astype(BF16),
            row2(ple_norm_g[i]), w_ple[i].astype(BF16), w_ple_gate[i].astype(BF16), row2(final_norm_g),
            final=(i == depth - 1)).reshape(B, S, D)
    return x
```

```python
import functools
import math

import numpy as np
import jax
import jax.numpy as jnp
from jax import lax
from jax.experimental import pallas as pl
from jax.experimental.pallas import tpu as pltpu

F32 = jnp.float32
BF16 = jnp.bfloat16

D_MODEL = 1024
HEAD_DIM = 64
ROT_DIM = 16
ROPE_THETA = 500000.0
EPS = 1e-6
PLE_DIM = 256
DA_HEADS = 4
DSA_HEADS = 8
IDX_HEADS = 8
DSA_TOPK_MAX = 256
NSA_GROUPS = 2
NSA_HEADS = 8
CMP_LEN = 32
CMP_STRIDE = 16
CMP_HIDDEN = 128
SLC_BLOCK = 64
SLC_TOPN_MAX = 16
WINDOW = 512
FORCE_SCORE = 1e4
BRANCH_WIDTH = 512
N_BRANCH = 3

IN_SPLITS = (512, 512, 512, 512, 512, 64, 64, 512, 512, 64, 8,
             512, 128, 128, 128, 128, 128, 128, 512, 24, 3 * D_MODEL)
(_A_Q, _A_K, _A_V, _A_G, _B_Q, _B_K, _B_V, _B_G, _I_Q, _I_K, _I_W,
 _C_Q, _C_KC, _C_VC, _C_KS, _C_VS, _C_KW, _C_VW, _C_G, _C_BG, _MERGE) = [
    int(v) for v in np.concatenate([[0], np.cumsum(IN_SPLITS)[:-1]])]

LANES = 128
SUBLANES = 8
TQ = 512
KC = 512
TM_PROJ = 512
TM_POST = 512
NEG_INF = float("-inf")
M_INIT = -1e30
N_NAT = 3584
N_TR = 864
BISECT_PLAIN = 10
ONES_ROWS = 16
HV = HEAD_DIM + ONES_ROWS
HV_A = 2 * HEAD_DIM + ONES_ROWS
LOG2E = math.log2(math.e)
VMEM_LIMIT = 56 * 1024 * 1024


def _dot_nt(a, b):
    return lax.dot_general(a, b, (((1,), (1,)), ((), ())), preferred_element_type=F32)


def _dot(a, b):
    return jnp.dot(a, b, preferred_element_type=F32)


def _rms(x, g):
    return x * lax.rsqrt(jnp.mean(x * x, axis=-1, keepdims=True) + EPS) * g


def _sigmoid(x):
    return 1.0 / (1.0 + jnp.exp(-x))


def _store_half_masked(src_ref, dst_ref, n_pairs):
    lane = lax.broadcasted_iota(jnp.int32, (src_ref.shape[0], LANES), 1)
    for hp in range(n_pairs):
        x = src_ref[:, LANES * hp:LANES * (hp + 1)]
        z = jnp.zeros_like(x)
        dst_ref[2 * hp] = jnp.where(lane < HEAD_DIM, x, z)
        dst_ref[2 * hp + 1] = jnp.where(lane >= HEAD_DIM, x, z)


def _fold_rows(x, op):
    n_acc = 2
    accs = [x[SUBLANES * r:SUBLANES * (r + 1)] for r in range(n_acc)]
    for r in range(n_acc, x.shape[0] // SUBLANES):
        accs[r % n_acc] = op(accs[r % n_acc], x[SUBLANES * r:SUBLANES * (r + 1)])
    return op(accs[0], accs[1])


class _Stage:
    def __init__(self, j0, j1, nh, hv, prep, score, values, m_ref, al_ref, acc_ref):
        self.j0 = jnp.asarray(j0, jnp.int32)
        self.j1 = jnp.asarray(j1, jnp.int32)
        self.nh, self.hv, self.prep, self.score, self.values = nh, hv, prep, score, values
        self.m_ref, self.al_ref, self.acc_ref = m_ref, al_ref, acc_ref

    def reset(self):
        self.m_ref[...] = jnp.full(self.m_ref.shape, M_INIT, F32)
        self.acc_ref[...] = jnp.zeros(self.acc_ref.shape, F32)

    def produce(self, sbuf_ref, j, prep=None):
        ctx = (prep or self.prep)(j)
        for h in range(self.nh):
            s = self.score(ctx, h)
            m_old = self.m_ref[h:h + 1, :]
            m_new = jnp.maximum(m_old, jnp.max(s, axis=0, keepdims=True))
            sbuf_ref[h] = s
            self.m_ref[h:h + 1, :] = m_new
            self.al_ref[h:h + 1, :] = jnp.exp2(m_old - m_new)

    def consume(self, sbuf_ref, j):
        for h in range(self.nh):
            p = jnp.exp2(sbuf_ref[h] - self.m_ref[h:h + 1, :])
            rows = slice(self.hv * h, self.hv * (h + 1))
            self.acc_ref[rows, :] = (self.al_ref[h:h + 1, :] * self.acc_ref[rows, :]
                                     + _dot(self.values(j, h), p.astype(BF16)))

    def loop(self, sbuf_ref, lo, hi):
        def body(j, carry):
            self.consume(sbuf_ref, j - 1)
            self.produce(sbuf_ref, j)
            return carry
        lax.fori_loop(lo, hi, body, 0)


def _attn_chain(stages, sbuf_ref):
    for st in stages:
        st.reset()
    stages[0].produce(sbuf_ref, stages[0].j0)
    for n, st in enumerate(stages):
        st.loop(sbuf_ref, st.j0 + 1, st.j1 + 1)
        st.consume(sbuf_ref, st.j1)
        if n + 1 < len(stages):
            stages[n + 1].produce(sbuf_ref, stages[n + 1].j0)


def _attn_pipelined(j0, j1, nh, hv, prep_fn, score_fn, vt_fn, sbuf_ref, m_ref, al_ref, acc_ref,
                    prep_last_fn=None):
    st = _Stage(j0, j1, nh, hv, prep_fn, score_fn, vt_fn, m_ref, al_ref, acc_ref)
    if prep_last_fn is None:
        _attn_chain([st], sbuf_ref)
        return
    st.reset()

    @pl.when(st.j1 == st.j0)
    def _():
        st.produce(sbuf_ref, st.j1, prep_last_fn)
        st.consume(sbuf_ref, st.j1)

    @pl.when(st.j1 > st.j0)
    def _():
        st.produce(sbuf_ref, st.j0)
        st.loop(sbuf_ref, st.j0 + 1, st.j1)
        st.consume(sbuf_ref, st.j1 - 1)
        st.produce(sbuf_ref, st.j1, prep_last_fn)
        st.consume(sbuf_ref, st.j1)


def _attn_scratch(nh, hv):
    return [pltpu.VMEM((nh, KC, TQ), F32), pltpu.VMEM((nh, TQ), F32),
            pltpu.VMEM((nh, TQ), F32), pltpu.VMEM((nh * hv, TQ), F32)]


def _layer_spec(a, layer, **kw):
    tail = (0,) * (a.ndim - 1)
    return pl.BlockSpec((None,) + a.shape[1:], lambda *_: (layer,) + tail, **kw)


def _ones_rows(n_cols):
    r = lax.broadcasted_iota(jnp.int32, (ONES_ROWS, n_cols), 0)
    return jnp.where(r == 0, 1.0, 0.0).astype(BF16)


def _proj_kernel(x_ref, g_ref, wn_ref, wt_ref, rc_ref, rs1_ref, rs2_ref,
                 qa_ref, ka_ref, qb_ref, iq_ref, qc_ref, qcr_ref, kb_ref, ik_ref,
                 ks_ref, kw_ref, ck_ref, cv_ref,
                 vat_ref, vbt_ref, vst_ref, vwt_ref, iwt_ref, bgt_ref):
    h = _rms(x_ref[...], g_ref[...]).astype(BF16)
    rc, rs1, rs2 = rc_ref[...], rs1_ref[...], rs2_ref[...]

    def rope(z):
        return z * rc + pltpu.roll(z, LANES - ROT_DIM // 2, 1) * rs1 + pltpu.roll(z, ROT_DIM // 2, 1) * rs2

    qscale = HEAD_DIM ** -0.5
    qscale2 = qscale * LOG2E
    segs = (
        (0, 512, ((qa_ref, True, qscale2),)),
        (512, 512, ((ka_ref, True, 1.0),)),
        (1024, 512, ((qb_ref, True, qscale2),)),
        (1536, 512, ((iq_ref, True, qscale),)),
        (2048, 512, ((qc_ref, False, qscale), (qcr_ref, True, qscale2))),
        (2560, 128, ((kb_ref, True, 1.0),)),
        (2688, 128, ((ik_ref, True, 1.0),)),
        (2816, 256, ((ks_ref, True, 1.0),)),
        (3072, 256, ((kw_ref, True, 1.0),)),
    )
    zc = _dot(h, wn_ref[:, 3328:3328 + 2 * LANES])
    ck_ref[...] = zc[:, 0:LANES]
    cv_ref[...] = zc[:, LANES:2 * LANES]
    for c0, width, outs in segs:
        z = _dot(h, wn_ref[:, c0:c0 + width])
        for out_ref, rot, scale in outs:
            for j in range(width // LANES):
                zj = z[:, LANES * j:LANES * (j + 1)]
                if rot:
                    zj = rope(zj)
                if scale != 1.0:
                    zj = zj * scale
                out_ref[:, LANES * j:LANES * (j + 1)] = zj.astype(out_ref.dtype)

    zt = _dot_nt(wt_ref[...], h)
    ones = _ones_rows(zt.shape[1])
    vd = 2 * HEAD_DIM
    for h in range(DA_HEADS):
        vat_ref[HV_A * h:HV_A * h + vd, :] = zt[vd * h:vd * (h + 1)].astype(BF16)
        vat_ref[HV_A * h + vd:HV_A * (h + 1), :] = ones
    vbt_ref[0:HEAD_DIM, :] = zt[512:576].astype(BF16)
    vbt_ref[HEAD_DIM:HV, :] = ones
    for g in range(NSA_GROUPS):
        for ref, base in ((vst_ref, 576), (vwt_ref, 704)):
            ref[HV * g:HV * g + HEAD_DIM, :] = zt[base + HEAD_DIM * g:base + HEAD_DIM * (g + 1)].astype(BF16)
            ref[HV * g + HEAD_DIM:HV * (g + 1), :] = ones
    iwt_ref[...] = zt[832:840] * (IDX_HEADS ** -0.5)
    bgt_ref[...] = zt[840:864]


def _proj_call(x, g, wn, wt, rc, rs1, rs2, layer):
    B, S, D = x.shape
    tm = TM_PROJ
    nat = lambda w: pl.BlockSpec((None, tm, w), lambda b, i: (b, i, 0))
    tr = lambda r: pl.BlockSpec((None, r, tm), lambda b, i: (b, 0, i))
    full = lambda a: _layer_spec(a, layer)
    sds = jax.ShapeDtypeStruct
    out_shape = (
        sds((B, S, 512), BF16), sds((B, S, 512), BF16), sds((B, S, 512), BF16), sds((B, S, 512), BF16),
        sds((B, S, 512), BF16), sds((B, S, 512), BF16), sds((B, S, 128), BF16), sds((B, S, 128), BF16),
        sds((B, S, 256), BF16), sds((B, S, 256), BF16),
        sds((B, S, LANES), F32), sds((B, S, LANES), F32),
        sds((B, DA_HEADS * HV_A, S), BF16), sds((B, HV, S), BF16),
        sds((B, NSA_GROUPS * HV, S), BF16), sds((B, NSA_GROUPS * HV, S), BF16),
        sds((B, 8, S), F32), sds((B, 24, S), F32),
    )
    out_specs = (nat(512), nat(512), nat(512), nat(512), nat(512), nat(512), nat(128), nat(128),
                 nat(256), nat(256), nat(LANES), nat(LANES),
                 tr(DA_HEADS * HV_A), tr(HV), tr(NSA_GROUPS * HV), tr(NSA_GROUPS * HV), tr(8), tr(24))
    return pl.pallas_call(
        _proj_kernel,
        grid=(B, S // tm),
        in_specs=[nat(D), full(g), full(wn), full(wt), nat(LANES), nat(LANES), nat(LANES)],
        out_specs=out_specs,
        out_shape=out_shape,
        compiler_params=pltpu.CompilerParams(
            dimension_semantics=("arbitrary", "arbitrary"), vmem_limit_bytes=VMEM_LIMIT),
        name="in_proj",
    )(x, g, wn, wt, rc, rs1, rs2)


def _compress_kernel(rk_ref, rv_ref, pea_ref, peb_ref, w1a_ref, w1b_ref, w2_ref, nat_ref, t_ref):
    for kind, r_ref in enumerate((rk_ref, rv_ref)):
        r = r_ref[...]
        xa = (r + pea_ref[kind]).astype(BF16)
        xb = (r + peb_ref[kind]).astype(BF16)
        nrow = r.shape[0]
        for g in range(NSA_GROUPS):
            n = NSA_GROUPS * kind + g
            hid = jax.nn.gelu(_dot(xa, w1a_ref[n]) + pltpu.roll(_dot(xb, w1b_ref[n]), nrow - 1, 0))
            o = _dot(hid.astype(BF16), w2_ref[kind])
            nat_ref[n] = jnp.concatenate([o, o], axis=1).astype(BF16)
            t_ref[n] = o.T.astype(BF16)


def _compress_call(rk, rv, pea, peb, w1a, w1b, w2, layer):
    B, nr, kdim = rk.shape
    full = lambda a: _layer_spec(a, layer)
    rspec = pl.BlockSpec((None, nr, kdim), lambda b: (b, 0, 0))
    return pl.pallas_call(
        _compress_kernel,
        grid=(B,),
        in_specs=[rspec, rspec, full(pea), full(peb), full(w1a), full(w1b), full(w2)],
        out_specs=(
            pl.BlockSpec((None, 2 * NSA_GROUPS, nr, 2 * HEAD_DIM), lambda b: (b, 0, 0, 0)),
            pl.BlockSpec((None, 2 * NSA_GROUPS, HEAD_DIM, nr), lambda b: (b, 0, 0, 0)),
        ),
        out_shape=(jax.ShapeDtypeStruct((B, 2 * NSA_GROUPS, nr, 2 * HEAD_DIM), BF16),
                   jax.ShapeDtypeStruct((B, 2 * NSA_GROUPS, HEAD_DIM, nr), BF16)),
        compiler_params=pltpu.CompilerParams(dimension_semantics=("arbitrary",)),
        name="nsa_compress",
    )(rk, rv, pea, peb, w1a, w1b, w2)


def _diff_kernel(q_ref, k_ref, vt_ref, lam_ref, sg_ref, o_ref,
                 qz_ref, sbuf_ref, m_ref, al_ref, acc_ref, *, lam_init):
    i = pl.program_id(1)
    lp = lam_ref[...]
    lam = (jnp.exp(jnp.sum(lp[0:1] * lp[1:2], axis=1, keepdims=True))
           - jnp.exp(jnp.sum(lp[2:3] * lp[3:4], axis=1, keepdims=True)) + lam_init)
    row_minus_col = (lax.broadcasted_iota(jnp.int32, (KC, TQ), 0)
                     - lax.broadcasted_iota(jnp.int32, (KC, TQ), 1))
    _store_half_masked(q_ref, qz_ref, DA_HEADS)
    vdim = 2 * HEAD_DIM
    n_maps = 2 * DA_HEADS

    def prep(j):
        return pl.multiple_of(j * KC, KC), jnp.where(row_minus_col <= (i - j) * KC, 0.0, NEG_INF)

    def score(ctx, hm):
        off, bias = ctx
        h = hm // 2
        return _dot_nt(k_ref[pl.ds(off, KC), vdim * h:vdim * (h + 1)], qz_ref[hm]) + bias

    def values(j, hm):
        h = hm // 2
        return vt_ref[HV_A * h:HV_A * (h + 1), pl.ds(pl.multiple_of(j * KC, KC), KC)]

    _attn_pipelined(0, i, n_maps, HV_A, prep, score, values, sbuf_ref, m_ref, al_ref, acc_ref)

    for h in range(DA_HEADS):
        r0, r1 = HV_A * 2 * h, HV_A * (2 * h + 1)
        o_t = (acc_ref[r0:r0 + vdim, :] / acc_ref[r0 + vdim:r0 + vdim + 1, :]
               - lam * (acc_ref[r1:r1 + vdim, :] / acc_ref[r1 + vdim:r1 + vdim + 1, :]))
        ms = jnp.mean(o_t * o_t, axis=0, keepdims=True)
        y = o_t * lax.rsqrt(ms + EPS) * sg_ref[...] * (1.0 - lam_init)
        o_ref[:, vdim * h:vdim * (h + 1)] = y.T


def _diff_call(qa, ka, vat, lam_p, sg, lam_init, layer):
    B, S, W = qa.shape
    n_maps = 2 * DA_HEADS
    return pl.pallas_call(
        functools.partial(_diff_kernel, lam_init=lam_init),
        grid=(B, S // TQ),
        in_specs=[
            pl.BlockSpec((None, TQ, W), lambda b, i: (b, i, 0)),
            pl.BlockSpec((None, S, W), lambda b, i: (b, 0, 0)),
            pl.BlockSpec((None, DA_HEADS * HV_A, S), lambda b, i: (b, 0, 0)),
            _layer_spec(lam_p, layer),
            _layer_spec(sg, layer),
        ],
        out_specs=pl.BlockSpec((None, TQ, W), lambda b, i: (b, i, 0)),
        out_shape=jax.ShapeDtypeStruct((B, S, W), F32),
        scratch_shapes=[pltpu.VMEM((n_maps, TQ, LANES), BF16)] + _attn_scratch(n_maps, HV_A),
        compiler_params=pltpu.CompilerParams(
            dimension_semantics=("arbitrary", "arbitrary"), vmem_limit_bytes=VMEM_LIMIT),
        name="diff_attn",
    )(qa, ka, vat, lam_p, sg)


def _dsa_kernel(q_ref, iq_ref, kb_ref, ik_ref, vt_ref, iw_ref, o_ref,
                sc_ref, tj_ref, tri_ref, qz_ref, sbuf_ref, m_ref, al_ref, acc_ref, *, k_sel, seq_len):
    i = pl.program_id(1)
    nch = i + 1
    row = lax.broadcasted_iota(jnp.int32, (KC, TQ), 0)
    col = lax.broadcasted_iota(jnp.int32, (KC, TQ), 1)
    causal = row <= col
    iw = iw_ref[...]
    kf = float(k_sel)
    _store_half_masked(iq_ref, qz_ref, IDX_HEADS // 2)

    def idx_chunk(j, carry, masked):
        mx, mn = carry
        off = pl.multiple_of(j * KC, KC)
        ikc = ik_ref[pl.ds(off, KC), :]
        sc = jnp.zeros((KC, TQ), F32)
        for h in range(IDX_HEADS):
            sc = sc + iw[h:h + 1, :] * jnp.maximum(_dot_nt(ikc, qz_ref[h]), 0.0)
        lo_src = sc
        if masked:
            lo_src = jnp.where(causal, sc, jnp.inf)
            sc = jnp.where(causal, sc, NEG_INF)
        sc_ref[pl.ds(off, KC), :] = sc
        return (jnp.maximum(mx, _fold_rows(sc, jnp.maximum)),
                jnp.minimum(mn, _fold_rows(lo_src, jnp.minimum)))

    carry = (jnp.full((SUBLANES, TQ), NEG_INF, F32), jnp.full((SUBLANES, TQ), jnp.inf, F32))
    carry = lax.fori_loop(0, i, functools.partial(idx_chunk, masked=False), carry)
    mx8, mn8 = idx_chunk(i, carry, True)
    row_max = jnp.max(mx8, axis=0, keepdims=True)
    row_min = jnp.min(mn8, axis=0, keepdims=True)

    def probe_pass(th, snap):
        thb = jnp.broadcast_to(th, (SUBLANES, TQ))

        def body(j, c):
            off = pl.multiple_of(j * KC, KC)
            c = list(c)
            for r in range(KC // SUBLANES):
                x = sc_ref[pl.ds(off + SUBLANES * r, SUBLANES), :]
                ge = x >= thb
                c[0] = c[0] + jnp.where(ge, 1.0, 0.0)
                if snap:
                    c[1] = jnp.maximum(c[1], jnp.where(ge, NEG_INF, x))
            return tuple(c)
        init = (jnp.zeros((SUBLANES, TQ), F32),)
        if snap:
            init += (jnp.full((SUBLANES, TQ), NEG_INF, F32),)
        out = lax.fori_loop(0, nch, body, init)
        cnt = jnp.sum(out[0], axis=0, keepdims=True)
        return (cnt, jnp.max(out[1], axis=0, keepdims=True)) if snap else cnt

    n_causal = (i * TQ + 1 + lax.broadcasted_iota(jnp.int32, (1, TQ), 1)).astype(F32)
    keep_all = n_causal <= kf

    def bisect(_, c):
        lo, hi, clo, chi = c
        mid = lo + (hi - lo) * 0.5
        cnt = probe_pass(mid, False)
        ge = cnt >= kf
        return (jnp.where(ge, mid, lo), jnp.where(ge, hi, mid),
                jnp.where(ge, cnt, clo), jnp.where(ge, chi, cnt))

    hi0 = row_max + (row_max - row_min) + 1.0
    lo, hi, clo, chi = lax.fori_loop(
        0, BISECT_PLAIN, bisect, (row_min, hi0, n_causal, jnp.zeros((1, TQ), F32)))

    def cond(st):
        return jnp.logical_and(st[0] < 256, st[-1] > 0.0)

    def snap_step(st):
        it, lo, hi, clo, chi, hie, known, stuck, done, _ = st
        top = jnp.where(known > 0.0, hie, hi)
        mid = lo + (top - lo) * 0.5
        inside = jnp.logical_and(mid > lo, mid < top)
        near_top = jnp.logical_or(kf - chi <= 2.0, stuck > 0.0)
        use_top = jnp.logical_or(jnp.logical_and(known > 0.0, near_top), jnp.logical_not(inside))
        th = jnp.where(use_top, top, mid)
        cnt, edn = probe_pass(th, True)
        live = done <= 0.0
        ge = jnp.logical_and(live, cnt >= kf)
        lt = jnp.logical_and(live, cnt < kf)
        stuck = jnp.where(jnp.logical_and(ge, cnt == clo), 1.0, 0.0)
        lo = jnp.where(ge, th, lo)
        clo = jnp.where(ge, cnt, clo)
        hi = jnp.where(lt, th, hi)
        chi = jnp.where(lt, cnt, chi)
        hie = jnp.where(lt, edn, hie)
        known = jnp.where(lt, 1.0, known)
        fin = jnp.logical_or(clo == kf, jnp.logical_and(known > 0.0, lo >= hie))
        done = jnp.where(fin, 1.0, done)
        return it + 1, lo, hi, clo, chi, hie, known, stuck, done, jnp.max(1.0 - done)

    zero = jnp.zeros((1, TQ), F32)
    done0 = jnp.where(jnp.logical_or(clo == kf, keep_all), 1.0, 0.0)
    st = lax.while_loop(cond, snap_step, (jnp.int32(0), lo, hi, clo, chi, jnp.full((1, TQ), NEG_INF, F32),
                                          zero, zero, done0, jnp.max(1.0 - done0)))
    thr, clo, chi = st[1], st[3], st[4]
    need = jnp.where(clo == kf, float(seq_len), kf - chi)
    thr = jnp.where(keep_all, NEG_INF, thr)
    need = jnp.where(keep_all, 0.0, need)
    tj_ref[2:3, :] = jnp.zeros((1, TQ), F32)

    _store_half_masked(q_ref, qz_ref, DSA_HEADS // 2)
    half = KC // 2
    tri_ref[...] = jnp.where(lax.broadcasted_iota(jnp.int32, (half, half), 0)
                             >= lax.broadcasted_iota(jnp.int32, (half, half), 1), 1.0, 0.0).astype(BF16)

    def prep(j):
        off = pl.multiple_of(j * KC, KC)
        x = sc_ref[pl.ds(off, KC), :]
        eq = x == thr
        e = jnp.where(eq, 1.0, 0.0).astype(BF16)
        r_top = _dot(tri_ref[...], e[:half]) + tj_ref[2:3, :]
        r_bot = _dot(tri_ref[...], e[half:]) + r_top[half - 1:half, :]
        tj_ref[2:3, :] = r_bot[half - 1:half, :]
        rank = jnp.concatenate([r_top, r_bot], axis=0)
        tie_bias = jnp.where(jnp.logical_and(eq, rank <= need), 0.0, NEG_INF)
        return kb_ref[pl.ds(off, KC), :], jnp.where(x > thr, 0.0, tie_bias)

    def score(ctx, h):
        kc, bias = ctx
        return _dot_nt(kc, qz_ref[h]) + bias

    def values(j, h):
        return vt_ref[:, pl.ds(pl.multiple_of(j * KC, KC), KC)]

    _attn_pipelined(0, i, DSA_HEADS, HV, prep, score, values, sbuf_ref, m_ref, al_ref, acc_ref)
    for hp in range(DSA_HEADS // 2):
        blk = jnp.concatenate(
            [acc_ref[HV * h:HV * h + HEAD_DIM, :] / acc_ref[HV * h + HEAD_DIM:HV * h + HEAD_DIM + 1, :]
             for h in (2 * hp, 2 * hp + 1)], axis=0)
        o_ref[:, LANES * hp:LANES * (hp + 1)] = blk.T


def _dsa_call(qb, iq, kb, ik, vbt, iwt, k_sel):
    B, S, W = qb.shape
    return pl.pallas_call(
        functools.partial(_dsa_kernel, k_sel=k_sel, seq_len=S),
        grid=(B, S // TQ),
        in_specs=[
            pl.BlockSpec((None, TQ, W), lambda b, i: (b, i, 0)),
            pl.BlockSpec((None, TQ, W), lambda b, i: (b, i, 0)),
            pl.BlockSpec((None, S, LANES), lambda b, i: (b, 0, 0)),
            pl.BlockSpec((None, S, LANES), lambda b, i: (b, 0, 0)),
            pl.BlockSpec((None, HV, S), lambda b, i: (b, 0, 0)),
            pl.BlockSpec((None, IDX_HEADS, TQ), lambda b, i: (b, 0, i)),
        ],
        out_specs=pl.BlockSpec((None, TQ, W), lambda b, i: (b, i, 0)),
        out_shape=jax.ShapeDtypeStruct((B, S, W), F32),
        scratch_shapes=[pltpu.VMEM((S, TQ), F32), pltpu.VMEM((SUBLANES, TQ), F32),
                        pltpu.VMEM((KC // 2, KC // 2), BF16),
                        pltpu.VMEM((DSA_HEADS, TQ, LANES), BF16)] + _attn_scratch(DSA_HEADS, HV),
        compiler_params=pltpu.CompilerParams(
            dimension_semantics=("arbitrary", "arbitrary"), vmem_limit_bytes=VMEM_LIMIT),
        name="dsa_attn",
    )(qb, iq, kb, ik, vbt, iwt)


def _nsa_kernel(q_ref, qr_ref, kc_ref, vct_ref, ovt_ref, ks_ref, vst_ref, kw_ref, vwt_ref, bg_ref,
                o_ref, cmp_ref, sel_ref, qz_ref, qrz_ref, mw_ref, alw_ref, accw_ref,
                sbuf_ref, m_ref, al_ref, acc_ref, *, n_sel):
    i = pl.program_id(1)
    row_minus_col = (lax.broadcasted_iota(jnp.int32, (KC, TQ), 0)
                     - lax.broadcasted_iota(jnp.int32, (KC, TQ), 1))
    tq = i * TQ + lax.broadcasted_iota(jnp.int32, (1, TQ), 1)
    hpg = NSA_HEADS // NSA_GROUPS
    nc = kc_ref.shape[1]
    n_blk = sel_ref.shape[1]
    gw = 2 * HEAD_DIM

    _store_half_masked(q_ref, qz_ref, NSA_HEADS // 2)
    _store_half_masked(qr_ref, qrz_ref, NSA_HEADS // 2)

    def compressed_and_select():
        cmp_valid = CMP_STRIDE * lax.broadcasted_iota(jnp.int32, (nc, TQ), 0) + (CMP_LEN - 1) <= tq
        blk = lax.broadcasted_iota(jnp.int32, (n_blk, TQ), 0)
        forced = jnp.logical_or(blk == jnp.right_shift(tq, int(math.log2(SLC_BLOCK))), blk == 0)
        blk_causal = blk * SLC_BLOCK <= tq
        for g in range(NSA_GROUPS):
            kcg = kc_ref[g]
            vcg = vct_ref[NSA_GROUPS + g]
            imp = jnp.zeros((n_blk, TQ), F32)
            for hh in range(hpg):
                h = hpg * g + hh
                s = jnp.where(cmp_valid, _dot_nt(kcg, qz_ref[h]), NEG_INF)
                m = jnp.maximum(jnp.max(s, axis=0, keepdims=True), M_INIT)
                p = jnp.exp(s - m)
                l = jnp.sum(p, axis=0, keepdims=True)
                pc = (p * jnp.where(l > 0.0, 1.0 / l, 0.0)).astype(BF16)
                cmp_ref[HEAD_DIM * h:HEAD_DIM * (h + 1), :] = _dot(vcg, pc)
                imp = imp + _dot(ovt_ref[...], pc)
            imp = jnp.where(forced, FORCE_SCORE, imp)
            imp = jnp.where(blk_causal, imp, NEG_INF)
            n_tiles = n_blk // SUBLANES
            sub = lax.broadcasted_iota(jnp.int32, (SUBLANES, TQ), 0)
            tiles = [imp[SUBLANES * t:SUBLANES * (t + 1)] for t in range(n_tiles)]
            ranks = [jnp.zeros((SUBLANES, TQ), F32)] * n_tiles
            for mp in range(n_blk):
                r = jnp.broadcast_to(imp[mp:mp + 1, :], (SUBLANES, TQ))
                for t in range(n_tiles):
                    if SUBLANES * t > mp:
                        ahead = r >= tiles[t]
                    elif SUBLANES * (t + 1) <= mp:
                        ahead = r > tiles[t]
                    else:
                        later = sub > mp - SUBLANES * t
                        ahead = jnp.logical_or(r > tiles[t], jnp.logical_and(r == tiles[t], later))
                    ranks[t] = ranks[t] + jnp.where(ahead, 1.0, 0.0)
            rank = jnp.concatenate(ranks, axis=0)
            sel_ref[g] = jnp.where(rank < float(n_sel), 0.0, NEG_INF)

    bpc = KC // SLC_BLOCK

    def slc_prep(j):
        off = pl.multiple_of(j * KC, KC)
        visible = row_minus_col <= (i - j) * KC
        biases = []
        for g in range(NSA_GROUPS):
            bias = jnp.concatenate(
                [jnp.broadcast_to(sel_ref[g, pl.ds(bpc * j + b, 1), :], (SLC_BLOCK, TQ))
                 for b in range(bpc)], axis=0)
            biases.append(jnp.where(visible, bias, NEG_INF))
        return off, biases

    def slc_score(ctx, h):
        off, biases = ctx
        g = h // hpg
        return _dot_nt(ks_ref[pl.ds(off, KC), gw * g:gw * (g + 1)], qrz_ref[h]) + biases[g]

    def slc_values(j, h):
        g = h // hpg
        return vst_ref[HV * g:HV * (g + 1), pl.ds(pl.multiple_of(j * KC, KC), KC)]

    def win_prep(j):
        off = pl.multiple_of(j * KC, KC)
        d = (i - j) * KC
        inside = jnp.logical_and(row_minus_col <= d, row_minus_col > d - WINDOW)
        return off, jnp.where(inside, 0.0, NEG_INF)

    def win_score(ctx, h):
        off, bias = ctx
        g = h // hpg
        return _dot_nt(kw_ref[pl.ds(off, KC), gw * g:gw * (g + 1)], qrz_ref[h]) + bias

    def win_values(j, h):
        g = h // hpg
        return vwt_ref[HV * g:HV * (g + 1), pl.ds(pl.multiple_of(j * KC, KC), KC)]

    compressed_and_select()
    slc = _Stage(0, i, NSA_HEADS, HV, slc_prep, slc_score, slc_values, m_ref, al_ref, acc_ref)
    win = _Stage(jnp.maximum(i - WINDOW // KC, 0), i, NSA_HEADS, HV, win_prep, win_score, win_values,
                 mw_ref, alw_ref, accw_ref)
    _attn_chain([slc, win], sbuf_ref)

    def head_out(ref, h):
        return ref[HV * h:HV * h + HEAD_DIM, :] / ref[HV * h + HEAD_DIM:HV * h + HEAD_DIM + 1, :]

    gb = _sigmoid(bg_ref[...])
    for hp in range(NSA_HEADS // 2):
        parts = []
        for h in (2 * hp, 2 * hp + 1):
            parts.append(gb[3 * h:3 * h + 1, :] * cmp_ref[HEAD_DIM * h:HEAD_DIM * (h + 1), :]
                         + gb[3 * h + 1:3 * h + 2, :] * head_out(acc_ref, h)
                         + gb[3 * h + 2:3 * h + 3, :] * head_out(accw_ref, h))
        o_ref[:, LANES * hp:LANES * (hp + 1)] = jnp.concatenate(parts, axis=0).T


def _nsa_call(qc, qcr, kcc, vct, ovt, ks, vst, kw, vwt, bgt, n_sel):
    B, S, W = qc.shape
    n_blk = S // SLC_BLOCK
    qspec = pl.BlockSpec((None, TQ, W), lambda b, i: (b, i, 0))
    return pl.pallas_call(
        functools.partial(_nsa_kernel, n_sel=n_sel),
        grid=(B, S // TQ),
        in_specs=[
            qspec, qspec,
            pl.BlockSpec((None,) + kcc.shape[1:], lambda b, i: (b, 0, 0, 0)),
            pl.BlockSpec((None,) + vct.shape[1:], lambda b, i: (b, 0, 0, 0)),
            pl.BlockSpec(ovt.shape, lambda b, i: (0, 0)),
            pl.BlockSpec((None, S, 2 * LANES), lambda b, i: (b, 0, 0)),
            pl.BlockSpec((None, NSA_GROUPS * HV, S), lambda b, i: (b, 0, 0)),
            pl.BlockSpec((None, S, 2 * LANES), lambda b, i: (b, 0, 0)),
            pl.BlockSpec((None, NSA_GROUPS * HV, S), lambda b, i: (b, 0, 0)),
            pl.BlockSpec((None, 3 * NSA_HEADS, TQ), lambda b, i: (b, 0, i)),
        ],
        out_specs=qspec,
        out_shape=jax.ShapeDtypeStruct((B, S, W), F32),
        scratch_shapes=[pltpu.VMEM((W, TQ), F32), pltpu.VMEM((NSA_GROUPS, n_blk, TQ), F32),
                        pltpu.VMEM((NSA_HEADS, TQ, LANES), BF16), pltpu.VMEM((NSA_HEADS, TQ, LANES), BF16)]
        + _attn_scratch(NSA_HEADS, HV)[1:] + _attn_scratch(NSA_HEADS, HV),
        compiler_params=pltpu.CompilerParams(
            dimension_semantics=("arbitrary", "arbitrary"), vmem_limit_bytes=VMEM_LIMIT),
        name="nsa_attn",
    )(qc, qcr, kcc, vct, ovt, ks, vst, kw, vwt, bgt)


def _post_kernel(x_ref, oa_ref, ob_ref, oc_ref, p_ref, g_ref, wg_ref, wup_ref, wout_ref,
                 pg_ref, wple_ref, wpg_ref, fg_ref, o_ref, *, final):
    x = x_ref[...]
    h = _rms(x, g_ref[...]).astype(BF16)
    mixed = jnp.zeros(x.shape, F32)
    for n, br_ref in enumerate((oa_ref, ob_ref, oc_ref)):
        gate = _dot(h, wg_ref[:, BRANCH_WIDTH * n:BRANCH_WIDTH * (n + 1)])
        og = br_ref[...] * (gate * _sigmoid(gate))
        up = _dot(og.astype(BF16), wup_ref[n])
        c0 = N_BRANCH * BRANCH_WIDTH + D_MODEL * n
        mixed = mixed + _sigmoid(_dot(h, wg_ref[:, c0:c0 + D_MODEL])) * up
    x1 = x + _dot(mixed.astype(BF16), wout_ref[...])
    emb = _dot(p_ref[...].astype(BF16), wple_ref[...])
    hg = _dot(_rms(x1, pg_ref[...]).astype(BF16), wpg_ref[...])
    x2 = x1 + emb * _sigmoid(hg)
    if final:
        x2 = _rms(x2, fg_ref[...])
    o_ref[...] = x2


def _post_call(x, oa, ob, oc, p, g, wg, wup, wout, pg, wple, wpg, fg, final, layer):
    T, D = x.shape
    tm = TM_POST
    tok = lambda w: pl.BlockSpec((tm, w), lambda i: (i, 0))
    full = lambda a: _layer_spec(a, layer, pipeline_mode=pl.Buffered(1))
    return pl.pallas_call(
        functools.partial(_post_kernel, final=final),
        grid=(T // tm,),
        in_specs=[tok(D), tok(BRANCH_WIDTH), tok(BRANCH_WIDTH), tok(BRANCH_WIDTH),
                  pl.BlockSpec((None, tm, PLE_DIM), lambda i: (layer, i, 0)),
                  full(g), full(wg), full(wup), full(wout), full(pg), full(wple), full(wpg),
                  pl.BlockSpec(fg.shape, lambda i: (0, 0))],
        out_specs=tok(D),
        out_shape=jax.ShapeDtypeStruct((T, D), F32),
        compiler_params=pltpu.CompilerParams(
            dimension_semantics=("arbitrary",), vmem_limit_bytes=VMEM_LIMIT),
        name="post_mix",
    )(x, oa, ob, oc, p, g, wg, wup, wout, pg, wple, wpg, fg)


def _rope_tables(positions):
    inv = ROPE_THETA ** (-jnp.arange(0, ROT_DIM, 2, dtype=F32) / ROT_DIM)
    ang = positions.astype(F32)[..., None] * inv
    cos, sin = jnp.cos(ang), jnp.sin(ang)
    half = ROT_DIM // 2
    pad = HEAD_DIM - ROT_DIM
    zeros = jnp.zeros(cos.shape[:-1] + (half,), F32)
    rc = jnp.concatenate([cos, cos, jnp.ones(cos.shape[:-1] + (pad,), F32)], axis=-1)
    rs1 = jnp.concatenate([-sin, zeros, jnp.zeros(cos.shape[:-1] + (pad,), F32)], axis=-1)
    rs2 = jnp.concatenate([zeros, sin, jnp.zeros(cos.shape[:-1] + (pad,), F32)], axis=-1)
    rep = lambda t: jnp.concatenate([t] * (LANES // HEAD_DIM), axis=-1)
    return rep(rc), rep(rs1), rep(rs2)


def _proj_weights(w_in):
    w = w_in.astype(BF16)
    sl = lambda o, n: w[..., o:o + n]
    dup = lambda o: jnp.concatenate([sl(o, HEAD_DIM)] * 2, axis=-1)
    nat = jnp.concatenate([
        sl(_A_Q, 512), sl(_A_K, 512), sl(_B_Q, 512), sl(_I_Q, 512), sl(_C_Q, 512),
        dup(_B_K), dup(_I_K),
        dup(_C_KS), dup(_C_KS + HEAD_DIM), dup(_C_KW), dup(_C_KW + HEAD_DIM),
        sl(_C_KC, 128), sl(_C_VC, 128)], axis=-1)
    tr = jnp.concatenate([
        sl(_A_V, 512), sl(_B_V, 64), sl(_C_VS, 128), sl(_C_VW, 128), sl(_I_W, 8), sl(_C_BG, 24)], axis=-1)
    gates = jnp.concatenate([sl(_A_G, 512), sl(_B_G, 512), sl(_C_G, 512), sl(_MERGE, 3 * D_MODEL)], axis=-1)
    return nat, jnp.swapaxes(tr, 1, 2), gates


def _compress_weights(pe_k, w1_k, w2_k, pe_v, w1_v, w2_v):
    L = pe_k.shape[0]
    half = CMP_LEN // 2
    kdim = half * NSA_GROUPS * HEAD_DIM
    pe = jnp.stack([pe_k, pe_v], axis=1)
    pe = jnp.broadcast_to(pe[:, :, :, None, :], (L, 2, CMP_LEN, NSA_GROUPS, HEAD_DIM))
    w1 = jnp.stack([w1_k, w1_v], axis=1).astype(BF16)
    zero = jnp.zeros_like(w1)
    w1g = jnp.stack([jnp.stack([w1 if h == g else zero for h in range(NSA_GROUPS)], axis=3)
                     for g in range(NSA_GROUPS)], axis=2)
    w1g = w1g.reshape(L, 2 * NSA_GROUPS, CMP_LEN, NSA_GROUPS * HEAD_DIM, CMP_HIDDEN)
    w2 = jnp.stack([w2_k, w2_v], axis=1).astype(BF16)
    return (pe[:, :, :half].reshape(L, 2, 1, kdim), pe[:, :, half:].reshape(L, 2, 1, kdim),
            w1g[:, :, :half].reshape(L, 2 * NSA_GROUPS, kdim, CMP_HIDDEN),
            w1g[:, :, half:].reshape(L, 2 * NSA_GROUPS, kdim, CMP_HIDDEN), w2)


def _overlap_t(S):
    nc = S // CMP_STRIDE
    n_blk = S // SLC_BLOCK
    cstart = np.arange(nc) * CMP_STRIDE
    sstart = np.arange(n_blk) * SLC_BLOCK
    ov = ((cstart[None, :] < sstart[:, None] + SLC_BLOCK) & (cstart[None, :] + CMP_LEN > sstart[:, None]))
    return jnp.asarray(ov.astype(np.float32), dtype=BF16)


def kernel(x, p, positions, norm_g, w_in, diff_lambda, diff_subln_g, cmp_pe_k, cmp_w1_k, cmp_w2_k,
           cmp_pe_v, cmp_w1_v, cmp_w2_v, w_up, w_out, ple_norm_g, w_ple, w_ple_gate, final_norm_g):
    B, S, D = x.shape
    depth = w_in.shape[0]
    k_sel = min(DSA_TOPK_MAX, S // 4)
    n_sel = min(SLC_TOPN_MAX, S // SLC_BLOCK)
    assert S % TM_PROJ == 0 and S % TQ == 0 and k_sel <= TQ and D == D_MODEL
    assert KC == TQ and WINDOW % KC == 0
    rc, rs1, rs2 = _rope_tables(positions)
    ovt = _overlap_t(S)
    half_len = CMP_LEN // 2
    kdim = half_len * NSA_GROUPS * HEAD_DIM

    wn, wt, wg = _proj_weights(w_in)
    cmp_w = _compress_weights(cmp_pe_k, cmp_w1_k, cmp_w2_k, cmp_pe_v, cmp_w1_v, cmp_w2_v)
    g_in = norm_g.reshape(depth, 1, D)
    g_ple = ple_norm_g.reshape(depth, 1, D)
    g_sub = diff_subln_g.reshape(depth, -1, 1)
    wup, wout, wple, wpg = (w.astype(BF16) for w in (w_up, w_out, w_ple, w_ple_gate))
    p2 = p.reshape(depth, B * S, PLE_DIM)

    for i in range(depth):
        lam_init = 0.8 - 0.6 * math.exp(-0.3 * i)
        (qa, ka, qb, iq, qc, qcr, kb, ik, ks, kw, ck, cv,
         vat, vbt, vst, vwt, iwt, bgt) = _proj_call(x, g_in, wn, wt, rc, rs1, rs2, i)

        rk = ck.reshape(B, S // half_len, kdim)
        rv = cv.reshape(B, S // half_len, kdim)
        kcc, vct = _compress_call(rk, rv, *cmp_w, i)

        oa = _diff_call(qa, ka, vat, diff_lambda, g_sub, lam_init, i)
        ob = _dsa_call(qb, iq, kb, ik, vbt, iwt, k_sel)
        oc = _nsa_call(qc, qcr, kcc, vct, ovt, ks, vst, kw, vwt, bgt, n_sel)

        x = _post_call(
            x.reshape(B * S, D), oa.reshape(B * S, -1), ob.reshape(B * S, -1), oc.reshape(B * S, -1),
            p2, g_in, wg, wup, wout, g_ple, wple, wpg, final_norm_g.reshape(1, D),
            final=(i == depth - 1), layer=i).reshape(B, S, D)
    return x
```

```python
import functools
import math

import numpy as np
import jax
import jax.numpy as jnp
from jax import lax
from jax.experimental import pallas as pl
from jax.experimental.pallas import tpu as pltpu

F32 = jnp.float32
BF16 = jnp.bfloat16

D_MODEL = 1024
HEAD_DIM = 64
ROT_DIM = 16
ROPE_THETA = 500000.0
EPS = 1e-6
PLE_DIM = 256
DA_HEADS = 4
DSA_HEADS = 8
IDX_HEADS = 8
DSA_TOPK_MAX = 256
NSA_GROUPS = 2
NSA_HEADS = 8
CMP_LEN = 32
CMP_STRIDE = 16
CMP_HIDDEN = 128
SLC_BLOCK = 64
SLC_TOPN_MAX = 16
WINDOW = 512
FORCE_SCORE = 1e4
BRANCH_WIDTH = 512
N_BRANCH = 3

IN_SPLITS = (512, 512, 512, 512, 512, 64, 64, 512, 512, 64, 8,
             512, 128, 128, 128, 128, 128, 128, 512, 24, 3 * D_MODEL)
(_A_Q, _A_K, _A_V, _A_G, _B_Q, _B_K, _B_V, _B_G, _I_Q, _I_K, _I_W,
 _C_Q, _C_KC, _C_VC, _C_KS, _C_VS, _C_KW, _C_VW, _C_G, _C_BG, _MERGE) = [
    int(v) for v in np.concatenate([[0], np.cumsum(IN_SPLITS)[:-1]])]

LANES = 128
SUBLANES = 8
TQ = 512
KC = 512
TM_PROJ = 512
TM_POST = 512
NEG_INF = float("-inf")
M_INIT = -1e30
N_NAT = 3584
N_TR = 864
BISECT_PLAIN = 10
ONES_ROWS = 16
HV = HEAD_DIM + ONES_ROWS
HV_A = 2 * HEAD_DIM + ONES_ROWS
LOG2E = math.log2(math.e)
VMEM_LIMIT = 56 * 1024 * 1024


def _dot_nt(a, b):
    return lax.dot_general(a, b, (((1,), (1,)), ((), ())), preferred_element_type=F32)


def _dot(a, b):
    return jnp.dot(a, b, preferred_element_type=F32)


def _rms(x, g):
    return x * lax.rsqrt(jnp.mean(x * x, axis=-1, keepdims=True) + EPS) * g


def _sigmoid(x):
    return 1.0 / (1.0 + jnp.exp(-x))


def _store_half_masked(src_ref, dst_ref, n_pairs):
    lane = lax.broadcasted_iota(jnp.int32, (src_ref.shape[0], LANES), 1)
    for hp in range(n_pairs):
        x = src_ref[:, LANES * hp:LANES * (hp + 1)]
        z = jnp.zeros_like(x)
        dst_ref[2 * hp] = jnp.where(lane < HEAD_DIM, x, z)
        dst_ref[2 * hp + 1] = jnp.where(lane >= HEAD_DIM, x, z)


def _fold_rows(x, op):
    n_acc = 2
    accs = [x[SUBLANES * r:SUBLANES * (r + 1)] for r in range(n_acc)]
    for r in range(n_acc, x.shape[0] // SUBLANES):
        accs[r % n_acc] = op(accs[r % n_acc], x[SUBLANES * r:SUBLANES * (r + 1)])
    return op(accs[0], accs[1])


HALF = KC // 2


def _scores(k, q, dead=None):
    if dead is None:
        return _dot_nt(k, q)
    neg = jnp.full((HALF, HALF), NEG_INF, F32)
    if dead == "bl":
        return jnp.concatenate(
            [_dot_nt(k[:HALF], q), jnp.concatenate([neg, _dot_nt(k[HALF:], q[HALF:])], axis=1)], axis=0)
    return jnp.concatenate(
        [jnp.concatenate([_dot_nt(k[:HALF], q[:HALF]), neg], axis=1), _dot_nt(k[HALF:], q)], axis=0)


def _weighted_values(v_t, p, dead=None):
    if dead is None:
        return _dot(v_t, p)
    if dead == "bl":
        return jnp.concatenate([_dot(v_t[:, :HALF], p[:HALF, :HALF]), _dot(v_t, p[:, HALF:])], axis=1)
    return jnp.concatenate([_dot(v_t, p[:, :HALF]), _dot(v_t[:, HALF:], p[HALF:, HALF:])], axis=1)


class _Stage:
    def __init__(self, j0, j1, nh, hv, prep, score, values, m_ref, al_ref, acc_ref,
                 dead_first=None, dead_last=None):
        self.j0 = jnp.asarray(j0, jnp.int32)
        self.j1 = jnp.asarray(j1, jnp.int32)
        self.nh, self.hv, self.prep, self.score, self.values = nh, hv, prep, score, values
        self.m_ref, self.al_ref, self.acc_ref = m_ref, al_ref, acc_ref
        self.dead_first, self.dead_last = dead_first, dead_last

    def reset(self):
        self.m_ref[...] = jnp.full(self.m_ref.shape, M_INIT, F32)
        self.acc_ref[...] = jnp.zeros(self.acc_ref.shape, F32)

    def produce(self, sbuf_ref, j, dead=None):
        ctx = self.prep(j)
        for h in range(self.nh):
            s = self.score(ctx, h, dead)
            m_old = self.m_ref[h:h + 1, :]
            m_new = jnp.maximum(m_old, jnp.max(s, axis=0, keepdims=True))
            sbuf_ref[h] = s
            self.m_ref[h:h + 1, :] = m_new
            self.al_ref[h:h + 1, :] = jnp.exp2(m_old - m_new)

    def consume(self, sbuf_ref, j, dead=None):
        for h in range(self.nh):
            p = jnp.exp2(sbuf_ref[h] - self.m_ref[h:h + 1, :])
            rows = slice(self.hv * h, self.hv * (h + 1))
            self.acc_ref[rows, :] = (self.al_ref[h:h + 1, :] * self.acc_ref[rows, :]
                                     + _weighted_values(self.values(j, h), p.astype(BF16), dead))

    def loop(self, sbuf_ref, lo, hi):
        def body(j, carry):
            self.consume(sbuf_ref, j - 1)
            self.produce(sbuf_ref, j)
            return carry
        lax.fori_loop(lo, hi, body, 0)


def _attn_chain(stages, sbuf_ref, tile):
    for st in stages:
        st.reset()

    @pl.when(tile == 0)
    def _():
        for st in stages:
            st.produce(sbuf_ref, st.j0)
            st.consume(sbuf_ref, st.j0)

    @pl.when(tile > 0)
    def _():
        stages[0].produce(sbuf_ref, stages[0].j0, stages[0].dead_first)
        for n, st in enumerate(stages):
            st.loop(sbuf_ref, st.j0 + 1, st.j1)
            st.consume(sbuf_ref, st.j1 - 1, st.dead_first)
            st.produce(sbuf_ref, st.j1, st.dead_last)
            st.consume(sbuf_ref, st.j1, st.dead_last)
            if n + 1 < len(stages):
                stages[n + 1].produce(sbuf_ref, stages[n + 1].j0, stages[n + 1].dead_first)


def _attn_scratch(nh, hv):
    return [pltpu.VMEM((nh, KC, TQ), F32), pltpu.VMEM((nh, TQ), F32),
            pltpu.VMEM((nh, TQ), F32), pltpu.VMEM((nh * hv, TQ), F32)]


def _layer_spec(a, layer, **kw):
    tail = (0,) * (a.ndim - 1)
    return pl.BlockSpec((None,) + a.shape[1:], lambda *_: (layer,) + tail, **kw)


def _ones_rows(n_cols):
    r = lax.broadcasted_iota(jnp.int32, (ONES_ROWS, n_cols), 0)
    return jnp.where(r == 0, 1.0, 0.0).astype(BF16)


def _proj_kernel(x_ref, g_ref, wn_ref, wt_ref, rc_ref, rs1_ref, rs2_ref,
                 qa_ref, ka_ref, qb_ref, iq_ref, qc_ref, qcr_ref, kb_ref, ik_ref,
                 ks_ref, kw_ref, ck_ref, cv_ref,
                 vat_ref, vbt_ref, vst_ref, vwt_ref, iwt_ref, bgt_ref):
    h = _rms(x_ref[...], g_ref[...]).astype(BF16)
    rc, rs1, rs2 = rc_ref[...], rs1_ref[...], rs2_ref[...]

    def rope(z):
        return z * rc + pltpu.roll(z, LANES - ROT_DIM // 2, 1) * rs1 + pltpu.roll(z, ROT_DIM // 2, 1) * rs2

    qscale = HEAD_DIM ** -0.5
    qscale2 = qscale * LOG2E
    segs = (
        (0, 512, ((qa_ref, True, qscale2),)),
        (512, 512, ((ka_ref, True, 1.0),)),
        (1024, 512, ((qb_ref, True, qscale2),)),
        (1536, 512, ((iq_ref, True, qscale),)),
        (2048, 512, ((qc_ref, False, qscale), (qcr_ref, True, qscale2))),
        (2560, 128, ((kb_ref, True, 1.0),)),
        (2688, 128, ((ik_ref, True, 1.0),)),
        (2816, 256, ((ks_ref, True, 1.0),)),
        (3072, 256, ((kw_ref, True, 1.0),)),
    )
    zc = _dot(h, wn_ref[:, 3328:3328 + 2 * LANES])
    ck_ref[...] = zc[:, 0:LANES]
    cv_ref[...] = zc[:, LANES:2 * LANES]
    for c0, width, outs in segs:
        z = _dot(h, wn_ref[:, c0:c0 + width])
        for out_ref, rot, scale in outs:
            for j in range(width // LANES):
                zj = z[:, LANES * j:LANES * (j + 1)]
                if rot:
                    zj = rope(zj)
                if scale != 1.0:
                    zj = zj * scale
                out_ref[:, LANES * j:LANES * (j + 1)] = zj.astype(out_ref.dtype)

    zt = _dot_nt(wt_ref[...], h)
    ones = _ones_rows(zt.shape[1])
    vd = 2 * HEAD_DIM
    for h in range(DA_HEADS):
        vat_ref[HV_A * h:HV_A * h + vd, :] = zt[vd * h:vd * (h + 1)].astype(BF16)
        vat_ref[HV_A * h + vd:HV_A * (h + 1), :] = ones
    vbt_ref[0:HEAD_DIM, :] = zt[512:576].astype(BF16)
    vbt_ref[HEAD_DIM:HV, :] = ones
    for g in range(NSA_GROUPS):
        for ref, base in ((vst_ref, 576), (vwt_ref, 704)):
            ref[HV * g:HV * g + HEAD_DIM, :] = zt[base + HEAD_DIM * g:base + HEAD_DIM * (g + 1)].astype(BF16)
            ref[HV * g + HEAD_DIM:HV * (g + 1), :] = ones
    iwt_ref[...] = zt[832:840] * (IDX_HEADS ** -0.5)
    bgt_ref[...] = zt[840:864]


def _proj_call(x, g, wn, wt, rc, rs1, rs2, layer):
    B, S, D = x.shape
    tm = TM_PROJ
    nat = lambda w: pl.BlockSpec((None, tm, w), lambda b, i: (b, i, 0))
    tr = lambda r: pl.BlockSpec((None, r, tm), lambda b, i: (b, 0, i))
    full = lambda a: _layer_spec(a, layer)
    sds = jax.ShapeDtypeStruct
    out_shape = (
        sds((B, S, 512), BF16), sds((B, S, 512), BF16), sds((B, S, 512), BF16), sds((B, S, 512), BF16),
        sds((B, S, 512), BF16), sds((B, S, 512), BF16), sds((B, S, 128), BF16), sds((B, S, 128), BF16),
        sds((B, S, 256), BF16), sds((B, S, 256), BF16),
        sds((B, S, LANES), F32), sds((B, S, LANES), F32),
        sds((B, DA_HEADS * HV_A, S), BF16), sds((B, HV, S), BF16),
        sds((B, NSA_GROUPS * HV, S), BF16), sds((B, NSA_GROUPS * HV, S), BF16),
        sds((B, 8, S), F32), sds((B, 24, S), F32),
    )
    out_specs = (nat(512), nat(512), nat(512), nat(512), nat(512), nat(512), nat(128), nat(128),
                 nat(256), nat(256), nat(LANES), nat(LANES),
                 tr(DA_HEADS * HV_A), tr(HV), tr(NSA_GROUPS * HV), tr(NSA_GROUPS * HV), tr(8), tr(24))
    return pl.pallas_call(
        _proj_kernel,
        grid=(B, S // tm),
        in_specs=[nat(D), full(g), full(wn), full(wt), nat(LANES), nat(LANES), nat(LANES)],
        out_specs=out_specs,
        out_shape=out_shape,
        compiler_params=pltpu.CompilerParams(
            dimension_semantics=("arbitrary", "arbitrary"), vmem_limit_bytes=VMEM_LIMIT),
        name="in_proj",
    )(x, g, wn, wt, rc, rs1, rs2)


def _compress_kernel(rk_ref, rv_ref, pea_ref, peb_ref, w1a_ref, w1b_ref, w2_ref, nat_ref, t_ref):
    for kind, r_ref in enumerate((rk_ref, rv_ref)):
        r = r_ref[...]
        xa = (r + pea_ref[kind]).astype(BF16)
        xb = (r + peb_ref[kind]).astype(BF16)
        nrow = r.shape[0]
        for g in range(NSA_GROUPS):
            n = NSA_GROUPS * kind + g
            hid = jax.nn.gelu(_dot(xa, w1a_ref[n]) + pltpu.roll(_dot(xb, w1b_ref[n]), nrow - 1, 0))
            o = _dot(hid.astype(BF16), w2_ref[kind])
            nat_ref[n] = jnp.concatenate([o, o], axis=1).astype(BF16)
            t_ref[n] = o.T.astype(BF16)


def _compress_call(rk, rv, pea, peb, w1a, w1b, w2, layer):
    B, nr, kdim = rk.shape
    full = lambda a: _layer_spec(a, layer)
    rspec = pl.BlockSpec((None, nr, kdim), lambda b: (b, 0, 0))
    return pl.pallas_call(
        _compress_kernel,
        grid=(B,),
        in_specs=[rspec, rspec, full(pea), full(peb), full(w1a), full(w1b), full(w2)],
        out_specs=(
            pl.BlockSpec((None, 2 * NSA_GROUPS, nr, 2 * HEAD_DIM), lambda b: (b, 0, 0, 0)),
            pl.BlockSpec((None, 2 * NSA_GROUPS, HEAD_DIM, nr), lambda b: (b, 0, 0, 0)),
        ),
        out_shape=(jax.ShapeDtypeStruct((B, 2 * NSA_GROUPS, nr, 2 * HEAD_DIM), BF16),
                   jax.ShapeDtypeStruct((B, 2 * NSA_GROUPS, HEAD_DIM, nr), BF16)),
        compiler_params=pltpu.CompilerParams(dimension_semantics=("arbitrary",)),
        name="nsa_compress",
    )(rk, rv, pea, peb, w1a, w1b, w2)


def _diff_kernel(q_ref, k_ref, vt_ref, lam_ref, sg_ref, o_ref,
                 qz_ref, sbuf_ref, m_ref, al_ref, acc_ref, *, lam_init):
    i = pl.program_id(1)
    lp = lam_ref[...]
    lam = (jnp.exp(jnp.sum(lp[0:1] * lp[1:2], axis=1, keepdims=True))
           - jnp.exp(jnp.sum(lp[2:3] * lp[3:4], axis=1, keepdims=True)) + lam_init)
    row_minus_col = (lax.broadcasted_iota(jnp.int32, (KC, TQ), 0)
                     - lax.broadcasted_iota(jnp.int32, (KC, TQ), 1))
    _store_half_masked(q_ref, qz_ref, DA_HEADS)
    vdim = 2 * HEAD_DIM
    n_maps = 2 * DA_HEADS

    def prep(j):
        return pl.multiple_of(j * KC, KC), jnp.where(row_minus_col <= (i - j) * KC, 0.0, NEG_INF)

    def score(ctx, hm, dead):
        off, bias = ctx
        h = hm // 2
        return _scores(k_ref[pl.ds(off, KC), vdim * h:vdim * (h + 1)], qz_ref[hm], dead) + bias

    def values(j, hm):
        h = hm // 2
        return vt_ref[HV_A * h:HV_A * (h + 1), pl.ds(pl.multiple_of(j * KC, KC), KC)]

    _attn_chain([_Stage(0, i, n_maps, HV_A, prep, score, values, m_ref, al_ref, acc_ref, dead_last="bl")],
                sbuf_ref, i)

    for h in range(DA_HEADS):
        r0, r1 = HV_A * 2 * h, HV_A * (2 * h + 1)
        o_t = (acc_ref[r0:r0 + vdim, :] / acc_ref[r0 + vdim:r0 + vdim + 1, :]
               - lam * (acc_ref[r1:r1 + vdim, :] / acc_ref[r1 + vdim:r1 + vdim + 1, :]))
        ms = jnp.mean(o_t * o_t, axis=0, keepdims=True)
        y = o_t * lax.rsqrt(ms + EPS) * sg_ref[...] * (1.0 - lam_init)
        o_ref[:, vdim * h:vdim * (h + 1)] = y.T


def _diff_call(qa, ka, vat, lam_p, sg, lam_init, layer):
    B, S, W = qa.shape
    n_maps = 2 * DA_HEADS
    return pl.pallas_call(
        functools.partial(_diff_kernel, lam_init=lam_init),
        grid=(B, S // TQ),
        in_specs=[
            pl.BlockSpec((None, TQ, W), lambda b, i: (b, i, 0)),
            pl.BlockSpec((None, S, W), lambda b, i: (b, 0, 0)),
            pl.BlockSpec((None, DA_HEADS * HV_A, S), lambda b, i: (b, 0, 0)),
            _layer_spec(lam_p, layer),
            _layer_spec(sg, layer),
        ],
        out_specs=pl.BlockSpec((None, TQ, W), lambda b, i: (b, i, 0)),
        out_shape=jax.ShapeDtypeStruct((B, S, W), F32),
        scratch_shapes=[pltpu.VMEM((n_maps, TQ, LANES), BF16)] + _attn_scratch(n_maps, HV_A),
        compiler_params=pltpu.CompilerParams(
            dimension_semantics=("arbitrary", "arbitrary"), vmem_limit_bytes=VMEM_LIMIT),
        name="diff_attn",
    )(qa, ka, vat, lam_p, sg)


def _dsa_kernel(q_ref, iq_ref, kb_ref, ik_ref, vt_ref, iw_ref, o_ref,
                sc_ref, tj_ref, tri_ref, qz_ref, sbuf_ref, m_ref, al_ref, acc_ref, *, k_sel, seq_len):
    i = pl.program_id(1)
    nch = i + 1
    row = lax.broadcasted_iota(jnp.int32, (KC, TQ), 0)
    col = lax.broadcasted_iota(jnp.int32, (KC, TQ), 1)
    causal = row <= col
    iw = iw_ref[...]
    kf = float(k_sel)
    _store_half_masked(iq_ref, qz_ref, IDX_HEADS // 2)

    def idx_chunk(j, carry, masked):
        mx, mn = carry
        off = pl.multiple_of(j * KC, KC)
        ikc = ik_ref[pl.ds(off, KC), :]
        sc = jnp.zeros((KC, TQ), F32)
        for h in range(IDX_HEADS):
            dots = _scores(ikc, qz_ref[h], "bl" if masked else None)
            sc = sc + iw[h:h + 1, :] * jnp.maximum(dots, 0.0)
        lo_src = sc
        if masked:
            lo_src = jnp.where(causal, sc, jnp.inf)
            sc = jnp.where(causal, sc, NEG_INF)
        sc_ref[pl.ds(off, KC), :] = sc
        return (jnp.maximum(mx, _fold_rows(sc, jnp.maximum)),
                jnp.minimum(mn, _fold_rows(lo_src, jnp.minimum)))

    carry = (jnp.full((SUBLANES, TQ), NEG_INF, F32), jnp.full((SUBLANES, TQ), jnp.inf, F32))
    carry = lax.fori_loop(0, i, functools.partial(idx_chunk, masked=False), carry)
    mx8, mn8 = idx_chunk(i, carry, True)
    row_max = jnp.max(mx8, axis=0, keepdims=True)
    row_min = jnp.min(mn8, axis=0, keepdims=True)

    def probe_pass(th, snap):
        thb = jnp.broadcast_to(th, (SUBLANES, TQ))

        def body(j, c):
            off = pl.multiple_of(j * KC, KC)
            c = list(c)
            for r in range(KC // SUBLANES):
                x = sc_ref[pl.ds(off + SUBLANES * r, SUBLANES), :]
                ge = x >= thb
                c[0] = c[0] + jnp.where(ge, 1.0, 0.0)
                if snap:
                    c[1] = jnp.maximum(c[1], jnp.where(ge, NEG_INF, x))
            return tuple(c)
        init = (jnp.zeros((SUBLANES, TQ), F32),)
        if snap:
            init += (jnp.full((SUBLANES, TQ), NEG_INF, F32),)
        out = lax.fori_loop(0, nch, body, init)
        cnt = jnp.sum(out[0], axis=0, keepdims=True)
        return (cnt, jnp.max(out[1], axis=0, keepdims=True)) if snap else cnt

    n_causal = (i * TQ + 1 + lax.broadcasted_iota(jnp.int32, (1, TQ), 1)).astype(F32)
    keep_all = n_causal <= kf

    def bisect(_, c):
        lo, hi, clo, chi = c
        mid = lo + (hi - lo) * 0.5
        cnt = probe_pass(mid, False)
        ge = cnt >= kf
        return (jnp.where(ge, mid, lo), jnp.where(ge, hi, mid),
                jnp.where(ge, cnt, clo), jnp.where(ge, chi, cnt))

    hi0 = row_max + (row_max - row_min) + 1.0
    lo, hi, clo, chi = lax.fori_loop(
        0, BISECT_PLAIN, bisect, (row_min, hi0, n_causal, jnp.zeros((1, TQ), F32)))

    def cond(st):
        return jnp.logical_and(st[0] < 256, st[-1] > 0.0)

    def snap_step(st):
        it, lo, hi, clo, chi, hie, known, stuck, done, _ = st
        top = jnp.where(known > 0.0, hie, hi)
        mid = lo + (top - lo) * 0.5
        inside = jnp.logical_and(mid > lo, mid < top)
        near_top = jnp.logical_or(kf - chi <= 2.0, stuck > 0.0)
        use_top = jnp.logical_or(jnp.logical_and(known > 0.0, near_top), jnp.logical_not(inside))
        th = jnp.where(use_top, top, mid)
        cnt, edn = probe_pass(th, True)
        live = done <= 0.0
        ge = jnp.logical_and(live, cnt >= kf)
        lt = jnp.logical_and(live, cnt < kf)
        stuck = jnp.where(jnp.logical_and(ge, cnt == clo), 1.0, 0.0)
        lo = jnp.where(ge, th, lo)
        clo = jnp.where(ge, cnt, clo)
        hi = jnp.where(lt, th, hi)
        chi = jnp.where(lt, cnt, chi)
        hie = jnp.where(lt, edn, hie)
        known = jnp.where(lt, 1.0, known)
        fin = jnp.logical_or(clo == kf, jnp.logical_and(known > 0.0, lo >= hie))
        done = jnp.where(fin, 1.0, done)
        return it + 1, lo, hi, clo, chi, hie, known, stuck, done, jnp.max(1.0 - done)

    zero = jnp.zeros((1, TQ), F32)
    done0 = jnp.where(jnp.logical_or(clo == kf, keep_all), 1.0, 0.0)
    st = lax.while_loop(cond, snap_step, (jnp.int32(0), lo, hi, clo, chi, jnp.full((1, TQ), NEG_INF, F32),
                                          zero, zero, done0, jnp.max(1.0 - done0)))
    thr, clo, chi = st[1], st[3], st[4]
    need = jnp.where(clo == kf, float(seq_len), kf - chi)
    thr = jnp.where(keep_all, NEG_INF, thr)
    need = jnp.where(keep_all, 0.0, need)
    tj_ref[2:3, :] = jnp.zeros((1, TQ), F32)

    _store_half_masked(q_ref, qz_ref, DSA_HEADS // 2)
    half = KC // 2
    tri_ref[...] = jnp.where(lax.broadcasted_iota(jnp.int32, (half, half), 0)
                             >= lax.broadcasted_iota(jnp.int32, (half, half), 1), 1.0, 0.0).astype(BF16)

    def prep(j):
        off = pl.multiple_of(j * KC, KC)
        x = sc_ref[pl.ds(off, KC), :]
        eq = x == thr
        e = jnp.where(eq, 1.0, 0.0).astype(BF16)
        r_top = _dot(tri_ref[...], e[:half]) + tj_ref[2:3, :]
        r_bot = _dot(tri_ref[...], e[half:]) + r_top[half - 1:half, :]
        tj_ref[2:3, :] = r_bot[half - 1:half, :]
        rank = jnp.concatenate([r_top, r_bot], axis=0)
        tie_bias = jnp.where(jnp.logical_and(eq, rank <= need), 0.0, NEG_INF)
        return kb_ref[pl.ds(off, KC), :], jnp.where(x > thr, 0.0, tie_bias)

    def score(ctx, h, dead):
        kc, bias = ctx
        return _scores(kc, qz_ref[h], dead) + bias

    def values(j, h):
        return vt_ref[:, pl.ds(pl.multiple_of(j * KC, KC), KC)]

    _attn_chain([_Stage(0, i, DSA_HEADS, HV, prep, score, values, m_ref, al_ref, acc_ref, dead_last="bl")],
                sbuf_ref, i)
    for hp in range(DSA_HEADS // 2):
        blk = jnp.concatenate(
            [acc_ref[HV * h:HV * h + HEAD_DIM, :] / acc_ref[HV * h + HEAD_DIM:HV * h + HEAD_DIM + 1, :]
             for h in (2 * hp, 2 * hp + 1)], axis=0)
        o_ref[:, LANES * hp:LANES * (hp + 1)] = blk.T


def _dsa_call(qb, iq, kb, ik, vbt, iwt, k_sel):
    B, S, W = qb.shape
    return pl.pallas_call(
        functools.partial(_dsa_kernel, k_sel=k_sel, seq_len=S),
        grid=(B, S // TQ),
        in_specs=[
            pl.BlockSpec((None, TQ, W), lambda b, i: (b, i, 0)),
            pl.BlockSpec((None, TQ, W), lambda b, i: (b, i, 0)),
            pl.BlockSpec((None, S, LANES), lambda b, i: (b, 0, 0)),
            pl.BlockSpec((None, S, LANES), lambda b, i: (b, 0, 0)),
            pl.BlockSpec((None, HV, S), lambda b, i: (b, 0, 0)),
            pl.BlockSpec((None, IDX_HEADS, TQ), lambda b, i: (b, 0, i)),
        ],
        out_specs=pl.BlockSpec((None, TQ, W), lambda b, i: (b, i, 0)),
        out_shape=jax.ShapeDtypeStruct((B, S, W), F32),
        scratch_shapes=[pltpu.VMEM((S, TQ), F32), pltpu.VMEM((SUBLANES, TQ), F32),
                        pltpu.VMEM((KC // 2, KC // 2), BF16),
                        pltpu.VMEM((DSA_HEADS, TQ, LANES), BF16)] + _attn_scratch(DSA_HEADS, HV),
        compiler_params=pltpu.CompilerParams(
            dimension_semantics=("arbitrary", "arbitrary"), vmem_limit_bytes=VMEM_LIMIT),
        name="dsa_attn",
    )(qb, iq, kb, ik, vbt, iwt)


def _nsa_kernel(q_ref, qr_ref, kc_ref, vct_ref, ovt_ref, ks_ref, vst_ref, kw_ref, vwt_ref, bg_ref,
                o_ref, cmp_ref, sel_ref, qz_ref, qrz_ref, mw_ref, alw_ref, accw_ref,
                sbuf_ref, m_ref, al_ref, acc_ref, *, n_sel):
    i = pl.program_id(1)
    row_minus_col = (lax.broadcasted_iota(jnp.int32, (KC, TQ), 0)
                     - lax.broadcasted_iota(jnp.int32, (KC, TQ), 1))
    tq = i * TQ + lax.broadcasted_iota(jnp.int32, (1, TQ), 1)
    hpg = NSA_HEADS // NSA_GROUPS
    nc = kc_ref.shape[1]
    n_blk = sel_ref.shape[1]
    gw = 2 * HEAD_DIM

    _store_half_masked(q_ref, qz_ref, NSA_HEADS // 2)
    _store_half_masked(qr_ref, qrz_ref, NSA_HEADS // 2)

    def compressed_and_select():
        cmp_valid = CMP_STRIDE * lax.broadcasted_iota(jnp.int32, (nc, TQ), 0) + (CMP_LEN - 1) <= tq
        blk = lax.broadcasted_iota(jnp.int32, (n_blk, TQ), 0)
        forced = jnp.logical_or(blk == jnp.right_shift(tq, int(math.log2(SLC_BLOCK))), blk == 0)
        blk_causal = blk * SLC_BLOCK <= tq
        for g in range(NSA_GROUPS):
            kcg = kc_ref[g]
            vcg = vct_ref[NSA_GROUPS + g]
            imp = jnp.zeros((n_blk, TQ), F32)
            for hh in range(hpg):
                h = hpg * g + hh
                s = jnp.where(cmp_valid, _dot_nt(kcg, qz_ref[h]), NEG_INF)
                m = jnp.maximum(jnp.max(s, axis=0, keepdims=True), M_INIT)
                p = jnp.exp(s - m)
                l = jnp.sum(p, axis=0, keepdims=True)
                pc = (p * jnp.where(l > 0.0, 1.0 / l, 0.0)).astype(BF16)
                cmp_ref[HEAD_DIM * h:HEAD_DIM * (h + 1), :] = _dot(vcg, pc)
                imp = imp + _dot(ovt_ref[...], pc)
            imp = jnp.where(forced, FORCE_SCORE, imp)
            imp = jnp.where(blk_causal, imp, NEG_INF)
            n_tiles = n_blk // SUBLANES
            sub = lax.broadcasted_iota(jnp.int32, (SUBLANES, TQ), 0)
            tiles = [imp[SUBLANES * t:SUBLANES * (t + 1)] for t in range(n_tiles)]
            ranks = [jnp.zeros((SUBLANES, TQ), F32)] * n_tiles
            for mp in range(n_blk):
                r = jnp.broadcast_to(imp[mp:mp + 1, :], (SUBLANES, TQ))
                for t in range(n_tiles):
                    if SUBLANES * t > mp:
                        ahead = r >= tiles[t]
                    elif SUBLANES * (t + 1) <= mp:
                        ahead = r > tiles[t]
                    else:
                        later = sub > mp - SUBLANES * t
                        ahead = jnp.logical_or(r > tiles[t], jnp.logical_and(r == tiles[t], later))
                    ranks[t] = ranks[t] + jnp.where(ahead, 1.0, 0.0)
            rank = jnp.concatenate(ranks, axis=0)
            sel_ref[g] = jnp.where(rank < float(n_sel), 0.0, NEG_INF)

    bpc = KC // SLC_BLOCK

    def slc_prep(j):
        off = pl.multiple_of(j * KC, KC)
        visible = row_minus_col <= (i - j) * KC
        biases = []
        for g in range(NSA_GROUPS):
            bias = jnp.concatenate(
                [jnp.broadcast_to(sel_ref[g, pl.ds(bpc * j + b, 1), :], (SLC_BLOCK, TQ))
                 for b in range(bpc)], axis=0)
            biases.append(jnp.where(visible, bias, NEG_INF))
        return off, biases

    def slc_score(ctx, h, dead):
        off, biases = ctx
        g = h // hpg
        return _scores(ks_ref[pl.ds(off, KC), gw * g:gw * (g + 1)], qrz_ref[h], dead) + biases[g]

    def slc_values(j, h):
        g = h // hpg
        return vst_ref[HV * g:HV * (g + 1), pl.ds(pl.multiple_of(j * KC, KC), KC)]

    def win_prep(j):
        off = pl.multiple_of(j * KC, KC)
        d = (i - j) * KC
        inside = jnp.logical_and(row_minus_col <= d, row_minus_col > d - WINDOW)
        return off, jnp.where(inside, 0.0, NEG_INF)

    def win_score(ctx, h, dead):
        off, bias = ctx
        g = h // hpg
        return _scores(kw_ref[pl.ds(off, KC), gw * g:gw * (g + 1)], qrz_ref[h], dead) + bias

    def win_values(j, h):
        g = h // hpg
        return vwt_ref[HV * g:HV * (g + 1), pl.ds(pl.multiple_of(j * KC, KC), KC)]

    compressed_and_select()
    assert WINDOW == KC
    slc = _Stage(0, i, NSA_HEADS, HV, slc_prep, slc_score, slc_values, m_ref, al_ref, acc_ref,
                 dead_last="bl")
    win = _Stage(jnp.maximum(i - 1, 0), i, NSA_HEADS, HV, win_prep, win_score, win_values,
                 mw_ref, alw_ref, accw_ref, dead_first="tr", dead_last="bl")
    _attn_chain([slc, win], sbuf_ref, i)

    def head_out(ref, h):
        return ref[HV * h:HV * h + HEAD_DIM, :] / ref[HV * h + HEAD_DIM:HV * h + HEAD_DIM + 1, :]

    gb = _sigmoid(bg_ref[...])
    for hp in range(NSA_HEADS // 2):
        parts = []
        for h in (2 * hp, 2 * hp + 1):
            parts.append(gb[3 * h:3 * h + 1, :] * cmp_ref[HEAD_DIM * h:HEAD_DIM * (h + 1), :]
                         + gb[3 * h + 1:3 * h + 2, :] * head_out(acc_ref, h)
                         + gb[3 * h + 2:3 * h + 3, :] * head_out(accw_ref, h))
        o_ref[:, LANES * hp:LANES * (hp + 1)] = jnp.concatenate(parts, axis=0).T


def _nsa_call(qc, qcr, kcc, vct, ovt, ks, vst, kw, vwt, bgt, n_sel):
    B, S, W = qc.shape
    n_blk = S // SLC_BLOCK
    qspec = pl.BlockSpec((None, TQ, W), lambda b, i: (b, i, 0))
    return pl.pallas_call(
        functools.partial(_nsa_kernel, n_sel=n_sel),
        grid=(B, S // TQ),
        in_specs=[
            qspec, qspec,
            pl.BlockSpec((None,) + kcc.shape[1:], lambda b, i: (b, 0, 0, 0)),
            pl.BlockSpec((None,) + vct.shape[1:], lambda b, i: (b, 0, 0, 0)),
            pl.BlockSpec(ovt.shape, lambda b, i: (0, 0)),
            pl.BlockSpec((None, S, 2 * LANES), lambda b, i: (b, 0, 0)),
            pl.BlockSpec((None, NSA_GROUPS * HV, S), lambda b, i: (b, 0, 0)),
            pl.BlockSpec((None, S, 2 * LANES), lambda b, i: (b, 0, 0)),
            pl.BlockSpec((None, NSA_GROUPS * HV, S), lambda b, i: (b, 0, 0)),
            pl.BlockSpec((None, 3 * NSA_HEADS, TQ), lambda b, i: (b, 0, i)),
        ],
        out_specs=qspec,
        out_shape=jax.ShapeDtypeStruct((B, S, W), F32),
        scratch_shapes=[pltpu.VMEM((W, TQ), F32), pltpu.VMEM((NSA_GROUPS, n_blk, TQ), F32),
                        pltpu.VMEM((NSA_HEADS, TQ, LANES), BF16), pltpu.VMEM((NSA_HEADS, TQ, LANES), BF16)]
        + _attn_scratch(NSA_HEADS, HV)[1:] + _attn_scratch(NSA_HEADS, HV),
        compiler_params=pltpu.CompilerParams(
            dimension_semantics=("arbitrary", "arbitrary"), vmem_limit_bytes=VMEM_LIMIT),
        name="nsa_attn",
    )(qc, qcr, kcc, vct, ovt, ks, vst, kw, vwt, bgt)


def _post_kernel(x_ref, oa_ref, ob_ref, oc_ref, p_ref, g_ref, wg_ref, wup_ref, wout_ref,
                 pg_ref, wple_ref, wpg_ref, fg_ref, o_ref, *, final):
    x = x_ref[...]
    h = _rms(x, g_ref[...]).astype(BF16)
    mixed = jnp.zeros(x.shape, F32)
    for n, br_ref in enumerate((oa_ref, ob_ref, oc_ref)):
        gate = _dot(h, wg_ref[:, BRANCH_WIDTH * n:BRANCH_WIDTH * (n + 1)])
        og = br_ref[...] * (gate * _sigmoid(gate))
        up = _dot(og.astype(BF16), wup_ref[n])
        c0 = N_BRANCH * BRANCH_WIDTH + D_MODEL * n
        mixed = mixed + _sigmoid(_dot(h, wg_ref[:, c0:c0 + D_MODEL])) * up
    x1 = x + _dot(mixed.astype(BF16), wout_ref[...])
    emb = _dot(p_ref[...].astype(BF16), wple_ref[...])
    hg = _dot(_rms(x1, pg_ref[...]).astype(BF16), wpg_ref[...])
    x2 = x1 + emb * _sigmoid(hg)
    if final:
        x2 = _rms(x2, fg_ref[...])
    o_ref[...] = x2


def _post_call(x, oa, ob, oc, p, g, wg, wup, wout, pg, wple, wpg, fg, final, layer):
    T, D = x.shape
    tm = TM_POST
    tok = lambda w: pl.BlockSpec((tm, w), lambda i: (i, 0))
    full = lambda a: _layer_spec(a, layer, pipeline_mode=pl.Buffered(1))
    return pl.pallas_call(
        functools.partial(_post_kernel, final=final),
        grid=(T // tm,),
        in_specs=[tok(D), tok(BRANCH_WIDTH), tok(BRANCH_WIDTH), tok(BRANCH_WIDTH),
                  pl.BlockSpec((None, tm, PLE_DIM), lambda i: (layer, i, 0)),
                  full(g), full(wg), full(wup), full(wout), full(pg), full(wple), full(wpg),
                  pl.BlockSpec(fg.shape, lambda i: (0, 0))],
        out_specs=tok(D),
        out_shape=jax.ShapeDtypeStruct((T, D), F32),
        compiler_params=pltpu.CompilerParams(
            dimension_semantics=("arbitrary",), vmem_limit_bytes=VMEM_LIMIT),
        name="post_mix",
    )(x, oa, ob, oc, p, g, wg, wup, wout, pg, wple, wpg, fg)


def _rope_tables(positions):
    inv = ROPE_THETA ** (-jnp.arange(0, ROT_DIM, 2, dtype=F32) / ROT_DIM)
    ang = positions.astype(F32)[..., None] * inv
    cos, sin = jnp.cos(ang), jnp.sin(ang)
    half = ROT_DIM // 2
    pad = HEAD_DIM - ROT_DIM
    zeros = jnp.zeros(cos.shape[:-1] + (half,), F32)
    rc = jnp.concatenate([cos, cos, jnp.ones(cos.shape[:-1] + (pad,), F32)], axis=-1)
    rs1 = jnp.concatenate([-sin, zeros, jnp.zeros(cos.shape[:-1] + (pad,), F32)], axis=-1)
    rs2 = jnp.concatenate([zeros, sin, jnp.zeros(cos.shape[:-1] + (pad,), F32)], axis=-1)
    rep = lambda t: jnp.concatenate([t] * (LANES // HEAD_DIM), axis=-1)
    return rep(rc), rep(rs1), rep(rs2)


def _proj_weights(w_in):
    w = w_in.astype(BF16)
    sl = lambda o, n: w[..., o:o + n]
    dup = lambda o: jnp.concatenate([sl(o, HEAD_DIM)] * 2, axis=-1)
    nat = jnp.concatenate([
        sl(_A_Q, 512), sl(_A_K, 512), sl(_B_Q, 512), sl(_I_Q, 512), sl(_C_Q, 512),
        dup(_B_K), dup(_I_K),
        dup(_C_KS), dup(_C_KS + HEAD_DIM), dup(_C_KW), dup(_C_KW + HEAD_DIM),
        sl(_C_KC, 128), sl(_C_VC, 128)], axis=-1)
    tr = jnp.concatenate([
        sl(_A_V, 512), sl(_B_V, 64), sl(_C_VS, 128), sl(_C_VW, 128), sl(_I_W, 8), sl(_C_BG, 24)], axis=-1)
    gates = jnp.concatenate([sl(_A_G, 512), sl(_B_G, 512), sl(_C_G, 512), sl(_MERGE, 3 * D_MODEL)], axis=-1)
    return nat, jnp.swapaxes(tr, 1, 2), gates


def _compress_weights(pe_k, w1_k, w2_k, pe_v, w1_v, w2_v):
    L = pe_k.shape[0]
    half = CMP_LEN // 2
    kdim = half * NSA_GROUPS * HEAD_DIM
    pe = jnp.stack([pe_k, pe_v], axis=1)
    pe = jnp.broadcast_to(pe[:, :, :, None, :], (L, 2, CMP_LEN, NSA_GROUPS, HEAD_DIM))
    w1 = jnp.stack([w1_k, w1_v], axis=1).astype(BF16)
    zero = jnp.zeros_like(w1)
    w1g = jnp.stack([jnp.stack([w1 if h == g else zero for h in range(NSA_GROUPS)], axis=3)
                     for g in range(NSA_GROUPS)], axis=2)
    w1g = w1g.reshape(L, 2 * NSA_GROUPS, CMP_LEN, NSA_GROUPS * HEAD_DIM, CMP_HIDDEN)
    w2 = jnp.stack([w2_k, w2_v], axis=1).astype(BF16)
    return (pe[:, :, :half].reshape(L, 2, 1, kdim), pe[:, :, half:].reshape(L, 2, 1, kdim),
            w1g[:, :, :half].reshape(L, 2 * NSA_GROUPS, kdim, CMP_HIDDEN),
            w1g[:, :, half:].reshape(L, 2 * NSA_GROUPS, kdim, CMP_HIDDEN), w2)


def _overlap_t(S):
    nc = S // CMP_STRIDE
    n_blk = S // SLC_BLOCK
    cstart = np.arange(nc) * CMP_STRIDE
    sstart = np.arange(n_blk) * SLC_BLOCK
    ov = ((cstart[None, :] < sstart[:, None] + SLC_BLOCK) & (cstart[None, :] + CMP_LEN > sstart[:, None]))
    return jnp.asarray(ov.astype(np.float32), dtype=BF16)


def kernel(x, p, positions, norm_g, w_in, diff_lambda, diff_subln_g, cmp_pe_k, cmp_w1_k, cmp_w2_k,
           cmp_pe_v, cmp_w1_v, cmp_w2_v, w_up, w_out, ple_norm_g, w_ple, w_ple_gate, final_norm_g):
    B, S, D = x.shape
    depth = w_in.shape[0]
    k_sel = min(DSA_TOPK_MAX, S // 4)
    n_sel = min(SLC_TOPN_MAX, S // SLC_BLOCK)
    assert S % TM_PROJ == 0 and S % TQ == 0 and k_sel <= TQ and D == D_MODEL
    assert KC == TQ and WINDOW % KC == 0
    rc, rs1, rs2 = _rope_tables(positions)
    ovt = _overlap_t(S)
    half_len = CMP_LEN // 2
    kdim = half_len * NSA_GROUPS * HEAD_DIM

    wn, wt, wg = _proj_weights(w_in)
    cmp_w = _compress_weights(cmp_pe_k, cmp_w1_k, cmp_w2_k, cmp_pe_v, cmp_w1_v, cmp_w2_v)
    g_in = norm_g.reshape(depth, 1, D)
    g_ple = ple_norm_g.reshape(depth, 1, D)
    g_sub = diff_subln_g.reshape(depth, -1, 1)
    wup, wout, wple, wpg = (w.astype(BF16) for w in (w_up, w_out, w_ple, w_ple_gate))
    p2 = p.reshape(depth, B * S, PLE_DIM)

    for i in range(depth):
        lam_init = 0.8 - 0.6 * math.exp(-0.3 * i)
        (qa, ka, qb, iq, qc, qcr, kb, ik, ks, kw, ck, cv,
         vat, vbt, vst, vwt, iwt, bgt) = _proj_call(x, g_in, wn, wt, rc, rs1, rs2, i)

        rk = ck.reshape(B, S // half_len, kdim)
        rv = cv.reshape(B, S // half_len, kdim)
        kcc, vct = _compress_call(rk, rv, *cmp_w, i)

        oa = _diff_call(qa, ka, vat, diff_lambda, g_sub, lam_init, i)
        ob = _dsa_call(qb, iq, kb, ik, vbt, iwt, k_sel)
        oc = _nsa_call(qc, qcr, kcc, vct, ovt, ks, vst, kw, vwt, bgt, n_sel)

        x = _post_call(
            x.reshape(B * S, D), oa.reshape(B * S, -1), ob.reshape(B * S, -1), oc.reshape(B * S, -1),
            p2, g_in, wg, wup, wout, g_ple, wple, wpg, final_norm_g.reshape(1, D),
            final=(i == depth - 1), layer=i).reshape(B, S, D)
    return x
```

```python
import functools
import math

import numpy as np
import jax
import jax.numpy as jnp
from jax import lax
from jax.experimental import pallas as pl
from jax.experimental.pallas import tpu as pltpu

F32 = jnp.float32
BF16 = jnp.bfloat16

D_MODEL = 1024
HEAD_DIM = 64
ROT_DIM = 16
ROPE_THETA = 500000.0
EPS = 1e-6
PLE_DIM = 256
DA_HEADS = 4
DSA_HEADS = 8
IDX_HEADS = 8
DSA_TOPK_MAX = 256
NSA_GROUPS = 2
NSA_HEADS = 8
CMP_LEN = 32
CMP_STRIDE = 16
CMP_HIDDEN = 128
SLC_BLOCK = 64
SLC_TOPN_MAX = 16
WINDOW = 512
FORCE_SCORE = 1e4
BRANCH_WIDTH = 512
N_BRANCH = 3

IN_SPLITS = (512, 512, 512, 512, 512, 64, 64, 512, 512, 64, 8,
             512, 128, 128, 128, 128, 128, 128, 512, 24, 3 * D_MODEL)
(_A_Q, _A_K, _A_V, _A_G, _B_Q, _B_K, _B_V, _B_G, _I_Q, _I_K, _I_W,
 _C_Q, _C_KC, _C_VC, _C_KS, _C_VS, _C_KW, _C_VW, _C_G, _C_BG, _MERGE) = [
    int(v) for v in np.concatenate([[0], np.cumsum(IN_SPLITS)[:-1]])]

LANES = 128
SUBLANES = 8
TQ = 512
KC = 512
TM_PROJ = 512
TM_POST = 512
NEG_INF = float("-inf")
M_INIT = -1e30
N_NAT = 3584
N_TR = 864
BISECT_PLAIN = 10
ONES_ROWS = 16
HV = HEAD_DIM + ONES_ROWS
HV_A = 2 * HEAD_DIM + ONES_ROWS
LOG2E = math.log2(math.e)
VMEM_LIMIT = 56 * 1024 * 1024


def _dot_nt(a, b):
    return lax.dot_general(a, b, (((1,), (1,)), ((), ())), preferred_element_type=F32)


def _dot(a, b):
    return jnp.dot(a, b, preferred_element_type=F32)


def _rms(x, g):
    return x * lax.rsqrt(jnp.mean(x * x, axis=-1, keepdims=True) + EPS) * g


def _sigmoid(x):
    return 1.0 / (1.0 + jnp.exp(-x))


def _store_half_masked(src_ref, dst_ref, n_pairs):
    lane = lax.broadcasted_iota(jnp.int32, (src_ref.shape[0], LANES), 1)
    for hp in range(n_pairs):
        x = src_ref[:, LANES * hp:LANES * (hp + 1)]
        z = jnp.zeros_like(x)
        dst_ref[2 * hp] = jnp.where(lane < HEAD_DIM, x, z)
        dst_ref[2 * hp + 1] = jnp.where(lane >= HEAD_DIM, x, z)


def _fold_rows(x, op):
    n_acc = 2
    accs = [x[SUBLANES * r:SUBLANES * (r + 1)] for r in range(n_acc)]
    for r in range(n_acc, x.shape[0] // SUBLANES):
        accs[r % n_acc] = op(accs[r % n_acc], x[SUBLANES * r:SUBLANES * (r + 1)])
    return op(accs[0], accs[1])


HALF = KC // 2


def _scores(k, q, dead=None):
    if dead is None:
        return _dot_nt(k, q)
    neg = jnp.full((HALF, HALF), NEG_INF, F32)
    if dead == "bl":
        return jnp.concatenate(
            [_dot_nt(k[:HALF], q), jnp.concatenate([neg, _dot_nt(k[HALF:], q[HALF:])], axis=1)], axis=0)
    return jnp.concatenate(
        [jnp.concatenate([_dot_nt(k[:HALF], q[:HALF]), neg], axis=1), _dot_nt(k[HALF:], q)], axis=0)


def _weighted_values(v_t, p, dead=None):
    if dead is None:
        return _dot(v_t, p)
    if dead == "bl":
        return jnp.concatenate([_dot(v_t[:, :HALF], p[:HALF, :HALF]), _dot(v_t, p[:, HALF:])], axis=1)
    return jnp.concatenate([_dot(v_t, p[:, :HALF]), _dot(v_t[:, HALF:], p[HALF:, HALF:])], axis=1)


class _Stage:
    def __init__(self, j0, j1, nh, hv, prep, score, values, m_ref, al_ref, acc_ref,
                 dead_first=None, dead_last=None):
        self.j0 = jnp.asarray(j0, jnp.int32)
        self.j1 = jnp.asarray(j1, jnp.int32)
        self.nh, self.hv, self.prep, self.score, self.values = nh, hv, prep, score, values
        self.m_ref, self.al_ref, self.acc_ref = m_ref, al_ref, acc_ref
        self.dead_first, self.dead_last = dead_first, dead_last

    def reset(self):
        self.m_ref[...] = jnp.full(self.m_ref.shape, M_INIT, F32)
        self.acc_ref[...] = jnp.zeros(self.acc_ref.shape, F32)

    def produce(self, sbuf_ref, j, dead=None):
        ctx = self.prep(j)
        for h in range(self.nh):
            s = self.score(ctx, h, dead)
            m_old = self.m_ref[h:h + 1, :]
            m_new = jnp.maximum(m_old, jnp.max(s, axis=0, keepdims=True))
            sbuf_ref[h] = s
            self.m_ref[h:h + 1, :] = m_new
            self.al_ref[h:h + 1, :] = jnp.exp2(m_old - m_new)

    def consume(self, sbuf_ref, j, dead=None):
        for h in range(self.nh):
            p = jnp.exp2(sbuf_ref[h] - self.m_ref[h:h + 1, :])
            rows = slice(self.hv * h, self.hv * (h + 1))
            self.acc_ref[rows, :] = (self.al_ref[h:h + 1, :] * self.acc_ref[rows, :]
                                     + _weighted_values(self.values(j, h), p.astype(BF16), dead))

    def loop(self, sbuf_ref, lo, hi):
        def body(j, carry):
            self.consume(sbuf_ref, j - 1)
            self.produce(sbuf_ref, j)
            return carry
        lax.fori_loop(lo, hi, body, 0)


def _attn_chain(stages, sbuf_ref, tile):
    for st in stages:
        st.reset()

    @pl.when(tile == 0)
    def _():
        for st in stages:
            st.produce(sbuf_ref, st.j0)
            st.consume(sbuf_ref, st.j0)

    @pl.when(tile > 0)
    def _():
        stages[0].produce(sbuf_ref, stages[0].j0, stages[0].dead_first)
        for n, st in enumerate(stages):
            st.loop(sbuf_ref, st.j0 + 1, st.j1)
            st.consume(sbuf_ref, st.j1 - 1, st.dead_first)
            st.produce(sbuf_ref, st.j1, st.dead_last)
            st.consume(sbuf_ref, st.j1, st.dead_last)
            if n + 1 < len(stages):
                stages[n + 1].produce(sbuf_ref, stages[n + 1].j0, stages[n + 1].dead_first)


def _attn_scratch(nh, hv):
    return [pltpu.VMEM((nh, KC, TQ), F32), pltpu.VMEM((nh, TQ), F32),
            pltpu.VMEM((nh, TQ), F32), pltpu.VMEM((nh * hv, TQ), F32)]


def _layer_spec(a, layer, **kw):
    tail = (0,) * (a.ndim - 1)
    return pl.BlockSpec((None,) + a.shape[1:], lambda *_: (layer,) + tail, **kw)


def _ones_rows(n_cols):
    r = lax.broadcasted_iota(jnp.int32, (ONES_ROWS, n_cols), 0)
    return jnp.where(r == 0, 1.0, 0.0).astype(BF16)


def _proj_kernel(x_ref, g_ref, wn_ref, wt_ref, rc_ref, rs1_ref, rs2_ref,
                 qa_ref, ka_ref, qb_ref, iq_ref, qc_ref, qcr_ref, kb_ref, ik_ref,
                 ks_ref, kw_ref, ck_ref, cv_ref,
                 vat_ref, vbt_ref, vst_ref, vwt_ref, iwt_ref, bgt_ref):
    h = _rms(x_ref[...], g_ref[...]).astype(BF16)
    rc, rs1, rs2 = rc_ref[...], rs1_ref[...], rs2_ref[...]

    def rope(z):
        return z * rc + pltpu.roll(z, LANES - ROT_DIM // 2, 1) * rs1 + pltpu.roll(z, ROT_DIM // 2, 1) * rs2

    qscale = HEAD_DIM ** -0.5
    qscale2 = qscale * LOG2E
    segs = (
        (0, 512, ((qa_ref, True, qscale2),)),
        (512, 512, ((ka_ref, True, 1.0),)),
        (1024, 512, ((qb_ref, True, qscale2),)),
        (1536, 512, ((iq_ref, True, qscale),)),
        (2048, 512, ((qc_ref, False, qscale), (qcr_ref, True, qscale2))),
        (2560, 128, ((kb_ref, True, 1.0),)),
        (2688, 128, ((ik_ref, True, 1.0),)),
        (2816, 256, ((ks_ref, True, 1.0),)),
        (3072, 256, ((kw_ref, True, 1.0),)),
    )
    zc = _dot(h, wn_ref[:, 3328:3328 + 2 * LANES])
    ck_ref[...] = zc[:, 0:LANES]
    cv_ref[...] = zc[:, LANES:2 * LANES]
    for c0, width, outs in segs:
        z = _dot(h, wn_ref[:, c0:c0 + width])
        for out_ref, rot, scale in outs:
            for j in range(width // LANES):
                zj = z[:, LANES * j:LANES * (j + 1)]
                if rot:
                    zj = rope(zj)
                if scale != 1.0:
                    zj = zj * scale
                out_ref[:, LANES * j:LANES * (j + 1)] = zj.astype(out_ref.dtype)

    zt = _dot_nt(wt_ref[...], h)
    ones = _ones_rows(zt.shape[1])
    vd = 2 * HEAD_DIM
    for h in range(DA_HEADS):
        vat_ref[HV_A * h:HV_A * h + vd, :] = zt[vd * h:vd * (h + 1)].astype(BF16)
        vat_ref[HV_A * h + vd:HV_A * (h + 1), :] = ones
    vbt_ref[0:HEAD_DIM, :] = zt[512:576].astype(BF16)
    vbt_ref[HEAD_DIM:HV, :] = ones
    for g in range(NSA_GROUPS):
        for ref, base in ((vst_ref, 576), (vwt_ref, 704)):
            ref[HV * g:HV * g + HEAD_DIM, :] = zt[base + HEAD_DIM * g:base + HEAD_DIM * (g + 1)].astype(BF16)
            ref[HV * g + HEAD_DIM:HV * (g + 1), :] = ones
    iwt_ref[...] = zt[832:840] * (IDX_HEADS ** -0.5)
    bgt_ref[...] = zt[840:864]


def _proj_call(x, g, wn, wt, rc, rs1, rs2, layer):
    B, S, D = x.shape
    tm = TM_PROJ
    nat = lambda w: pl.BlockSpec((None, tm, w), lambda b, i: (b, i, 0))
    tr = lambda r: pl.BlockSpec((None, r, tm), lambda b, i: (b, 0, i))
    full = lambda a: _layer_spec(a, layer)
    sds = jax.ShapeDtypeStruct
    out_shape = (
        sds((B, S, 512), BF16), sds((B, S, 512), BF16), sds((B, S, 512), BF16), sds((B, S, 512), BF16),
        sds((B, S, 512), BF16), sds((B, S, 512), BF16), sds((B, S, 128), BF16), sds((B, S, 128), BF16),
        sds((B, S, 256), BF16), sds((B, S, 256), BF16),
        sds((B, S, LANES), F32), sds((B, S, LANES), F32),
        sds((B, DA_HEADS * HV_A, S), BF16), sds((B, HV, S), BF16),
        sds((B, NSA_GROUPS * HV, S), BF16), sds((B, NSA_GROUPS * HV, S), BF16),
        sds((B, 8, S), F32), sds((B, 24, S), F32),
    )
    out_specs = (nat(512), nat(512), nat(512), nat(512), nat(512), nat(512), nat(128), nat(128),
                 nat(256), nat(256), nat(LANES), nat(LANES),
                 tr(DA_HEADS * HV_A), tr(HV), tr(NSA_GROUPS * HV), tr(NSA_GROUPS * HV), tr(8), tr(24))
    return pl.pallas_call(
        _proj_kernel,
        grid=(B, S // tm),
        in_specs=[nat(D), full(g), full(wn), full(wt), nat(LANES), nat(LANES), nat(LANES)],
        out_specs=out_specs,
        out_shape=out_shape,
        compiler_params=pltpu.CompilerParams(
            dimension_semantics=("arbitrary", "arbitrary"), vmem_limit_bytes=VMEM_LIMIT),
        name="in_proj",
    )(x, g, wn, wt, rc, rs1, rs2)


def _compress_kernel(rk_ref, rv_ref, pea_ref, peb_ref, w1a_ref, w1b_ref, w2_ref, nat_ref, t_ref):
    for kind, r_ref in enumerate((rk_ref, rv_ref)):
        r = r_ref[...]
        xa = (r + pea_ref[kind]).astype(BF16)
        xb = (r + peb_ref[kind]).astype(BF16)
        nrow = r.shape[0]
        for g in range(NSA_GROUPS):
            n = NSA_GROUPS * kind + g
            hid = jax.nn.gelu(_dot(xa, w1a_ref[n]) + pltpu.roll(_dot(xb, w1b_ref[n]), nrow - 1, 0))
            o = _dot(hid.astype(BF16), w2_ref[kind])
            nat_ref[n] = jnp.concatenate([o, o], axis=1).astype(BF16)
            t_ref[n] = o.T.astype(BF16)


def _compress_call(rk, rv, pea, peb, w1a, w1b, w2, layer):
    B, nr, kdim = rk.shape
    full = lambda a: _layer_spec(a, layer)
    rspec = pl.BlockSpec((None, nr, kdim), lambda b: (b, 0, 0))
    return pl.pallas_call(
        _compress_kernel,
        grid=(B,),
        in_specs=[rspec, rspec, full(pea), full(peb), full(w1a), full(w1b), full(w2)],
        out_specs=(
            pl.BlockSpec((None, 2 * NSA_GROUPS, nr, 2 * HEAD_DIM), lambda b: (b, 0, 0, 0)),
            pl.BlockSpec((None, 2 * NSA_GROUPS, HEAD_DIM, nr), lambda b: (b, 0, 0, 0)),
        ),
        out_shape=(jax.ShapeDtypeStruct((B, 2 * NSA_GROUPS, nr, 2 * HEAD_DIM), BF16),
                   jax.ShapeDtypeStruct((B, 2 * NSA_GROUPS, HEAD_DIM, nr), BF16)),
        compiler_params=pltpu.CompilerParams(dimension_semantics=("arbitrary",)),
        name="nsa_compress",
    )(rk, rv, pea, peb, w1a, w1b, w2)


def _diff_kernel(q_ref, k_ref, vt_ref, lam_ref, sg_ref, o_ref,
                 qz_ref, sbuf_ref, m_ref, al_ref, acc_ref, *, lam_init):
    i = pl.program_id(1)
    lp = lam_ref[...]
    lam = (jnp.exp(jnp.sum(lp[0:1] * lp[1:2], axis=1, keepdims=True))
           - jnp.exp(jnp.sum(lp[2:3] * lp[3:4], axis=1, keepdims=True)) + lam_init)
    row_minus_col = (lax.broadcasted_iota(jnp.int32, (KC, TQ), 0)
                     - lax.broadcasted_iota(jnp.int32, (KC, TQ), 1))
    _store_half_masked(q_ref, qz_ref, DA_HEADS)
    vdim = 2 * HEAD_DIM
    n_maps = 2 * DA_HEADS

    def prep(j):
        return pl.multiple_of(j * KC, KC), jnp.where(row_minus_col <= (i - j) * KC, 0.0, NEG_INF)

    def score(ctx, hm, dead):
        off, bias = ctx
        h = hm // 2
        return _scores(k_ref[pl.ds(off, KC), vdim * h:vdim * (h + 1)], qz_ref[hm], dead) + bias

    def values(j, hm):
        h = hm // 2
        return vt_ref[HV_A * h:HV_A * (h + 1), pl.ds(pl.multiple_of(j * KC, KC), KC)]

    _attn_chain([_Stage(0, i, n_maps, HV_A, prep, score, values, m_ref, al_ref, acc_ref, dead_last="bl")],
                sbuf_ref, i)

    for h in range(DA_HEADS):
        r0, r1 = HV_A * 2 * h, HV_A * (2 * h + 1)
        o_t = (acc_ref[r0:r0 + vdim, :] / acc_ref[r0 + vdim:r0 + vdim + 1, :]
               - lam * (acc_ref[r1:r1 + vdim, :] / acc_ref[r1 + vdim:r1 + vdim + 1, :]))
        ms = jnp.mean(o_t * o_t, axis=0, keepdims=True)
        y = o_t * lax.rsqrt(ms + EPS) * sg_ref[...] * (1.0 - lam_init)
        o_ref[:, vdim * h:vdim * (h + 1)] = y.T


def _diff_call(qa, ka, vat, lam_p, sg, lam_init, layer):
    B, S, W = qa.shape
    n_maps = 2 * DA_HEADS
    return pl.pallas_call(
        functools.partial(_diff_kernel, lam_init=lam_init),
        grid=(B, S // TQ),
        in_specs=[
            pl.BlockSpec((None, TQ, W), lambda b, i: (b, i, 0)),
            pl.BlockSpec((None, S, W), lambda b, i: (b, 0, 0)),
            pl.BlockSpec((None, DA_HEADS * HV_A, S), lambda b, i: (b, 0, 0)),
            _layer_spec(lam_p, layer),
            _layer_spec(sg, layer),
        ],
        out_specs=pl.BlockSpec((None, TQ, W), lambda b, i: (b, i, 0)),
        out_shape=jax.ShapeDtypeStruct((B, S, W), F32),
        scratch_shapes=[pltpu.VMEM((n_maps, TQ, LANES), BF16)] + _attn_scratch(n_maps, HV_A),
        compiler_params=pltpu.CompilerParams(
            dimension_semantics=("arbitrary", "arbitrary"), vmem_limit_bytes=VMEM_LIMIT),
        name="diff_attn",
    )(qa, ka, vat, lam_p, sg)


def _dsa_kernel(q_ref, iq_ref, kb_ref, ik_ref, vt_ref, iw_ref, o_ref,
                sc_ref, tj_ref, tri_ref, qz_ref, sbuf_ref, m_ref, al_ref, acc_ref, *, k_sel, seq_len):
    i = pl.program_id(1)
    nch = i + 1
    row = lax.broadcasted_iota(jnp.int32, (KC, TQ), 0)
    col = lax.broadcasted_iota(jnp.int32, (KC, TQ), 1)
    causal = row <= col
    iw = iw_ref[...]
    kf = float(k_sel)
    _store_half_masked(iq_ref, qz_ref, IDX_HEADS // 2)

    def idx_chunk(j, carry, masked):
        mx, mn = carry
        off = pl.multiple_of(j * KC, KC)
        ikc = ik_ref[pl.ds(off, KC), :]
        sc = jnp.zeros((KC, TQ), F32)
        for h in range(IDX_HEADS):
            dots = _scores(ikc, qz_ref[h], "bl" if masked else None)
            sc = sc + iw[h:h + 1, :] * jnp.maximum(dots, 0.0)
        lo_src = sc
        if masked:
            lo_src = jnp.where(causal, sc, jnp.inf)
            sc = jnp.where(causal, sc, NEG_INF)
        sc_ref[pl.ds(off, KC), :] = sc
        return (jnp.maximum(mx, _fold_rows(sc, jnp.maximum)),
                jnp.minimum(mn, _fold_rows(lo_src, jnp.minimum)))

    carry = (jnp.full((SUBLANES, TQ), NEG_INF, F32), jnp.full((SUBLANES, TQ), jnp.inf, F32))
    carry = lax.fori_loop(0, i, functools.partial(idx_chunk, masked=False), carry)
    mx8, mn8 = idx_chunk(i, carry, True)
    row_max = jnp.max(mx8, axis=0, keepdims=True)
    row_min = jnp.min(mn8, axis=0, keepdims=True)

    def probe_pass(th, snap):
        thb = jnp.broadcast_to(th, (SUBLANES, TQ))

        def body(j, c):
            off = pl.multiple_of(j * KC, KC)
            c = list(c)
            for r in range(KC // SUBLANES):
                x = sc_ref[pl.ds(off + SUBLANES * r, SUBLANES), :]
                ge = x >= thb
                c[0] = c[0] + jnp.where(ge, 1.0, 0.0)
                if snap:
                    c[1] = jnp.maximum(c[1], jnp.where(ge, NEG_INF, x))
            return tuple(c)
        init = (jnp.zeros((SUBLANES, TQ), F32),)
        if snap:
            init += (jnp.full((SUBLANES, TQ), NEG_INF, F32),)
        out = lax.fori_loop(0, nch, body, init)
        cnt = jnp.sum(out[0], axis=0, keepdims=True)
        return (cnt, jnp.max(out[1], axis=0, keepdims=True)) if snap else cnt

    n_causal = (i * TQ + 1 + lax.broadcasted_iota(jnp.int32, (1, TQ), 1)).astype(F32)
    keep_all = n_causal <= kf

    def bisect(_, c):
        lo, hi, clo, chi = c
        mid = lo + (hi - lo) * 0.5
        cnt = probe_pass(mid, False)
        ge = cnt >= kf
        return (jnp.where(ge, mid, lo), jnp.where(ge, hi, mid),
                jnp.where(ge, cnt, clo), jnp.where(ge, chi, cnt))

    hi0 = row_max + (row_max - row_min) + 1.0
    lo, hi, clo, chi = lax.fori_loop(
        0, BISECT_PLAIN, bisect, (row_min, hi0, n_causal, jnp.zeros((1, TQ), F32)))

    def cond(st):
        return jnp.logical_and(st[0] < 256, st[-1] > 0.0)

    def snap_step(st):
        it, lo, hi, clo, chi, hie, known, stuck, done, _ = st
        top = jnp.where(known > 0.0, hie, hi)
        mid = lo + (top - lo) * 0.5
        inside = jnp.logical_and(mid > lo, mid < top)
        near_top = jnp.logical_or(kf - chi <= 2.0, stuck > 0.0)
        use_top = jnp.logical_or(jnp.logical_and(known > 0.0, near_top), jnp.logical_not(inside))
        th = jnp.where(use_top, top, mid)
        cnt, edn = probe_pass(th, True)
        live = done <= 0.0
        ge = jnp.logical_and(live, cnt >= kf)
        lt = jnp.logical_and(live, cnt < kf)
        stuck = jnp.where(jnp.logical_and(ge, cnt == clo), 1.0, 0.0)
        lo = jnp.where(ge, th, lo)
        clo = jnp.where(ge, cnt, clo)
        hi = jnp.where(lt, th, hi)
        chi = jnp.where(lt, cnt, chi)
        hie = jnp.where(lt, edn, hie)
        known = jnp.where(lt, 1.0, known)
        fin = jnp.logical_or(clo == kf, jnp.logical_and(known > 0.0, lo >= hie))
        done = jnp.where(fin, 1.0, done)
        return it + 1, lo, hi, clo, chi, hie, known, stuck, done, jnp.max(1.0 - done)

    zero = jnp.zeros((1, TQ), F32)
    done0 = jnp.where(jnp.logical_or(clo == kf, keep_all), 1.0, 0.0)
    st = lax.while_loop(cond, snap_step, (jnp.int32(0), lo, hi, clo, chi, jnp.full((1, TQ), NEG_INF, F32),
                                          zero, zero, done0, jnp.max(1.0 - done0)))
    thr, clo, chi = st[1], st[3], st[4]
    need = jnp.where(clo == kf, float(seq_len), kf - chi)
    thr = jnp.where(keep_all, NEG_INF, thr)
    need = jnp.where(keep_all, 0.0, need)
    tj_ref[2:3, :] = jnp.zeros((1, TQ), F32)

    _store_half_masked(q_ref, qz_ref, DSA_HEADS // 2)
    half = KC // 2
    tri_ref[...] = jnp.where(lax.broadcasted_iota(jnp.int32, (half, half), 0)
                             >= lax.broadcasted_iota(jnp.int32, (half, half), 1), 1.0, 0.0).astype(BF16)

    def prep(j):
        off = pl.multiple_of(j * KC, KC)
        x = sc_ref[pl.ds(off, KC), :]
        eq = x == thr
        e = jnp.where(eq, 1.0, 0.0).astype(BF16)
        r_top = _dot(tri_ref[...], e[:half]) + tj_ref[2:3, :]
        r_bot = _dot(tri_ref[...], e[half:]) + r_top[half - 1:half, :]
        tj_ref[2:3, :] = r_bot[half - 1:half, :]
        rank = jnp.concatenate([r_top, r_bot], axis=0)
        tie_bias = jnp.where(jnp.logical_and(eq, rank <= need), 0.0, NEG_INF)
        return kb_ref[pl.ds(off, KC), :], jnp.where(x > thr, 0.0, tie_bias)

    def score(ctx, h, dead):
        kc, bias = ctx
        return _scores(kc, qz_ref[h], dead) + bias

    def values(j, h):
        return vt_ref[:, pl.ds(pl.multiple_of(j * KC, KC), KC)]

    _attn_chain([_Stage(0, i, DSA_HEADS, HV, prep, score, values, m_ref, al_ref, acc_ref, dead_last="bl")],
                sbuf_ref, i)
    for hp in range(DSA_HEADS // 2):
        blk = jnp.concatenate(
            [acc_ref[HV * h:HV * h + HEAD_DIM, :] / acc_ref[HV * h + HEAD_DIM:HV * h + HEAD_DIM + 1, :]
             for h in (2 * hp, 2 * hp + 1)], axis=0)
        o_ref[:, LANES * hp:LANES * (hp + 1)] = blk.T


def _dsa_call(qb, iq, kb, ik, vbt, iwt, k_sel):
    B, S, W = qb.shape
    return pl.pallas_call(
        functools.partial(_dsa_kernel, k_sel=k_sel, seq_len=S),
        grid=(B, S // TQ),
        in_specs=[
            pl.BlockSpec((None, TQ, W), lambda b, i: (b, i, 0)),
            pl.BlockSpec((None, TQ, W), lambda b, i: (b, i, 0)),
            pl.BlockSpec((None, S, LANES), lambda b, i: (b, 0, 0)),
            pl.BlockSpec((None, S, LANES), lambda b, i: (b, 0, 0)),
            pl.BlockSpec((None, HV, S), lambda b, i: (b, 0, 0)),
            pl.BlockSpec((None, IDX_HEADS, TQ), lambda b, i: (b, 0, i)),
        ],
        out_specs=pl.BlockSpec((None, TQ, W), lambda b, i: (b, i, 0)),
        out_shape=jax.ShapeDtypeStruct((B, S, W), F32),
        scratch_shapes=[pltpu.VMEM((S, TQ), F32), pltpu.VMEM((SUBLANES, TQ), F32),
                        pltpu.VMEM((KC // 2, KC // 2), BF16),
                        pltpu.VMEM((DSA_HEADS, TQ, LANES), BF16)] + _attn_scratch(DSA_HEADS, HV),
        compiler_params=pltpu.CompilerParams(
            dimension_semantics=("arbitrary", "arbitrary"), vmem_limit_bytes=VMEM_LIMIT),
        name="dsa_attn",
    )(qb, iq, kb, ik, vbt, iwt)


def _nsa_kernel(q_ref, qr_ref, kc_ref, vct_ref, ovt_ref, ks_ref, vst_ref, kw_ref, vwt_ref, bg_ref,
                o_ref, cmp_ref, sel_ref, qz_ref, qrz_ref, mw_ref, alw_ref, accw_ref,
                sbuf_ref, m_ref, al_ref, acc_ref, *, n_sel):
    i = pl.program_id(1)
    row_minus_col = (lax.broadcasted_iota(jnp.int32, (KC, TQ), 0)
                     - lax.broadcasted_iota(jnp.int32, (KC, TQ), 1))
    tq = i * TQ + lax.broadcasted_iota(jnp.int32, (1, TQ), 1)
    hpg = NSA_HEADS // NSA_GROUPS
    nc = kc_ref.shape[1]
    n_blk = sel_ref.shape[1]
    gw = 2 * HEAD_DIM

    _store_half_masked(q_ref, qz_ref, NSA_HEADS // 2)
    _store_half_masked(qr_ref, qrz_ref, NSA_HEADS // 2)

    def compressed_and_select():
        cmp_valid = CMP_STRIDE * lax.broadcasted_iota(jnp.int32, (nc, TQ), 0) + (CMP_LEN - 1) <= tq
        blk = lax.broadcasted_iota(jnp.int32, (n_blk, TQ), 0)
        forced = jnp.logical_or(blk == jnp.right_shift(tq, int(math.log2(SLC_BLOCK))), blk == 0)
        blk_causal = blk * SLC_BLOCK <= tq
        for h in range(NSA_HEADS):
            s = jnp.where(cmp_valid, _dot_nt(kc_ref[h // hpg], qz_ref[h]), NEG_INF)
            sbuf_ref[h, 0:nc, :] = s
            m_ref[h:h + 1, :] = jnp.maximum(jnp.max(s, axis=0, keepdims=True), M_INIT)
        for g in range(NSA_GROUPS):
            vcg = vct_ref[NSA_GROUPS + g]
            imp = jnp.zeros((n_blk, TQ), F32)
            for hh in range(hpg):
                h = hpg * g + hh
                p = jnp.exp(sbuf_ref[h, 0:nc, :] - m_ref[h:h + 1, :])
                l = jnp.sum(p, axis=0, keepdims=True)
                pc = (p * jnp.where(l > 0.0, 1.0 / l, 0.0)).astype(BF16)
                cmp_ref[HEAD_DIM * h:HEAD_DIM * (h + 1), :] = _dot(vcg, pc)
                imp = imp + _dot(ovt_ref[...], pc)
            imp = jnp.where(forced, FORCE_SCORE, imp)
            imp = jnp.where(blk_causal, imp, NEG_INF)
            n_tiles = n_blk // SUBLANES
            sub = lax.broadcasted_iota(jnp.int32, (SUBLANES, TQ), 0)
            tiles = [imp[SUBLANES * t:SUBLANES * (t + 1)] for t in range(n_tiles)]
            ranks = [jnp.zeros((SUBLANES, TQ), F32)] * n_tiles
            for mp in range(n_blk):
                r = jnp.broadcast_to(imp[mp:mp + 1, :], (SUBLANES, TQ))
                for t in range(n_tiles):
                    if SUBLANES * t > mp:
                        ahead = r >= tiles[t]
                    elif SUBLANES * (t + 1) <= mp:
                        ahead = r > tiles[t]
                    else:
                        later = sub > mp - SUBLANES * t
                        ahead = jnp.logical_or(r > tiles[t], jnp.logical_and(r == tiles[t], later))
                    ranks[t] = ranks[t] + jnp.where(ahead, 1.0, 0.0)
            rank = jnp.concatenate(ranks, axis=0)
            sel_ref[g] = jnp.where(rank < float(n_sel), 0.0, NEG_INF)

    bpc = KC // SLC_BLOCK

    def slc_prep(j):
        off = pl.multiple_of(j * KC, KC)
        visible = row_minus_col <= (i - j) * KC
        biases = []
        for g in range(NSA_GROUPS):
            bias = jnp.concatenate(
                [jnp.broadcast_to(sel_ref[g, pl.ds(bpc * j + b, 1), :], (SLC_BLOCK, TQ))
                 for b in range(bpc)], axis=0)
            biases.append(jnp.where(visible, bias, NEG_INF))
        return off, biases

    def slc_score(ctx, h, dead):
        off, biases = ctx
        g = h // hpg
        return _scores(ks_ref[pl.ds(off, KC), gw * g:gw * (g + 1)], qrz_ref[h], dead) + biases[g]

    def slc_values(j, h):
        g = h // hpg
        return vst_ref[HV * g:HV * (g + 1), pl.ds(pl.multiple_of(j * KC, KC), KC)]

    def win_prep(j):
        off = pl.multiple_of(j * KC, KC)
        d = (i - j) * KC
        inside = jnp.logical_and(row_minus_col <= d, row_minus_col > d - WINDOW)
        return off, jnp.where(inside, 0.0, NEG_INF)

    def win_score(ctx, h, dead):
        off, bias = ctx
        g = h // hpg
        return _scores(kw_ref[pl.ds(off, KC), gw * g:gw * (g + 1)], qrz_ref[h], dead) + bias

    def win_values(j, h):
        g = h // hpg
        return vwt_ref[HV * g:HV * (g + 1), pl.ds(pl.multiple_of(j * KC, KC), KC)]

    compressed_and_select()
    assert WINDOW == KC
    slc = _Stage(0, i, NSA_HEADS, HV, slc_prep, slc_score, slc_values, m_ref, al_ref, acc_ref,
                 dead_last="bl")
    win = _Stage(jnp.maximum(i - 1, 0), i, NSA_HEADS, HV, win_prep, win_score, win_values,
                 mw_ref, alw_ref, accw_ref, dead_first="tr", dead_last="bl")
    _attn_chain([slc, win], sbuf_ref, i)

    def head_out(ref, h):
        return ref[HV * h:HV * h + HEAD_DIM, :] / ref[HV * h + HEAD_DIM:HV * h + HEAD_DIM + 1, :]

    gb = _sigmoid(bg_ref[...])
    for hp in range(NSA_HEADS // 2):
        parts = []
        for h in (2 * hp, 2 * hp + 1):
            parts.append(gb[3 * h:3 * h + 1, :] * cmp_ref[HEAD_DIM * h:HEAD_DIM * (h + 1), :]
                         + gb[3 * h + 1:3 * h + 2, :] * head_out(acc_ref, h)
                         + gb[3 * h + 2:3 * h + 3, :] * head_out(accw_ref, h))
        o_ref[:, LANES * hp:LANES * (hp + 1)] = jnp.concatenate(parts, axis=0).T


def _nsa_call(qc, qcr, kcc, vct, ovt, ks, vst, kw, vwt, bgt, n_sel):
    B, S, W = qc.shape
    n_blk = S // SLC_BLOCK
    qspec = pl.BlockSpec((None, TQ, W), lambda b, i: (b, i, 0))
    return pl.pallas_call(
        functools.partial(_nsa_kernel, n_sel=n_sel),
        grid=(B, S // TQ),
        in_specs=[
            qspec, qspec,
            pl.BlockSpec((None,) + kcc.shape[1:], lambda b, i: (b, 0, 0, 0)),
            pl.BlockSpec((None,) + vct.shape[1:], lambda b, i: (b, 0, 0, 0)),
            pl.BlockSpec(ovt.shape, lambda b, i: (0, 0)),
            pl.BlockSpec((None, S, 2 * LANES), lambda b, i: (b, 0, 0)),
            pl.BlockSpec((None, NSA_GROUPS * HV, S), lambda b, i: (b, 0, 0)),
            pl.BlockSpec((None, S, 2 * LANES), lambda b, i: (b, 0, 0)),
            pl.BlockSpec((None, NSA_GROUPS * HV, S), lambda b, i: (b, 0, 0)),
            pl.BlockSpec((None, 3 * NSA_HEADS, TQ), lambda b, i: (b, 0, i)),
        ],
        out_specs=qspec,
        out_shape=jax.ShapeDtypeStruct((B, S, W), F32),
        scratch_shapes=[pltpu.VMEM((W, TQ), F32), pltpu.VMEM((NSA_GROUPS, n_blk, TQ), F32),
                        pltpu.VMEM((NSA_HEADS, TQ, LANES), BF16), pltpu.VMEM((NSA_HEADS, TQ, LANES), BF16)]
        + _attn_scratch(NSA_HEADS, HV)[1:] + _attn_scratch(NSA_HEADS, HV),
        compiler_params=pltpu.CompilerParams(
            dimension_semantics=("arbitrary", "arbitrary"), vmem_limit_bytes=VMEM_LIMIT),
        name="nsa_attn",
    )(qc, qcr, kcc, vct, ovt, ks, vst, kw, vwt, bgt)


def _post_kernel(x_ref, oa_ref, ob_ref, oc_ref, p_ref, g_ref, wg_ref, wup_ref, wout_ref,
                 pg_ref, wple_ref, wpg_ref, fg_ref, o_ref, *, final):
    x = x_ref[...]
    h = _rms(x, g_ref[...]).astype(BF16)
    mixed = jnp.zeros(x.shape, F32)
    for n, br_ref in enumerate((oa_ref, ob_ref, oc_ref)):
        gate = _dot(h, wg_ref[:, BRANCH_WIDTH * n:BRANCH_WIDTH * (n + 1)])
        og = br_ref[...] * (gate * _sigmoid(gate))
        up = _dot(og.astype(BF16), wup_ref[n])
        c0 = N_BRANCH * BRANCH_WIDTH + D_MODEL * n
        mixed = mixed + _sigmoid(_dot(h, wg_ref[:, c0:c0 + D_MODEL])) * up
    x1 = x + _dot(mixed.astype(BF16), wout_ref[...])
    emb = _dot(p_ref[...].astype(BF16), wple_ref[...])
    hg = _dot(_rms(x1, pg_ref[...]).astype(BF16), wpg_ref[...])
    x2 = x1 + emb * _sigmoid(hg)
    if final:
        x2 = _rms(x2, fg_ref[...])
    o_ref[...] = x2


def _post_call(x, oa, ob, oc, p, g, wg, wup, wout, pg, wple, wpg, fg, final, layer):
    T, D = x.shape
    tm = TM_POST
    tok = lambda w: pl.BlockSpec((tm, w), lambda i: (i, 0))
    full = lambda a: _layer_spec(a, layer, pipeline_mode=pl.Buffered(1))
    return pl.pallas_call(
        functools.partial(_post_kernel, final=final),
        grid=(T // tm,),
        in_specs=[tok(D), tok(BRANCH_WIDTH), tok(BRANCH_WIDTH), tok(BRANCH_WIDTH),
                  pl.BlockSpec((None, tm, PLE_DIM), lambda i: (layer, i, 0)),
                  full(g), full(wg), full(wup), full(wout), full(pg), full(wple), full(wpg),
                  pl.BlockSpec(fg.shape, lambda i: (0, 0))],
        out_specs=tok(D),
        out_shape=jax.ShapeDtypeStruct((T, D), F32),
        compiler_params=pltpu.CompilerParams(
            dimension_semantics=("arbitrary",), vmem_limit_bytes=VMEM_LIMIT),
        name="post_mix",
    )(x, oa, ob, oc, p, g, wg, wup, wout, pg, wple, wpg, fg)


def _rope_tables(positions):
    inv = ROPE_THETA ** (-jnp.arange(0, ROT_DIM, 2, dtype=F32) / ROT_DIM)
    ang = positions.astype(F32)[..., None] * inv
    cos, sin = jnp.cos(ang), jnp.sin(ang)
    half = ROT_DIM // 2
    pad = HEAD_DIM - ROT_DIM
    zeros = jnp.zeros(cos.shape[:-1] + (half,), F32)
    rc = jnp.concatenate([cos, cos, jnp.ones(cos.shape[:-1] + (pad,), F32)], axis=-1)
    rs1 = jnp.concatenate([-sin, zeros, jnp.zeros(cos.shape[:-1] + (pad,), F32)], axis=-1)
    rs2 = jnp.concatenate([zeros, sin, jnp.zeros(cos.shape[:-1] + (pad,), F32)], axis=-1)
    rep = lambda t: jnp.concatenate([t] * (LANES // HEAD_DIM), axis=-1)
    return rep(rc), rep(rs1), rep(rs2)


def _proj_weights(w_in):
    w = w_in.astype(BF16)
    sl = lambda o, n: w[..., o:o + n]
    dup = lambda o: jnp.concatenate([sl(o, HEAD_DIM)] * 2, axis=-1)
    nat = jnp.concatenate([
        sl(_A_Q, 512), sl(_A_K, 512), sl(_B_Q, 512), sl(_I_Q, 512), sl(_C_Q, 512),
        dup(_B_K), dup(_I_K),
        dup(_C_KS), dup(_C_KS + HEAD_DIM), dup(_C_KW), dup(_C_KW + HEAD_DIM),
        sl(_C_KC, 128), sl(_C_VC, 128)], axis=-1)
    tr = jnp.concatenate([
        sl(_A_V, 512), sl(_B_V, 64), sl(_C_VS, 128), sl(_C_VW, 128), sl(_I_W, 8), sl(_C_BG, 24)], axis=-1)
    gates = jnp.concatenate([sl(_A_G, 512), sl(_B_G, 512), sl(_C_G, 512), sl(_MERGE, 3 * D_MODEL)], axis=-1)
    return nat, jnp.swapaxes(tr, 1, 2), gates


def _compress_weights(pe_k, w1_k, w2_k, pe_v, w1_v, w2_v):
    L = pe_k.shape[0]
    half = CMP_LEN // 2
    kdim = half * NSA_GROUPS * HEAD_DIM
    pe = jnp.stack([pe_k, pe_v], axis=1)
    pe = jnp.broadcast_to(pe[:, :, :, None, :], (L, 2, CMP_LEN, NSA_GROUPS, HEAD_DIM))
    w1 = jnp.stack([w1_k, w1_v], axis=1).astype(BF16)
    zero = jnp.zeros_like(w1)
    w1g = jnp.stack([jnp.stack([w1 if h == g else zero for h in range(NSA_GROUPS)], axis=3)
                     for g in range(NSA_GROUPS)], axis=2)
    w1g = w1g.reshape(L, 2 * NSA_GROUPS, CMP_LEN, NSA_GROUPS * HEAD_DIM, CMP_HIDDEN)
    w2 = jnp.stack([w2_k, w2_v], axis=1).astype(BF16)
    return (pe[:, :, :half].reshape(L, 2, 1, kdim), pe[:, :, half:].reshape(L, 2, 1, kdim),
            w1g[:, :, :half].reshape(L, 2 * NSA_GROUPS, kdim, CMP_HIDDEN),
            w1g[:, :, half:].reshape(L, 2 * NSA_GROUPS, kdim, CMP_HIDDEN), w2)


def _overlap_t(S):
    nc = S // CMP_STRIDE
    n_blk = S // SLC_BLOCK
    cstart = np.arange(nc) * CMP_STRIDE
    sstart = np.arange(n_blk) * SLC_BLOCK
    ov = ((cstart[None, :] < sstart[:, None] + SLC_BLOCK) & (cstart[None, :] + CMP_LEN > sstart[:, None]))
    return jnp.asarray(ov.astype(np.float32), dtype=BF16)


def kernel(x, p, positions, norm_g, w_in, diff_lambda, diff_subln_g, cmp_pe_k, cmp_w1_k, cmp_w2_k,
           cmp_pe_v, cmp_w1_v, cmp_w2_v, w_up, w_out, ple_norm_g, w_ple, w_ple_gate, final_norm_g):
    B, S, D = x.shape
    depth = w_in.shape[0]
    k_sel = min(DSA_TOPK_MAX, S // 4)
    n_sel = min(SLC_TOPN_MAX, S // SLC_BLOCK)
    assert S % TM_PROJ == 0 and S % TQ == 0 and k_sel <= TQ and D == D_MODEL
    assert KC == TQ and WINDOW % KC == 0
    rc, rs1, rs2 = _rope_tables(positions)
    ovt = _overlap_t(S)
    half_len = CMP_LEN // 2
    kdim = half_len * NSA_GROUPS * HEAD_DIM

    wn, wt, wg = _proj_weights(w_in)
    cmp_w = _compress_weights(cmp_pe_k, cmp_w1_k, cmp_w2_k, cmp_pe_v, cmp_w1_v, cmp_w2_v)
    g_in = norm_g.reshape(depth, 1, D)
    g_ple = ple_norm_g.reshape(depth, 1, D)
    g_sub = diff_subln_g.reshape(depth, -1, 1)
    wup, wout, wple, wpg = (w.astype(BF16) for w in (w_up, w_out, w_ple, w_ple_gate))
    p2 = p.reshape(depth, B * S, PLE_DIM)

    for i in range(depth):
        lam_init = 0.8 - 0.6 * math.exp(-0.3 * i)
        (qa, ka, qb, iq, qc, qcr, kb, ik, ks, kw, ck, cv,
         vat, vbt, vst, vwt, iwt, bgt) = _proj_call(x, g_in, wn, wt, rc, rs1, rs2, i)

        rk = ck.reshape(B, S // half_len, kdim)
        rv = cv.reshape(B, S // half_len, kdim)
        kcc, vct = _compress_call(rk, rv, *cmp_w, i)

        oa = _diff_call(qa, ka, vat, diff_lambda, g_sub, lam_init, i)
        ob = _dsa_call(qb, iq, kb, ik, vbt, iwt, k_sel)
        oc = _nsa_call(qc, qcr, kcc, vct, ovt, ks, vst, kw, vwt, bgt, n_sel)

        x = _post_call(
            x.reshape(B * S, D), oa.reshape(B * S, -1), ob.reshape(B * S, -1), oc.reshape(B * S, -1),
            p2, g_in, wg, wup, wout, g_ple, wple, wpg, final_norm_g.reshape(1, D),
            final=(i == depth - 1), layer=i).reshape(B, S, D)
    return x
```

```python
import functools
import math

import numpy as np
import jax
import jax.numpy as jnp
from jax import lax
from jax.experimental import pallas as pl
from jax.experimental.pallas import tpu as pltpu

F32 = jnp.float32
BF16 = jnp.bfloat16

D_MODEL = 1024
HEAD_DIM = 64
ROT_DIM = 16
ROPE_THETA = 500000.0
EPS = 1e-6
PLE_DIM = 256
DA_HEADS = 4
DSA_HEADS = 8
IDX_HEADS = 8
DSA_TOPK_MAX = 256
NSA_GROUPS = 2
NSA_HEADS = 8
CMP_LEN = 32
CMP_STRIDE = 16
CMP_HIDDEN = 128
SLC_BLOCK = 64
SLC_TOPN_MAX = 16
WINDOW = 512
FORCE_SCORE = 1e4
BRANCH_WIDTH = 512
N_BRANCH = 3

IN_SPLITS = (512, 512, 512, 512, 512, 64, 64, 512, 512, 64, 8,
             512, 128, 128, 128, 128, 128, 128, 512, 24, 3 * D_MODEL)
(_A_Q, _A_K, _A_V, _A_G, _B_Q, _B_K, _B_V, _B_G, _I_Q, _I_K, _I_W,
 _C_Q, _C_KC, _C_VC, _C_KS, _C_VS, _C_KW, _C_VW, _C_G, _C_BG, _MERGE) = [
    int(v) for v in np.concatenate([[0], np.cumsum(IN_SPLITS)[:-1]])]

LANES = 128
SUBLANES = 8
TQ = 512
KC = 512
TM_PROJ = 512
TM_POST = 512
NEG_INF = float("-inf")
M_INIT = -1e30
N_NAT = 3584
N_TR = 864
BISECT_PLAIN = 10
ONES_ROWS = 16
HV = HEAD_DIM + ONES_ROWS
HV_A = 2 * HEAD_DIM + ONES_ROWS
LOG2E = math.log2(math.e)
VMEM_LIMIT = 56 * 1024 * 1024


def _dot_nt(a, b):
    return lax.dot_general(a, b, (((1,), (1,)), ((), ())), preferred_element_type=F32)


def _dot(a, b):
    return jnp.dot(a, b, preferred_element_type=F32)


def _rms(x, g):
    return x * lax.rsqrt(jnp.mean(x * x, axis=-1, keepdims=True) + EPS) * g


def _sigmoid(x):
    return 1.0 / (1.0 + jnp.exp(-x))


def _store_half_masked(src_ref, dst_ref, n_pairs):
    lane = lax.broadcasted_iota(jnp.int32, (src_ref.shape[0], LANES), 1)
    for hp in range(n_pairs):
        x = src_ref[:, LANES * hp:LANES * (hp + 1)]
        z = jnp.zeros_like(x)
        dst_ref[2 * hp] = jnp.where(lane < HEAD_DIM, x, z)
        dst_ref[2 * hp + 1] = jnp.where(lane >= HEAD_DIM, x, z)


def _fold_rows(x, op):
    n_acc = 2
    accs = [x[SUBLANES * r:SUBLANES * (r + 1)] for r in range(n_acc)]
    for r in range(n_acc, x.shape[0] // SUBLANES):
        accs[r % n_acc] = op(accs[r % n_acc], x[SUBLANES * r:SUBLANES * (r + 1)])
    return op(accs[0], accs[1])


HALF = KC // 2


def _scores(k, q, dead=None):
    if dead is None:
        return _dot_nt(k, q)
    neg = jnp.full((HALF, HALF), NEG_INF, F32)
    if dead == "bl":
        return jnp.concatenate(
            [_dot_nt(k[:HALF], q), jnp.concatenate([neg, _dot_nt(k[HALF:], q[HALF:])], axis=1)], axis=0)
    return jnp.concatenate(
        [jnp.concatenate([_dot_nt(k[:HALF], q[:HALF]), neg], axis=1), _dot_nt(k[HALF:], q)], axis=0)


def _weighted_values(v_t, p, dead=None):
    if dead is None:
        return _dot(v_t, p)
    if dead == "bl":
        return jnp.concatenate([_dot(v_t[:, :HALF], p[:HALF, :HALF]), _dot(v_t, p[:, HALF:])], axis=1)
    return jnp.concatenate([_dot(v_t, p[:, :HALF]), _dot(v_t[:, HALF:], p[HALF:, HALF:])], axis=1)


class _Stage:
    def __init__(self, j0, j1, nh, hv, prep, score, values, m_ref, al_ref, acc_ref,
                 dead_first=None, dead_last=None):
        self.j0 = jnp.asarray(j0, jnp.int32)
        self.j1 = jnp.asarray(j1, jnp.int32)
        self.nh, self.hv, self.prep, self.score, self.values = nh, hv, prep, score, values
        self.m_ref, self.al_ref, self.acc_ref = m_ref, al_ref, acc_ref
        self.dead_first, self.dead_last = dead_first, dead_last

    def reset(self):
        self.m_ref[...] = jnp.full(self.m_ref.shape, M_INIT, F32)
        self.acc_ref[...] = jnp.zeros(self.acc_ref.shape, F32)

    def produce(self, sbuf_ref, j, dead=None):
        ctx = self.prep(j)
        for h in range(self.nh):
            s = self.score(ctx, h, dead)
            m_old = self.m_ref[h:h + 1, :]
            m_new = jnp.maximum(m_old, jnp.max(s, axis=0, keepdims=True))
            sbuf_ref[h] = s
            self.m_ref[h:h + 1, :] = m_new
            self.al_ref[h:h + 1, :] = jnp.exp2(m_old - m_new)

    def consume(self, sbuf_ref, j, dead=None):
        for h in range(self.nh):
            p = jnp.exp2(sbuf_ref[h] - self.m_ref[h:h + 1, :])
            rows = slice(self.hv * h, self.hv * (h + 1))
            self.acc_ref[rows, :] = (self.al_ref[h:h + 1, :] * self.acc_ref[rows, :]
                                     + _weighted_values(self.values(j, h), p.astype(BF16), dead))

    def loop(self, sbuf_ref, lo, hi):
        def body(j, carry):
            self.consume(sbuf_ref, j - 1)
            self.produce(sbuf_ref, j)
            return carry
        lax.fori_loop(lo, hi, body, 0)


def _attn_chain(stages, sbuf_ref, tile):
    for st in stages:
        st.reset()

    @pl.when(tile == 0)
    def _():
        for st in stages:
            st.produce(sbuf_ref, st.j0)
            st.consume(sbuf_ref, st.j0)

    @pl.when(tile > 0)
    def _():
        stages[0].produce(sbuf_ref, stages[0].j0, stages[0].dead_first)
        for n, st in enumerate(stages):
            st.loop(sbuf_ref, st.j0 + 1, st.j1)
            st.consume(sbuf_ref, st.j1 - 1, st.dead_first)
            st.produce(sbuf_ref, st.j1, st.dead_last)
            st.consume(sbuf_ref, st.j1, st.dead_last)
            if n + 1 < len(stages):
                stages[n + 1].produce(sbuf_ref, stages[n + 1].j0, stages[n + 1].dead_first)


def _attn_scratch(nh, hv):
    return [pltpu.VMEM((nh, KC, TQ), F32), pltpu.VMEM((nh, TQ), F32),
            pltpu.VMEM((nh, TQ), F32), pltpu.VMEM((nh * hv, TQ), F32)]


def _layer_spec(a, layer, **kw):
    tail = (0,) * (a.ndim - 1)
    return pl.BlockSpec((None,) + a.shape[1:], lambda *_: (layer,) + tail, **kw)


def _ones_rows(n_cols):
    r = lax.broadcasted_iota(jnp.int32, (ONES_ROWS, n_cols), 0)
    return jnp.where(r == 0, 1.0, 0.0).astype(BF16)


def _proj_kernel(x_ref, g_ref, wn_ref, wt_ref, rc_ref, rs1_ref, rs2_ref,
                 qa_ref, ka_ref, qb_ref, iq_ref, qc_ref, qcr_ref, kb_ref, ik_ref,
                 ks_ref, kw_ref, ck_ref, cv_ref,
                 vat_ref, vbt_ref, vst_ref, vwt_ref, iwt_ref, bgt_ref):
    h = _rms(x_ref[...], g_ref[...]).astype(BF16)
    rc, rs1, rs2 = rc_ref[...], rs1_ref[...], rs2_ref[...]

    def rope(z):
        return z * rc + pltpu.roll(z, LANES - ROT_DIM // 2, 1) * rs1 + pltpu.roll(z, ROT_DIM // 2, 1) * rs2

    qscale = HEAD_DIM ** -0.5
    qscale2 = qscale * LOG2E
    segs = (
        (0, 512, ((qa_ref, True, qscale2),)),
        (512, 512, ((ka_ref, True, 1.0),)),
        (1024, 512, ((qb_ref, True, qscale2),)),
        (1536, 512, ((iq_ref, True, qscale),)),
        (2048, 512, ((qc_ref, False, qscale), (qcr_ref, True, qscale2))),
        (2560, 128, ((kb_ref, True, 1.0),)),
        (2688, 128, ((ik_ref, True, 1.0),)),
        (2816, 256, ((ks_ref, True, 1.0),)),
        (3072, 256, ((kw_ref, True, 1.0),)),
    )
    zc = _dot(h, wn_ref[:, 3328:3328 + 2 * LANES])
    ck_ref[...] = zc[:, 0:LANES]
    cv_ref[...] = zc[:, LANES:2 * LANES]
    for c0, width, outs in segs:
        z = _dot(h, wn_ref[:, c0:c0 + width])
        for out_ref, rot, scale in outs:
            for j in range(width // LANES):
                zj = z[:, LANES * j:LANES * (j + 1)]
                if rot:
                    zj = rope(zj)
                if scale != 1.0:
                    zj = zj * scale
                out_ref[:, LANES * j:LANES * (j + 1)] = zj.astype(out_ref.dtype)

    zt = _dot_nt(wt_ref[...], h)
    ones = _ones_rows(zt.shape[1])
    vd = 2 * HEAD_DIM
    for h in range(DA_HEADS):
        vat_ref[HV_A * h:HV_A * h + vd, :] = zt[vd * h:vd * (h + 1)].astype(BF16)
        vat_ref[HV_A * h + vd:HV_A * (h + 1), :] = ones
    vbt_ref[0:HEAD_DIM, :] = zt[512:576].astype(BF16)
    vbt_ref[HEAD_DIM:HV, :] = ones
    for g in range(NSA_GROUPS):
        for ref, base in ((vst_ref, 576), (vwt_ref, 704)):
            ref[HV * g:HV * g + HEAD_DIM, :] = zt[base + HEAD_DIM * g:base + HEAD_DIM * (g + 1)].astype(BF16)
            ref[HV * g + HEAD_DIM:HV * (g + 1), :] = ones
    iwt_ref[...] = zt[832:840] * (IDX_HEADS ** -0.5)
    bgt_ref[...] = zt[840:864]


def _proj_call(x, g, wn, wt, rc, rs1, rs2, layer):
    B, S, D = x.shape
    tm = TM_PROJ
    nat = lambda w: pl.BlockSpec((None, tm, w), lambda b, i: (b, i, 0))
    tr = lambda r: pl.BlockSpec((None, r, tm), lambda b, i: (b, 0, i))
    full = lambda a: _layer_spec(a, layer)
    sds = jax.ShapeDtypeStruct
    out_shape = (
        sds((B, S, 512), BF16), sds((B, S, 512), BF16), sds((B, S, 512), BF16), sds((B, S, 512), BF16),
        sds((B, S, 512), BF16), sds((B, S, 512), BF16), sds((B, S, 128), BF16), sds((B, S, 128), BF16),
        sds((B, S, 256), BF16), sds((B, S, 256), BF16),
        sds((B, S, LANES), F32), sds((B, S, LANES), F32),
        sds((B, DA_HEADS * HV_A, S), BF16), sds((B, HV, S), BF16),
        sds((B, NSA_GROUPS * HV, S), BF16), sds((B, NSA_GROUPS * HV, S), BF16),
        sds((B, 8, S), F32), sds((B, 24, S), F32),
    )
    out_specs = (nat(512), nat(512), nat(512), nat(512), nat(512), nat(512), nat(128), nat(128),
                 nat(256), nat(256), nat(LANES), nat(LANES),
                 tr(DA_HEADS * HV_A), tr(HV), tr(NSA_GROUPS * HV), tr(NSA_GROUPS * HV), tr(8), tr(24))
    return pl.pallas_call(
        _proj_kernel,
        grid=(B, S // tm),
        in_specs=[nat(D), full(g), full(wn), full(wt), nat(LANES), nat(LANES), nat(LANES)],
        out_specs=out_specs,
        out_shape=out_shape,
        compiler_params=pltpu.CompilerParams(
            dimension_semantics=("arbitrary", "arbitrary"), vmem_limit_bytes=VMEM_LIMIT),
        name="in_proj",
    )(x, g, wn, wt, rc, rs1, rs2)


def _compress_kernel(rk_ref, rv_ref, pea_ref, peb_ref, w1a_ref, w1b_ref, w2_ref, nat_ref, t_ref):
    for kind, r_ref in enumerate((rk_ref, rv_ref)):
        r = r_ref[...]
        xa = (r + pea_ref[kind]).astype(BF16)
        xb = (r + peb_ref[kind]).astype(BF16)
        nrow = r.shape[0]
        for g in range(NSA_GROUPS):
            n = NSA_GROUPS * kind + g
            hid = jax.nn.gelu(_dot(xa, w1a_ref[n]) + pltpu.roll(_dot(xb, w1b_ref[n]), nrow - 1, 0))
            o = _dot(hid.astype(BF16), w2_ref[kind])
            nat_ref[n] = jnp.concatenate([o, o], axis=1).astype(BF16)
            t_ref[n] = o.T.astype(BF16)


def _compress_call(rk, rv, pea, peb, w1a, w1b, w2, layer):
    B, nr, kdim = rk.shape
    full = lambda a: _layer_spec(a, layer)
    rspec = pl.BlockSpec((None, nr, kdim), lambda b: (b, 0, 0))
    return pl.pallas_call(
        _compress_kernel,
        grid=(B,),
        in_specs=[rspec, rspec, full(pea), full(peb), full(w1a), full(w1b), full(w2)],
        out_specs=(
            pl.BlockSpec((None, 2 * NSA_GROUPS, nr, 2 * HEAD_DIM), lambda b: (b, 0, 0, 0)),
            pl.BlockSpec((None, 2 * NSA_GROUPS, HEAD_DIM, nr), lambda b: (b, 0, 0, 0)),
        ),
        out_shape=(jax.ShapeDtypeStruct((B, 2 * NSA_GROUPS, nr, 2 * HEAD_DIM), BF16),
                   jax.ShapeDtypeStruct((B, 2 * NSA_GROUPS, HEAD_DIM, nr), BF16)),
        compiler_params=pltpu.CompilerParams(dimension_semantics=("arbitrary",)),
        name="nsa_compress",
    )(rk, rv, pea, peb, w1a, w1b, w2)


def _diff_kernel(q_ref, k_ref, vt_ref, lam_ref, sg_ref, o_ref,
                 qz_ref, sbuf_ref, m_ref, al_ref, acc_ref, *, lam_init):
    i = pl.program_id(1)
    lp = lam_ref[...]
    lam = (jnp.exp(jnp.sum(lp[0:1] * lp[1:2], axis=1, keepdims=True))
           - jnp.exp(jnp.sum(lp[2:3] * lp[3:4], axis=1, keepdims=True)) + lam_init)
    row_minus_col = (lax.broadcasted_iota(jnp.int32, (KC, TQ), 0)
                     - lax.broadcasted_iota(jnp.int32, (KC, TQ), 1))
    _store_half_masked(q_ref, qz_ref, DA_HEADS)
    vdim = 2 * HEAD_DIM
    n_maps = 2 * DA_HEADS

    def prep(j):
        return pl.multiple_of(j * KC, KC), jnp.where(row_minus_col <= (i - j) * KC, 0.0, NEG_INF)

    def score(ctx, hm, dead):
        off, bias = ctx
        h = hm // 2
        return _scores(k_ref[pl.ds(off, KC), vdim * h:vdim * (h + 1)], qz_ref[hm], dead) + bias

    def values(j, hm):
        h = hm // 2
        return vt_ref[HV_A * h:HV_A * (h + 1), pl.ds(pl.multiple_of(j * KC, KC), KC)]

    _attn_chain([_Stage(0, i, n_maps, HV_A, prep, score, values, m_ref, al_ref, acc_ref, dead_last="bl")],
                sbuf_ref, i)

    for h in range(DA_HEADS):
        r0, r1 = HV_A * 2 * h, HV_A * (2 * h + 1)
        o_t = (acc_ref[r0:r0 + vdim, :] / acc_ref[r0 + vdim:r0 + vdim + 1, :]
               - lam * (acc_ref[r1:r1 + vdim, :] / acc_ref[r1 + vdim:r1 + vdim + 1, :]))
        ms = jnp.mean(o_t * o_t, axis=0, keepdims=True)
        y = o_t * lax.rsqrt(ms + EPS) * sg_ref[...] * (1.0 - lam_init)
        o_ref[:, vdim * h:vdim * (h + 1)] = y.T


def _diff_call(qa, ka, vat, lam_p, sg, lam_init, layer):
    B, S, W = qa.shape
    n_maps = 2 * DA_HEADS
    return pl.pallas_call(
        functools.partial(_diff_kernel, lam_init=lam_init),
        grid=(B, S // TQ),
        in_specs=[
            pl.BlockSpec((None, TQ, W), lambda b, i: (b, i, 0)),
            pl.BlockSpec((None, S, W), lambda b, i: (b, 0, 0)),
            pl.BlockSpec((None, DA_HEADS * HV_A, S), lambda b, i: (b, 0, 0)),
            _layer_spec(lam_p, layer),
            _layer_spec(sg, layer),
        ],
        out_specs=pl.BlockSpec((None, TQ, W), lambda b, i: (b, i, 0)),
        out_shape=jax.ShapeDtypeStruct((B, S, W), F32),
        scratch_shapes=[pltpu.VMEM((n_maps, TQ, LANES), BF16)] + _attn_scratch(n_maps, HV_A),
        compiler_params=pltpu.CompilerParams(
            dimension_semantics=("arbitrary", "arbitrary"), vmem_limit_bytes=VMEM_LIMIT),
        name="diff_attn",
    )(qa, ka, vat, lam_p, sg)


def _dsa_kernel(q_ref, iq_ref, kb_ref, ik_ref, vt_ref, iw_ref, o_ref,
                sc_ref, tj_ref, tri_ref, qz_ref, sbuf_ref, m_ref, al_ref, acc_ref, *, k_sel, seq_len):
    i = pl.program_id(1)
    nch = i + 1
    row = lax.broadcasted_iota(jnp.int32, (KC, TQ), 0)
    col = lax.broadcasted_iota(jnp.int32, (KC, TQ), 1)
    causal = row <= col
    iw = iw_ref[...]
    kf = float(k_sel)
    _store_half_masked(iq_ref, qz_ref, IDX_HEADS // 2)

    def idx_chunk(j, carry, masked):
        mx, mn = carry
        off = pl.multiple_of(j * KC, KC)
        ikc = ik_ref[pl.ds(off, KC), :]
        sc = jnp.zeros((KC, TQ), F32)
        for h in range(IDX_HEADS):
            dots = _scores(ikc, qz_ref[h], "bl" if masked else None)
            sc = sc + iw[h:h + 1, :] * jnp.maximum(dots, 0.0)
        lo_src = sc
        if masked:
            lo_src = jnp.where(causal, sc, jnp.inf)
            sc = jnp.where(causal, sc, NEG_INF)
        sc_ref[pl.ds(off, KC), :] = sc
        return (jnp.maximum(mx, _fold_rows(sc, jnp.maximum)),
                jnp.minimum(mn, _fold_rows(lo_src, jnp.minimum)))

    carry = (jnp.full((SUBLANES, TQ), NEG_INF, F32), jnp.full((SUBLANES, TQ), jnp.inf, F32))
    carry = lax.fori_loop(0, i, functools.partial(idx_chunk, masked=False), carry)
    mx8, mn8 = idx_chunk(i, carry, True)
    row_max = jnp.max(mx8, axis=0, keepdims=True)
    row_min = jnp.min(mn8, axis=0, keepdims=True)

    def probe_pass(th, snap):
        thb = jnp.broadcast_to(th, (SUBLANES, TQ))

        def body(j, c):
            off = pl.multiple_of(j * KC, KC)
            c = list(c)
            for r in range(KC // SUBLANES):
                x = sc_ref[pl.ds(off + SUBLANES * r, SUBLANES), :]
                ge = x >= thb
                c[0] = c[0] + jnp.where(ge, 1.0, 0.0)
                if snap:
                    c[1] = jnp.maximum(c[1], jnp.where(ge, NEG_INF, x))
            return tuple(c)
        init = (jnp.zeros((SUBLANES, TQ), F32),)
        if snap:
            init += (jnp.full((SUBLANES, TQ), NEG_INF, F32),)
        out = lax.fori_loop(0, nch, body, init)
        cnt = jnp.sum(out[0], axis=0, keepdims=True)
        return (cnt, jnp.max(out[1], axis=0, keepdims=True)) if snap else cnt

    n_causal = (i * TQ + 1 + lax.broadcasted_iota(jnp.int32, (1, TQ), 1)).astype(F32)
    keep_all = n_causal <= kf

    def bisect(_, c):
        lo, hi, clo, chi = c
        mid = lo + (hi - lo) * 0.5
        cnt = probe_pass(mid, False)
        ge = cnt >= kf
        return (jnp.where(ge, mid, lo), jnp.where(ge, hi, mid),
                jnp.where(ge, cnt, clo), jnp.where(ge, chi, cnt))

    hi0 = row_max + (row_max - row_min) + 1.0
    lo, hi, clo, chi = lax.fori_loop(
        0, BISECT_PLAIN, bisect, (row_min, hi0, n_causal, jnp.zeros((1, TQ), F32)))

    def cond(st):
        return jnp.logical_and(st[0] < 256, st[-1] > 0.0)

    def snap_step(st):
        it, lo, hi, clo, chi, hie, known, stuck, done, _ = st
        top = jnp.where(known > 0.0, hie, hi)
        mid = lo + (top - lo) * 0.5
        inside = jnp.logical_and(mid > lo, mid < top)
        near_top = jnp.logical_or(kf - chi <= 2.0, stuck > 0.0)
        use_top = jnp.logical_or(jnp.logical_and(known > 0.0, near_top), jnp.logical_not(inside))
        th = jnp.where(use_top, top, mid)
        cnt, edn = probe_pass(th, True)
        live = done <= 0.0
        ge = jnp.logical_and(live, cnt >= kf)
        lt = jnp.logical_and(live, cnt < kf)
        stuck = jnp.where(jnp.logical_and(ge, cnt == clo), 1.0, 0.0)
        lo = jnp.where(ge, th, lo)
        clo = jnp.where(ge, cnt, clo)
        hi = jnp.where(lt, th, hi)
        chi = jnp.where(lt, cnt, chi)
        hie = jnp.where(lt, edn, hie)
        known = jnp.where(lt, 1.0, known)
        fin = jnp.logical_or(clo == kf, jnp.logical_and(known > 0.0, lo >= hie))
        done = jnp.where(fin, 1.0, done)
        return it + 1, lo, hi, clo, chi, hie, known, stuck, done, jnp.max(1.0 - done)

    zero = jnp.zeros((1, TQ), F32)
    done0 = jnp.where(jnp.logical_or(clo == kf, keep_all), 1.0, 0.0)
    st = lax.while_loop(cond, snap_step, (jnp.int32(0), lo, hi, clo, chi, jnp.full((1, TQ), NEG_INF, F32),
                                          zero, zero, done0, jnp.max(1.0 - done0)))
    thr, clo, chi = st[1], st[3], st[4]
    need = jnp.where(clo == kf, float(seq_len), kf - chi)
    thr = jnp.where(keep_all, NEG_INF, thr)
    need = jnp.where(keep_all, 0.0, need)
    tj_ref[2:3, :] = jnp.zeros((1, TQ), F32)

    _store_half_masked(q_ref, qz_ref, DSA_HEADS // 2)
    half = KC // 2
    tri_ref[...] = jnp.where(lax.broadcasted_iota(jnp.int32, (half, half), 0)
                             >= lax.broadcasted_iota(jnp.int32, (half, half), 1), 1.0, 0.0).astype(BF16)

    def prep(j):
        off = pl.multiple_of(j * KC, KC)
        x = sc_ref[pl.ds(off, KC), :]
        eq = x == thr
        e = jnp.where(eq, 1.0, 0.0).astype(BF16)
        r_top = _dot(tri_ref[...], e[:half]) + tj_ref[2:3, :]
        r_bot = _dot(tri_ref[...], e[half:]) + r_top[half - 1:half, :]
        tj_ref[2:3, :] = r_bot[half - 1:half, :]
        rank = jnp.concatenate([r_top, r_bot], axis=0)
        tie_bias = jnp.where(jnp.logical_and(eq, rank <= need), 0.0, NEG_INF)
        return kb_ref[pl.ds(off, KC), :], jnp.where(x > thr, 0.0, tie_bias)

    def score(ctx, h, dead):
        kc, bias = ctx
        return _scores(kc, qz_ref[h], dead) + bias

    def values(j, h):
        return vt_ref[:, pl.ds(pl.multiple_of(j * KC, KC), KC)]

    _attn_chain([_Stage(0, i, DSA_HEADS, HV, prep, score, values, m_ref, al_ref, acc_ref, dead_last="bl")],
                sbuf_ref, i)
    for hp in range(DSA_HEADS // 2):
        blk = jnp.concatenate(
            [acc_ref[HV * h:HV * h + HEAD_DIM, :] / acc_ref[HV * h + HEAD_DIM:HV * h + HEAD_DIM + 1, :]
             for h in (2 * hp, 2 * hp + 1)], axis=0)
        o_ref[:, LANES * hp:LANES * (hp + 1)] = blk.T


def _dsa_call(qb, iq, kb, ik, vbt, iwt, k_sel):
    B, S, W = qb.shape
    return pl.pallas_call(
        functools.partial(_dsa_kernel, k_sel=k_sel, seq_len=S),
        grid=(B, S // TQ),
        in_specs=[
            pl.BlockSpec((None, TQ, W), lambda b, i: (b, i, 0)),
            pl.BlockSpec((None, TQ, W), lambda b, i: (b, i, 0)),
            pl.BlockSpec((None, S, LANES), lambda b, i: (b, 0, 0)),
            pl.BlockSpec((None, S, LANES), lambda b, i: (b, 0, 0)),
            pl.BlockSpec((None, HV, S), lambda b, i: (b, 0, 0)),
            pl.BlockSpec((None, IDX_HEADS, TQ), lambda b, i: (b, 0, i)),
        ],
        out_specs=pl.BlockSpec((None, TQ, W), lambda b, i: (b, i, 0)),
        out_shape=jax.ShapeDtypeStruct((B, S, W), F32),
        scratch_shapes=[pltpu.VMEM((S, TQ), F32), pltpu.VMEM((SUBLANES, TQ), F32),
                        pltpu.VMEM((KC // 2, KC // 2), BF16),
                        pltpu.VMEM((DSA_HEADS, TQ, LANES), BF16)] + _attn_scratch(DSA_HEADS, HV),
        compiler_params=pltpu.CompilerParams(
            dimension_semantics=("arbitrary", "arbitrary"), vmem_limit_bytes=VMEM_LIMIT),
        name="dsa_attn",
    )(qb, iq, kb, ik, vbt, iwt)


def _nsa_kernel(q_ref, qr_ref, kc_ref, vct_ref, ovt_ref, ks_ref, vst_ref, kw_ref, vwt_ref, bg_ref,
                o_ref, cmp_ref, sel_ref, qz_ref, qrz_ref, mw_ref, alw_ref, accw_ref,
                sbuf_ref, m_ref, al_ref, acc_ref, *, n_sel):
    i = pl.program_id(1)
    row_minus_col = (lax.broadcasted_iota(jnp.int32, (KC, TQ), 0)
                     - lax.broadcasted_iota(jnp.int32, (KC, TQ), 1))
    tq = i * TQ + lax.broadcasted_iota(jnp.int32, (1, TQ), 1)
    hpg = NSA_HEADS // NSA_GROUPS
    nc = kc_ref.shape[1]
    n_blk = sel_ref.shape[1]
    gw = 2 * HEAD_DIM

    _store_half_masked(q_ref, qz_ref, NSA_HEADS // 2)
    _store_half_masked(qr_ref, qrz_ref, NSA_HEADS // 2)

    def compressed_and_select():
        cmp_valid = CMP_STRIDE * lax.broadcasted_iota(jnp.int32, (nc, TQ), 0) + (CMP_LEN - 1) <= tq
        blk = lax.broadcasted_iota(jnp.int32, (n_blk, TQ), 0)
        forced = jnp.logical_or(blk == jnp.right_shift(tq, int(math.log2(SLC_BLOCK))), blk == 0)
        blk_causal = blk * SLC_BLOCK <= tq
        for h in range(NSA_HEADS):
            s = jnp.where(cmp_valid, _dot_nt(kc_ref[h // hpg], qz_ref[h]), NEG_INF)
            sbuf_ref[h, 0:nc, :] = s
            m_ref[h:h + 1, :] = jnp.maximum(jnp.max(s, axis=0, keepdims=True), M_INIT)
        for g in range(NSA_GROUPS):
            vcg = vct_ref[NSA_GROUPS + g]
            imp = jnp.zeros((n_blk, TQ), F32)
            for hh in range(hpg):
                h = hpg * g + hh
                p = jnp.exp(sbuf_ref[h, 0:nc, :] - m_ref[h:h + 1, :])
                l = jnp.sum(p, axis=0, keepdims=True)
                pc = (p * jnp.where(l > 0.0, 1.0 / l, 0.0)).astype(BF16)
                cmp_ref[HEAD_DIM * h:HEAD_DIM * (h + 1), :] = _dot(vcg, pc)
                imp = imp + _dot(ovt_ref[...], pc)
            imp = jnp.where(forced, FORCE_SCORE, imp)
            imp = jnp.where(blk_causal, imp, NEG_INF)
            n_tiles = n_blk // SUBLANES
            sub = lax.broadcasted_iota(jnp.int32, (SUBLANES, TQ), 0)
            tiles = [imp[SUBLANES * t:SUBLANES * (t + 1)] for t in range(n_tiles)]
            ranks = [jnp.zeros((SUBLANES, TQ), F32)] * n_tiles
            for mp in range(n_blk):
                r = jnp.broadcast_to(imp[mp:mp + 1, :], (SUBLANES, TQ))
                for t in range(n_tiles):
                    if SUBLANES * t > mp:
                        ahead = r >= tiles[t]
                    elif SUBLANES * (t + 1) <= mp:
                        ahead = r > tiles[t]
                    else:
                        later = sub > mp - SUBLANES * t
                        ahead = jnp.logical_or(r > tiles[t], jnp.logical_and(r == tiles[t], later))
                    ranks[t] = ranks[t] + jnp.where(ahead, 1.0, 0.0)
            rank = jnp.concatenate(ranks, axis=0)
            sel_ref[g] = jnp.where(rank < float(n_sel), 0.0, NEG_INF)

    bpc = KC // SLC_BLOCK

    def slc_prep(j):
        off = pl.multiple_of(j * KC, KC)
        visible = row_minus_col <= (i - j) * KC
        biases = []
        for g in range(NSA_GROUPS):
            bias = jnp.concatenate(
                [jnp.broadcast_to(sel_ref[g, pl.ds(bpc * j + b, 1), :], (SLC_BLOCK, TQ))
                 for b in range(bpc)], axis=0)
            biases.append(jnp.where(visible, bias, NEG_INF))
        return off, biases

    def slc_score(ctx, h, dead):
        off, biases = ctx
        g = h // hpg
        return _scores(ks_ref[pl.ds(off, KC), gw * g:gw * (g + 1)], qrz_ref[h], dead) + biases[g]

    def slc_values(j, h):
        g = h // hpg
        return vst_ref[HV * g:HV * (g + 1), pl.ds(pl.multiple_of(j * KC, KC), KC)]

    def win_prep(j):
        off = pl.multiple_of(j * KC, KC)
        d = (i - j) * KC
        inside = jnp.logical_and(row_minus_col <= d, row_minus_col > d - WINDOW)
        return off, jnp.where(inside, 0.0, NEG_INF)

    def win_score(ctx, h, dead):
        off, bias = ctx
        g = h // hpg
        return _scores(kw_ref[pl.ds(off, KC), gw * g:gw * (g + 1)], qrz_ref[h], dead) + bias

    def win_values(j, h):
        g = h // hpg
        return vwt_ref[HV * g:HV * (g + 1), pl.ds(pl.multiple_of(j * KC, KC), KC)]

    compressed_and_select()
    assert WINDOW == KC
    slc = _Stage(0, i, NSA_HEADS, HV, slc_prep, slc_score, slc_values, m_ref, al_ref, acc_ref,
                 dead_last="bl")
    win = _Stage(jnp.maximum(i - 1, 0), i, NSA_HEADS, HV, win_prep, win_score, win_values,
                 mw_ref, alw_ref, accw_ref, dead_first="tr", dead_last="bl")
    _attn_chain([slc, win], sbuf_ref, i)

    def head_out(ref, h):
        return ref[HV * h:HV * h + HEAD_DIM, :] / ref[HV * h + HEAD_DIM:HV * h + HEAD_DIM + 1, :]

    gb = _sigmoid(bg_ref[...])
    for hp in range(NSA_HEADS // 2):
        parts = []
        for h in (2 * hp, 2 * hp + 1):
            parts.append(gb[3 * h:3 * h + 1, :] * cmp_ref[HEAD_DIM * h:HEAD_DIM * (h + 1), :]
                         + gb[3 * h + 1:3 * h + 2, :] * head_out(acc_ref, h)
                         + gb[3 * h + 2:3 * h + 3, :] * head_out(accw_ref, h))
        o_ref[:, LANES * hp:LANES * (hp + 1)] = jnp.concatenate(parts, axis=0).T


def _nsa_call(qc, qcr, kcc, vct, ovt, ks, vst, kw, vwt, bgt, n_sel):
    B, S, W = qc.shape
    n_blk = S // SLC_BLOCK
    qspec = pl.BlockSpec((None, TQ, W), lambda b, i: (b, i, 0))
    return pl.pallas_call(
        functools.partial(_nsa_kernel, n_sel=n_sel),
        grid=(B, S // TQ),
        in_specs=[
            qspec, qspec,
            pl.BlockSpec((None,) + kcc.shape[1:], lambda b, i: (b, 0, 0, 0)),
            pl.BlockSpec((None,) + vct.shape[1:], lambda b, i: (b, 0, 0, 0)),
            pl.BlockSpec(ovt.shape, lambda b, i: (0, 0)),
            pl.BlockSpec((None, S, 2 * LANES), lambda b, i: (b, 0, 0)),
            pl.BlockSpec((None, NSA_GROUPS * HV, S), lambda b, i: (b, 0, 0)),
            pl.BlockSpec((None, S, 2 * LANES), lambda b, i: (b, 0, 0)),
            pl.BlockSpec((None, NSA_GROUPS * HV, S), lambda b, i: (b, 0, 0)),
            pl.BlockSpec((None, 3 * NSA_HEADS, TQ), lambda b, i: (b, 0, i)),
        ],
        out_specs=qspec,
        out_shape=jax.ShapeDtypeStruct((B, S, W), F32),
        scratch_shapes=[pltpu.VMEM((W, TQ), F32), pltpu.VMEM((NSA_GROUPS, n_blk, TQ), F32),
                        pltpu.VMEM((NSA_HEADS, TQ, LANES), BF16), pltpu.VMEM((NSA_HEADS, TQ, LANES), BF16)]
        + _attn_scratch(NSA_HEADS, HV)[1:] + _attn_scratch(NSA_HEADS, HV),
        compiler_params=pltpu.CompilerParams(
            dimension_semantics=("arbitrary", "arbitrary"), vmem_limit_bytes=VMEM_LIMIT),
        name="nsa_attn",
    )(qc, qcr, kcc, vct, ovt, ks, vst, kw, vwt, bgt)


def _post_kernel(x_ref, oa_ref, ob_ref, oc_ref, p_ref, g_ref, wg_ref, wup_ref, wout_ref,
                 pg_ref, wple_ref, wpg_ref, fg_ref, o_ref, *, final):
    x = x_ref[...]
    h = _rms(x, g_ref[...]).astype(BF16)
    mixed = jnp.zeros(x.shape, F32)
    for n, br_ref in enumerate((oa_ref, ob_ref, oc_ref)):
        gate = _dot(h, wg_ref[:, BRANCH_WIDTH * n:BRANCH_WIDTH * (n + 1)])
        og = br_ref[...] * (gate * _sigmoid(gate))
        up = _dot(og.astype(BF16), wup_ref[n])
        c0 = N_BRANCH * BRANCH_WIDTH + D_MODEL * n
        mixed = mixed + _sigmoid(_dot(h, wg_ref[:, c0:c0 + D_MODEL])) * up
    x1 = x + _dot(mixed.astype(BF16), wout_ref[...])
    emb = _dot(p_ref[...].astype(BF16), wple_ref[...])
    hg = _dot(_rms(x1, pg_ref[...]).astype(BF16), wpg_ref[...])
    x2 = x1 + emb * _sigmoid(hg)
    if final:
        x2 = _rms(x2, fg_ref[...])
    o_ref[...] = x2


def _post_call(x, oa, ob, oc, p, g, wg, wup, wout, pg, wple, wpg, fg, final, layer):
    T, D = x.shape
    tm = TM_POST
    tok = lambda w: pl.BlockSpec((tm, w), lambda i: (i, 0))
    full = lambda a: _layer_spec(a, layer, pipeline_mode=pl.Buffered(1))
    return pl.pallas_call(
        functools.partial(_post_kernel, final=final),
        grid=(T // tm,),
        in_specs=[tok(D), tok(BRANCH_WIDTH), tok(BRANCH_WIDTH), tok(BRANCH_WIDTH),
                  pl.BlockSpec((None, tm, PLE_DIM), lambda i: (layer, i, 0)),
                  full(g), full(wg), full(wup), full(wout), full(pg), full(wple), full(wpg),
                  pl.BlockSpec(fg.shape, lambda i: (0, 0))],
        out_specs=tok(D),
        out_shape=jax.ShapeDtypeStruct((T, D), F32),
        compiler_params=pltpu.CompilerParams(
            dimension_semantics=("arbitrary",), vmem_limit_bytes=VMEM_LIMIT),
        name="post_mix",
    )(x, oa, ob, oc, p, g, wg, wup, wout, pg, wple, wpg, fg)


def _rope_tables(positions):
    inv = ROPE_THETA ** (-jnp.arange(0, ROT_DIM, 2, dtype=F32) / ROT_DIM)
    ang = positions.astype(F32)[..., None] * inv
    cos, sin = jnp.cos(ang), jnp.sin(ang)
    half = ROT_DIM // 2
    pad = HEAD_DIM - ROT_DIM
    zeros = jnp.zeros(cos.shape[:-1] + (half,), F32)
    rc = jnp.concatenate([cos, cos, jnp.ones(cos.shape[:-1] + (pad,), F32)], axis=-1)
    rs1 = jnp.concatenate([-sin, zeros, jnp.zeros(cos.shape[:-1] + (pad,), F32)], axis=-1)
    rs2 = jnp.concatenate([zeros, sin, jnp.zeros(cos.shape[:-1] + (pad,), F32)], axis=-1)
    rep = lambda t: jnp.concatenate([t] * (LANES // HEAD_DIM), axis=-1)
    return rep(rc), rep(rs1), rep(rs2)


_NAT_PIECES = (
    (0, _A_Q, 512), (512, _A_K, 512), (1024, _B_Q, 512), (1536, _I_Q, 512), (2048, _C_Q, 512),
    (2560, _B_K, 64), (2624, _B_K, 64), (2688, _I_K, 64), (2752, _I_K, 64),
    (2816, _C_KS, 64), (2880, _C_KS, 64), (2944, _C_KS + 64, 64), (3008, _C_KS + 64, 64),
    (3072, _C_KW, 64), (3136, _C_KW, 64), (3200, _C_KW + 64, 64), (3264, _C_KW + 64, 64),
    (3328, _C_KC, 128), (3456, _C_VC, 128))
_GATE_PIECES = ((0, _A_G, 512), (512, _B_G, 512), (1024, _C_G, 512), (1536, _MERGE, 3 * D_MODEL))
_TR_PIECES = ((0, _A_V, 512), (512, _B_V, 64), (576, _C_VS, 128), (704, _C_VW, 128),
              (832, _I_W, 8), (840, _C_BG, 24))
TR_PAD = 896
RELAYOUT_ROWS = 256


def _relayout_kernel(w_ref, nat_ref, gates_ref, trt_ref, trn_ref):
    for dst_ref, pieces in ((nat_ref, _NAT_PIECES), (gates_ref, _GATE_PIECES)):
        for d0, s0, n in pieces:
            dst_ref[:, d0:d0 + n] = w_ref[:, s0:s0 + n].astype(BF16)
    trn_ref[:, N_TR:TR_PAD] = jnp.zeros((trn_ref.shape[0], TR_PAD - N_TR), F32)
    for d0, s0, n in _TR_PIECES:
        trn_ref[:, d0:d0 + n] = w_ref[:, s0:s0 + n]
    for c in range(TR_PAD // LANES):
        rows = min(LANES, N_TR - LANES * c)
        trt_ref[LANES * c:LANES * c + rows, :] = (
            trn_ref[:, LANES * c:LANES * (c + 1)].T[0:rows].astype(BF16))


def _proj_weights(w_in):
    L, D, n_in = w_in.shape
    rb = RELAYOUT_ROWS
    return pl.pallas_call(
        _relayout_kernel,
        grid=(L, D // rb),
        in_specs=[pl.BlockSpec((None, rb, n_in), lambda l, r: (l, r, 0))],
        out_specs=(pl.BlockSpec((None, rb, N_NAT), lambda l, r: (l, r, 0)),
                   pl.BlockSpec((None, rb, BRANCH_WIDTH * N_BRANCH + N_BRANCH * D_MODEL), lambda l, r: (l, r, 0)),
                   pl.BlockSpec((None, N_TR, rb), lambda l, r: (l, 0, r))),
        out_shape=(jax.ShapeDtypeStruct((L, D, N_NAT), BF16),
                   jax.ShapeDtypeStruct((L, D, BRANCH_WIDTH * N_BRANCH + N_BRANCH * D_MODEL), BF16),
                   jax.ShapeDtypeStruct((L, N_TR, D), BF16)),
        scratch_shapes=[pltpu.VMEM((rb, TR_PAD), F32)],
        compiler_params=pltpu.CompilerParams(
            dimension_semantics=("arbitrary", "arbitrary"), vmem_limit_bytes=VMEM_LIMIT),
        name="weight_relayout",
    )(w_in)


def _compress_weights(pe_k, w1_k, w2_k, pe_v, w1_v, w2_v):
    L = pe_k.shape[0]
    half = CMP_LEN // 2
    kdim = half * NSA_GROUPS * HEAD_DIM
    pe = jnp.stack([pe_k, pe_v], axis=1)
    pe = jnp.broadcast_to(pe[:, :, :, None, :], (L, 2, CMP_LEN, NSA_GROUPS, HEAD_DIM))
    w1 = jnp.stack([w1_k, w1_v], axis=1).astype(BF16)
    zero = jnp.zeros_like(w1)
    w1g = jnp.stack([jnp.stack([w1 if h == g else zero for h in range(NSA_GROUPS)], axis=3)
                     for g in range(NSA_GROUPS)], axis=2)
    w1g = w1g.reshape(L, 2 * NSA_GROUPS, CMP_LEN, NSA_GROUPS * HEAD_DIM, CMP_HIDDEN)
    w2 = jnp.stack([w2_k, w2_v], axis=1).astype(BF16)
    return (pe[:, :, :half].reshape(L, 2, 1, kdim), pe[:, :, half:].reshape(L, 2, 1, kdim),
            w1g[:, :, :half].reshape(L, 2 * NSA_GROUPS, kdim, CMP_HIDDEN),
            w1g[:, :, half:].reshape(L, 2 * NSA_GROUPS, kdim, CMP_HIDDEN), w2)


def _overlap_t(S):
    nc = S // CMP_STRIDE
    n_blk = S // SLC_BLOCK
    cstart = np.arange(nc) * CMP_STRIDE
    sstart = np.arange(n_blk) * SLC_BLOCK
    ov = ((cstart[None, :] < sstart[:, None] + SLC_BLOCK) & (cstart[None, :] + CMP_LEN > sstart[:, None]))
    return jnp.asarray(ov.astype(np.float32), dtype=BF16)


def kernel(x, p, positions, norm_g, w_in, diff_lambda, diff_subln_g, cmp_pe_k, cmp_w1_k, cmp_w2_k,
           cmp_pe_v, cmp_w1_v, cmp_w2_v, w_up, w_out, ple_norm_g, w_ple, w_ple_gate, final_norm_g):
    B, S, D = x.shape
    depth = w_in.shape[0]
    k_sel = min(DSA_TOPK_MAX, S // 4)
    n_sel = min(SLC_TOPN_MAX, S // SLC_BLOCK)
    assert S % TM_PROJ == 0 and S % TQ == 0 and k_sel <= TQ and D == D_MODEL
    assert KC == TQ and WINDOW % KC == 0
    rc, rs1, rs2 = _rope_tables(positions)
    ovt = _overlap_t(S)
    half_len = CMP_LEN // 2
    kdim = half_len * NSA_GROUPS * HEAD_DIM

    wn, wg, wt = _proj_weights(w_in)
    cmp_w = _compress_weights(cmp_pe_k, cmp_w1_k, cmp_w2_k, cmp_pe_v, cmp_w1_v, cmp_w2_v)
    g_in = norm_g.reshape(depth, 1, D)
    g_ple = ple_norm_g.reshape(depth, 1, D)
    g_sub = diff_subln_g.reshape(depth, -1, 1)
    wup, wout, wple, wpg = (w.astype(BF16) for w in (w_up, w_out, w_ple, w_ple_gate))
    p2 = p.reshape(depth, B * S, PLE_DIM)

    for i in range(depth):
        lam_init = 0.8 - 0.6 * math.exp(-0.3 * i)
        (qa, ka, qb, iq, qc, qcr, kb, ik, ks, kw, ck, cv,
         vat, vbt, vst, vwt, iwt, bgt) = _proj_call(x, g_in, wn, wt, rc, rs1, rs2, i)

        rk = ck.reshape(B, S // half_len, kdim)
        rv = cv.reshape(B, S // half_len, kdim)
        kcc, vct = _compress_call(rk, rv, *cmp_w, i)

        oa = _diff_call(qa, ka, vat, diff_lambda, g_sub, lam_init, i)
        ob = _dsa_call(qb, iq, kb, ik, vbt, iwt, k_sel)
        oc = _nsa_call(qc, qcr, kcc, vct, ovt, ks, vst, kw, vwt, bgt, n_sel)

        x = _post_call(
            x.reshape(B * S, D), oa.reshape(B * S, -1), ob.reshape(B * S, -1), oc.reshape(B * S, -1),
            p2, g_in, wg, wup, wout, g_ple, wple, wpg, final_norm_g.reshape(1, D),
            final=(i == depth - 1), layer=i).reshape(B, S, D)
    return x
```

```python
import functools
import math

import numpy as np
import jax
import jax.numpy as jnp
from jax import lax
from jax.experimental import pallas as pl
from jax.experimental.pallas import tpu as pltpu

F32 = jnp.float32
BF16 = jnp.bfloat16

D_MODEL = 1024
HEAD_DIM = 64
ROT_DIM = 16
ROPE_THETA = 500000.0
EPS = 1e-6
PLE_DIM = 256
DA_HEADS = 4
DSA_HEADS = 8
IDX_HEADS = 8
DSA_TOPK_MAX = 256
NSA_GROUPS = 2
NSA_HEADS = 8
CMP_LEN = 32
CMP_STRIDE = 16
CMP_HIDDEN = 128
SLC_BLOCK = 64
SLC_TOPN_MAX = 16
WINDOW = 512
FORCE_SCORE = 1e4
BRANCH_WIDTH = 512
N_BRANCH = 3

IN_SPLITS = (512, 512, 512, 512, 512, 64, 64, 512, 512, 64, 8,
             512, 128, 128, 128, 128, 128, 128, 512, 24, 3 * D_MODEL)
(_A_Q, _A_K, _A_V, _A_G, _B_Q, _B_K, _B_V, _B_G, _I_Q, _I_K, _I_W,
 _C_Q, _C_KC, _C_VC, _C_KS, _C_VS, _C_KW, _C_VW, _C_G, _C_BG, _MERGE) = [
    int(v) for v in np.concatenate([[0], np.cumsum(IN_SPLITS)[:-1]])]

LANES = 128
SUBLANES = 8
TQ = 512
KC = 512
TM_PROJ = 512
TM_POST = 512
NEG_INF = float("-inf")
M_INIT = -1e30
N_NAT = 3584
N_TR = 864
BISECT_PLAIN = 10
ONES_ROWS = 16
HV = HEAD_DIM + ONES_ROWS
HV_A = 2 * HEAD_DIM + ONES_ROWS
LOG2E = math.log2(math.e)
VMEM_LIMIT = 56 * 1024 * 1024


def _dot_nt(a, b):
    return lax.dot_general(a, b, (((1,), (1,)), ((), ())), preferred_element_type=F32)


def _dot(a, b):
    return jnp.dot(a, b, preferred_element_type=F32)


def _rms(x, g):
    return x * lax.rsqrt(jnp.mean(x * x, axis=-1, keepdims=True) + EPS) * g


def _sigmoid(x):
    return 1.0 / (1.0 + jnp.exp(-x))


def _store_half_masked(src_ref, dst_ref, n_pairs):
    lane = lax.broadcasted_iota(jnp.int32, (src_ref.shape[0], LANES), 1)
    for hp in range(n_pairs):
        x = src_ref[:, LANES * hp:LANES * (hp + 1)]
        z = jnp.zeros_like(x)
        dst_ref[2 * hp] = jnp.where(lane < HEAD_DIM, x, z)
        dst_ref[2 * hp + 1] = jnp.where(lane >= HEAD_DIM, x, z)


def _fold_rows(x, op):
    n_acc = 2
    accs = [x[SUBLANES * r:SUBLANES * (r + 1)] for r in range(n_acc)]
    for r in range(n_acc, x.shape[0] // SUBLANES):
        accs[r % n_acc] = op(accs[r % n_acc], x[SUBLANES * r:SUBLANES * (r + 1)])
    return op(accs[0], accs[1])


HALF = KC // 2


def _scores(k, q, dead=None):
    if dead is None:
        return _dot_nt(k, q)
    neg = jnp.full((HALF, HALF), NEG_INF, F32)
    if dead == "bl":
        return jnp.concatenate(
            [_dot_nt(k[:HALF], q), jnp.concatenate([neg, _dot_nt(k[HALF:], q[HALF:])], axis=1)], axis=0)
    return jnp.concatenate(
        [jnp.concatenate([_dot_nt(k[:HALF], q[:HALF]), neg], axis=1), _dot_nt(k[HALF:], q)], axis=0)


def _weighted_values(v_t, p, dead=None):
    if dead is None:
        return _dot(v_t, p)
    if dead == "bl":
        return jnp.concatenate([_dot(v_t[:, :HALF], p[:HALF, :HALF]), _dot(v_t, p[:, HALF:])], axis=1)
    return jnp.concatenate([_dot(v_t, p[:, :HALF]), _dot(v_t[:, HALF:], p[HALF:, HALF:])], axis=1)


class _Stage:
    def __init__(self, j0, j1, nh, hv, prep, score, values, m_ref, al_ref, acc_ref,
                 dead_first=None, dead_last=None):
        self.j0 = jnp.asarray(j0, jnp.int32)
        self.j1 = jnp.asarray(j1, jnp.int32)
        self.nh, self.hv, self.prep, self.score, self.values = nh, hv, prep, score, values
        self.m_ref, self.al_ref, self.acc_ref = m_ref, al_ref, acc_ref
        self.dead_first, self.dead_last = dead_first, dead_last

    def reset(self):
        self.m_ref[...] = jnp.full(self.m_ref.shape, M_INIT, F32)
        self.acc_ref[...] = jnp.zeros(self.acc_ref.shape, F32)

    def produce(self, sbuf_ref, j, dead=None):
        ctx = self.prep(j)
        for h in range(self.nh):
            s = self.score(ctx, h, dead)
            m_old = self.m_ref[h:h + 1, :]
            m_new = jnp.maximum(m_old, jnp.max(s, axis=0, keepdims=True))
            sbuf_ref[h] = s
            self.m_ref[h:h + 1, :] = m_new
            self.al_ref[h:h + 1, :] = jnp.exp2(m_old - m_new)

    def consume(self, sbuf_ref, j, dead=None):
        for h in range(self.nh):
            p = jnp.exp2(sbuf_ref[h] - self.m_ref[h:h + 1, :])
            rows = slice(self.hv * h, self.hv * (h + 1))
            self.acc_ref[rows, :] = (self.al_ref[h:h + 1, :] * self.acc_ref[rows, :]
                                     + _weighted_values(self.values(j, h), p.astype(BF16), dead))

    def loop(self, sbuf_ref, lo, hi):
        def body(j, carry):
            self.consume(sbuf_ref, j - 1)
            self.produce(sbuf_ref, j)
            return carry
        lax.fori_loop(lo, hi, body, 0)


def _attn_chain(stages, sbuf_ref, tile):
    for st in stages:
        st.reset()

    @pl.when(tile == 0)
    def _():
        for st in stages:
            st.produce(sbuf_ref, st.j0)
            st.consume(sbuf_ref, st.j0)

    @pl.when(tile > 0)
    def _():
        stages[0].produce(sbuf_ref, stages[0].j0, stages[0].dead_first)
        for n, st in enumerate(stages):
            st.loop(sbuf_ref, st.j0 + 1, st.j1)
            st.consume(sbuf_ref, st.j1 - 1, st.dead_first)
            st.produce(sbuf_ref, st.j1, st.dead_last)
            st.consume(sbuf_ref, st.j1, st.dead_last)
            if n + 1 < len(stages):
                stages[n + 1].produce(sbuf_ref, stages[n + 1].j0, stages[n + 1].dead_first)


def _attn_scratch(nh, hv):
    return [pltpu.VMEM((nh, KC, TQ), F32), pltpu.VMEM((nh, TQ), F32),
            pltpu.VMEM((nh, TQ), F32), pltpu.VMEM((nh * hv, TQ), F32)]


def _layer_spec(a, layer, **kw):
    tail = (0,) * (a.ndim - 1)
    return pl.BlockSpec((None,) + a.shape[1:], lambda *_: (layer,) + tail, **kw)


def _ones_rows(n_cols):
    r = lax.broadcasted_iota(jnp.int32, (ONES_ROWS, n_cols), 0)
    return jnp.where(r == 0, 1.0, 0.0).astype(BF16)


def _proj_kernel(x_ref, g_ref, wn_ref, wt_ref, cs_ref, place_ref,
                 qa_ref, ka_ref, qb_ref, iq_ref, qc_ref, qcr_ref, kb_ref, ik_ref,
                 ks_ref, kw_ref, ck_ref, cv_ref,
                 vat_ref, vbt_ref, vst_ref, vwt_ref, iwt_ref, bgt_ref):
    h = _rms(x_ref[...], g_ref[...]).astype(BF16)
    cs = cs_ref[...]
    hi = cs.astype(BF16)
    rest = cs - hi.astype(F32)
    mid = rest.astype(BF16)
    low = (rest - mid.astype(F32)).astype(BF16)
    spread = _dot(hi, place_ref[...]) + _dot(mid, place_ref[...]) + _dot(low, place_ref[...])
    rc, rs1, rs2 = spread[:, 0:LANES], spread[:, LANES:2 * LANES], spread[:, 2 * LANES:3 * LANES]

    def rope(z):
        return z * rc + pltpu.roll(z, LANES - ROT_DIM // 2, 1) * rs1 + pltpu.roll(z, ROT_DIM // 2, 1) * rs2

    qscale = HEAD_DIM ** -0.5
    qscale2 = qscale * LOG2E
    segs = (
        (0, 512, ((qa_ref, True, qscale2),)),
        (512, 512, ((ka_ref, True, 1.0),)),
        (1024, 512, ((qb_ref, True, qscale2),)),
        (1536, 512, ((iq_ref, True, qscale),)),
        (2048, 512, ((qc_ref, False, qscale), (qcr_ref, True, qscale2))),
        (2560, 128, ((kb_ref, True, 1.0),)),
        (2688, 128, ((ik_ref, True, 1.0),)),
        (2816, 256, ((ks_ref, True, 1.0),)),
        (3072, 256, ((kw_ref, True, 1.0),)),
    )
    zc = _dot(h, wn_ref[:, 3328:3328 + 2 * LANES])
    ck_ref[...] = zc[:, 0:LANES]
    cv_ref[...] = zc[:, LANES:2 * LANES]
    for c0, width, outs in segs:
        z = _dot(h, wn_ref[:, c0:c0 + width])
        for out_ref, rot, scale in outs:
            for j in range(width // LANES):
                zj = z[:, LANES * j:LANES * (j + 1)]
                if rot:
                    zj = rope(zj)
                if scale != 1.0:
                    zj = zj * scale
                out_ref[:, LANES * j:LANES * (j + 1)] = zj.astype(out_ref.dtype)

    zt = _dot_nt(wt_ref[...], h)
    ones = _ones_rows(zt.shape[1])
    vd = 2 * HEAD_DIM
    for h in range(DA_HEADS):
        vat_ref[HV_A * h:HV_A * h + vd, :] = zt[vd * h:vd * (h + 1)].astype(BF16)
        vat_ref[HV_A * h + vd:HV_A * (h + 1), :] = ones
    vbt_ref[0:HEAD_DIM, :] = zt[512:576].astype(BF16)
    vbt_ref[HEAD_DIM:HV, :] = ones
    for g in range(NSA_GROUPS):
        for ref, base in ((vst_ref, 576), (vwt_ref, 704)):
            ref[HV * g:HV * g + HEAD_DIM, :] = zt[base + HEAD_DIM * g:base + HEAD_DIM * (g + 1)].astype(BF16)
            ref[HV * g + HEAD_DIM:HV * (g + 1), :] = ones
    iwt_ref[...] = zt[832:840] * (IDX_HEADS ** -0.5)
    bgt_ref[...] = zt[840:864]


def _proj_call(x, g, wn, wt, cs, place, layer):
    B, S, D = x.shape
    tm = TM_PROJ
    nat = lambda w: pl.BlockSpec((None, tm, w), lambda b, i: (b, i, 0))
    tr = lambda r: pl.BlockSpec((None, r, tm), lambda b, i: (b, 0, i))
    full = lambda a: _layer_spec(a, layer)
    sds = jax.ShapeDtypeStruct
    out_shape = (
        sds((B, S, 512), BF16), sds((B, S, 512), BF16), sds((B, S, 512), BF16), sds((B, S, 512), BF16),
        sds((B, S, 512), BF16), sds((B, S, 512), BF16), sds((B, S, 128), BF16), sds((B, S, 128), BF16),
        sds((B, S, 256), BF16), sds((B, S, 256), BF16),
        sds((B, S, LANES), F32), sds((B, S, LANES), F32),
        sds((B, DA_HEADS * HV_A, S), BF16), sds((B, HV, S), BF16),
        sds((B, NSA_GROUPS * HV, S), BF16), sds((B, NSA_GROUPS * HV, S), BF16),
        sds((B, 8, S), F32), sds((B, 24, S), F32),
    )
    out_specs = (nat(512), nat(512), nat(512), nat(512), nat(512), nat(512), nat(128), nat(128),
                 nat(256), nat(256), nat(LANES), nat(LANES),
                 tr(DA_HEADS * HV_A), tr(HV), tr(NSA_GROUPS * HV), tr(NSA_GROUPS * HV), tr(8), tr(24))
    return pl.pallas_call(
        _proj_kernel,
        grid=(B, S // tm),
        in_specs=[nat(D), full(g), full(wn), full(wt), nat(ROPE_COLS),
                  pl.BlockSpec(place.shape, lambda b, i: (0, 0))],
        out_specs=out_specs,
        out_shape=out_shape,
        compiler_params=pltpu.CompilerParams(
            dimension_semantics=("arbitrary", "arbitrary"), vmem_limit_bytes=VMEM_LIMIT),
        name="in_proj",
    )(x, g, wn, wt, cs, place)


def _compress_kernel(rk_ref, rv_ref, pea_ref, peb_ref, w1a_ref, w1b_ref, w2_ref, nat_ref, t_ref):
    for kind, r_ref in enumerate((rk_ref, rv_ref)):
        r = r_ref[...]
        xa = (r + pea_ref[kind]).astype(BF16)
        xb = (r + peb_ref[kind]).astype(BF16)
        nrow = r.shape[0]
        for g in range(NSA_GROUPS):
            n = NSA_GROUPS * kind + g
            hid = jax.nn.gelu(_dot(xa, w1a_ref[n]) + pltpu.roll(_dot(xb, w1b_ref[n]), nrow - 1, 0))
            o = _dot(hid.astype(BF16), w2_ref[kind])
            nat_ref[n] = jnp.concatenate([o, o], axis=1).astype(BF16)
            t_ref[n] = o.T.astype(BF16)


def _compress_call(rk, rv, pea, peb, w1a, w1b, w2, layer):
    B, nr, kdim = rk.shape
    full = lambda a: _layer_spec(a, layer)
    rspec = pl.BlockSpec((None, nr, kdim), lambda b: (b, 0, 0))
    return pl.pallas_call(
        _compress_kernel,
        grid=(B,),
        in_specs=[rspec, rspec, full(pea), full(peb), full(w1a), full(w1b), full(w2)],
        out_specs=(
            pl.BlockSpec((None, 2 * NSA_GROUPS, nr, 2 * HEAD_DIM), lambda b: (b, 0, 0, 0)),
            pl.BlockSpec((None, 2 * NSA_GROUPS, HEAD_DIM, nr), lambda b: (b, 0, 0, 0)),
        ),
        out_shape=(jax.ShapeDtypeStruct((B, 2 * NSA_GROUPS, nr, 2 * HEAD_DIM), BF16),
                   jax.ShapeDtypeStruct((B, 2 * NSA_GROUPS, HEAD_DIM, nr), BF16)),
        compiler_params=pltpu.CompilerParams(dimension_semantics=("arbitrary",)),
        name="nsa_compress",
    )(rk, rv, pea, peb, w1a, w1b, w2)


def _diff_kernel(q_ref, k_ref, vt_ref, lam_ref, sg_ref, o_ref,
                 qz_ref, sbuf_ref, m_ref, al_ref, acc_ref, *, lam_init):
    i = pl.program_id(1)
    lp = lam_ref[...]
    lam = (jnp.exp(jnp.sum(lp[0:1] * lp[1:2], axis=1, keepdims=True))
           - jnp.exp(jnp.sum(lp[2:3] * lp[3:4], axis=1, keepdims=True)) + lam_init)
    row_minus_col = (lax.broadcasted_iota(jnp.int32, (KC, TQ), 0)
                     - lax.broadcasted_iota(jnp.int32, (KC, TQ), 1))
    _store_half_masked(q_ref, qz_ref, DA_HEADS)
    vdim = 2 * HEAD_DIM
    n_maps = 2 * DA_HEADS

    def prep(j):
        return pl.multiple_of(j * KC, KC), jnp.where(row_minus_col <= (i - j) * KC, 0.0, NEG_INF)

    def score(ctx, hm, dead):
        off, bias = ctx
        h = hm // 2
        return _scores(k_ref[pl.ds(off, KC), vdim * h:vdim * (h + 1)], qz_ref[hm], dead) + bias

    def values(j, hm):
        h = hm // 2
        return vt_ref[HV_A * h:HV_A * (h + 1), pl.ds(pl.multiple_of(j * KC, KC), KC)]

    _attn_chain([_Stage(0, i, n_maps, HV_A, prep, score, values, m_ref, al_ref, acc_ref, dead_last="bl")],
                sbuf_ref, i)

    for h in range(DA_HEADS):
        r0, r1 = HV_A * 2 * h, HV_A * (2 * h + 1)
        o_t = (acc_ref[r0:r0 + vdim, :] / acc_ref[r0 + vdim:r0 + vdim + 1, :]
               - lam * (acc_ref[r1:r1 + vdim, :] / acc_ref[r1 + vdim:r1 + vdim + 1, :]))
        ms = jnp.mean(o_t * o_t, axis=0, keepdims=True)
        y = o_t * lax.rsqrt(ms + EPS) * sg_ref[...] * (1.0 - lam_init)
        o_ref[:, vdim * h:vdim * (h + 1)] = y.T


def _diff_call(qa, ka, vat, lam_p, sg, lam_init, layer):
    B, S, W = qa.shape
    n_maps = 2 * DA_HEADS
    return pl.pallas_call(
        functools.partial(_diff_kernel, lam_init=lam_init),
        grid=(B, S // TQ),
        in_specs=[
            pl.BlockSpec((None, TQ, W), lambda b, i: (b, i, 0)),
            pl.BlockSpec((None, S, W), lambda b, i: (b, 0, 0)),
            pl.BlockSpec((None, DA_HEADS * HV_A, S), lambda b, i: (b, 0, 0)),
            _layer_spec(lam_p, layer),
            _layer_spec(sg, layer),
        ],
        out_specs=pl.BlockSpec((None, TQ, W), lambda b, i: (b, i, 0)),
        out_shape=jax.ShapeDtypeStruct((B, S, W), F32),
        scratch_shapes=[pltpu.VMEM((n_maps, TQ, LANES), BF16)] + _attn_scratch(n_maps, HV_A),
        compiler_params=pltpu.CompilerParams(
            dimension_semantics=("arbitrary", "arbitrary"), vmem_limit_bytes=VMEM_LIMIT),
        name="diff_attn",
    )(qa, ka, vat, lam_p, sg)


def _dsa_kernel(q_ref, iq_ref, kb_ref, ik_ref, vt_ref, iw_ref, o_ref,
                sc_ref, tj_ref, tri_ref, qz_ref, sbuf_ref, m_ref, al_ref, acc_ref, *, k_sel, seq_len):
    i = pl.program_id(1)
    nch = i + 1
    row = lax.broadcasted_iota(jnp.int32, (KC, TQ), 0)
    col = lax.broadcasted_iota(jnp.int32, (KC, TQ), 1)
    causal = row <= col
    iw = iw_ref[...]
    kf = float(k_sel)
    _store_half_masked(iq_ref, qz_ref, IDX_HEADS // 2)

    def idx_chunk(j, carry, masked):
        mx, mn = carry
        off = pl.multiple_of(j * KC, KC)
        ikc = ik_ref[pl.ds(off, KC), :]
        sc = jnp.zeros((KC, TQ), F32)
        for h in range(IDX_HEADS):
            dots = _scores(ikc, qz_ref[h], "bl" if masked else None)
            sc = sc + iw[h:h + 1, :] * jnp.maximum(dots, 0.0)
        lo_src = sc
        if masked:
            lo_src = jnp.where(causal, sc, jnp.inf)
            sc = jnp.where(causal, sc, NEG_INF)
        sc_ref[pl.ds(off, KC), :] = sc
        return (jnp.maximum(mx, _fold_rows(sc, jnp.maximum)),
                jnp.minimum(mn, _fold_rows(lo_src, jnp.minimum)))

    carry = (jnp.full((SUBLANES, TQ), NEG_INF, F32), jnp.full((SUBLANES, TQ), jnp.inf, F32))
    carry = lax.fori_loop(0, i, functools.partial(idx_chunk, masked=False), carry)
    mx8, mn8 = idx_chunk(i, carry, True)
    row_max = jnp.max(mx8, axis=0, keepdims=True)
    row_min = jnp.min(mn8, axis=0, keepdims=True)

    def probe_pass(th, snap):
        thb = jnp.broadcast_to(th, (SUBLANES, TQ))

        def body(j, c):
            off = pl.multiple_of(j * KC, KC)
            c = list(c)
            for r in range(KC // SUBLANES):
                x = sc_ref[pl.ds(off + SUBLANES * r, SUBLANES), :]
                ge = x >= thb
                c[0] = c[0] + jnp.where(ge, 1.0, 0.0)
                if snap:
                    c[1] = jnp.maximum(c[1], jnp.where(ge, NEG_INF, x))
            return tuple(c)
        init = (jnp.zeros((SUBLANES, TQ), F32),)
        if snap:
            init += (jnp.full((SUBLANES, TQ), NEG_INF, F32),)
        out = lax.fori_loop(0, nch, body, init)
        cnt = jnp.sum(out[0], axis=0, keepdims=True)
        return (cnt, jnp.max(out[1], axis=0, keepdims=True)) if snap else cnt

    n_causal = (i * TQ + 1 + lax.broadcasted_iota(jnp.int32, (1, TQ), 1)).astype(F32)
    keep_all = n_causal <= kf

    def bisect(_, c):
        lo, hi, clo, chi = c
        mid = lo + (hi - lo) * 0.5
        cnt = probe_pass(mid, False)
        ge = cnt >= kf
        return (jnp.where(ge, mid, lo), jnp.where(ge, hi, mid),
                jnp.where(ge, cnt, clo), jnp.where(ge, chi, cnt))

    hi0 = row_max + (row_max - row_min) + 1.0
    lo, hi, clo, chi = lax.fori_loop(
        0, BISECT_PLAIN, bisect, (row_min, hi0, n_causal, jnp.zeros((1, TQ), F32)))

    def cond(st):
        return jnp.logical_and(st[0] < 256, st[-1] > 0.0)

    def snap_step(st):
        it, lo, hi, clo, chi, hie, known, stuck, done, _ = st
        top = jnp.where(known > 0.0, hie, hi)
        mid = lo + (top - lo) * 0.5
        inside = jnp.logical_and(mid > lo, mid < top)
        near_top = jnp.logical_or(kf - chi <= 2.0, stuck > 0.0)
        use_top = jnp.logical_or(jnp.logical_and(known > 0.0, near_top), jnp.logical_not(inside))
        th = jnp.where(use_top, top, mid)
        cnt, edn = probe_pass(th, True)
        live = done <= 0.0
        ge = jnp.logical_and(live, cnt >= kf)
        lt = jnp.logical_and(live, cnt < kf)
        stuck = jnp.where(jnp.logical_and(ge, cnt == clo), 1.0, 0.0)
        lo = jnp.where(ge, th, lo)
        clo = jnp.where(ge, cnt, clo)
        hi = jnp.where(lt, th, hi)
        chi = jnp.where(lt, cnt, chi)
        hie = jnp.where(lt, edn, hie)
        known = jnp.where(lt, 1.0, known)
        fin = jnp.logical_or(clo == kf, jnp.logical_and(known > 0.0, lo >= hie))
        done = jnp.where(fin, 1.0, done)
        return it + 1, lo, hi, clo, chi, hie, known, stuck, done, jnp.max(1.0 - done)

    zero = jnp.zeros((1, TQ), F32)
    done0 = jnp.where(jnp.logical_or(clo == kf, keep_all), 1.0, 0.0)
    st = lax.while_loop(cond, snap_step, (jnp.int32(0), lo, hi, clo, chi, jnp.full((1, TQ), NEG_INF, F32),
                                          zero, zero, done0, jnp.max(1.0 - done0)))
    thr, clo, chi = st[1], st[3], st[4]
    need = jnp.where(clo == kf, float(seq_len), kf - chi)
    thr = jnp.where(keep_all, NEG_INF, thr)
    need = jnp.where(keep_all, 0.0, need)
    tj_ref[2:3, :] = jnp.zeros((1, TQ), F32)

    _store_half_masked(q_ref, qz_ref, DSA_HEADS // 2)
    half = KC // 2
    tri_ref[...] = jnp.where(lax.broadcasted_iota(jnp.int32, (half, half), 0)
                             >= lax.broadcasted_iota(jnp.int32, (half, half), 1), 1.0, 0.0).astype(BF16)

    def prep(j):
        off = pl.multiple_of(j * KC, KC)
        x = sc_ref[pl.ds(off, KC), :]
        eq = x == thr
        e = jnp.where(eq, 1.0, 0.0).astype(BF16)
        r_top = _dot(tri_ref[...], e[:half]) + tj_ref[2:3, :]
        r_bot = _dot(tri_ref[...], e[half:]) + r_top[half - 1:half, :]
        tj_ref[2:3, :] = r_bot[half - 1:half, :]
        rank = jnp.concatenate([r_top, r_bot], axis=0)
        tie_bias = jnp.where(jnp.logical_and(eq, rank <= need), 0.0, NEG_INF)
        return kb_ref[pl.ds(off, KC), :], jnp.where(x > thr, 0.0, tie_bias)

    def score(ctx, h, dead):
        kc, bias = ctx
        return _scores(kc, qz_ref[h], dead) + bias

    def values(j, h):
        return vt_ref[:, pl.ds(pl.multiple_of(j * KC, KC), KC)]

    _attn_chain([_Stage(0, i, DSA_HEADS, HV, prep, score, values, m_ref, al_ref, acc_ref, dead_last="bl")],
                sbuf_ref, i)
    for hp in range(DSA_HEADS // 2):
        blk = jnp.concatenate(
            [acc_ref[HV * h:HV * h + HEAD_DIM, :] / acc_ref[HV * h + HEAD_DIM:HV * h + HEAD_DIM + 1, :]
             for h in (2 * hp, 2 * hp + 1)], axis=0)
        o_ref[:, LANES * hp:LANES * (hp + 1)] = blk.T


def _dsa_call(qb, iq, kb, ik, vbt, iwt, k_sel):
    B, S, W = qb.shape
    return pl.pallas_call(
        functools.partial(_dsa_kernel, k_sel=k_sel, seq_len=S),
        grid=(B, S // TQ),
        in_specs=[
            pl.BlockSpec((None, TQ, W), lambda b, i: (b, i, 0)),
            pl.BlockSpec((None, TQ, W), lambda b, i: (b, i, 0)),
            pl.BlockSpec((None, S, LANES), lambda b, i: (b, 0, 0)),
            pl.BlockSpec((None, S, LANES), lambda b, i: (b, 0, 0)),
            pl.BlockSpec((None, HV, S), lambda b, i: (b, 0, 0)),
            pl.BlockSpec((None, IDX_HEADS, TQ), lambda b, i: (b, 0, i)),
        ],
        out_specs=pl.BlockSpec((None, TQ, W), lambda b, i: (b, i, 0)),
        out_shape=jax.ShapeDtypeStruct((B, S, W), F32),
        scratch_shapes=[pltpu.VMEM((S, TQ), F32), pltpu.VMEM((SUBLANES, TQ), F32),
                        pltpu.VMEM((KC // 2, KC // 2), BF16),
                        pltpu.VMEM((DSA_HEADS, TQ, LANES), BF16)] + _attn_scratch(DSA_HEADS, HV),
        compiler_params=pltpu.CompilerParams(
            dimension_semantics=("arbitrary", "arbitrary"), vmem_limit_bytes=VMEM_LIMIT),
        name="dsa_attn",
    )(qb, iq, kb, ik, vbt, iwt)


def _nsa_kernel(q_ref, qr_ref, kc_ref, vct_ref, ovt_ref, ks_ref, vst_ref, kw_ref, vwt_ref, bg_ref,
                o_ref, cmp_ref, sel_ref, qz_ref, qrz_ref, mw_ref, alw_ref, accw_ref,
                sbuf_ref, m_ref, al_ref, acc_ref, *, n_sel):
    i = pl.program_id(1)
    row_minus_col = (lax.broadcasted_iota(jnp.int32, (KC, TQ), 0)
                     - lax.broadcasted_iota(jnp.int32, (KC, TQ), 1))
    tq = i * TQ + lax.broadcasted_iota(jnp.int32, (1, TQ), 1)
    hpg = NSA_HEADS // NSA_GROUPS
    nc = kc_ref.shape[1]
    n_blk = sel_ref.shape[1]
    gw = 2 * HEAD_DIM

    _store_half_masked(q_ref, qz_ref, NSA_HEADS // 2)
    _store_half_masked(qr_ref, qrz_ref, NSA_HEADS // 2)

    def compressed_and_select():
        cmp_valid = CMP_STRIDE * lax.broadcasted_iota(jnp.int32, (nc, TQ), 0) + (CMP_LEN - 1) <= tq
        blk = lax.broadcasted_iota(jnp.int32, (n_blk, TQ), 0)
        forced = jnp.logical_or(blk == jnp.right_shift(tq, int(math.log2(SLC_BLOCK))), blk == 0)
        blk_causal = blk * SLC_BLOCK <= tq
        for h in range(NSA_HEADS):
            s = jnp.where(cmp_valid, _dot_nt(kc_ref[h // hpg], qz_ref[h]), NEG_INF)
            sbuf_ref[h, 0:nc, :] = s
            m_ref[h:h + 1, :] = jnp.maximum(jnp.max(s, axis=0, keepdims=True), M_INIT)
        for g in range(NSA_GROUPS):
            vcg = vct_ref[NSA_GROUPS + g]
            imp = jnp.zeros((n_blk, TQ), F32)
            for hh in range(hpg):
                h = hpg * g + hh
                p = jnp.exp(sbuf_ref[h, 0:nc, :] - m_ref[h:h + 1, :])
                l = jnp.sum(p, axis=0, keepdims=True)
                pc = (p * jnp.where(l > 0.0, 1.0 / l, 0.0)).astype(BF16)
                cmp_ref[HEAD_DIM * h:HEAD_DIM * (h + 1), :] = _dot(vcg, pc)
                imp = imp + _dot(ovt_ref[...], pc)
            imp = jnp.where(forced, FORCE_SCORE, imp)
            imp = jnp.where(blk_causal, imp, NEG_INF)
            n_tiles = n_blk // SUBLANES
            sub = lax.broadcasted_iota(jnp.int32, (SUBLANES, TQ), 0)
            tiles = [imp[SUBLANES * t:SUBLANES * (t + 1)] for t in range(n_tiles)]
            ranks = [jnp.zeros((SUBLANES, TQ), F32)] * n_tiles
            for mp in range(n_blk):
                r = jnp.broadcast_to(imp[mp:mp + 1, :], (SUBLANES, TQ))
                for t in range(n_tiles):
                    if SUBLANES * t > mp:
                        ahead = r >= tiles[t]
                    elif SUBLANES * (t + 1) <= mp:
                        ahead = r > tiles[t]
                    else:
                        later = sub > mp - SUBLANES * t
                        ahead = jnp.logical_or(r > tiles[t], jnp.logical_and(r == tiles[t], later))
                    ranks[t] = ranks[t] + jnp.where(ahead, 1.0, 0.0)
            rank = jnp.concatenate(ranks, axis=0)
            sel_ref[g] = jnp.where(rank < float(n_sel), 0.0, NEG_INF)

    bpc = KC // SLC_BLOCK

    def slc_prep(j):
        off = pl.multiple_of(j * KC, KC)
        visible = row_minus_col <= (i - j) * KC
        biases = []
        for g in range(NSA_GROUPS):
            bias = jnp.concatenate(
                [jnp.broadcast_to(sel_ref[g, pl.ds(bpc * j + b, 1), :], (SLC_BLOCK, TQ))
                 for b in range(bpc)], axis=0)
            biases.append(jnp.where(visible, bias, NEG_INF))
        return off, biases

    def slc_score(ctx, h, dead):
        off, biases = ctx
        g = h // hpg
        return _scores(ks_ref[pl.ds(off, KC), gw * g:gw * (g + 1)], qrz_ref[h], dead) + biases[g]

    def slc_values(j, h):
        g = h // hpg
        return vst_ref[HV * g:HV * (g + 1), pl.ds(pl.multiple_of(j * KC, KC), KC)]

    def win_prep(j):
        off = pl.multiple_of(j * KC, KC)
        d = (i - j) * KC
        inside = jnp.logical_and(row_minus_col <= d, row_minus_col > d - WINDOW)
        return off, jnp.where(inside, 0.0, NEG_INF)

    def win_score(ctx, h, dead):
        off, bias = ctx
        g = h // hpg
        return _scores(kw_ref[pl.ds(off, KC), gw * g:gw * (g + 1)], qrz_ref[h], dead) + bias

    def win_values(j, h):
        g = h // hpg
        return vwt_ref[HV * g:HV * (g + 1), pl.ds(pl.multiple_of(j * KC, KC), KC)]

    compressed_and_select()
    assert WINDOW == KC
    slc = _Stage(0, i, NSA_HEADS, HV, slc_prep, slc_score, slc_values, m_ref, al_ref, acc_ref,
                 dead_last="bl")
    win = _Stage(jnp.maximum(i - 1, 0), i, NSA_HEADS, HV, win_prep, win_score, win_values,
                 mw_ref, alw_ref, accw_ref, dead_first="tr", dead_last="bl")
    _attn_chain([slc, win], sbuf_ref, i)

    def head_out(ref, h):
        return ref[HV * h:HV * h + HEAD_DIM, :] / ref[HV * h + HEAD_DIM:HV * h + HEAD_DIM + 1, :]

    gb = _sigmoid(bg_ref[...])
    for hp in range(NSA_HEADS // 2):
        parts = []
        for h in (2 * hp, 2 * hp + 1):
            parts.append(gb[3 * h:3 * h + 1, :] * cmp_ref[HEAD_DIM * h:HEAD_DIM * (h + 1), :]
                         + gb[3 * h + 1:3 * h + 2, :] * head_out(acc_ref, h)
                         + gb[3 * h + 2:3 * h + 3, :] * head_out(accw_ref, h))
        o_ref[:, LANES * hp:LANES * (hp + 1)] = jnp.concatenate(parts, axis=0).T


def _nsa_call(qc, qcr, kcc, vct, ovt, ks, vst, kw, vwt, bgt, n_sel):
    B, S, W = qc.shape
    n_blk = S // SLC_BLOCK
    qspec = pl.BlockSpec((None, TQ, W), lambda b, i: (b, i, 0))
    return pl.pallas_call(
        functools.partial(_nsa_kernel, n_sel=n_sel),
        grid=(B, S // TQ),
        in_specs=[
            qspec, qspec,
            pl.BlockSpec((None,) + kcc.shape[1:], lambda b, i: (b, 0, 0, 0)),
            pl.BlockSpec((None,) + vct.shape[1:], lambda b, i: (b, 0, 0, 0)),
            pl.BlockSpec(ovt.shape, lambda b, i: (0, 0)),
            pl.BlockSpec((None, S, 2 * LANES), lambda b, i: (b, 0, 0)),
            pl.BlockSpec((None, NSA_GROUPS * HV, S), lambda b, i: (b, 0, 0)),
            pl.BlockSpec((None, S, 2 * LANES), lambda b, i: (b, 0, 0)),
            pl.BlockSpec((None, NSA_GROUPS * HV, S), lambda b, i: (b, 0, 0)),
            pl.BlockSpec((None, 3 * NSA_HEADS, TQ), lambda b, i: (b, 0, i)),
        ],
        out_specs=qspec,
        out_shape=jax.ShapeDtypeStruct((B, S, W), F32),
        scratch_shapes=[pltpu.VMEM((W, TQ), F32), pltpu.VMEM((NSA_GROUPS, n_blk, TQ), F32),
                        pltpu.VMEM((NSA_HEADS, TQ, LANES), BF16), pltpu.VMEM((NSA_HEADS, TQ, LANES), BF16)]
        + _attn_scratch(NSA_HEADS, HV)[1:] + _attn_scratch(NSA_HEADS, HV),
        compiler_params=pltpu.CompilerParams(
            dimension_semantics=("arbitrary", "arbitrary"), vmem_limit_bytes=VMEM_LIMIT),
        name="nsa_attn",
    )(qc, qcr, kcc, vct, ovt, ks, vst, kw, vwt, bgt)


def _post_kernel(x_ref, oa_ref, ob_ref, oc_ref, p_ref, g_ref, wg_ref, wup_ref, wout_ref,
                 pg_ref, wple_ref, wpg_ref, fg_ref, o_ref, *, final):
    x = x_ref[...]
    h = _rms(x, g_ref[...]).astype(BF16)
    mixed = jnp.zeros(x.shape, F32)
    for n, br_ref in enumerate((oa_ref, ob_ref, oc_ref)):
        gate = _dot(h, wg_ref[:, BRANCH_WIDTH * n:BRANCH_WIDTH * (n + 1)])
        og = br_ref[...] * (gate * _sigmoid(gate))
        up = _dot(og.astype(BF16), wup_ref[n])
        c0 = N_BRANCH * BRANCH_WIDTH + D_MODEL * n
        mixed = mixed + _sigmoid(_dot(h, wg_ref[:, c0:c0 + D_MODEL])) * up
    x1 = x + _dot(mixed.astype(BF16), wout_ref[...])
    emb = _dot(p_ref[...].astype(BF16), wple_ref[...])
    hg = _dot(_rms(x1, pg_ref[...]).astype(BF16), wpg_ref[...])
    x2 = x1 + emb * _sigmoid(hg)
    if final:
        x2 = _rms(x2, fg_ref[...])
    o_ref[...] = x2


def _post_call(x, oa, ob, oc, p, g, wg, wup, wout, pg, wple, wpg, fg, final, layer):
    T, D = x.shape
    tm = TM_POST
    tok = lambda w: pl.BlockSpec((tm, w), lambda i: (i, 0))
    full = lambda a: _layer_spec(a, layer, pipeline_mode=pl.Buffered(1))
    return pl.pallas_call(
        functools.partial(_post_kernel, final=final),
        grid=(T // tm,),
        in_specs=[tok(D), tok(BRANCH_WIDTH), tok(BRANCH_WIDTH), tok(BRANCH_WIDTH),
                  pl.BlockSpec((None, tm, PLE_DIM), lambda i: (layer, i, 0)),
                  full(g), full(wg), full(wup), full(wout), full(pg), full(wple), full(wpg),
                  pl.BlockSpec(fg.shape, lambda i: (0, 0))],
        out_specs=tok(D),
        out_shape=jax.ShapeDtypeStruct((T, D), F32),
        compiler_params=pltpu.CompilerParams(
            dimension_semantics=("arbitrary",), vmem_limit_bytes=VMEM_LIMIT),
        name="post_mix",
    )(x, oa, ob, oc, p, g, wg, wup, wout, pg, wple, wpg, fg)


ROPE_COLS = 3 * (ROT_DIM // 2)


def _rope_inputs(positions):
    half = ROT_DIM // 2
    inv = ROPE_THETA ** (-jnp.arange(0, ROT_DIM, 2, dtype=F32) / ROT_DIM)
    ang = positions.astype(F32)[..., None] * inv
    cs = jnp.concatenate([jnp.cos(ang), jnp.sin(ang), jnp.ones_like(ang)], axis=-1)
    place = np.zeros((ROPE_COLS, 3 * LANES), np.float32)
    for head in range(LANES // HEAD_DIM):
        base = HEAD_DIM * head
        for j in range(half):
            place[j, base + j] = 1.0
            place[j, base + half + j] = 1.0
            place[half + j, LANES + base + j] = -1.0
            place[half + j, 2 * LANES + base + half + j] = 1.0
        place[2 * half, base + ROT_DIM:base + HEAD_DIM] = 1.0
    return cs, jnp.asarray(place, dtype=BF16)


_NAT_PIECES = (
    (0, _A_Q, 512), (512, _A_K, 512), (1024, _B_Q, 512), (1536, _I_Q, 512), (2048, _C_Q, 512),
    (2560, _B_K, 64), (2624, _B_K, 64), (2688, _I_K, 64), (2752, _I_K, 64),
    (2816, _C_KS, 64), (2880, _C_KS, 64), (2944, _C_KS + 64, 64), (3008, _C_KS + 64, 64),
    (3072, _C_KW, 64), (3136, _C_KW, 64), (3200, _C_KW + 64, 64), (3264, _C_KW + 64, 64),
    (3328, _C_KC, 128), (3456, _C_VC, 128))
_GATE_PIECES = ((0, _A_G, 512), (512, _B_G, 512), (1024, _C_G, 512), (1536, _MERGE, 3 * D_MODEL))
_TR_PIECES = ((0, _A_V, 512), (512, _B_V, 64), (576, _C_VS, 128), (704, _C_VW, 128),
              (832, _I_W, 8), (840, _C_BG, 24))
def _relayout_kernel(wt_ref, nat_ref, gates_ref, trt_ref):
    for dst_ref, pieces in ((nat_ref, _NAT_PIECES), (gates_ref, _GATE_PIECES)):
        for d0, s0, n in pieces:
            for r in range(0, n, LANES):
                m = min(LANES, n - r)
                dst_ref[:, d0 + r:d0 + r + m] = wt_ref[s0 + r:s0 + r + m, :].T.astype(BF16)
    groups, cur = [], []
    for d0, s0, n in _TR_PIECES:
        if cur and d0 % ONES_ROWS == 0:
            groups.append(cur)
            cur = []
        cur.append((d0, s0, n))
    groups.append(cur)
    for grp in groups:
        rows = jnp.concatenate([wt_ref[s0:s0 + n, :] for _, s0, n in grp], axis=0)
        trt_ref[grp[0][0]:grp[0][0] + rows.shape[0], :] = rows.astype(BF16)


def _proj_weights(w_in):
    L, D, n_in = w_in.shape
    w_t = jnp.swapaxes(w_in, 1, 2)
    return pl.pallas_call(
        _relayout_kernel,
        grid=(L, D // LANES),
        in_specs=[pl.BlockSpec((None, n_in, LANES), lambda l, r: (l, 0, r))],
        out_specs=(pl.BlockSpec((None, LANES, N_NAT), lambda l, r: (l, r, 0)),
                   pl.BlockSpec((None, LANES, BRANCH_WIDTH * N_BRANCH + N_BRANCH * D_MODEL), lambda l, r: (l, r, 0)),
                   pl.BlockSpec((None, N_TR, LANES), lambda l, r: (l, 0, r))),
        out_shape=(jax.ShapeDtypeStruct((L, D, N_NAT), BF16),
                   jax.ShapeDtypeStruct((L, D, BRANCH_WIDTH * N_BRANCH + N_BRANCH * D_MODEL), BF16),
                   jax.ShapeDtypeStruct((L, N_TR, D), BF16)),
        compiler_params=pltpu.CompilerParams(
            dimension_semantics=("arbitrary", "arbitrary"), vmem_limit_bytes=VMEM_LIMIT),
        name="weight_relayout",
    )(w_t)


def _compress_weights(pe_k, w1_k, w2_k, pe_v, w1_v, w2_v):
    L = pe_k.shape[0]
    half = CMP_LEN // 2
    kdim = half * NSA_GROUPS * HEAD_DIM
    pe = jnp.stack([pe_k, pe_v], axis=1)
    pe = jnp.broadcast_to(pe[:, :, :, None, :], (L, 2, CMP_LEN, NSA_GROUPS, HEAD_DIM))
    w1 = jnp.stack([w1_k, w1_v], axis=1).astype(BF16)
    zero = jnp.zeros_like(w1)
    w1g = jnp.stack([jnp.stack([w1 if h == g else zero for h in range(NSA_GROUPS)], axis=3)
                     for g in range(NSA_GROUPS)], axis=2)
    w1g = w1g.reshape(L, 2 * NSA_GROUPS, CMP_LEN, NSA_GROUPS * HEAD_DIM, CMP_HIDDEN)
    w2 = jnp.stack([w2_k, w2_v], axis=1).astype(BF16)
    return (pe[:, :, :half].reshape(L, 2, 1, kdim), pe[:, :, half:].reshape(L, 2, 1, kdim),
            w1g[:, :, :half].reshape(L, 2 * NSA_GROUPS, kdim, CMP_HIDDEN),
            w1g[:, :, half:].reshape(L, 2 * NSA_GROUPS, kdim, CMP_HIDDEN), w2)


def _overlap_t(S):
    nc = S // CMP_STRIDE
    n_blk = S // SLC_BLOCK
    cstart = np.arange(nc) * CMP_STRIDE
    sstart = np.arange(n_blk) * SLC_BLOCK
    ov = ((cstart[None, :] < sstart[:, None] + SLC_BLOCK) & (cstart[None, :] + CMP_LEN > sstart[:, None]))
    return jnp.asarray(ov.astype(np.float32), dtype=BF16)


def kernel(x, p, positions, norm_g, w_in, diff_lambda, diff_subln_g, cmp_pe_k, cmp_w1_k, cmp_w2_k,
           cmp_pe_v, cmp_w1_v, cmp_w2_v, w_up, w_out, ple_norm_g, w_ple, w_ple_gate, final_norm_g):
    B, S, D = x.shape
    depth = w_in.shape[0]
    k_sel = min(DSA_TOPK_MAX, S // 4)
    n_sel = min(SLC_TOPN_MAX, S // SLC_BLOCK)
    assert S % TM_PROJ == 0 and S % TQ == 0 and k_sel <= TQ and D == D_MODEL
    assert KC == TQ and WINDOW % KC == 0
    cs, place = _rope_inputs(positions)
    ovt = _overlap_t(S)
    half_len = CMP_LEN // 2
    kdim = half_len * NSA_GROUPS * HEAD_DIM

    wn, wg, wt = _proj_weights(w_in)
    cmp_w = _compress_weights(cmp_pe_k, cmp_w1_k, cmp_w2_k, cmp_pe_v, cmp_w1_v, cmp_w2_v)
    g_in = norm_g.reshape(depth, 1, D)
    g_ple = ple_norm_g.reshape(depth, 1, D)
    g_sub = diff_subln_g.reshape(depth, -1, 1)
    wup, wout, wple, wpg = (w.astype(BF16) for w in (w_up, w_out, w_ple, w_ple_gate))
    p2 = p.reshape(depth, B * S, PLE_DIM)

    for i in range(depth):
        lam_init = 0.8 - 0.6 * math.exp(-0.3 * i)
        (qa, ka, qb, iq, qc, qcr, kb, ik, ks, kw, ck, cv,
         vat, vbt, vst, vwt, iwt, bgt) = _proj_call(x, g_in, wn, wt, cs, place, i)

        rk = ck.reshape(B, S // half_len, kdim)
        rv = cv.reshape(B, S // half_len, kdim)
        kcc, vct = _compress_call(rk, rv, *cmp_w, i)

        oa = _diff_call(qa, ka, vat, diff_lambda, g_sub, lam_init, i)
        ob = _dsa_call(qb, iq, kb, ik, vbt, iwt, k_sel)
        oc = _nsa_call(qc, qcr, kcc, vct, ovt, ks, vst, kw, vwt, bgt, n_sel)

        x = _post_call(
            x.reshape(B * S, D), oa.reshape(B * S, -1), ob.reshape(B * S, -1), oc.reshape(B * S, -1),
            p2, g_in, wg, wup, wout, g_ple, wple, wpg, final_norm_g.reshape(1, D),
            final=(i == depth - 1), layer=i).reshape(B, S, D)
    return x
```

```python
import functools
import math

import numpy as np
import jax
import jax.numpy as jnp
from jax import lax
from jax.experimental import pallas as pl
from jax.experimental.pallas import tpu as pltpu

F32 = jnp.float32
BF16 = jnp.bfloat16

D_MODEL = 1024
HEAD_DIM = 64
ROT_DIM = 16
ROPE_THETA = 500000.0
EPS = 1e-6
PLE_DIM = 256
DA_HEADS = 4
DSA_HEADS = 8
IDX_HEADS = 8
DSA_TOPK_MAX = 256
NSA_GROUPS = 2
NSA_HEADS = 8
CMP_LEN = 32
CMP_STRIDE = 16
CMP_HIDDEN = 128
SLC_BLOCK = 64
SLC_TOPN_MAX = 16
WINDOW = 512
FORCE_SCORE = 1e4
BRANCH_WIDTH = 512
N_BRANCH = 3

IN_SPLITS = (512, 512, 512, 512, 512, 64, 64, 512, 512, 64, 8,
             512, 128, 128, 128, 128, 128, 128, 512, 24, 3 * D_MODEL)
(_A_Q, _A_K, _A_V, _A_G, _B_Q, _B_K, _B_V, _B_G, _I_Q, _I_K, _I_W,
 _C_Q, _C_KC, _C_VC, _C_KS, _C_VS, _C_KW, _C_VW, _C_G, _C_BG, _MERGE) = [
    int(v) for v in np.concatenate([[0], np.cumsum(IN_SPLITS)[:-1]])]

LANES = 128
SUBLANES = 8
TQ = 512
KC = 512
TM_PROJ = 512
TM_POST = 512
NEG_INF = float("-inf")
M_INIT = -1e30
N_NAT = 3584
N_TR = 864
BISECT_PLAIN = 10
ONES_ROWS = 16
HV = HEAD_DIM + ONES_ROWS
HV_A = 2 * HEAD_DIM + ONES_ROWS
LOG2E = math.log2(math.e)
VMEM_LIMIT = 56 * 1024 * 1024


def _dot_nt(a, b):
    return lax.dot_general(a, b, (((1,), (1,)), ((), ())), preferred_element_type=F32)


def _dot(a, b):
    return jnp.dot(a, b, preferred_element_type=F32)


def _rms(x, g):
    return x * lax.rsqrt(jnp.mean(x * x, axis=-1, keepdims=True) + EPS) * g


def _sigmoid(x):
    return 1.0 / (1.0 + jnp.exp(-x))


def _store_half_masked(src_ref, dst_ref, n_pairs):
    lane = lax.broadcasted_iota(jnp.int32, (src_ref.shape[0], LANES), 1)
    for hp in range(n_pairs):
        x = src_ref[:, LANES * hp:LANES * (hp + 1)]
        z = jnp.zeros_like(x)
        dst_ref[2 * hp] = jnp.where(lane < HEAD_DIM, x, z)
        dst_ref[2 * hp + 1] = jnp.where(lane >= HEAD_DIM, x, z)


def _fold_rows(x, op):
    n_acc = 2
    accs = [x[SUBLANES * r:SUBLANES * (r + 1)] for r in range(n_acc)]
    for r in range(n_acc, x.shape[0] // SUBLANES):
        accs[r % n_acc] = op(accs[r % n_acc], x[SUBLANES * r:SUBLANES * (r + 1)])
    return op(accs[0], accs[1])


HALF = KC // 2


def _scores(k, q, dead=None):
    if dead is None:
        return _dot_nt(k, q)
    neg = jnp.full((HALF, HALF), NEG_INF, F32)
    if dead == "bl":
        return jnp.concatenate(
            [_dot_nt(k[:HALF], q), jnp.concatenate([neg, _dot_nt(k[HALF:], q[HALF:])], axis=1)], axis=0)
    return jnp.concatenate(
        [jnp.concatenate([_dot_nt(k[:HALF], q[:HALF]), neg], axis=1), _dot_nt(k[HALF:], q)], axis=0)


def _weighted_values(v_t, p, dead=None):
    if dead is None:
        return _dot(v_t, p)
    if dead == "bl":
        return jnp.concatenate([_dot(v_t[:, :HALF], p[:HALF, :HALF]), _dot(v_t, p[:, HALF:])], axis=1)
    return jnp.concatenate([_dot(v_t, p[:, :HALF]), _dot(v_t[:, HALF:], p[HALF:, HALF:])], axis=1)


class _Stage:
    def __init__(self, j0, j1, nh, hv, prep, score, values, m_ref, al_ref, acc_ref,
                 dead_first=None, dead_last=None):
        self.j0 = jnp.asarray(j0, jnp.int32)
        self.j1 = jnp.asarray(j1, jnp.int32)
        self.nh, self.hv, self.prep, self.score, self.values = nh, hv, prep, score, values
        self.m_ref, self.al_ref, self.acc_ref = m_ref, al_ref, acc_ref
        self.dead_first, self.dead_last = dead_first, dead_last

    def reset(self):
        self.m_ref[...] = jnp.full(self.m_ref.shape, M_INIT, F32)
        self.acc_ref[...] = jnp.zeros(self.acc_ref.shape, F32)

    def produce(self, sbuf_ref, j, dead=None):
        ctx = self.prep(j)
        for h in range(self.nh):
            s = self.score(ctx, h, dead)
            m_old = self.m_ref[h:h + 1, :]
            m_new = jnp.maximum(m_old, jnp.max(s, axis=0, keepdims=True))
            sbuf_ref[h] = s
            self.m_ref[h:h + 1, :] = m_new
            self.al_ref[h:h + 1, :] = jnp.exp2(m_old - m_new)

    def consume(self, sbuf_ref, j, dead=None):
        for h in range(self.nh):
            p = jnp.exp2(sbuf_ref[h] - self.m_ref[h:h + 1, :])
            rows = slice(self.hv * h, self.hv * (h + 1))
            self.acc_ref[rows, :] = (self.al_ref[h:h + 1, :] * self.acc_ref[rows, :]
                                     + _weighted_values(self.values(j, h), p.astype(BF16), dead))

    def loop(self, sbuf_ref, lo, hi):
        def body(j, carry):
            self.consume(sbuf_ref, j - 1)
            self.produce(sbuf_ref, j)
            return carry
        lax.fori_loop(lo, hi, body, 0)


def _attn_chain(stages, sbuf_ref, tile):
    for st in stages:
        st.reset()

    @pl.when(tile == 0)
    def _():
        for st in stages:
            st.produce(sbuf_ref, st.j0)
            st.consume(sbuf_ref, st.j0)

    @pl.when(tile > 0)
    def _():
        stages[0].produce(sbuf_ref, stages[0].j0, stages[0].dead_first)
        for n, st in enumerate(stages):
            st.loop(sbuf_ref, st.j0 + 1, st.j1)
            st.consume(sbuf_ref, st.j1 - 1, st.dead_first)
            st.produce(sbuf_ref, st.j1, st.dead_last)
            st.consume(sbuf_ref, st.j1, st.dead_last)
            if n + 1 < len(stages):
                stages[n + 1].produce(sbuf_ref, stages[n + 1].j0, stages[n + 1].dead_first)


def _attn_scratch(nh, hv):
    return [pltpu.VMEM((nh, KC, TQ), F32), pltpu.VMEM((nh, TQ), F32),
            pltpu.VMEM((nh, TQ), F32), pltpu.VMEM((nh * hv, TQ), F32)]


def _layer_spec(a, layer, **kw):
    tail = (0,) * (a.ndim - 1)
    return pl.BlockSpec((None,) + a.shape[1:], lambda *_: (layer,) + tail, **kw)


def _ones_rows(n_cols):
    r = lax.broadcasted_iota(jnp.int32, (ONES_ROWS, n_cols), 0)
    return jnp.where(r == 0, 1.0, 0.0).astype(BF16)


def _proj_kernel(x_ref, g_ref, wn_ref, wt_ref, cs_ref, place_ref,
                 qa_ref, ka_ref, qb_ref, iq_ref, qc_ref, qcr_ref, kb_ref, ik_ref,
                 ks_ref, kw_ref, ck_ref, cv_ref,
                 vat_ref, vbt_ref, vst_ref, vwt_ref, iwt_ref, bgt_ref, zc_ref):
    h = _rms(x_ref[...], g_ref[...]).astype(BF16)
    cs = cs_ref[...]
    hi = cs.astype(BF16)
    rest = cs - hi.astype(F32)
    mid = rest.astype(BF16)
    low = (rest - mid.astype(F32)).astype(BF16)
    spread = _dot(hi, place_ref[...]) + _dot(mid, place_ref[...]) + _dot(low, place_ref[...])
    rc, rs1, rs2 = spread[:, 0:LANES], spread[:, LANES:2 * LANES], spread[:, 2 * LANES:3 * LANES]

    def rope(z):
        return z * rc + pltpu.roll(z, LANES - ROT_DIM // 2, 1) * rs1 + pltpu.roll(z, ROT_DIM // 2, 1) * rs2

    qscale = HEAD_DIM ** -0.5
    qscale2 = qscale * LOG2E
    segs = (
        (0, 512, ((qa_ref, True, qscale2),)),
        (512, 512, ((ka_ref, True, 1.0),)),
        (1024, 512, ((qb_ref, True, qscale2),)),
        (1536, 512, ((iq_ref, True, qscale),)),
        (2048, 512, ((qc_ref, False, qscale), (qcr_ref, True, qscale2))),
        (2560, 128, ((kb_ref, True, 1.0),)),
        (2688, 128, ((ik_ref, True, 1.0),)),
        (2816, 256, ((ks_ref, True, 1.0),)),
        (3072, 256, ((kw_ref, True, 1.0),)),
    )
    zc = _dot(h, wn_ref[:, 3328:3328 + 2 * LANES])
    half_len = CMP_LEN // 2
    for n, c_ref in enumerate((ck_ref, cv_ref)):
        zc_ref[n] = zc[:, LANES * n:LANES * (n + 1)]
        for l in range(half_len):
            c_ref[:, LANES * l:LANES * (l + 1)] = zc_ref[n, pl.ds(l, zc.shape[0] // half_len, stride=half_len), :]
    for c0, width, outs in segs:
        z = _dot(h, wn_ref[:, c0:c0 + width])
        for out_ref, rot, scale in outs:
            for j in range(width // LANES):
                zj = z[:, LANES * j:LANES * (j + 1)]
                if rot:
                    zj = rope(zj)
                if scale != 1.0:
                    zj = zj * scale
                out_ref[:, LANES * j:LANES * (j + 1)] = zj.astype(out_ref.dtype)

    zt = _dot_nt(wt_ref[...], h)
    ones = _ones_rows(zt.shape[1])
    vd = 2 * HEAD_DIM
    for h in range(DA_HEADS):
        vat_ref[HV_A * h:HV_A * h + vd, :] = zt[vd * h:vd * (h + 1)].astype(BF16)
        vat_ref[HV_A * h + vd:HV_A * (h + 1), :] = ones
    vbt_ref[0:HEAD_DIM, :] = zt[512:576].astype(BF16)
    vbt_ref[HEAD_DIM:HV, :] = ones
    for g in range(NSA_GROUPS):
        for ref, base in ((vst_ref, 576), (vwt_ref, 704)):
            ref[HV * g:HV * g + HEAD_DIM, :] = zt[base + HEAD_DIM * g:base + HEAD_DIM * (g + 1)].astype(BF16)
            ref[HV * g + HEAD_DIM:HV * (g + 1), :] = ones
    iwt_ref[...] = zt[832:840] * (IDX_HEADS ** -0.5)
    bgt_ref[...] = zt[840:864]


def _proj_call(x, g, wn, wt, cs, place, layer):
    B, S, D = x.shape
    tm = TM_PROJ
    nat = lambda w: pl.BlockSpec((None, tm, w), lambda b, i: (b, i, 0))
    tr = lambda r: pl.BlockSpec((None, r, tm), lambda b, i: (b, 0, i))
    full = lambda a: _layer_spec(a, layer)
    sds = jax.ShapeDtypeStruct
    half_len = CMP_LEN // 2
    out_shape = (
        sds((B, S, 512), BF16), sds((B, S, 512), BF16), sds((B, S, 512), BF16), sds((B, S, 512), BF16),
        sds((B, S, 512), BF16), sds((B, S, 512), BF16), sds((B, S, 128), BF16), sds((B, S, 128), BF16),
        sds((B, S, 256), BF16), sds((B, S, 256), BF16),
        sds((B, S // half_len, half_len * LANES), F32), sds((B, S // half_len, half_len * LANES), F32),
        sds((B, DA_HEADS * HV_A, S), BF16), sds((B, HV, S), BF16),
        sds((B, NSA_GROUPS * HV, S), BF16), sds((B, NSA_GROUPS * HV, S), BF16),
        sds((B, IDX_HEADS, S), F32), sds((B, 3 * NSA_HEADS, S), F32),
    )
    cmp_rows = pl.BlockSpec((None, tm // half_len, half_len * LANES), lambda b, i: (b, i, 0))
    out_specs = (nat(512), nat(512), nat(512), nat(512), nat(512), nat(512), nat(128), nat(128),
                 nat(256), nat(256), cmp_rows, cmp_rows,
                 tr(DA_HEADS * HV_A), tr(HV), tr(NSA_GROUPS * HV), tr(NSA_GROUPS * HV),
                 tr(IDX_HEADS), tr(3 * NSA_HEADS))
    return pl.pallas_call(
        _proj_kernel,
        grid=(B, S // tm),
        in_specs=[nat(D), full(g), full(wn), full(wt), nat(ROPE_COLS),
                  pl.BlockSpec(place.shape, lambda b, i: (0, 0))],
        out_specs=out_specs,
        out_shape=out_shape,
        scratch_shapes=[pltpu.VMEM((2, tm, LANES), F32)],
        compiler_params=pltpu.CompilerParams(
            dimension_semantics=("arbitrary", "arbitrary"), vmem_limit_bytes=VMEM_LIMIT),
        name="in_proj",
    )(x, g, wn, wt, cs, place)


def _compress_kernel(rk_ref, rv_ref, pea_ref, peb_ref, w1a_ref, w1b_ref, w2_ref, nat_ref, t_ref):
    for kind, r_ref in enumerate((rk_ref, rv_ref)):
        r = r_ref[...]
        xa = (r + pea_ref[kind]).astype(BF16)
        xb = (r + peb_ref[kind]).astype(BF16)
        nrow = r.shape[0]
        for g in range(NSA_GROUPS):
            n = NSA_GROUPS * kind + g
            hid = jax.nn.gelu(_dot(xa, w1a_ref[n]) + pltpu.roll(_dot(xb, w1b_ref[n]), nrow - 1, 0))
            o = _dot(hid.astype(BF16), w2_ref[kind])
            nat_ref[n] = jnp.concatenate([o, o], axis=1).astype(BF16)
            t_ref[n] = o.T.astype(BF16)


def _compress_call(rk, rv, pea, peb, w1a, w1b, w2, layer):
    B, nr, kdim = rk.shape
    full = lambda a: _layer_spec(a, layer)
    rspec = pl.BlockSpec((None, nr, kdim), lambda b: (b, 0, 0))
    return pl.pallas_call(
        _compress_kernel,
        grid=(B,),
        in_specs=[rspec, rspec, full(pea), full(peb), full(w1a), full(w1b), full(w2)],
        out_specs=(
            pl.BlockSpec((None, 2 * NSA_GROUPS, nr, 2 * HEAD_DIM), lambda b: (b, 0, 0, 0)),
            pl.BlockSpec((None, 2 * NSA_GROUPS, HEAD_DIM, nr), lambda b: (b, 0, 0, 0)),
        ),
        out_shape=(jax.ShapeDtypeStruct((B, 2 * NSA_GROUPS, nr, 2 * HEAD_DIM), BF16),
                   jax.ShapeDtypeStruct((B, 2 * NSA_GROUPS, HEAD_DIM, nr), BF16)),
        compiler_params=pltpu.CompilerParams(dimension_semantics=("arbitrary",)),
        name="nsa_compress",
    )(rk, rv, pea, peb, w1a, w1b, w2)


def _diff_kernel(q_ref, k_ref, vt_ref, lam_ref, sg_ref, o_ref,
                 qz_ref, sbuf_ref, m_ref, al_ref, acc_ref, *, lam_init):
    i = pl.program_id(1)
    lp = lam_ref[...]
    lam = (jnp.exp(jnp.sum(lp[0:1] * lp[1:2], axis=1, keepdims=True))
           - jnp.exp(jnp.sum(lp[2:3] * lp[3:4], axis=1, keepdims=True)) + lam_init)
    row_minus_col = (lax.broadcasted_iota(jnp.int32, (KC, TQ), 0)
                     - lax.broadcasted_iota(jnp.int32, (KC, TQ), 1))
    _store_half_masked(q_ref, qz_ref, DA_HEADS)
    vdim = 2 * HEAD_DIM
    n_maps = 2 * DA_HEADS

    def prep(j):
        return pl.multiple_of(j * KC, KC), jnp.where(row_minus_col <= (i - j) * KC, 0.0, NEG_INF)

    def score(ctx, hm, dead):
        off, bias = ctx
        h = hm // 2
        return _scores(k_ref[pl.ds(off, KC), vdim * h:vdim * (h + 1)], qz_ref[hm], dead) + bias

    def values(j, hm):
        h = hm // 2
        return vt_ref[HV_A * h:HV_A * (h + 1), pl.ds(pl.multiple_of(j * KC, KC), KC)]

    _attn_chain([_Stage(0, i, n_maps, HV_A, prep, score, values, m_ref, al_ref, acc_ref, dead_last="bl")],
                sbuf_ref, i)

    for h in range(DA_HEADS):
        r0, r1 = HV_A * 2 * h, HV_A * (2 * h + 1)
        o_t = (acc_ref[r0:r0 + vdim, :] / acc_ref[r0 + vdim:r0 + vdim + 1, :]
               - lam * (acc_ref[r1:r1 + vdim, :] / acc_ref[r1 + vdim:r1 + vdim + 1, :]))
        ms = jnp.mean(o_t * o_t, axis=0, keepdims=True)
        y = o_t * lax.rsqrt(ms + EPS) * sg_ref[...] * (1.0 - lam_init)
        o_ref[:, vdim * h:vdim * (h + 1)] = y.T


def _diff_call(qa, ka, vat, lam_p, sg, lam_init, layer):
    B, S, W = qa.shape
    n_maps = 2 * DA_HEADS
    return pl.pallas_call(
        functools.partial(_diff_kernel, lam_init=lam_init),
        grid=(B, S // TQ),
        in_specs=[
            pl.BlockSpec((None, TQ, W), lambda b, i: (b, i, 0)),
            pl.BlockSpec((None, S, W), lambda b, i: (b, 0, 0)),
            pl.BlockSpec((None, DA_HEADS * HV_A, S), lambda b, i: (b, 0, 0)),
            _layer_spec(lam_p, layer),
            _layer_spec(sg, layer),
        ],
        out_specs=pl.BlockSpec((None, TQ, W), lambda b, i: (b, i, 0)),
        out_shape=jax.ShapeDtypeStruct((B, S, W), F32),
        scratch_shapes=[pltpu.VMEM((n_maps, TQ, LANES), BF16)] + _attn_scratch(n_maps, HV_A),
        compiler_params=pltpu.CompilerParams(
            dimension_semantics=("arbitrary", "arbitrary"), vmem_limit_bytes=VMEM_LIMIT),
        name="diff_attn",
    )(qa, ka, vat, lam_p, sg)


def _dsa_kernel(q_ref, iq_ref, kb_ref, ik_ref, vt_ref, iw_ref, o_ref,
                sc_ref, tj_ref, tri_ref, qz_ref, sbuf_ref, m_ref, al_ref, acc_ref, *, k_sel, seq_len):
    i = pl.program_id(1)
    nch = i + 1
    row = lax.broadcasted_iota(jnp.int32, (KC, TQ), 0)
    col = lax.broadcasted_iota(jnp.int32, (KC, TQ), 1)
    causal = row <= col
    iw = iw_ref[...]
    kf = float(k_sel)
    _store_half_masked(iq_ref, qz_ref, IDX_HEADS // 2)

    def idx_chunk(j, carry, masked):
        mx, mn = carry
        off = pl.multiple_of(j * KC, KC)
        ikc = ik_ref[pl.ds(off, KC), :]
        sc = jnp.zeros((KC, TQ), F32)
        for h in range(IDX_HEADS):
            dots = _scores(ikc, qz_ref[h], "bl" if masked else None)
            sc = sc + iw[h:h + 1, :] * jnp.maximum(dots, 0.0)
        lo_src = sc
        if masked:
            lo_src = jnp.where(causal, sc, jnp.inf)
            sc = jnp.where(causal, sc, NEG_INF)
        sc_ref[pl.ds(off, KC), :] = sc
        return (jnp.maximum(mx, _fold_rows(sc, jnp.maximum)),
                jnp.minimum(mn, _fold_rows(lo_src, jnp.minimum)))

    carry = (jnp.full((SUBLANES, TQ), NEG_INF, F32), jnp.full((SUBLANES, TQ), jnp.inf, F32))
    carry = lax.fori_loop(0, i, functools.partial(idx_chunk, masked=False), carry)
    mx8, mn8 = idx_chunk(i, carry, True)
    row_max = jnp.max(mx8, axis=0, keepdims=True)
    row_min = jnp.min(mn8, axis=0, keepdims=True)

    def probe_pass(th, snap):
        thb = jnp.broadcast_to(th, (SUBLANES, TQ))

        def body(j, c):
            off = pl.multiple_of(j * KC, KC)
            c = list(c)
            for r in range(KC // SUBLANES):
                x = sc_ref[pl.ds(off + SUBLANES * r, SUBLANES), :]
                ge = x >= thb
                c[0] = c[0] + jnp.where(ge, 1.0, 0.0)
                if snap:
                    c[1] = jnp.maximum(c[1], jnp.where(ge, NEG_INF, x))
            return tuple(c)
        init = (jnp.zeros((SUBLANES, TQ), F32),)
        if snap:
            init += (jnp.full((SUBLANES, TQ), NEG_INF, F32),)
        out = lax.fori_loop(0, nch, body, init)
        cnt = jnp.sum(out[0], axis=0, keepdims=True)
        return (cnt, jnp.max(out[1], axis=0, keepdims=True)) if snap else cnt

    n_causal = (i * TQ + 1 + lax.broadcasted_iota(jnp.int32, (1, TQ), 1)).astype(F32)
    keep_all = n_causal <= kf

    def bisect(_, c):
        lo, hi, clo, chi = c
        mid = lo + (hi - lo) * 0.5
        cnt = probe_pass(mid, False)
        ge = cnt >= kf
        return (jnp.where(ge, mid, lo), jnp.where(ge, hi, mid),
                jnp.where(ge, cnt, clo), jnp.where(ge, chi, cnt))

    hi0 = row_max + (row_max - row_min) + 1.0
    lo, hi, clo, chi = lax.fori_loop(
        0, BISECT_PLAIN, bisect, (row_min, hi0, n_causal, jnp.zeros((1, TQ), F32)))

    def cond(st):
        return jnp.logical_and(st[0] < 256, st[-1] > 0.0)

    def snap_step(st):
        it, lo, hi, clo, chi, hie, known, stuck, done, _ = st
        top = jnp.where(known > 0.0, hie, hi)
        mid = lo + (top - lo) * 0.5
        inside = jnp.logical_and(mid > lo, mid < top)
        near_top = jnp.logical_or(kf - chi <= 2.0, stuck > 0.0)
        use_top = jnp.logical_or(jnp.logical_and(known > 0.0, near_top), jnp.logical_not(inside))
        th = jnp.where(use_top, top, mid)
        cnt, edn = probe_pass(th, True)
        live = done <= 0.0
        ge = jnp.logical_and(live, cnt >= kf)
        lt = jnp.logical_and(live, cnt < kf)
        stuck = jnp.where(jnp.logical_and(ge, cnt == clo), 1.0, 0.0)
        lo = jnp.where(ge, th, lo)
        clo = jnp.where(ge, cnt, clo)
        hi = jnp.where(lt, th, hi)
        chi = jnp.where(lt, cnt, chi)
        hie = jnp.where(lt, edn, hie)
        known = jnp.where(lt, 1.0, known)
        fin = jnp.logical_or(clo == kf, jnp.logical_and(known > 0.0, lo >= hie))
        done = jnp.where(fin, 1.0, done)
        return it + 1, lo, hi, clo, chi, hie, known, stuck, done, jnp.max(1.0 - done)

    zero = jnp.zeros((1, TQ), F32)
    done0 = jnp.where(jnp.logical_or(clo == kf, keep_all), 1.0, 0.0)
    st = lax.while_loop(cond, snap_step, (jnp.int32(0), lo, hi, clo, chi, jnp.full((1, TQ), NEG_INF, F32),
                                          zero, zero, done0, jnp.max(1.0 - done0)))
    thr, clo, chi = st[1], st[3], st[4]
    need = jnp.where(clo == kf, float(seq_len), kf - chi)
    thr = jnp.where(keep_all, NEG_INF, thr)
    need = jnp.where(keep_all, 0.0, need)
    tj_ref[2:3, :] = jnp.zeros((1, TQ), F32)

    _store_half_masked(q_ref, qz_ref, DSA_HEADS // 2)
    half = KC // 2
    tri_ref[...] = jnp.where(lax.broadcasted_iota(jnp.int32, (half, half), 0)
                             >= lax.broadcasted_iota(jnp.int32, (half, half), 1), 1.0, 0.0).astype(BF16)

    def prep(j):
        off = pl.multiple_of(j * KC, KC)
        x = sc_ref[pl.ds(off, KC), :]
        eq = x == thr
        e = jnp.where(eq, 1.0, 0.0).astype(BF16)
        r_top = _dot(tri_ref[...], e[:half]) + tj_ref[2:3, :]
        r_bot = _dot(tri_ref[...], e[half:]) + r_top[half - 1:half, :]
        tj_ref[2:3, :] = r_bot[half - 1:half, :]
        rank = jnp.concatenate([r_top, r_bot], axis=0)
        tie_bias = jnp.where(jnp.logical_and(eq, rank <= need), 0.0, NEG_INF)
        return kb_ref[pl.ds(off, KC), :], jnp.where(x > thr, 0.0, tie_bias)

    def score(ctx, h, dead):
        kc, bias = ctx
        return _scores(kc, qz_ref[h], dead) + bias

    def values(j, h):
        return vt_ref[:, pl.ds(pl.multiple_of(j * KC, KC), KC)]

    _attn_chain([_Stage(0, i, DSA_HEADS, HV, prep, score, values, m_ref, al_ref, acc_ref, dead_last="bl")],
                sbuf_ref, i)
    for hp in range(DSA_HEADS // 2):
        blk = jnp.concatenate(
            [acc_ref[HV * h:HV * h + HEAD_DIM, :] / acc_ref[HV * h + HEAD_DIM:HV * h + HEAD_DIM + 1, :]
             for h in (2 * hp, 2 * hp + 1)], axis=0)
        o_ref[:, LANES * hp:LANES * (hp + 1)] = blk.T


def _dsa_call(qb, iq, kb, ik, vbt, iwt, k_sel):
    B, S, W = qb.shape
    return pl.pallas_call(
        functools.partial(_dsa_kernel, k_sel=k_sel, seq_len=S),
        grid=(B, S // TQ),
        in_specs=[
            pl.BlockSpec((None, TQ, W), lambda b, i: (b, i, 0)),
            pl.BlockSpec((None, TQ, W), lambda b, i: (b, i, 0)),
            pl.BlockSpec((None, S, LANES), lambda b, i: (b, 0, 0)),
            pl.BlockSpec((None, S, LANES), lambda b, i: (b, 0, 0)),
            pl.BlockSpec((None, HV, S), lambda b, i: (b, 0, 0)),
            pl.BlockSpec((None, IDX_HEADS, TQ), lambda b, i: (b, 0, i)),
        ],
        out_specs=pl.BlockSpec((None, TQ, W), lambda b, i: (b, i, 0)),
        out_shape=jax.ShapeDtypeStruct((B, S, W), F32),
        scratch_shapes=[pltpu.VMEM((S, TQ), F32), pltpu.VMEM((SUBLANES, TQ), F32),
                        pltpu.VMEM((KC // 2, KC // 2), BF16),
                        pltpu.VMEM((DSA_HEADS, TQ, LANES), BF16)] + _attn_scratch(DSA_HEADS, HV),
        compiler_params=pltpu.CompilerParams(
            dimension_semantics=("arbitrary", "arbitrary"), vmem_limit_bytes=VMEM_LIMIT),
        name="dsa_attn",
    )(qb, iq, kb, ik, vbt, iwt)


def _nsa_kernel(q_ref, qr_ref, kc_ref, vct_ref, ovt_ref, ks_ref, vst_ref, kw_ref, vwt_ref, bg_ref,
                o_ref, cmp_ref, sel_ref, qz_ref, qrz_ref, mw_ref, alw_ref, accw_ref,
                sbuf_ref, m_ref, al_ref, acc_ref, *, n_sel):
    i = pl.program_id(1)
    row_minus_col = (lax.broadcasted_iota(jnp.int32, (KC, TQ), 0)
                     - lax.broadcasted_iota(jnp.int32, (KC, TQ), 1))
    tq = i * TQ + lax.broadcasted_iota(jnp.int32, (1, TQ), 1)
    hpg = NSA_HEADS // NSA_GROUPS
    nc = kc_ref.shape[1]
    n_blk = sel_ref.shape[1]
    gw = 2 * HEAD_DIM

    _store_half_masked(q_ref, qz_ref, NSA_HEADS // 2)
    _store_half_masked(qr_ref, qrz_ref, NSA_HEADS // 2)

    def compressed_and_select():
        cmp_valid = CMP_STRIDE * lax.broadcasted_iota(jnp.int32, (nc, TQ), 0) + (CMP_LEN - 1) <= tq
        blk = lax.broadcasted_iota(jnp.int32, (n_blk, TQ), 0)
        forced = jnp.logical_or(blk == jnp.right_shift(tq, int(math.log2(SLC_BLOCK))), blk == 0)
        blk_causal = blk * SLC_BLOCK <= tq
        for h in range(NSA_HEADS):
            s = jnp.where(cmp_valid, _dot_nt(kc_ref[h // hpg], qz_ref[h]), NEG_INF)
            sbuf_ref[h, 0:nc, :] = s
            m_ref[h:h + 1, :] = jnp.maximum(jnp.max(s, axis=0, keepdims=True), M_INIT)
        for g in range(NSA_GROUPS):
            vcg = vct_ref[NSA_GROUPS + g]
            imp = jnp.zeros((n_blk, TQ), F32)
            for hh in range(hpg):
                h = hpg * g + hh
                p = jnp.exp(sbuf_ref[h, 0:nc, :] - m_ref[h:h + 1, :])
                l = jnp.sum(p, axis=0, keepdims=True)
                pc = (p * jnp.where(l > 0.0, 1.0 / l, 0.0)).astype(BF16)
                cmp_ref[HEAD_DIM * h:HEAD_DIM * (h + 1), :] = _dot(vcg, pc)
                imp = imp + _dot(ovt_ref[...], pc)
            imp = jnp.where(forced, FORCE_SCORE, imp)
            imp = jnp.where(blk_causal, imp, NEG_INF)
            n_tiles = n_blk // SUBLANES
            sub = lax.broadcasted_iota(jnp.int32, (SUBLANES, TQ), 0)
            tiles = [imp[SUBLANES * t:SUBLANES * (t + 1)] for t in range(n_tiles)]
            ranks = [jnp.zeros((SUBLANES, TQ), F32)] * n_tiles
            for mp in range(n_blk):
                r = jnp.broadcast_to(imp[mp:mp + 1, :], (SUBLANES, TQ))
                for t in range(n_tiles):
                    if SUBLANES * t > mp:
                        ahead = r >= tiles[t]
                    elif SUBLANES * (t + 1) <= mp:
                        ahead = r > tiles[t]
                    else:
                        later = sub > mp - SUBLANES * t
                        ahead = jnp.logical_or(r > tiles[t], jnp.logical_and(r == tiles[t], later))
                    ranks[t] = ranks[t] + jnp.where(ahead, 1.0, 0.0)
            rank = jnp.concatenate(ranks, axis=0)
            sel_ref[g] = jnp.where(rank < float(n_sel), 0.0, NEG_INF)

    bpc = KC // SLC_BLOCK

    def slc_prep(j):
        off = pl.multiple_of(j * KC, KC)
        visible = row_minus_col <= (i - j) * KC
        biases = []
        for g in range(NSA_GROUPS):
            bias = jnp.concatenate(
                [jnp.broadcast_to(sel_ref[g, pl.ds(bpc * j + b, 1), :], (SLC_BLOCK, TQ))
                 for b in range(bpc)], axis=0)
            biases.append(jnp.where(visible, bias, NEG_INF))
        return off, biases

    def slc_score(ctx, h, dead):
        off, biases = ctx
        g = h // hpg
        return _scores(ks_ref[pl.ds(off, KC), gw * g:gw * (g + 1)], qrz_ref[h], dead) + biases[g]

    def slc_values(j, h):
        g = h // hpg
        return vst_ref[HV * g:HV * (g + 1), pl.ds(pl.multiple_of(j * KC, KC), KC)]

    def win_prep(j):
        off = pl.multiple_of(j * KC, KC)
        d = (i - j) * KC
        inside = jnp.logical_and(row_minus_col <= d, row_minus_col > d - WINDOW)
        return off, jnp.where(inside, 0.0, NEG_INF)

    def win_score(ctx, h, dead):
        off, bias = ctx
        g = h // hpg
        return _scores(kw_ref[pl.ds(off, KC), gw * g:gw * (g + 1)], qrz_ref[h], dead) + bias

    def win_values(j, h):
        g = h // hpg
        return vwt_ref[HV * g:HV * (g + 1), pl.ds(pl.multiple_of(j * KC, KC), KC)]

    compressed_and_select()
    assert WINDOW == KC
    slc = _Stage(0, i, NSA_HEADS, HV, slc_prep, slc_score, slc_values, m_ref, al_ref, acc_ref,
                 dead_last="bl")
    win = _Stage(jnp.maximum(i - 1, 0), i, NSA_HEADS, HV, win_prep, win_score, win_values,
                 mw_ref, alw_ref, accw_ref, dead_first="tr", dead_last="bl")
    _attn_chain([slc, win], sbuf_ref, i)

    def head_out(ref, h):
        return ref[HV * h:HV * h + HEAD_DIM, :] / ref[HV * h + HEAD_DIM:HV * h + HEAD_DIM + 1, :]

    gb = _sigmoid(bg_ref[...])
    for hp in range(NSA_HEADS // 2):
        parts = []
        for h in (2 * hp, 2 * hp + 1):
            parts.append(gb[3 * h:3 * h + 1, :] * cmp_ref[HEAD_DIM * h:HEAD_DIM * (h + 1), :]
                         + gb[3 * h + 1:3 * h + 2, :] * head_out(acc_ref, h)
                         + gb[3 * h + 2:3 * h + 3, :] * head_out(accw_ref, h))
        o_ref[:, LANES * hp:LANES * (hp + 1)] = jnp.concatenate(parts, axis=0).T


def _nsa_call(qc, qcr, kcc, vct, ovt, ks, vst, kw, vwt, bgt, n_sel):
    B, S, W = qc.shape
    n_blk = S // SLC_BLOCK
    qspec = pl.BlockSpec((None, TQ, W), lambda b, i: (b, i, 0))
    return pl.pallas_call(
        functools.partial(_nsa_kernel, n_sel=n_sel),
        grid=(B, S // TQ),
        in_specs=[
            qspec, qspec,
            pl.BlockSpec((None,) + kcc.shape[1:], lambda b, i: (b, 0, 0, 0)),
            pl.BlockSpec((None,) + vct.shape[1:], lambda b, i: (b, 0, 0, 0)),
            pl.BlockSpec(ovt.shape, lambda b, i: (0, 0)),
            pl.BlockSpec((None, S, 2 * LANES), lambda b, i: (b, 0, 0)),
            pl.BlockSpec((None, NSA_GROUPS * HV, S), lambda b, i: (b, 0, 0)),
            pl.BlockSpec((None, S, 2 * LANES), lambda b, i: (b, 0, 0)),
            pl.BlockSpec((None, NSA_GROUPS * HV, S), lambda b, i: (b, 0, 0)),
            pl.BlockSpec((None, 3 * NSA_HEADS, TQ), lambda b, i: (b, 0, i)),
        ],
        out_specs=qspec,
        out_shape=jax.ShapeDtypeStruct((B, S, W), F32),
        scratch_shapes=[pltpu.VMEM((W, TQ), F32), pltpu.VMEM((NSA_GROUPS, n_blk, TQ), F32),
                        pltpu.VMEM((NSA_HEADS, TQ, LANES), BF16), pltpu.VMEM((NSA_HEADS, TQ, LANES), BF16)]
        + _attn_scratch(NSA_HEADS, HV)[1:] + _attn_scratch(NSA_HEADS, HV),
        compiler_params=pltpu.CompilerParams(
            dimension_semantics=("arbitrary", "arbitrary"), vmem_limit_bytes=VMEM_LIMIT),
        name="nsa_attn",
    )(qc, qcr, kcc, vct, ovt, ks, vst, kw, vwt, bgt)


def _post_kernel(x_ref, oa_ref, ob_ref, oc_ref, p_ref, g_ref, wg_ref, wup_ref, wout_ref,
                 pg_ref, wple_ref, wpg_ref, fg_ref, o_ref, *, final):
    x = x_ref[...]
    h = _rms(x, g_ref[...]).astype(BF16)
    mixed = jnp.zeros(x.shape, F32)
    for n, br_ref in enumerate((oa_ref, ob_ref, oc_ref)):
        gate = _dot(h, wg_ref[:, BRANCH_WIDTH * n:BRANCH_WIDTH * (n + 1)])
        og = br_ref[...] * (gate * _sigmoid(gate))
        up = _dot(og.astype(BF16), wup_ref[n])
        c0 = N_BRANCH * BRANCH_WIDTH + D_MODEL * n
        mixed = mixed + _sigmoid(_dot(h, wg_ref[:, c0:c0 + D_MODEL])) * up
    x1 = x + _dot(mixed.astype(BF16), wout_ref[...])
    emb = _dot(p_ref[...].astype(BF16), wple_ref[...])
    hg = _dot(_rms(x1, pg_ref[...]).astype(BF16), wpg_ref[...])
    x2 = x1 + emb * _sigmoid(hg)
    if final:
        x2 = _rms(x2, fg_ref[...])
    o_ref[...] = x2


def _post_call(x, oa, ob, oc, p, g, wg, wup, wout, pg, wple, wpg, fg, final, layer):
    T, D = x.shape
    tm = TM_POST
    tok = lambda w: pl.BlockSpec((tm, w), lambda i: (i, 0))
    full = lambda a: _layer_spec(a, layer, pipeline_mode=pl.Buffered(1))
    return pl.pallas_call(
        functools.partial(_post_kernel, final=final),
        grid=(T // tm,),
        in_specs=[tok(D), tok(BRANCH_WIDTH), tok(BRANCH_WIDTH), tok(BRANCH_WIDTH),
                  pl.BlockSpec((None, tm, PLE_DIM), lambda i: (layer, i, 0)),
                  full(g), full(wg), full(wup), full(wout), full(pg), full(wple), full(wpg),
                  pl.BlockSpec(fg.shape, lambda i: (0, 0))],
        out_specs=tok(D),
        out_shape=jax.ShapeDtypeStruct((T, D), F32),
        compiler_params=pltpu.CompilerParams(
            dimension_semantics=("arbitrary",), vmem_limit_bytes=VMEM_LIMIT),
        name="post_mix",
    )(x, oa, ob, oc, p, g, wg, wup, wout, pg, wple, wpg, fg)


ROPE_COLS = 3 * (ROT_DIM // 2)


def _rope_inputs(positions):
    half = ROT_DIM // 2
    inv = ROPE_THETA ** (-jnp.arange(0, ROT_DIM, 2, dtype=F32) / ROT_DIM)
    ang = positions.astype(F32)[..., None] * inv
    cs = jnp.concatenate([jnp.cos(ang), jnp.sin(ang), jnp.ones_like(ang)], axis=-1)
    place = np.zeros((ROPE_COLS, 3 * LANES), np.float32)
    for head in range(LANES // HEAD_DIM):
        base = HEAD_DIM * head
        for j in range(half):
            place[j, base + j] = 1.0
            place[j, base + half + j] = 1.0
            place[half + j, LANES + base + j] = -1.0
            place[half + j, 2 * LANES + base + half + j] = 1.0
        place[2 * half, base + ROT_DIM:base + HEAD_DIM] = 1.0
    return cs, jnp.asarray(place, dtype=BF16)


_NAT_PIECES = (
    (0, _A_Q, 512), (512, _A_K, 512), (1024, _B_Q, 512), (1536, _I_Q, 512), (2048, _C_Q, 512),
    (2560, _B_K, 64), (2624, _B_K, 64), (2688, _I_K, 64), (2752, _I_K, 64),
    (2816, _C_KS, 64), (2880, _C_KS, 64), (2944, _C_KS + 64, 64), (3008, _C_KS + 64, 64),
    (3072, _C_KW, 64), (3136, _C_KW, 64), (3200, _C_KW + 64, 64), (3264, _C_KW + 64, 64),
    (3328, _C_KC, 128), (3456, _C_VC, 128))
_GATE_PIECES = ((0, _A_G, 512), (512, _B_G, 512), (1024, _C_G, 512), (1536, _MERGE, 3 * D_MODEL))
_TR_PIECES = ((0, _A_V, 512), (512, _B_V, 64), (576, _C_VS, 128), (704, _C_VW, 128),
              (832, _I_W, 8), (840, _C_BG, 24))
def _relayout_kernel(wt_ref, nat_ref, gates_ref, trt_ref):
    for dst_ref, pieces in ((nat_ref, _NAT_PIECES), (gates_ref, _GATE_PIECES)):
        for d0, s0, n in pieces:
            for r in range(0, n, LANES):
                m = min(LANES, n - r)
                dst_ref[:, d0 + r:d0 + r + m] = wt_ref[s0 + r:s0 + r + m, :].T.astype(BF16)
    groups, cur = [], []
    for d0, s0, n in _TR_PIECES:
        if cur and d0 % ONES_ROWS == 0:
            groups.append(cur)
            cur = []
        cur.append((d0, s0, n))
    groups.append(cur)
    for grp in groups:
        rows = jnp.concatenate([wt_ref[s0:s0 + n, :] for _, s0, n in grp], axis=0)
        trt_ref[grp[0][0]:grp[0][0] + rows.shape[0], :] = rows.astype(BF16)


def _proj_weights(w_in):
    L, D, n_in = w_in.shape
    w_t = jnp.swapaxes(w_in, 1, 2)
    return pl.pallas_call(
        _relayout_kernel,
        grid=(L, D // LANES),
        in_specs=[pl.BlockSpec((None, n_in, LANES), lambda l, r: (l, 0, r))],
        out_specs=(pl.BlockSpec((None, LANES, N_NAT), lambda l, r: (l, r, 0)),
                   pl.BlockSpec((None, LANES, BRANCH_WIDTH * N_BRANCH + N_BRANCH * D_MODEL), lambda l, r: (l, r, 0)),
                   pl.BlockSpec((None, N_TR, LANES), lambda l, r: (l, 0, r))),
        out_shape=(jax.ShapeDtypeStruct((L, D, N_NAT), BF16),
                   jax.ShapeDtypeStruct((L, D, BRANCH_WIDTH * N_BRANCH + N_BRANCH * D_MODEL), BF16),
                   jax.ShapeDtypeStruct((L, N_TR, D), BF16)),
        compiler_params=pltpu.CompilerParams(
            dimension_semantics=("arbitrary", "arbitrary"), vmem_limit_bytes=VMEM_LIMIT),
        name="weight_relayout",
    )(w_t)


def _compress_weights(pe_k, w1_k, w2_k, pe_v, w1_v, w2_v):
    L = pe_k.shape[0]
    half = CMP_LEN // 2
    kdim = half * NSA_GROUPS * HEAD_DIM
    pe = jnp.stack([pe_k, pe_v], axis=1)
    pe = jnp.broadcast_to(pe[:, :, :, None, :], (L, 2, CMP_LEN, NSA_GROUPS, HEAD_DIM))
    w1 = jnp.stack([w1_k, w1_v], axis=1).astype(BF16)
    zero = jnp.zeros_like(w1)
    w1g = jnp.stack([jnp.stack([w1 if h == g else zero for h in range(NSA_GROUPS)], axis=3)
                     for g in range(NSA_GROUPS)], axis=2)
    w1g = w1g.reshape(L, 2 * NSA_GROUPS, CMP_LEN, NSA_GROUPS * HEAD_DIM, CMP_HIDDEN)
    w2 = jnp.stack([w2_k, w2_v], axis=1).astype(BF16)
    return (pe[:, :, :half].reshape(L, 2, 1, kdim), pe[:, :, half:].reshape(L, 2, 1, kdim),
            w1g[:, :, :half].reshape(L, 2 * NSA_GROUPS, kdim, CMP_HIDDEN),
            w1g[:, :, half:].reshape(L, 2 * NSA_GROUPS, kdim, CMP_HIDDEN), w2)


def _overlap_t(S):
    nc = S // CMP_STRIDE
    n_blk = S // SLC_BLOCK
    cstart = np.arange(nc) * CMP_STRIDE
    sstart = np.arange(n_blk) * SLC_BLOCK
    ov = ((cstart[None, :] < sstart[:, None] + SLC_BLOCK) & (cstart[None, :] + CMP_LEN > sstart[:, None]))
    return jnp.asarray(ov.astype(np.float32), dtype=BF16)


def kernel(x, p, positions, norm_g, w_in, diff_lambda, diff_subln_g, cmp_pe_k, cmp_w1_k, cmp_w2_k,
           cmp_pe_v, cmp_w1_v, cmp_w2_v, w_up, w_out, ple_norm_g, w_ple, w_ple_gate, final_norm_g):
    B, S, D = x.shape
    depth = w_in.shape[0]
    k_sel = min(DSA_TOPK_MAX, S // 4)
    n_sel = min(SLC_TOPN_MAX, S // SLC_BLOCK)
    assert S % TM_PROJ == 0 and S % TQ == 0 and k_sel <= TQ and D == D_MODEL
    assert KC == TQ and WINDOW % KC == 0
    cs, place = _rope_inputs(positions)
    ovt = _overlap_t(S)

    wn, wg, wt = _proj_weights(w_in)
    cmp_w = _compress_weights(cmp_pe_k, cmp_w1_k, cmp_w2_k, cmp_pe_v, cmp_w1_v, cmp_w2_v)
    g_in = norm_g.reshape(depth, 1, D)
    g_ple = ple_norm_g.reshape(depth, 1, D)
    g_sub = diff_subln_g.reshape(depth, -1, 1)
    wup, wout, wple, wpg = (w.astype(BF16) for w in (w_up, w_out, w_ple, w_ple_gate))
    p2 = p.reshape(depth, B * S, PLE_DIM)

    for i in range(depth):
        lam_init = 0.8 - 0.6 * math.exp(-0.3 * i)
        (qa, ka, qb, iq, qc, qcr, kb, ik, ks, kw, ck, cv,
         vat, vbt, vst, vwt, iwt, bgt) = _proj_call(x, g_in, wn, wt, cs, place, i)

        kcc, vct = _compress_call(ck, cv, *cmp_w, i)

        oa = _diff_call(qa, ka, vat, diff_lambda, g_sub, lam_init, i)
        ob = _dsa_call(qb, iq, kb, ik, vbt, iwt, k_sel)
        oc = _nsa_call(qc, qcr, kcc, vct, ovt, ks, vst, kw, vwt, bgt, n_sel)

        x = _post_call(
            x.reshape(B * S, D), oa.reshape(B * S, -1), ob.reshape(B * S, -1), oc.reshape(B * S, -1),
            p2, g_in, wg, wup, wout, g_ple, wple, wpg, final_norm_g.reshape(1, D),
            final=(i == depth - 1), layer=i).reshape(B, S, D)
    return x
```

```python
import functools
import math

import numpy as np
import jax
import jax.numpy as jnp
from jax import lax
from jax.experimental import pallas as pl
from jax.experimental.pallas import tpu as pltpu

F32 = jnp.float32
BF16 = jnp.bfloat16

D_MODEL = 1024
HEAD_DIM = 64
ROT_DIM = 16
ROPE_THETA = 500000.0
EPS = 1e-6
PLE_DIM = 256
DA_HEADS = 4
DSA_HEADS = 8
IDX_HEADS = 8
DSA_TOPK_MAX = 256
NSA_GROUPS = 2
NSA_HEADS = 8
CMP_LEN = 32
CMP_STRIDE = 16
CMP_HIDDEN = 128
SLC_BLOCK = 64
SLC_TOPN_MAX = 16
WINDOW = 512
FORCE_SCORE = 1e4
BRANCH_WIDTH = 512
N_BRANCH = 3

IN_SPLITS = (512, 512, 512, 512, 512, 64, 64, 512, 512, 64, 8,
             512, 128, 128, 128, 128, 128, 128, 512, 24, 3 * D_MODEL)
(_A_Q, _A_K, _A_V, _A_G, _B_Q, _B_K, _B_V, _B_G, _I_Q, _I_K, _I_W,
 _C_Q, _C_KC, _C_VC, _C_KS, _C_VS, _C_KW, _C_VW, _C_G, _C_BG, _MERGE) = [
    int(v) for v in np.concatenate([[0], np.cumsum(IN_SPLITS)[:-1]])]

LANES = 128
SUBLANES = 8
TQ = 512
KC = 512
TM_PROJ = 512
TM_POST = 512
NEG_INF = float("-inf")
M_INIT = -1e30
N_NAT = 3584
N_TR = 864
BISECT_PLAIN = 9
ONES_ROWS = 16
HV = HEAD_DIM + ONES_ROWS
HV_A = 2 * HEAD_DIM + ONES_ROWS
LOG2E = math.log2(math.e)
VMEM_LIMIT = 56 * 1024 * 1024


def _dot_nt(a, b):
    return lax.dot_general(a, b, (((1,), (1,)), ((), ())), preferred_element_type=F32)


def _dot(a, b):
    return jnp.dot(a, b, preferred_element_type=F32)


def _rms(x, g):
    return x * lax.rsqrt(jnp.mean(x * x, axis=-1, keepdims=True) + EPS) * g


def _sigmoid(x):
    return 1.0 / (1.0 + jnp.exp(-x))


def _store_half_masked(src_ref, dst_ref, n_pairs):
    lane = lax.broadcasted_iota(jnp.int32, (src_ref.shape[0], LANES), 1)
    for hp in range(n_pairs):
        x = src_ref[:, LANES * hp:LANES * (hp + 1)]
        z = jnp.zeros_like(x)
        dst_ref[2 * hp] = jnp.where(lane < HEAD_DIM, x, z)
        dst_ref[2 * hp + 1] = jnp.where(lane >= HEAD_DIM, x, z)


def _fold_rows(x, op):
    n_acc = 2
    accs = [x[SUBLANES * r:SUBLANES * (r + 1)] for r in range(n_acc)]
    for r in range(n_acc, x.shape[0] // SUBLANES):
        accs[r % n_acc] = op(accs[r % n_acc], x[SUBLANES * r:SUBLANES * (r + 1)])
    return op(accs[0], accs[1])


HALF = KC // 2


def _scores(k, q, dead=None):
    if dead is None:
        return _dot_nt(k, q)
    neg = jnp.full((HALF, HALF), NEG_INF, F32)
    if dead == "bl":
        return jnp.concatenate(
            [_dot_nt(k[:HALF], q), jnp.concatenate([neg, _dot_nt(k[HALF:], q[HALF:])], axis=1)], axis=0)
    return jnp.concatenate(
        [jnp.concatenate([_dot_nt(k[:HALF], q[:HALF]), neg], axis=1), _dot_nt(k[HALF:], q)], axis=0)


def _weighted_values(v_t, p, dead=None):
    if dead is None:
        return _dot(v_t, p)
    if dead == "bl":
        return jnp.concatenate([_dot(v_t[:, :HALF], p[:HALF, :HALF]), _dot(v_t, p[:, HALF:])], axis=1)
    return jnp.concatenate([_dot(v_t, p[:, :HALF]), _dot(v_t[:, HALF:], p[HALF:, HALF:])], axis=1)


class _Stage:
    def __init__(self, j0, j1, nh, hv, prep, score, values, m_ref, al_ref, acc_ref,
                 dead_first=None, dead_last=None):
        self.j0 = jnp.asarray(j0, jnp.int32)
        self.j1 = jnp.asarray(j1, jnp.int32)
        self.nh, self.hv, self.prep, self.score, self.values = nh, hv, prep, score, values
        self.m_ref, self.al_ref, self.acc_ref = m_ref, al_ref, acc_ref
        self.dead_first, self.dead_last = dead_first, dead_last

    def reset(self):
        self.m_ref[...] = jnp.full(self.m_ref.shape, M_INIT, F32)
        self.acc_ref[...] = jnp.zeros(self.acc_ref.shape, F32)

    def produce(self, sbuf_ref, j, dead=None):
        ctx = self.prep(j)
        for h in range(self.nh):
            s = self.score(ctx, h, dead)
            m_old = self.m_ref[h:h + 1, :]
            m_new = jnp.maximum(m_old, jnp.max(s, axis=0, keepdims=True))
            sbuf_ref[h] = s
            self.m_ref[h:h + 1, :] = m_new
            self.al_ref[h:h + 1, :] = jnp.exp2(m_old - m_new)

    def consume(self, sbuf_ref, j, dead=None):
        for h in range(self.nh):
            p = jnp.exp2(sbuf_ref[h] - self.m_ref[h:h + 1, :])
            rows = slice(self.hv * h, self.hv * (h + 1))
            self.acc_ref[rows, :] = (self.al_ref[h:h + 1, :] * self.acc_ref[rows, :]
                                     + _weighted_values(self.values(j, h), p.astype(BF16), dead))

    def loop(self, sbuf_ref, lo, hi):
        def body(j, carry):
            self.consume(sbuf_ref, j - 1)
            self.produce(sbuf_ref, j)
            return carry
        lax.fori_loop(lo, hi, body, 0)


def _attn_chain(stages, sbuf_ref, tile):
    for st in stages:
        st.reset()

    @pl.when(tile == 0)
    def _():
        for st in stages:
            st.produce(sbuf_ref, st.j0)
            st.consume(sbuf_ref, st.j0)

    @pl.when(tile > 0)
    def _():
        stages[0].produce(sbuf_ref, stages[0].j0, stages[0].dead_first)
        for n, st in enumerate(stages):
            st.loop(sbuf_ref, st.j0 + 1, st.j1)
            st.consume(sbuf_ref, st.j1 - 1, st.dead_first)
            st.produce(sbuf_ref, st.j1, st.dead_last)
            st.consume(sbuf_ref, st.j1, st.dead_last)
            if n + 1 < len(stages):
                stages[n + 1].produce(sbuf_ref, stages[n + 1].j0, stages[n + 1].dead_first)


def _attn_scratch(nh, hv):
    return [pltpu.VMEM((nh, KC, TQ), F32), pltpu.VMEM((nh, TQ), F32),
            pltpu.VMEM((nh, TQ), F32), pltpu.VMEM((nh * hv, TQ), F32)]


def _layer_spec(a, layer, **kw):
    tail = (0,) * (a.ndim - 1)
    return pl.BlockSpec((None,) + a.shape[1:], lambda *_: (layer,) + tail, **kw)


def _ones_rows(n_cols):
    r = lax.broadcasted_iota(jnp.int32, (ONES_ROWS, n_cols), 0)
    return jnp.where(r == 0, 1.0, 0.0).astype(BF16)


def _proj_kernel(x_ref, g_ref, wn_ref, wt_ref, cs_ref, place_ref,
                 qa_ref, ka_ref, qb_ref, iq_ref, qc_ref, qcr_ref, kb_ref, ik_ref,
                 ks_ref, kw_ref, ck_ref, cv_ref,
                 vat_ref, vbt_ref, vst_ref, vwt_ref, iwt_ref, bgt_ref, zc_ref):
    h = _rms(x_ref[...], g_ref[...]).astype(BF16)
    cs = cs_ref[...]
    hi = cs.astype(BF16)
    rest = cs - hi.astype(F32)
    mid = rest.astype(BF16)
    low = (rest - mid.astype(F32)).astype(BF16)
    spread = _dot(hi, place_ref[...]) + _dot(mid, place_ref[...]) + _dot(low, place_ref[...])
    rc, rs1, rs2 = spread[:, 0:LANES], spread[:, LANES:2 * LANES], spread[:, 2 * LANES:3 * LANES]

    def rope(z):
        return z * rc + pltpu.roll(z, LANES - ROT_DIM // 2, 1) * rs1 + pltpu.roll(z, ROT_DIM // 2, 1) * rs2

    qscale = HEAD_DIM ** -0.5
    qscale2 = qscale * LOG2E
    segs = (
        (0, 512, ((qa_ref, True, qscale2),)),
        (512, 512, ((ka_ref, True, 1.0),)),
        (1024, 512, ((qb_ref, True, qscale2),)),
        (1536, 512, ((iq_ref, True, qscale),)),
        (2048, 512, ((qc_ref, False, qscale), (qcr_ref, True, qscale2))),
        (2560, 128, ((kb_ref, True, 1.0),)),
        (2688, 128, ((ik_ref, True, 1.0),)),
        (2816, 256, ((ks_ref, True, 1.0),)),
        (3072, 256, ((kw_ref, True, 1.0),)),
    )
    zc = _dot(h, wn_ref[:, 3328:3328 + 2 * LANES])
    half_len = CMP_LEN // 2
    for n, c_ref in enumerate((ck_ref, cv_ref)):
        zc_ref[n] = zc[:, LANES * n:LANES * (n + 1)]
        for l in range(half_len):
            c_ref[:, LANES * l:LANES * (l + 1)] = zc_ref[n, pl.ds(l, zc.shape[0] // half_len, stride=half_len), :]
    for c0, width, outs in segs:
        z = _dot(h, wn_ref[:, c0:c0 + width])
        for out_ref, rot, scale in outs:
            for j in range(width // LANES):
                zj = z[:, LANES * j:LANES * (j + 1)]
                if rot:
                    zj = rope(zj)
                if scale != 1.0:
                    zj = zj * scale
                out_ref[:, LANES * j:LANES * (j + 1)] = zj.astype(out_ref.dtype)

    zt = _dot_nt(wt_ref[...], h)
    ones = _ones_rows(zt.shape[1])
    vd = 2 * HEAD_DIM
    for h in range(DA_HEADS):
        vat_ref[HV_A * h:HV_A * h + vd, :] = zt[vd * h:vd * (h + 1)].astype(BF16)
        vat_ref[HV_A * h + vd:HV_A * (h + 1), :] = ones
    vbt_ref[0:HEAD_DIM, :] = zt[512:576].astype(BF16)
    vbt_ref[HEAD_DIM:HV, :] = ones
    for g in range(NSA_GROUPS):
        for ref, base in ((vst_ref, 576), (vwt_ref, 704)):
            ref[HV * g:HV * g + HEAD_DIM, :] = zt[base + HEAD_DIM * g:base + HEAD_DIM * (g + 1)].astype(BF16)
            ref[HV * g + HEAD_DIM:HV * (g + 1), :] = ones
    iwt_ref[...] = zt[832:840] * (IDX_HEADS ** -0.5)
    bgt_ref[...] = zt[840:864]


def _proj_call(x, g, wn, wt, cs, place, layer):
    B, S, D = x.shape
    tm = TM_PROJ
    nat = lambda w: pl.BlockSpec((None, tm, w), lambda b, i: (b, i, 0))
    tr = lambda r: pl.BlockSpec((None, r, tm), lambda b, i: (b, 0, i))
    full = lambda a: _layer_spec(a, layer)
    sds = jax.ShapeDtypeStruct
    half_len = CMP_LEN // 2
    out_shape = (
        sds((B, S, 512), BF16), sds((B, S, 512), BF16), sds((B, S, 512), BF16), sds((B, S, 512), BF16),
        sds((B, S, 512), BF16), sds((B, S, 512), BF16), sds((B, S, 128), BF16), sds((B, S, 128), BF16),
        sds((B, S, 256), BF16), sds((B, S, 256), BF16),
        sds((B, S // half_len, half_len * LANES), F32), sds((B, S // half_len, half_len * LANES), F32),
        sds((B, DA_HEADS * HV_A, S), BF16), sds((B, HV, S), BF16),
        sds((B, NSA_GROUPS * HV, S), BF16), sds((B, NSA_GROUPS * HV, S), BF16),
        sds((B, IDX_HEADS, S), F32), sds((B, 3 * NSA_HEADS, S), F32),
    )
    cmp_rows = pl.BlockSpec((None, tm // half_len, half_len * LANES), lambda b, i: (b, i, 0))
    out_specs = (nat(512), nat(512), nat(512), nat(512), nat(512), nat(512), nat(128), nat(128),
                 nat(256), nat(256), cmp_rows, cmp_rows,
                 tr(DA_HEADS * HV_A), tr(HV), tr(NSA_GROUPS * HV), tr(NSA_GROUPS * HV),
                 tr(IDX_HEADS), tr(3 * NSA_HEADS))
    return pl.pallas_call(
        _proj_kernel,
        grid=(B, S // tm),
        in_specs=[nat(D), full(g), full(wn), full(wt), nat(ROPE_COLS),
                  pl.BlockSpec(place.shape, lambda b, i: (0, 0))],
        out_specs=out_specs,
        out_shape=out_shape,
        scratch_shapes=[pltpu.VMEM((2, tm, LANES), F32)],
        compiler_params=pltpu.CompilerParams(
            dimension_semantics=("arbitrary", "arbitrary"), vmem_limit_bytes=VMEM_LIMIT),
        name="in_proj",
    )(x, g, wn, wt, cs, place)


def _compress_kernel(rk_ref, rv_ref, pea_ref, peb_ref, w1a_ref, w1b_ref, w2_ref, nat_ref, t_ref):
    for kind, r_ref in enumerate((rk_ref, rv_ref)):
        r = r_ref[...]
        xa = (r + pea_ref[kind]).astype(BF16)
        xb = (r + peb_ref[kind]).astype(BF16)
        nrow = r.shape[0]
        for g in range(NSA_GROUPS):
            n = NSA_GROUPS * kind + g
            hid = jax.nn.gelu(_dot(xa, w1a_ref[n]) + pltpu.roll(_dot(xb, w1b_ref[n]), nrow - 1, 0))
            o = _dot(hid.astype(BF16), w2_ref[kind])
            nat_ref[n] = jnp.concatenate([o, o], axis=1).astype(BF16)
            t_ref[n] = o.T.astype(BF16)


def _compress_call(rk, rv, pea, peb, w1a, w1b, w2, layer):
    B, nr, kdim = rk.shape
    full = lambda a: _layer_spec(a, layer)
    rspec = pl.BlockSpec((None, nr, kdim), lambda b: (b, 0, 0))
    return pl.pallas_call(
        _compress_kernel,
        grid=(B,),
        in_specs=[rspec, rspec, full(pea), full(peb), full(w1a), full(w1b), full(w2)],
        out_specs=(
            pl.BlockSpec((None, 2 * NSA_GROUPS, nr, 2 * HEAD_DIM), lambda b: (b, 0, 0, 0)),
            pl.BlockSpec((None, 2 * NSA_GROUPS, HEAD_DIM, nr), lambda b: (b, 0, 0, 0)),
        ),
        out_shape=(jax.ShapeDtypeStruct((B, 2 * NSA_GROUPS, nr, 2 * HEAD_DIM), BF16),
                   jax.ShapeDtypeStruct((B, 2 * NSA_GROUPS, HEAD_DIM, nr), BF16)),
        compiler_params=pltpu.CompilerParams(dimension_semantics=("arbitrary",)),
        name="nsa_compress",
    )(rk, rv, pea, peb, w1a, w1b, w2)


def _diff_kernel(q_ref, k_ref, vt_ref, lam_ref, sg_ref, o_ref,
                 qz_ref, sbuf_ref, m_ref, al_ref, acc_ref, *, lam_init):
    i = pl.program_id(1)
    lp = lam_ref[...]
    lam = (jnp.exp(jnp.sum(lp[0:1] * lp[1:2], axis=1, keepdims=True))
           - jnp.exp(jnp.sum(lp[2:3] * lp[3:4], axis=1, keepdims=True)) + lam_init)
    row_minus_col = (lax.broadcasted_iota(jnp.int32, (KC, TQ), 0)
                     - lax.broadcasted_iota(jnp.int32, (KC, TQ), 1))
    _store_half_masked(q_ref, qz_ref, DA_HEADS)
    vdim = 2 * HEAD_DIM
    n_maps = 2 * DA_HEADS

    def prep(j):
        return pl.multiple_of(j * KC, KC), jnp.where(row_minus_col <= (i - j) * KC, 0.0, NEG_INF)

    def score(ctx, hm, dead):
        off, bias = ctx
        h = hm // 2
        return _scores(k_ref[pl.ds(off, KC), vdim * h:vdim * (h + 1)], qz_ref[hm], dead) + bias

    def values(j, hm):
        h = hm // 2
        return vt_ref[HV_A * h:HV_A * (h + 1), pl.ds(pl.multiple_of(j * KC, KC), KC)]

    _attn_chain([_Stage(0, i, n_maps, HV_A, prep, score, values, m_ref, al_ref, acc_ref, dead_last="bl")],
                sbuf_ref, i)

    for h in range(DA_HEADS):
        r0, r1 = HV_A * 2 * h, HV_A * (2 * h + 1)
        o_t = (acc_ref[r0:r0 + vdim, :] / acc_ref[r0 + vdim:r0 + vdim + 1, :]
               - lam * (acc_ref[r1:r1 + vdim, :] / acc_ref[r1 + vdim:r1 + vdim + 1, :]))
        ms = jnp.mean(o_t * o_t, axis=0, keepdims=True)
        y = o_t * lax.rsqrt(ms + EPS) * sg_ref[...] * (1.0 - lam_init)
        o_ref[:, vdim * h:vdim * (h + 1)] = y.T


def _diff_call(qa, ka, vat, lam_p, sg, lam_init, layer):
    B, S, W = qa.shape
    n_maps = 2 * DA_HEADS
    return pl.pallas_call(
        functools.partial(_diff_kernel, lam_init=lam_init),
        grid=(B, S // TQ),
        in_specs=[
            pl.BlockSpec((None, TQ, W), lambda b, i: (b, i, 0)),
            pl.BlockSpec((None, S, W), lambda b, i: (b, 0, 0)),
            pl.BlockSpec((None, DA_HEADS * HV_A, S), lambda b, i: (b, 0, 0)),
            _layer_spec(lam_p, layer),
            _layer_spec(sg, layer),
        ],
        out_specs=pl.BlockSpec((None, TQ, W), lambda b, i: (b, i, 0)),
        out_shape=jax.ShapeDtypeStruct((B, S, W), F32),
        scratch_shapes=[pltpu.VMEM((n_maps, TQ, LANES), BF16)] + _attn_scratch(n_maps, HV_A),
        compiler_params=pltpu.CompilerParams(
            dimension_semantics=("arbitrary", "arbitrary"), vmem_limit_bytes=VMEM_LIMIT),
        name="diff_attn",
    )(qa, ka, vat, lam_p, sg)


def _dsa_kernel(q_ref, iq_ref, kb_ref, ik_ref, vt_ref, iw_ref, o_ref,
                sc_ref, tj_ref, tri_ref, qz_ref, sbuf_ref, m_ref, al_ref, acc_ref, *, k_sel, seq_len):
    i = pl.program_id(1)
    nch = i + 1
    row = lax.broadcasted_iota(jnp.int32, (KC, TQ), 0)
    col = lax.broadcasted_iota(jnp.int32, (KC, TQ), 1)
    causal = row <= col
    iw = iw_ref[...]
    kf = float(k_sel)
    _store_half_masked(iq_ref, qz_ref, IDX_HEADS // 2)

    def idx_chunk(j, carry, masked):
        mx, mn = carry
        off = pl.multiple_of(j * KC, KC)
        ikc = ik_ref[pl.ds(off, KC), :]
        sc = jnp.zeros((KC, TQ), F32)
        for h in range(IDX_HEADS):
            dots = _scores(ikc, qz_ref[h], "bl" if masked else None)
            sc = sc + iw[h:h + 1, :] * jnp.maximum(dots, 0.0)
        lo_src = sc
        if masked:
            lo_src = jnp.where(causal, sc, jnp.inf)
            sc = jnp.where(causal, sc, NEG_INF)
        sc_ref[pl.ds(off, KC), :] = sc
        return (jnp.maximum(mx, _fold_rows(sc, jnp.maximum)),
                jnp.minimum(mn, _fold_rows(lo_src, jnp.minimum)))

    carry = (jnp.full((SUBLANES, TQ), NEG_INF, F32), jnp.full((SUBLANES, TQ), jnp.inf, F32))
    carry = lax.fori_loop(0, i, functools.partial(idx_chunk, masked=False), carry)
    mx8, mn8 = idx_chunk(i, carry, True)
    row_max = jnp.max(mx8, axis=0, keepdims=True)
    row_min = jnp.min(mn8, axis=0, keepdims=True)

    def probe_pass(th, snap):
        thb = jnp.broadcast_to(th, (SUBLANES, TQ))

        def body(j, c):
            off = pl.multiple_of(j * KC, KC)
            c = list(c)
            for r in range(KC // SUBLANES):
                x = sc_ref[pl.ds(off + SUBLANES * r, SUBLANES), :]
                ge = x >= thb
                c[0] = c[0] + jnp.where(ge, 1.0, 0.0)
                if snap:
                    c[1] = jnp.maximum(c[1], jnp.where(ge, NEG_INF, x))
            return tuple(c)
        init = (jnp.zeros((SUBLANES, TQ), F32),)
        if snap:
            init += (jnp.full((SUBLANES, TQ), NEG_INF, F32),)
        out = lax.fori_loop(0, nch, body, init)
        cnt = jnp.sum(out[0], axis=0, keepdims=True)
        return (cnt, jnp.max(out[1], axis=0, keepdims=True)) if snap else cnt

    n_causal = (i * TQ + 1 + lax.broadcasted_iota(jnp.int32, (1, TQ), 1)).astype(F32)
    keep_all = n_causal <= kf

    def bisect(_, c):
        lo, hi, clo, chi = c
        mid = lo + (hi - lo) * 0.5
        cnt = probe_pass(mid, False)
        ge = cnt >= kf
        return (jnp.where(ge, mid, lo), jnp.where(ge, hi, mid),
                jnp.where(ge, cnt, clo), jnp.where(ge, chi, cnt))

    hi0 = row_max + jnp.abs(row_max) * 1e-6 + 1e-30
    lo, hi, clo, chi = lax.fori_loop(
        0, BISECT_PLAIN, bisect, (row_min, hi0, n_causal, jnp.zeros((1, TQ), F32)))

    def cond(st):
        return jnp.logical_and(st[0] < 256, st[-1] > 0.0)

    def snap_step(st):
        it, lo, hi, clo, chi, hie, known, stuck, done, _ = st
        top = jnp.where(known > 0.0, hie, hi)
        mid = lo + (top - lo) * 0.5
        inside = jnp.logical_and(mid > lo, mid < top)
        near_top = jnp.logical_or(kf - chi <= 2.0, stuck > 0.0)
        use_top = jnp.logical_or(jnp.logical_and(known > 0.0, near_top), jnp.logical_not(inside))
        th = jnp.where(use_top, top, mid)
        cnt, edn = probe_pass(th, True)
        live = done <= 0.0
        ge = jnp.logical_and(live, cnt >= kf)
        lt = jnp.logical_and(live, cnt < kf)
        stuck = jnp.where(jnp.logical_and(ge, cnt == clo), 1.0, 0.0)
        lo = jnp.where(ge, th, lo)
        clo = jnp.where(ge, cnt, clo)
        hi = jnp.where(lt, th, hi)
        chi = jnp.where(lt, cnt, chi)
        hie = jnp.where(lt, edn, hie)
        known = jnp.where(lt, 1.0, known)
        fin = jnp.logical_or(clo == kf, jnp.logical_and(known > 0.0, lo >= hie))
        done = jnp.where(fin, 1.0, done)
        return it + 1, lo, hi, clo, chi, hie, known, stuck, done, jnp.max(1.0 - done)

    zero = jnp.zeros((1, TQ), F32)
    done0 = jnp.where(jnp.logical_or(clo == kf, keep_all), 1.0, 0.0)
    st = lax.while_loop(cond, snap_step, (jnp.int32(0), lo, hi, clo, chi, jnp.full((1, TQ), NEG_INF, F32),
                                          zero, zero, done0, jnp.max(1.0 - done0)))
    thr, clo, chi = st[1], st[3], st[4]
    need = jnp.where(clo == kf, float(seq_len), kf - chi)
    thr = jnp.where(keep_all, NEG_INF, thr)
    need = jnp.where(keep_all, 0.0, need)
    tj_ref[2:3, :] = jnp.zeros((1, TQ), F32)

    _store_half_masked(q_ref, qz_ref, DSA_HEADS // 2)
    half = KC // 2
    tri_ref[...] = jnp.where(lax.broadcasted_iota(jnp.int32, (half, half), 0)
                             >= lax.broadcasted_iota(jnp.int32, (half, half), 1), 1.0, 0.0).astype(BF16)

    def prep(j):
        off = pl.multiple_of(j * KC, KC)
        x = sc_ref[pl.ds(off, KC), :]
        eq = x == thr
        e = jnp.where(eq, 1.0, 0.0).astype(BF16)
        r_top = _dot(tri_ref[...], e[:half]) + tj_ref[2:3, :]
        r_bot = _dot(tri_ref[...], e[half:]) + r_top[half - 1:half, :]
        tj_ref[2:3, :] = r_bot[half - 1:half, :]
        rank = jnp.concatenate([r_top, r_bot], axis=0)
        tie_bias = jnp.where(jnp.logical_and(eq, rank <= need), 0.0, NEG_INF)
        return kb_ref[pl.ds(off, KC), :], jnp.where(x > thr, 0.0, tie_bias)

    def score(ctx, h, dead):
        kc, bias = ctx
        return _scores(kc, qz_ref[h], dead) + bias

    def values(j, h):
        return vt_ref[:, pl.ds(pl.multiple_of(j * KC, KC), KC)]

    _attn_chain([_Stage(0, i, DSA_HEADS, HV, prep, score, values, m_ref, al_ref, acc_ref, dead_last="bl")],
                sbuf_ref, i)
    for hp in range(DSA_HEADS // 2):
        blk = jnp.concatenate(
            [acc_ref[HV * h:HV * h + HEAD_DIM, :] / acc_ref[HV * h + HEAD_DIM:HV * h + HEAD_DIM + 1, :]
             for h in (2 * hp, 2 * hp + 1)], axis=0)
        o_ref[:, LANES * hp:LANES * (hp + 1)] = blk.T


def _dsa_call(qb, iq, kb, ik, vbt, iwt, k_sel):
    B, S, W = qb.shape
    return pl.pallas_call(
        functools.partial(_dsa_kernel, k_sel=k_sel, seq_len=S),
        grid=(B, S // TQ),
        in_specs=[
            pl.BlockSpec((None, TQ, W), lambda b, i: (b, i, 0)),
            pl.BlockSpec((None, TQ, W), lambda b, i: (b, i, 0)),
            pl.BlockSpec((None, S, LANES), lambda b, i: (b, 0, 0)),
            pl.BlockSpec((None, S, LANES), lambda b, i: (b, 0, 0)),
            pl.BlockSpec((None, HV, S), lambda b, i: (b, 0, 0)),
            pl.BlockSpec((None, IDX_HEADS, TQ), lambda b, i: (b, 0, i)),
        ],
        out_specs=pl.BlockSpec((None, TQ, W), lambda b, i: (b, i, 0)),
        out_shape=jax.ShapeDtypeStruct((B, S, W), F32),
        scratch_shapes=[pltpu.VMEM((S, TQ), F32), pltpu.VMEM((SUBLANES, TQ), F32),
                        pltpu.VMEM((KC // 2, KC // 2), BF16),
                        pltpu.VMEM((DSA_HEADS, TQ, LANES), BF16)] + _attn_scratch(DSA_HEADS, HV),
        compiler_params=pltpu.CompilerParams(
            dimension_semantics=("arbitrary", "arbitrary"), vmem_limit_bytes=VMEM_LIMIT),
        name="dsa_attn",
    )(qb, iq, kb, ik, vbt, iwt)


def _nsa_kernel(q_ref, qr_ref, kc_ref, vct_ref, ovt_ref, ks_ref, vst_ref, kw_ref, vwt_ref, bg_ref,
                o_ref, cmp_ref, sel_ref, qz_ref, qrz_ref, mw_ref, alw_ref, accw_ref,
                sbuf_ref, m_ref, al_ref, acc_ref, *, n_sel):
    i = pl.program_id(1)
    row_minus_col = (lax.broadcasted_iota(jnp.int32, (KC, TQ), 0)
                     - lax.broadcasted_iota(jnp.int32, (KC, TQ), 1))
    tq = i * TQ + lax.broadcasted_iota(jnp.int32, (1, TQ), 1)
    hpg = NSA_HEADS // NSA_GROUPS
    nc = kc_ref.shape[1]
    n_blk = sel_ref.shape[1]
    gw = 2 * HEAD_DIM

    _store_half_masked(q_ref, qz_ref, NSA_HEADS // 2)
    _store_half_masked(qr_ref, qrz_ref, NSA_HEADS // 2)

    def compressed_and_select():
        cmp_valid = CMP_STRIDE * lax.broadcasted_iota(jnp.int32, (nc, TQ), 0) + (CMP_LEN - 1) <= tq
        blk = lax.broadcasted_iota(jnp.int32, (n_blk, TQ), 0)
        forced = jnp.logical_or(blk == jnp.right_shift(tq, int(math.log2(SLC_BLOCK))), blk == 0)
        blk_causal = blk * SLC_BLOCK <= tq
        for h in range(NSA_HEADS):
            s = jnp.where(cmp_valid, _dot_nt(kc_ref[h // hpg], qz_ref[h]), NEG_INF)
            sbuf_ref[h, 0:nc, :] = s
            m_ref[h:h + 1, :] = jnp.maximum(jnp.max(s, axis=0, keepdims=True), M_INIT)
        for g in range(NSA_GROUPS):
            vcg = vct_ref[NSA_GROUPS + g]
            imp = jnp.zeros((n_blk, TQ), F32)
            for hh in range(hpg):
                h = hpg * g + hh
                p = jnp.exp(sbuf_ref[h, 0:nc, :] - m_ref[h:h + 1, :])
                l = jnp.sum(p, axis=0, keepdims=True)
                pc = (p * jnp.where(l > 0.0, 1.0 / l, 0.0)).astype(BF16)
                cmp_ref[HEAD_DIM * h:HEAD_DIM * (h + 1), :] = _dot(vcg, pc)
                imp = imp + _dot(ovt_ref[...], pc)
            imp = jnp.where(forced, FORCE_SCORE, imp)
            imp = jnp.where(blk_causal, imp, NEG_INF)
            n_tiles = n_blk // SUBLANES
            sub = lax.broadcasted_iota(jnp.int32, (SUBLANES, TQ), 0)
            tiles = [imp[SUBLANES * t:SUBLANES * (t + 1)] for t in range(n_tiles)]
            ranks = [jnp.zeros((SUBLANES, TQ), F32)] * n_tiles
            for mp in range(n_blk):
                r = jnp.broadcast_to(imp[mp:mp + 1, :], (SUBLANES, TQ))
                for t in range(n_tiles):
                    if SUBLANES * t > mp:
                        ahead = r >= tiles[t]
                    elif SUBLANES * (t + 1) <= mp:
                        ahead = r > tiles[t]
                    else:
                        later = sub > mp - SUBLANES * t
                        ahead = jnp.logical_or(r > tiles[t], jnp.logical_and(r == tiles[t], later))
                    ranks[t] = ranks[t] + jnp.where(ahead, 1.0, 0.0)
            rank = jnp.concatenate(ranks, axis=0)
            sel_ref[g] = jnp.where(rank < float(n_sel), 0.0, NEG_INF)

    bpc = KC // SLC_BLOCK

    def slc_prep(j):
        off = pl.multiple_of(j * KC, KC)
        visible = row_minus_col <= (i - j) * KC
        biases = []
        for g in range(NSA_GROUPS):
            bias = jnp.concatenate(
                [jnp.broadcast_to(sel_ref[g, pl.ds(bpc * j + b, 1), :], (SLC_BLOCK, TQ))
                 for b in range(bpc)], axis=0)
            biases.append(jnp.where(visible, bias, NEG_INF))
        return off, biases

    def slc_score(ctx, h, dead):
        off, biases = ctx
        g = h // hpg
        return _scores(ks_ref[pl.ds(off, KC), gw * g:gw * (g + 1)], qrz_ref[h], dead) + biases[g]

    def slc_values(j, h):
        g = h // hpg
        return vst_ref[HV * g:HV * (g + 1), pl.ds(pl.multiple_of(j * KC, KC), KC)]

    def win_prep(j):
        off = pl.multiple_of(j * KC, KC)
        d = (i - j) * KC
        inside = jnp.logical_and(row_minus_col <= d, row_minus_col > d - WINDOW)
        return off, jnp.where(inside, 0.0, NEG_INF)

    def win_score(ctx, h, dead):
        off, bias = ctx
        g = h // hpg
        return _scores(kw_ref[pl.ds(off, KC), gw * g:gw * (g + 1)], qrz_ref[h], dead) + bias

    def win_values(j, h):
        g = h // hpg
        return vwt_ref[HV * g:HV * (g + 1), pl.ds(pl.multiple_of(j * KC, KC), KC)]

    compressed_and_select()
    assert WINDOW == KC
    slc = _Stage(0, i, NSA_HEADS, HV, slc_prep, slc_score, slc_values, m_ref, al_ref, acc_ref,
                 dead_last="bl")
    win = _Stage(jnp.maximum(i - 1, 0), i, NSA_HEADS, HV, win_prep, win_score, win_values,
                 mw_ref, alw_ref, accw_ref, dead_first="tr", dead_last="bl")
    _attn_chain([slc, win], sbuf_ref, i)

    def head_out(ref, h):
        return ref[HV * h:HV * h + HEAD_DIM, :] / ref[HV * h + HEAD_DIM:HV * h + HEAD_DIM + 1, :]

    gb = _sigmoid(bg_ref[...])
    for hp in range(NSA_HEADS // 2):
        parts = []
        for h in (2 * hp, 2 * hp + 1):
            parts.append(gb[3 * h:3 * h + 1, :] * cmp_ref[HEAD_DIM * h:HEAD_DIM * (h + 1), :]
                         + gb[3 * h + 1:3 * h + 2, :] * head_out(acc_ref, h)
                         + gb[3 * h + 2:3 * h + 3, :] * head_out(accw_ref, h))
        o_ref[:, LANES * hp:LANES * (hp + 1)] = jnp.concatenate(parts, axis=0).T


def _nsa_call(qc, qcr, kcc, vct, ovt, ks, vst, kw, vwt, bgt, n_sel):
    B, S, W = qc.shape
    n_blk = S // SLC_BLOCK
    qspec = pl.BlockSpec((None, TQ, W), lambda b, i: (b, i, 0))
    return pl.pallas_call(
        functools.partial(_nsa_kernel, n_sel=n_sel),
        grid=(B, S // TQ),
        in_specs=[
            qspec, qspec,
            pl.BlockSpec((None,) + kcc.shape[1:], lambda b, i: (b, 0, 0, 0)),
            pl.BlockSpec((None,) + vct.shape[1:], lambda b, i: (b, 0, 0, 0)),
            pl.BlockSpec(ovt.shape, lambda b, i: (0, 0)),
            pl.BlockSpec((None, S, 2 * LANES), lambda b, i: (b, 0, 0)),
            pl.BlockSpec((None, NSA_GROUPS * HV, S), lambda b, i: (b, 0, 0)),
            pl.BlockSpec((None, S, 2 * LANES), lambda b, i: (b, 0, 0)),
            pl.BlockSpec((None, NSA_GROUPS * HV, S), lambda b, i: (b, 0, 0)),
            pl.BlockSpec((None, 3 * NSA_HEADS, TQ), lambda b, i: (b, 0, i)),
        ],
        out_specs=qspec,
        out_shape=jax.ShapeDtypeStruct((B, S, W), F32),
        scratch_shapes=[pltpu.VMEM((W, TQ), F32), pltpu.VMEM((NSA_GROUPS, n_blk, TQ), F32),
                        pltpu.VMEM((NSA_HEADS, TQ, LANES), BF16), pltpu.VMEM((NSA_HEADS, TQ, LANES), BF16)]
        + _attn_scratch(NSA_HEADS, HV)[1:] + _attn_scratch(NSA_HEADS, HV),
        compiler_params=pltpu.CompilerParams(
            dimension_semantics=("arbitrary", "arbitrary"), vmem_limit_bytes=VMEM_LIMIT),
        name="nsa_attn",
    )(qc, qcr, kcc, vct, ovt, ks, vst, kw, vwt, bgt)


def _post_kernel(x_ref, oa_ref, ob_ref, oc_ref, p_ref, g_ref, wg_ref, wup_ref, wout_ref,
                 pg_ref, wple_ref, wpg_ref, fg_ref, o_ref, *, final):
    x = x_ref[...]
    h = _rms(x, g_ref[...]).astype(BF16)
    mixed = jnp.zeros(x.shape, F32)
    for n, br_ref in enumerate((oa_ref, ob_ref, oc_ref)):
        gate = _dot(h, wg_ref[:, BRANCH_WIDTH * n:BRANCH_WIDTH * (n + 1)])
        og = br_ref[...] * (gate * _sigmoid(gate))
        up = _dot(og.astype(BF16), wup_ref[n])
        c0 = N_BRANCH * BRANCH_WIDTH + D_MODEL * n
        mixed = mixed + _sigmoid(_dot(h, wg_ref[:, c0:c0 + D_MODEL])) * up
    x1 = x + _dot(mixed.astype(BF16), wout_ref[...])
    emb = _dot(p_ref[...].astype(BF16), wple_ref[...])
    hg = _dot(_rms(x1, pg_ref[...]).astype(BF16), wpg_ref[...])
    x2 = x1 + emb * _sigmoid(hg)
    if final:
        x2 = _rms(x2, fg_ref[...])
    o_ref[...] = x2


def _post_call(x, oa, ob, oc, p, g, wg, wup, wout, pg, wple, wpg, fg, final, layer):
    T, D = x.shape
    tm = TM_POST
    tok = lambda w: pl.BlockSpec((tm, w), lambda i: (i, 0))
    full = lambda a: _layer_spec(a, layer, pipeline_mode=pl.Buffered(1))
    return pl.pallas_call(
        functools.partial(_post_kernel, final=final),
        grid=(T // tm,),
        in_specs=[tok(D), tok(BRANCH_WIDTH), tok(BRANCH_WIDTH), tok(BRANCH_WIDTH),
                  pl.BlockSpec((None, tm, PLE_DIM), lambda i: (layer, i, 0)),
                  full(g), full(wg), full(wup), full(wout), full(pg), full(wple), full(wpg),
                  pl.BlockSpec(fg.shape, lambda i: (0, 0))],
        out_specs=tok(D),
        out_shape=jax.ShapeDtypeStruct((T, D), F32),
        compiler_params=pltpu.CompilerParams(
            dimension_semantics=("arbitrary",), vmem_limit_bytes=VMEM_LIMIT),
        name="post_mix",
    )(x, oa, ob, oc, p, g, wg, wup, wout, pg, wple, wpg, fg)


ROPE_COLS = 3 * (ROT_DIM // 2)


def _rope_inputs(positions):
    half = ROT_DIM // 2
    inv = ROPE_THETA ** (-jnp.arange(0, ROT_DIM, 2, dtype=F32) / ROT_DIM)
    ang = positions.astype(F32)[..., None] * inv
    cs = jnp.concatenate([jnp.cos(ang), jnp.sin(ang), jnp.ones_like(ang)], axis=-1)
    place = np.zeros((ROPE_COLS, 3 * LANES), np.float32)
    for head in range(LANES // HEAD_DIM):
        base = HEAD_DIM * head
        for j in range(half):
            place[j, base + j] = 1.0
            place[j, base + half + j] = 1.0
            place[half + j, LANES + base + j] = -1.0
            place[half + j, 2 * LANES + base + half + j] = 1.0
        place[2 * half, base + ROT_DIM:base + HEAD_DIM] = 1.0
    return cs, jnp.asarray(place, dtype=BF16)


_NAT_PIECES = (
    (0, _A_Q, 512), (512, _A_K, 512), (1024, _B_Q, 512), (1536, _I_Q, 512), (2048, _C_Q, 512),
    (2560, _B_K, 64), (2624, _B_K, 64), (2688, _I_K, 64), (2752, _I_K, 64),
    (2816, _C_KS, 64), (2880, _C_KS, 64), (2944, _C_KS + 64, 64), (3008, _C_KS + 64, 64),
    (3072, _C_KW, 64), (3136, _C_KW, 64), (3200, _C_KW + 64, 64), (3264, _C_KW + 64, 64),
    (3328, _C_KC, 128), (3456, _C_VC, 128))
_GATE_PIECES = ((0, _A_G, 512), (512, _B_G, 512), (1024, _C_G, 512), (1536, _MERGE, 3 * D_MODEL))
_TR_PIECES = ((0, _A_V, 512), (512, _B_V, 64), (576, _C_VS, 128), (704, _C_VW, 128),
              (832, _I_W, 8), (840, _C_BG, 24))
def _relayout_kernel(wt_ref, nat_ref, gates_ref, trt_ref):
    for dst_ref, pieces in ((nat_ref, _NAT_PIECES), (gates_ref, _GATE_PIECES)):
        for d0, s0, n in pieces:
            for r in range(0, n, LANES):
                m = min(LANES, n - r)
                dst_ref[:, d0 + r:d0 + r + m] = wt_ref[s0 + r:s0 + r + m, :].T.astype(BF16)
    groups, cur = [], []
    for d0, s0, n in _TR_PIECES:
        if cur and d0 % ONES_ROWS == 0:
            groups.append(cur)
            cur = []
        cur.append((d0, s0, n))
    groups.append(cur)
    for grp in groups:
        rows = jnp.concatenate([wt_ref[s0:s0 + n, :] for _, s0, n in grp], axis=0)
        trt_ref[grp[0][0]:grp[0][0] + rows.shape[0], :] = rows.astype(BF16)


def _proj_weights(w_in):
    L, D, n_in = w_in.shape
    w_t = jnp.swapaxes(w_in, 1, 2)
    return pl.pallas_call(
        _relayout_kernel,
        grid=(L, D // LANES),
        in_specs=[pl.BlockSpec((None, n_in, LANES), lambda l, r: (l, 0, r))],
        out_specs=(pl.BlockSpec((None, LANES, N_NAT), lambda l, r: (l, r, 0)),
                   pl.BlockSpec((None, LANES, BRANCH_WIDTH * N_BRANCH + N_BRANCH * D_MODEL), lambda l, r: (l, r, 0)),
                   pl.BlockSpec((None, N_TR, LANES), lambda l, r: (l, 0, r))),
        out_shape=(jax.ShapeDtypeStruct((L, D, N_NAT), BF16),
                   jax.ShapeDtypeStruct((L, D, BRANCH_WIDTH * N_BRANCH + N_BRANCH * D_MODEL), BF16),
                   jax.ShapeDtypeStruct((L, N_TR, D), BF16)),
        compiler_params=pltpu.CompilerParams(
            dimension_semantics=("arbitrary", "arbitrary"), vmem_limit_bytes=VMEM_LIMIT),
        name="weight_relayout",
    )(w_t)


def _compress_weights(pe_k, w1_k, w2_k, pe_v, w1_v, w2_v):
    L = pe_k.shape[0]
    half = CMP_LEN // 2
    kdim = half * NSA_GROUPS * HEAD_DIM
    pe = jnp.stack([pe_k, pe_v], axis=1)
    pe = jnp.broadcast_to(pe[:, :, :, None, :], (L, 2, CMP_LEN, NSA_GROUPS, HEAD_DIM))
    w1 = jnp.stack([w1_k, w1_v], axis=1).astype(BF16)
    zero = jnp.zeros_like(w1)
    w1g = jnp.stack([jnp.stack([w1 if h == g else zero for h in range(NSA_GROUPS)], axis=3)
                     for g in range(NSA_GROUPS)], axis=2)
    w1g = w1g.reshape(L, 2 * NSA_GROUPS, CMP_LEN, NSA_GROUPS * HEAD_DIM, CMP_HIDDEN)
    w2 = jnp.stack([w2_k, w2_v], axis=1).astype(BF16)
    return (pe[:, :, :half].reshape(L, 2, 1, kdim), pe[:, :, half:].reshape(L, 2, 1, kdim),
            w1g[:, :, :half].reshape(L, 2 * NSA_GROUPS, kdim, CMP_HIDDEN),
            w1g[:, :, half:].reshape(L, 2 * NSA_GROUPS, kdim, CMP_HIDDEN), w2)


def _overlap_t(S):
    nc = S // CMP_STRIDE
    n_blk = S // SLC_BLOCK
    cstart = np.arange(nc) * CMP_STRIDE
    sstart = np.arange(n_blk) * SLC_BLOCK
    ov = ((cstart[None, :] < sstart[:, None] + SLC_BLOCK) & (cstart[None, :] + CMP_LEN > sstart[:, None]))
    return jnp.asarray(ov.astype(np.float32), dtype=BF16)


def kernel(x, p, positions, norm_g, w_in, diff_lambda, diff_subln_g, cmp_pe_k, cmp_w1_k, cmp_w2_k,
           cmp_pe_v, cmp_w1_v, cmp_w2_v, w_up, w_out, ple_norm_g, w_ple, w_ple_gate, final_norm_g):
    B, S, D = x.shape
    depth = w_in.shape[0]
    k_sel = min(DSA_TOPK_MAX, S // 4)
    n_sel = min(SLC_TOPN_MAX, S // SLC_BLOCK)
    assert S % TM_PROJ == 0 and S % TQ == 0 and k_sel <= TQ and D == D_MODEL
    assert KC == TQ and WINDOW % KC == 0
    cs, place = _rope_inputs(positions)
    ovt = _overlap_t(S)

    wn, wg, wt = _proj_weights(w_in)
    cmp_w = _compress_weights(cmp_pe_k, cmp_w1_k, cmp_w2_k, cmp_pe_v, cmp_w1_v, cmp_w2_v)
    g_in = norm_g.reshape(depth, 1, D)
    g_ple = ple_norm_g.reshape(depth, 1, D)
    g_sub = diff_subln_g.reshape(depth, -1, 1)
    wup, wout, wple, wpg = (w.astype(BF16) for w in (w_up, w_out, w_ple, w_ple_gate))
    p2 = p.reshape(depth, B * S, PLE_DIM)

    for i in range(depth):
        lam_init = 0.8 - 0.6 * math.exp(-0.3 * i)
        (qa, ka, qb, iq, qc, qcr, kb, ik, ks, kw, ck, cv,
         vat, vbt, vst, vwt, iwt, bgt) = _proj_call(x, g_in, wn, wt, cs, place, i)

        kcc, vct = _compress_call(ck, cv, *cmp_w, i)

        oa = _diff_call(qa, ka, vat, diff_lambda, g_sub, lam_init, i)
        ob = _dsa_call(qb, iq, kb, ik, vbt, iwt, k_sel)
        oc = _nsa_call(qc, qcr, kcc, vct, ovt, ks, vst, kw, vwt, bgt, n_sel)

        x = _post_call(
            x.reshape(B * S, D), oa.reshape(B * S, -1), ob.reshape(B * S, -1), oc.reshape(B * S, -1),
            p2, g_in, wg, wup, wout, g_ple, wple, wpg, final_norm_g.reshape(1, D),
            final=(i == depth - 1), layer=i).reshape(B, S, D)
    return x
```

```python
import functools
import math

import numpy as np
import jax
import jax.numpy as jnp
from jax import lax
from jax.experimental import pallas as pl
from jax.experimental.pallas import tpu as pltpu

F32 = jnp.float32
BF16 = jnp.bfloat16

D_MODEL = 1024
HEAD_DIM = 64
ROT_DIM = 16
ROPE_THETA = 500000.0
EPS = 1e-6
PLE_DIM = 256
DA_HEADS = 4
DSA_HEADS = 8
IDX_HEADS = 8
DSA_TOPK_MAX = 256
NSA_GROUPS = 2
NSA_HEADS = 8
CMP_LEN = 32
CMP_STRIDE = 16
CMP_HIDDEN = 128
SLC_BLOCK = 64
SLC_TOPN_MAX = 16
WINDOW = 512
FORCE_SCORE = 1e4
BRANCH_WIDTH = 512
N_BRANCH = 3

IN_SPLITS = (512, 512, 512, 512, 512, 64, 64, 512, 512, 64, 8,
             512, 128, 128, 128, 128, 128, 128, 512, 24, 3 * D_MODEL)
(_A_Q, _A_K, _A_V, _A_G, _B_Q, _B_K, _B_V, _B_G, _I_Q, _I_K, _I_W,
 _C_Q, _C_KC, _C_VC, _C_KS, _C_VS, _C_KW, _C_VW, _C_G, _C_BG, _MERGE) = [
    int(v) for v in np.concatenate([[0], np.cumsum(IN_SPLITS)[:-1]])]

LANES = 128
SUBLANES = 8
TQ = 512
KC = 512
TM_PROJ = 512
TM_POST = 512
NEG_INF = float("-inf")
M_INIT = -1e30
N_NAT = 3584
N_TR = 864
BISECT_PLAIN = 9
ONES_ROWS = 16
HV = HEAD_DIM + ONES_ROWS
HV_A = 2 * HEAD_DIM + ONES_ROWS
LOG2E = math.log2(math.e)
VMEM_LIMIT = 56 * 1024 * 1024


def _dot_nt(a, b):
    return lax.dot_general(a, b, (((1,), (1,)), ((), ())), preferred_element_type=F32)


def _dot(a, b):
    return jnp.dot(a, b, preferred_element_type=F32)


def _rms(x, g):
    return x * lax.rsqrt(jnp.mean(x * x, axis=-1, keepdims=True) + EPS) * g


def _sigmoid(x):
    return 1.0 / (1.0 + jnp.exp(-x))


def _store_half_masked(src_ref, dst_ref, n_pairs):
    lane = lax.broadcasted_iota(jnp.int32, (src_ref.shape[0], LANES), 1)
    for hp in range(n_pairs):
        x = src_ref[:, LANES * hp:LANES * (hp + 1)]
        z = jnp.zeros_like(x)
        dst_ref[2 * hp] = jnp.where(lane < HEAD_DIM, x, z)
        dst_ref[2 * hp + 1] = jnp.where(lane >= HEAD_DIM, x, z)


def _fold_rows(x, op):
    n_acc = 2
    accs = [x[SUBLANES * r:SUBLANES * (r + 1)] for r in range(n_acc)]
    for r in range(n_acc, x.shape[0] // SUBLANES):
        accs[r % n_acc] = op(accs[r % n_acc], x[SUBLANES * r:SUBLANES * (r + 1)])
    return op(accs[0], accs[1])


HALF = KC // 2


def _scores(k, q, dead=None):
    if dead is None:
        return _dot_nt(k, q)
    neg = jnp.full((HALF, HALF), NEG_INF, F32)
    if dead == "bl":
        return jnp.concatenate(
            [_dot_nt(k[:HALF], q), jnp.concatenate([neg, _dot_nt(k[HALF:], q[HALF:])], axis=1)], axis=0)
    return jnp.concatenate(
        [jnp.concatenate([_dot_nt(k[:HALF], q[:HALF]), neg], axis=1), _dot_nt(k[HALF:], q)], axis=0)


def _weighted_values(v_t, p, dead=None):
    if dead is None:
        return _dot(v_t, p)
    if dead == "bl":
        return jnp.concatenate([_dot(v_t[:, :HALF], p[:HALF, :HALF]), _dot(v_t, p[:, HALF:])], axis=1)
    return jnp.concatenate([_dot(v_t, p[:, :HALF]), _dot(v_t[:, HALF:], p[HALF:, HALF:])], axis=1)


class _Stage:
    def __init__(self, j0, j1, nh, hv, prep, score, values, m_ref, al_ref, acc_ref,
                 dead_first=None, dead_last=None):
        self.j0 = jnp.asarray(j0, jnp.int32)
        self.j1 = jnp.asarray(j1, jnp.int32)
        self.nh, self.hv, self.prep, self.score, self.values = nh, hv, prep, score, values
        self.m_ref, self.al_ref, self.acc_ref = m_ref, al_ref, acc_ref
        self.dead_first, self.dead_last = dead_first, dead_last

    def reset(self):
        self.m_ref[...] = jnp.full(self.m_ref.shape, M_INIT, F32)
        self.acc_ref[...] = jnp.zeros(self.acc_ref.shape, F32)

    def produce(self, sbuf_ref, j, dead=None):
        ctx = self.prep(j)
        for h in range(self.nh):
            s = self.score(ctx, h, dead)
            m_old = self.m_ref[h:h + 1, :]
            m_new = jnp.maximum(m_old, jnp.max(s, axis=0, keepdims=True))
            sbuf_ref[h] = s
            self.m_ref[h:h + 1, :] = m_new
            self.al_ref[h:h + 1, :] = jnp.exp2(m_old - m_new)

    def consume(self, sbuf_ref, j, dead=None):
        for h in range(self.nh):
            p = jnp.exp2(sbuf_ref[h] - self.m_ref[h:h + 1, :])
            rows = slice(self.hv * h, self.hv * (h + 1))
            self.acc_ref[rows, :] = (self.al_ref[h:h + 1, :] * self.acc_ref[rows, :]
                                     + _weighted_values(self.values(j, h), p.astype(BF16), dead))

    def loop(self, sbuf_ref, lo, hi):
        def body(j, carry):
            self.consume(sbuf_ref, j - 1)
            self.produce(sbuf_ref, j)
            return carry
        lax.fori_loop(lo, hi, body, 0)


def _attn_chain(stages, sbuf_ref, tile):
    for st in stages:
        st.reset()

    @pl.when(tile == 0)
    def _():
        for st in stages:
            st.produce(sbuf_ref, st.j0)
            st.consume(sbuf_ref, st.j0)

    @pl.when(tile > 0)
    def _():
        stages[0].produce(sbuf_ref, stages[0].j0, stages[0].dead_first)
        for n, st in enumerate(stages):
            st.loop(sbuf_ref, st.j0 + 1, st.j1)
            st.consume(sbuf_ref, st.j1 - 1, st.dead_first)
            st.produce(sbuf_ref, st.j1, st.dead_last)
            st.consume(sbuf_ref, st.j1, st.dead_last)
            if n + 1 < len(stages):
                stages[n + 1].produce(sbuf_ref, stages[n + 1].j0, stages[n + 1].dead_first)


def _attn_scratch(nh, hv):
    return [pltpu.VMEM((nh, KC, TQ), F32), pltpu.VMEM((nh, TQ), F32),
            pltpu.VMEM((nh, TQ), F32), pltpu.VMEM((nh * hv, TQ), F32)]


def _layer_spec(a, layer, **kw):
    tail = (0,) * (a.ndim - 1)
    return pl.BlockSpec((None,) + a.shape[1:], lambda *_: (layer,) + tail, **kw)


def _ones_rows(n_cols):
    r = lax.broadcasted_iota(jnp.int32, (ONES_ROWS, n_cols), 0)
    return jnp.where(r == 0, 1.0, 0.0).astype(BF16)


def _proj_kernel(x_ref, g_ref, wn_ref, wt_ref, cs_ref, place_ref,
                 qa_ref, ka_ref, qb_ref, iq_ref, qc_ref, qcr_ref, kb_ref, ik_ref,
                 ks_ref, kw_ref, ck_ref, cv_ref,
                 vat_ref, vbt_ref, vst_ref, vwt_ref, iwt_ref, bgt_ref, zc_ref):
    h = _rms(x_ref[...], g_ref[...]).astype(BF16)
    cs = cs_ref[...]
    hi = cs.astype(BF16)
    rest = cs - hi.astype(F32)
    mid = rest.astype(BF16)
    low = (rest - mid.astype(F32)).astype(BF16)
    spread = _dot(hi, place_ref[...]) + _dot(mid, place_ref[...]) + _dot(low, place_ref[...])
    rc, rs1, rs2 = spread[:, 0:LANES], spread[:, LANES:2 * LANES], spread[:, 2 * LANES:3 * LANES]

    def rope(z):
        return z * rc + pltpu.roll(z, LANES - ROT_DIM // 2, 1) * rs1 + pltpu.roll(z, ROT_DIM // 2, 1) * rs2

    qscale = HEAD_DIM ** -0.5
    qscale2 = qscale * LOG2E
    segs = (
        (0, 512, ((qa_ref, True, qscale2),)),
        (512, 512, ((ka_ref, True, 1.0),)),
        (1024, 512, ((qb_ref, True, qscale2),)),
        (1536, 512, ((iq_ref, True, qscale),)),
        (2048, 512, ((qc_ref, False, qscale), (qcr_ref, True, qscale2))),
        (2560, 128, ((kb_ref, True, 1.0),)),
        (2688, 128, ((ik_ref, True, 1.0),)),
        (2816, 256, ((ks_ref, True, 1.0),)),
        (3072, 256, ((kw_ref, True, 1.0),)),
    )
    zc = _dot(h, wn_ref[:, 3328:3328 + 2 * LANES])
    half_len = CMP_LEN // 2
    for n, c_ref in enumerate((ck_ref, cv_ref)):
        zc_ref[n] = zc[:, LANES * n:LANES * (n + 1)]
        for l in range(half_len):
            c_ref[:, LANES * l:LANES * (l + 1)] = zc_ref[n, pl.ds(l, zc.shape[0] // half_len, stride=half_len), :]
    for c0, width, outs in segs:
        z = _dot(h, wn_ref[:, c0:c0 + width])
        for out_ref, rot, scale in outs:
            for j in range(width // LANES):
                zj = z[:, LANES * j:LANES * (j + 1)]
                if rot:
                    zj = rope(zj)
                if scale != 1.0:
                    zj = zj * scale
                out_ref[:, LANES * j:LANES * (j + 1)] = zj.astype(out_ref.dtype)

    zt = _dot_nt(wt_ref[...], h)
    ones = _ones_rows(zt.shape[1])
    vd = 2 * HEAD_DIM
    for h in range(DA_HEADS):
        vat_ref[HV_A * h:HV_A * h + vd, :] = zt[vd * h:vd * (h + 1)].astype(BF16)
        vat_ref[HV_A * h + vd:HV_A * (h + 1), :] = ones
    vbt_ref[0:HEAD_DIM, :] = zt[512:576].astype(BF16)
    vbt_ref[HEAD_DIM:HV, :] = ones
    for g in range(NSA_GROUPS):
        for ref, base in ((vst_ref, 576), (vwt_ref, 704)):
            ref[HV * g:HV * g + HEAD_DIM, :] = zt[base + HEAD_DIM * g:base + HEAD_DIM * (g + 1)].astype(BF16)
            ref[HV * g + HEAD_DIM:HV * (g + 1), :] = ones
    iwt_ref[...] = zt[832:840] * (IDX_HEADS ** -0.5)
    bgt_ref[...] = zt[840:864]


def _proj_call(x, g, wn, wt, cs, place, layer):
    B, S, D = x.shape
    tm = TM_PROJ
    nat = lambda w: pl.BlockSpec((None, tm, w), lambda b, i: (b, i, 0))
    tr = lambda r: pl.BlockSpec((None, r, tm), lambda b, i: (b, 0, i))
    full = lambda a: _layer_spec(a, layer)
    sds = jax.ShapeDtypeStruct
    half_len = CMP_LEN // 2
    out_shape = (
        sds((B, S, 512), BF16), sds((B, S, 512), BF16), sds((B, S, 512), BF16), sds((B, S, 512), BF16),
        sds((B, S, 512), BF16), sds((B, S, 512), BF16), sds((B, S, 128), BF16), sds((B, S, 128), BF16),
        sds((B, S, 256), BF16), sds((B, S, 256), BF16),
        sds((B, S // half_len, half_len * LANES), F32), sds((B, S // half_len, half_len * LANES), F32),
        sds((B, DA_HEADS * HV_A, S), BF16), sds((B, HV, S), BF16),
        sds((B, NSA_GROUPS * HV, S), BF16), sds((B, NSA_GROUPS * HV, S), BF16),
        sds((B, IDX_HEADS, S), F32), sds((B, 3 * NSA_HEADS, S), F32),
    )
    cmp_rows = pl.BlockSpec((None, tm // half_len, half_len * LANES), lambda b, i: (b, i, 0))
    out_specs = (nat(512), nat(512), nat(512), nat(512), nat(512), nat(512), nat(128), nat(128),
                 nat(256), nat(256), cmp_rows, cmp_rows,
                 tr(DA_HEADS * HV_A), tr(HV), tr(NSA_GROUPS * HV), tr(NSA_GROUPS * HV),
                 tr(IDX_HEADS), tr(3 * NSA_HEADS))
    return pl.pallas_call(
        _proj_kernel,
        grid=(B, S // tm),
        in_specs=[nat(D), full(g), full(wn), full(wt), nat(ROPE_COLS),
                  pl.BlockSpec(place.shape, lambda b, i: (0, 0))],
        out_specs=out_specs,
        out_shape=out_shape,
        scratch_shapes=[pltpu.VMEM((2, tm, LANES), F32)],
        compiler_params=pltpu.CompilerParams(
            dimension_semantics=("arbitrary", "arbitrary"), vmem_limit_bytes=VMEM_LIMIT),
        name="in_proj",
    )(x, g, wn, wt, cs, place)


def _compress_kernel(rk_ref, rv_ref, pea_ref, peb_ref, w1a_ref, w1b_ref, w2_ref, nat_ref, t_ref):
    for kind, r_ref in enumerate((rk_ref, rv_ref)):
        r = r_ref[...]
        xa = (r + pea_ref[kind]).astype(BF16)
        xb = (r + peb_ref[kind]).astype(BF16)
        nrow = r.shape[0]
        for g in range(NSA_GROUPS):
            n = NSA_GROUPS * kind + g
            hid = jax.nn.gelu(_dot(xa, w1a_ref[n]) + pltpu.roll(_dot(xb, w1b_ref[n]), nrow - 1, 0))
            o = _dot(hid.astype(BF16), w2_ref[kind])
            nat_ref[n] = jnp.concatenate([o, o], axis=1).astype(BF16)
            t_ref[n] = o.T.astype(BF16)


def _compress_call(rk, rv, pea, peb, w1a, w1b, w2, layer):
    B, nr, kdim = rk.shape
    full = lambda a: _layer_spec(a, layer)
    rspec = pl.BlockSpec((None, nr, kdim), lambda b: (b, 0, 0))
    return pl.pallas_call(
        _compress_kernel,
        grid=(B,),
        in_specs=[rspec, rspec, full(pea), full(peb), full(w1a), full(w1b), full(w2)],
        out_specs=(
            pl.BlockSpec((None, 2 * NSA_GROUPS, nr, 2 * HEAD_DIM), lambda b: (b, 0, 0, 0)),
            pl.BlockSpec((None, 2 * NSA_GROUPS, HEAD_DIM, nr), lambda b: (b, 0, 0, 0)),
        ),
        out_shape=(jax.ShapeDtypeStruct((B, 2 * NSA_GROUPS, nr, 2 * HEAD_DIM), BF16),
                   jax.ShapeDtypeStruct((B, 2 * NSA_GROUPS, HEAD_DIM, nr), BF16)),
        compiler_params=pltpu.CompilerParams(dimension_semantics=("arbitrary",)),
        name="nsa_compress",
    )(rk, rv, pea, peb, w1a, w1b, w2)


def _diff_kernel(q_ref, k_ref, vt_ref, lam_ref, sg_ref, o_ref,
                 qz_ref, sbuf_ref, m_ref, al_ref, acc_ref, *, lam_init):
    i = pl.program_id(1)
    lp = lam_ref[...]
    lam = (jnp.exp(jnp.sum(lp[0:1] * lp[1:2], axis=1, keepdims=True))
           - jnp.exp(jnp.sum(lp[2:3] * lp[3:4], axis=1, keepdims=True)) + lam_init)
    row_minus_col = (lax.broadcasted_iota(jnp.int32, (KC, TQ), 0)
                     - lax.broadcasted_iota(jnp.int32, (KC, TQ), 1))
    _store_half_masked(q_ref, qz_ref, DA_HEADS)
    vdim = 2 * HEAD_DIM
    n_maps = 2 * DA_HEADS

    def prep(j):
        return pl.multiple_of(j * KC, KC), jnp.where(row_minus_col <= (i - j) * KC, 0.0, NEG_INF)

    def score(ctx, hm, dead):
        off, bias = ctx
        h = hm // 2
        return _scores(k_ref[pl.ds(off, KC), vdim * h:vdim * (h + 1)], qz_ref[hm], dead) + bias

    def values(j, hm):
        h = hm // 2
        return vt_ref[HV_A * h:HV_A * (h + 1), pl.ds(pl.multiple_of(j * KC, KC), KC)]

    _attn_chain([_Stage(0, i, n_maps, HV_A, prep, score, values, m_ref, al_ref, acc_ref, dead_last="bl")],
                sbuf_ref, i)

    for h in range(DA_HEADS):
        r0, r1 = HV_A * 2 * h, HV_A * (2 * h + 1)
        o_t = (acc_ref[r0:r0 + vdim, :] / acc_ref[r0 + vdim:r0 + vdim + 1, :]
               - lam * (acc_ref[r1:r1 + vdim, :] / acc_ref[r1 + vdim:r1 + vdim + 1, :]))
        ms = jnp.mean(o_t * o_t, axis=0, keepdims=True)
        y = o_t * lax.rsqrt(ms + EPS) * sg_ref[...] * (1.0 - lam_init)
        o_ref[:, vdim * h:vdim * (h + 1)] = y.T


def _diff_call(qa, ka, vat, lam_p, sg, lam_init, layer):
    B, S, W = qa.shape
    n_maps = 2 * DA_HEADS
    return pl.pallas_call(
        functools.partial(_diff_kernel, lam_init=lam_init),
        grid=(B, S // TQ),
        in_specs=[
            pl.BlockSpec((None, TQ, W), lambda b, i: (b, i, 0)),
            pl.BlockSpec((None, S, W), lambda b, i: (b, 0, 0)),
            pl.BlockSpec((None, DA_HEADS * HV_A, S), lambda b, i: (b, 0, 0)),
            _layer_spec(lam_p, layer),
            _layer_spec(sg, layer),
        ],
        out_specs=pl.BlockSpec((None, TQ, W), lambda b, i: (b, i, 0)),
        out_shape=jax.ShapeDtypeStruct((B, S, W), F32),
        scratch_shapes=[pltpu.VMEM((n_maps, TQ, LANES), BF16)] + _attn_scratch(n_maps, HV_A),
        compiler_params=pltpu.CompilerParams(
            dimension_semantics=("arbitrary", "arbitrary"), vmem_limit_bytes=VMEM_LIMIT),
        name="diff_attn",
    )(qa, ka, vat, lam_p, sg)


def _dsa_kernel(q_ref, iq_ref, kb_ref, ik_ref, vt_ref, iw_ref, o_ref,
                sc_ref, tj_ref, tri_ref, qz_ref, sbuf_ref, m_ref, al_ref, acc_ref, *, k_sel, seq_len):
    i = pl.program_id(1)
    nch = i + 1
    row = lax.broadcasted_iota(jnp.int32, (KC, TQ), 0)
    col = lax.broadcasted_iota(jnp.int32, (KC, TQ), 1)
    causal = row <= col
    iw = iw_ref[...]
    kf = float(k_sel)
    _store_half_masked(iq_ref, qz_ref, IDX_HEADS // 2)

    def idx_chunk(j, carry, masked):
        mx, mn = carry
        off = pl.multiple_of(j * KC, KC)
        ikc = ik_ref[pl.ds(off, KC), :]
        sc = jnp.zeros((KC, TQ), F32)
        for h in range(IDX_HEADS):
            dots = _scores(ikc, qz_ref[h], "bl" if masked else None)
            sc = sc + iw[h:h + 1, :] * jnp.maximum(dots, 0.0)
        lo_src = sc
        if masked:
            lo_src = jnp.where(causal, sc, jnp.inf)
            sc = jnp.where(causal, sc, NEG_INF)
        sc_ref[pl.ds(off, KC), :] = sc
        return (jnp.maximum(mx, _fold_rows(sc, jnp.maximum)),
                jnp.minimum(mn, _fold_rows(lo_src, jnp.minimum)))

    carry = (jnp.full((SUBLANES, TQ), NEG_INF, F32), jnp.full((SUBLANES, TQ), jnp.inf, F32))
    carry = lax.fori_loop(0, i, functools.partial(idx_chunk, masked=False), carry)
    mx8, mn8 = idx_chunk(i, carry, True)
    row_max = jnp.max(mx8, axis=0, keepdims=True)
    row_min = jnp.min(mn8, axis=0, keepdims=True)

    def probe_pass(th, snap):
        thb = jnp.broadcast_to(th, (SUBLANES, TQ))

        def body(j, c):
            off = pl.multiple_of(j * KC, KC)
            c = list(c)
            for r in range(KC // SUBLANES):
                x = sc_ref[pl.ds(off + SUBLANES * r, SUBLANES), :]
                ge = x >= thb
                c[0] = c[0] + jnp.where(ge, 1.0, 0.0)
                if snap:
                    c[1] = jnp.maximum(c[1], jnp.where(ge, NEG_INF, x))
            return tuple(c)
        init = (jnp.zeros((SUBLANES, TQ), F32),)
        if snap:
            init += (jnp.full((SUBLANES, TQ), NEG_INF, F32),)
        out = lax.fori_loop(0, nch, body, init)
        cnt = jnp.sum(out[0], axis=0, keepdims=True)
        return (cnt, jnp.max(out[1], axis=0, keepdims=True)) if snap else cnt

    n_causal = (i * TQ + 1 + lax.broadcasted_iota(jnp.int32, (1, TQ), 1)).astype(F32)
    keep_all = n_causal <= kf

    def bisect(_, c):
        lo, hi, clo, chi = c
        mid = lo + (hi - lo) * 0.5
        cnt = probe_pass(mid, False)
        ge = cnt >= kf
        return (jnp.where(ge, mid, lo), jnp.where(ge, hi, mid),
                jnp.where(ge, cnt, clo), jnp.where(ge, chi, cnt))

    hi0 = row_max + jnp.abs(row_max) * 1e-6 + 1e-30
    lo, hi, clo, chi = lax.fori_loop(
        0, BISECT_PLAIN, bisect, (row_min, hi0, n_causal, jnp.zeros((1, TQ), F32)))

    def cond(st):
        return jnp.logical_and(st[0] < 256, st[-1] > 0.0)

    def snap_step(st):
        it, lo, hi, clo, chi, hie, known, stuck, done, _ = st
        top = jnp.where(known > 0.0, hie, hi)
        mid = lo + (top - lo) * 0.5
        inside = jnp.logical_and(mid > lo, mid < top)
        near_top = jnp.logical_or(kf - chi <= 2.0, stuck > 0.0)
        use_top = jnp.logical_or(jnp.logical_and(known > 0.0, near_top), jnp.logical_not(inside))
        th = jnp.where(use_top, top, mid)
        cnt, edn = probe_pass(th, True)
        live = done <= 0.0
        ge = jnp.logical_and(live, cnt >= kf)
        lt = jnp.logical_and(live, cnt < kf)
        stuck = jnp.where(jnp.logical_and(ge, cnt == clo), 1.0, 0.0)
        lo = jnp.where(ge, th, lo)
        clo = jnp.where(ge, cnt, clo)
        hi = jnp.where(lt, th, hi)
        chi = jnp.where(lt, cnt, chi)
        hie = jnp.where(lt, edn, hie)
        known = jnp.where(lt, 1.0, known)
        fin = jnp.logical_or(clo == kf, jnp.logical_and(known > 0.0, lo >= hie))
        done = jnp.where(fin, 1.0, done)
        return it + 1, lo, hi, clo, chi, hie, known, stuck, done, jnp.max(1.0 - done)

    zero = jnp.zeros((1, TQ), F32)
    done0 = jnp.where(jnp.logical_or(clo == kf, keep_all), 1.0, 0.0)
    st = lax.while_loop(cond, snap_step, (jnp.int32(0), lo, hi, clo, chi, jnp.full((1, TQ), NEG_INF, F32),
                                          zero, zero, done0, jnp.max(1.0 - done0)))
    thr, clo, chi = st[1], st[3], st[4]
    need = jnp.where(clo == kf, float(seq_len), kf - chi)
    thr = jnp.where(keep_all, NEG_INF, thr)
    need = jnp.where(keep_all, 0.0, need)
    tj_ref[2:3, :] = jnp.zeros((1, TQ), F32)

    _store_half_masked(q_ref, qz_ref, DSA_HEADS // 2)
    half = KC // 2
    tri_ref[...] = jnp.where(lax.broadcasted_iota(jnp.int32, (half, half), 0)
                             >= lax.broadcasted_iota(jnp.int32, (half, half), 1), 1.0, 0.0).astype(BF16)

    def prep(j):
        off = pl.multiple_of(j * KC, KC)
        x = sc_ref[pl.ds(off, KC), :]
        eq = x == thr
        e = jnp.where(eq, 1.0, 0.0).astype(BF16)
        r_top = _dot(tri_ref[...], e[:half]) + tj_ref[2:3, :]
        r_bot = _dot(tri_ref[...], e[half:]) + r_top[half - 1:half, :]
        tj_ref[2:3, :] = r_bot[half - 1:half, :]
        rank = jnp.concatenate([r_top, r_bot], axis=0)
        tie_bias = jnp.where(jnp.logical_and(eq, rank <= need), 0.0, NEG_INF)
        return kb_ref[pl.ds(off, KC), :], jnp.where(x > thr, 0.0, tie_bias)

    def score(ctx, h, dead):
        kc, bias = ctx
        return _scores(kc, qz_ref[h], dead) + bias

    def values(j, h):
        return vt_ref[:, pl.ds(pl.multiple_of(j * KC, KC), KC)]

    _attn_chain([_Stage(0, i, DSA_HEADS, HV, prep, score, values, m_ref, al_ref, acc_ref, dead_last="bl")],
                sbuf_ref, i)
    for hp in range(DSA_HEADS // 2):
        blk = jnp.concatenate(
            [acc_ref[HV * h:HV * h + HEAD_DIM, :] / acc_ref[HV * h + HEAD_DIM:HV * h + HEAD_DIM + 1, :]
             for h in (2 * hp, 2 * hp + 1)], axis=0)
        o_ref[:, LANES * hp:LANES * (hp + 1)] = blk.T


def _dsa_call(qb, iq, kb, ik, vbt, iwt, k_sel):
    B, S, W = qb.shape
    return pl.pallas_call(
        functools.partial(_dsa_kernel, k_sel=k_sel, seq_len=S),
        grid=(B, S // TQ),
        in_specs=[
            pl.BlockSpec((None, TQ, W), lambda b, i: (b, i, 0)),
            pl.BlockSpec((None, TQ, W), lambda b, i: (b, i, 0)),
            pl.BlockSpec((None, S, LANES), lambda b, i: (b, 0, 0)),
            pl.BlockSpec((None, S, LANES), lambda b, i: (b, 0, 0)),
            pl.BlockSpec((None, HV, S), lambda b, i: (b, 0, 0)),
            pl.BlockSpec((None, IDX_HEADS, TQ), lambda b, i: (b, 0, i)),
        ],
        out_specs=pl.BlockSpec((None, TQ, W), lambda b, i: (b, i, 0)),
        out_shape=jax.ShapeDtypeStruct((B, S, W), F32),
        scratch_shapes=[pltpu.VMEM((S, TQ), F32), pltpu.VMEM((SUBLANES, TQ), F32),
                        pltpu.VMEM((KC // 2, KC // 2), BF16),
                        pltpu.VMEM((DSA_HEADS, TQ, LANES), BF16)] + _attn_scratch(DSA_HEADS, HV),
        compiler_params=pltpu.CompilerParams(
            dimension_semantics=("arbitrary", "arbitrary"), vmem_limit_bytes=VMEM_LIMIT),
        name="dsa_attn",
    )(qb, iq, kb, ik, vbt, iwt)


def _rank_tile(rk_ref, m, lanes):
    rows = lambda t: slice(SUBLANES * t, SUBLANES * (t + 1))
    own = rk_ref[0, rows(m), lanes]
    sub = lax.broadcasted_iota(jnp.int32, own.shape, 0)
    own_rows = [jnp.broadcast_to(own[r:r + 1], own.shape) for r in range(SUBLANES)]
    rank = jnp.zeros(own.shape, F32)
    for r, x in enumerate(own_rows):
        ahead = jnp.logical_or(x > own, jnp.logical_and(x == own, sub > r))
        rank = rank + jnp.where(ahead, 1.0, 0.0)
    for t in range(m):
        older = rk_ref[0, rows(t), lanes]
        rank_t = rk_ref[1, rows(t), lanes]
        for r in range(SUBLANES):
            rank = rank + jnp.where(jnp.broadcast_to(older[r:r + 1], own.shape) >= own, 1.0, 0.0)
            rank_t = rank_t + jnp.where(own_rows[r] > older, 1.0, 0.0)
        rk_ref[1, rows(t), lanes] = rank_t
    rk_ref[1, rows(m), lanes] = rank


def _nsa_kernel(q_ref, qr_ref, kc_ref, vct_ref, ovt_ref, ks_ref, vst_ref, kw_ref, vwt_ref, bg_ref,
                o_ref, cmp_ref, sel_ref, rk_ref, qz_ref, qrz_ref, mw_ref, alw_ref, accw_ref,
                sbuf_ref, m_ref, al_ref, acc_ref, *, n_sel):
    i = pl.program_id(1)
    row_minus_col = (lax.broadcasted_iota(jnp.int32, (KC, TQ), 0)
                     - lax.broadcasted_iota(jnp.int32, (KC, TQ), 1))
    tq = i * TQ + lax.broadcasted_iota(jnp.int32, (1, TQ), 1)
    hpg = NSA_HEADS // NSA_GROUPS
    nc = kc_ref.shape[1]
    n_blk = sel_ref.shape[1]
    gw = 2 * HEAD_DIM

    _store_half_masked(q_ref, qz_ref, NSA_HEADS // 2)
    _store_half_masked(qr_ref, qrz_ref, NSA_HEADS // 2)

    def compressed_and_select():
        cmp_valid = CMP_STRIDE * lax.broadcasted_iota(jnp.int32, (nc, TQ), 0) + (CMP_LEN - 1) <= tq
        blk = lax.broadcasted_iota(jnp.int32, (n_blk, TQ), 0)
        forced = jnp.logical_or(blk == jnp.right_shift(tq, int(math.log2(SLC_BLOCK))), blk == 0)
        blk_causal = blk * SLC_BLOCK <= tq
        for h in range(NSA_HEADS):
            s = jnp.where(cmp_valid, _dot_nt(kc_ref[h // hpg], qz_ref[h]), NEG_INF)
            sbuf_ref[h, 0:nc, :] = s
            m_ref[h:h + 1, :] = jnp.maximum(jnp.max(s, axis=0, keepdims=True), M_INIT)
        for g in range(NSA_GROUPS):
            vcg = vct_ref[NSA_GROUPS + g]
            imp = jnp.zeros((n_blk, TQ), F32)
            for hh in range(hpg):
                h = hpg * g + hh
                p = jnp.exp(sbuf_ref[h, 0:nc, :] - m_ref[h:h + 1, :])
                l = jnp.sum(p, axis=0, keepdims=True)
                pc = (p * jnp.where(l > 0.0, 1.0 / l, 0.0)).astype(BF16)
                cmp_ref[HEAD_DIM * h:HEAD_DIM * (h + 1), :] = _dot(vcg, pc)
                imp = imp + _dot(ovt_ref[...], pc)
            imp = jnp.where(forced, FORCE_SCORE, imp)
            imp = jnp.where(blk_causal, imp, NEG_INF)
            rk_ref[0] = imp
            rk_ref[1] = jnp.zeros((n_blk, TQ), F32)
            for m in range(n_blk // SUBLANES):
                @pl.when(m * SUBLANES * SLC_BLOCK < (i + 1) * TQ)
                def _(m=m):
                    for l in range(TQ // LANES):
                        _rank_tile(rk_ref, m, slice(LANES * l, LANES * (l + 1)))
            sel_ref[g] = jnp.where(rk_ref[1] < float(n_sel), 0.0, NEG_INF)

    bpc = KC // SLC_BLOCK

    def slc_prep(j):
        off = pl.multiple_of(j * KC, KC)
        visible = row_minus_col <= (i - j) * KC
        biases = []
        for g in range(NSA_GROUPS):
            bias = jnp.concatenate(
                [jnp.broadcast_to(sel_ref[g, pl.ds(bpc * j + b, 1), :], (SLC_BLOCK, TQ))
                 for b in range(bpc)], axis=0)
            biases.append(jnp.where(visible, bias, NEG_INF))
        return off, biases

    def slc_score(ctx, h, dead):
        off, biases = ctx
        g = h // hpg
        return _scores(ks_ref[pl.ds(off, KC), gw * g:gw * (g + 1)], qrz_ref[h], dead) + biases[g]

    def slc_values(j, h):
        g = h // hpg
        return vst_ref[HV * g:HV * (g + 1), pl.ds(pl.multiple_of(j * KC, KC), KC)]

    def win_prep(j):
        off = pl.multiple_of(j * KC, KC)
        d = (i - j) * KC
        inside = jnp.logical_and(row_minus_col <= d, row_minus_col > d - WINDOW)
        return off, jnp.where(inside, 0.0, NEG_INF)

    def win_score(ctx, h, dead):
        off, bias = ctx
        g = h // hpg
        return _scores(kw_ref[pl.ds(off, KC), gw * g:gw * (g + 1)], qrz_ref[h], dead) + bias

    def win_values(j, h):
        g = h // hpg
        return vwt_ref[HV * g:HV * (g + 1), pl.ds(pl.multiple_of(j * KC, KC), KC)]

    compressed_and_select()
    assert WINDOW == KC
    slc = _Stage(0, i, NSA_HEADS, HV, slc_prep, slc_score, slc_values, m_ref, al_ref, acc_ref,
                 dead_last="bl")
    win = _Stage(jnp.maximum(i - 1, 0), i, NSA_HEADS, HV, win_prep, win_score, win_values,
                 mw_ref, alw_ref, accw_ref, dead_first="tr", dead_last="bl")
    _attn_chain([slc, win], sbuf_ref, i)

    def head_out(ref, h):
        return ref[HV * h:HV * h + HEAD_DIM, :] / ref[HV * h + HEAD_DIM:HV * h + HEAD_DIM + 1, :]

    gb = _sigmoid(bg_ref[...])
    for hp in range(NSA_HEADS // 2):
        parts = []
        for h in (2 * hp, 2 * hp + 1):
            parts.append(gb[3 * h:3 * h + 1, :] * cmp_ref[HEAD_DIM * h:HEAD_DIM * (h + 1), :]
                         + gb[3 * h + 1:3 * h + 2, :] * head_out(acc_ref, h)
                         + gb[3 * h + 2:3 * h + 3, :] * head_out(accw_ref, h))
        o_ref[:, LANES * hp:LANES * (hp + 1)] = jnp.concatenate(parts, axis=0).T


def _nsa_call(qc, qcr, kcc, vct, ovt, ks, vst, kw, vwt, bgt, n_sel):
    B, S, W = qc.shape
    n_blk = S // SLC_BLOCK
    qspec = pl.BlockSpec((None, TQ, W), lambda b, i: (b, i, 0))
    return pl.pallas_call(
        functools.partial(_nsa_kernel, n_sel=n_sel),
        grid=(B, S // TQ),
        in_specs=[
            qspec, qspec,
            pl.BlockSpec((None,) + kcc.shape[1:], lambda b, i: (b, 0, 0, 0)),
            pl.BlockSpec((None,) + vct.shape[1:], lambda b, i: (b, 0, 0, 0)),
            pl.BlockSpec(ovt.shape, lambda b, i: (0, 0)),
            pl.BlockSpec((None, S, 2 * LANES), lambda b, i: (b, 0, 0)),
            pl.BlockSpec((None, NSA_GROUPS * HV, S), lambda b, i: (b, 0, 0)),
            pl.BlockSpec((None, S, 2 * LANES), lambda b, i: (b, 0, 0)),
            pl.BlockSpec((None, NSA_GROUPS * HV, S), lambda b, i: (b, 0, 0)),
            pl.BlockSpec((None, 3 * NSA_HEADS, TQ), lambda b, i: (b, 0, i)),
        ],
        out_specs=qspec,
        out_shape=jax.ShapeDtypeStruct((B, S, W), F32),
        scratch_shapes=[pltpu.VMEM((W, TQ), F32), pltpu.VMEM((NSA_GROUPS, n_blk, TQ), F32),
                        pltpu.VMEM((2, n_blk, TQ), F32), pltpu.VMEM((NSA_HEADS, TQ, LANES), BF16), pltpu.VMEM((NSA_HEADS, TQ, LANES), BF16)]
        + _attn_scratch(NSA_HEADS, HV)[1:] + _attn_scratch(NSA_HEADS, HV),
        compiler_params=pltpu.CompilerParams(
            dimension_semantics=("arbitrary", "arbitrary"), vmem_limit_bytes=VMEM_LIMIT),
        name="nsa_attn",
    )(qc, qcr, kcc, vct, ovt, ks, vst, kw, vwt, bgt)


def _post_kernel(x_ref, oa_ref, ob_ref, oc_ref, p_ref, g_ref, wg_ref, wup_ref, wout_ref,
                 pg_ref, wple_ref, wpg_ref, fg_ref, o_ref, *, final):
    x = x_ref[...]
    h = _rms(x, g_ref[...]).astype(BF16)
    mixed = jnp.zeros(x.shape, F32)
    for n, br_ref in enumerate((oa_ref, ob_ref, oc_ref)):
        gate = _dot(h, wg_ref[:, BRANCH_WIDTH * n:BRANCH_WIDTH * (n + 1)])
        og = br_ref[...] * (gate * _sigmoid(gate))
        up = _dot(og.astype(BF16), wup_ref[n])
        c0 = N_BRANCH * BRANCH_WIDTH + D_MODEL * n
        mixed = mixed + _sigmoid(_dot(h, wg_ref[:, c0:c0 + D_MODEL])) * up
    x1 = x + _dot(mixed.astype(BF16), wout_ref[...])
    emb = _dot(p_ref[...].astype(BF16), wple_ref[...])
    hg = _dot(_rms(x1, pg_ref[...]).astype(BF16), wpg_ref[...])
    x2 = x1 + emb * _sigmoid(hg)
    if final:
        x2 = _rms(x2, fg_ref[...])
    o_ref[...] = x2


def _post_call(x, oa, ob, oc, p, g, wg, wup, wout, pg, wple, wpg, fg, final, layer):
    T, D = x.shape
    tm = TM_POST
    tok = lambda w: pl.BlockSpec((tm, w), lambda i: (i, 0))
    full = lambda a: _layer_spec(a, layer, pipeline_mode=pl.Buffered(1))
    return pl.pallas_call(
        functools.partial(_post_kernel, final=final),
        grid=(T // tm,),
        in_specs=[tok(D), tok(BRANCH_WIDTH), tok(BRANCH_WIDTH), tok(BRANCH_WIDTH),
                  pl.BlockSpec((None, tm, PLE_DIM), lambda i: (layer, i, 0)),
                  full(g), full(wg), full(wup), full(wout), full(pg), full(wple), full(wpg),
                  pl.BlockSpec(fg.shape, lambda i: (0, 0))],
        out_specs=tok(D),
        out_shape=jax.ShapeDtypeStruct((T, D), F32),
        compiler_params=pltpu.CompilerParams(
            dimension_semantics=("arbitrary",), vmem_limit_bytes=VMEM_LIMIT),
        name="post_mix",
    )(x, oa, ob, oc, p, g, wg, wup, wout, pg, wple, wpg, fg)


ROPE_COLS = 3 * (ROT_DIM // 2)


def _rope_inputs(positions):
    half = ROT_DIM // 2
    inv = ROPE_THETA ** (-jnp.arange(0, ROT_DIM, 2, dtype=F32) / ROT_DIM)
    ang = positions.astype(F32)[..., None] * inv
    cs = jnp.concatenate([jnp.cos(ang), jnp.sin(ang), jnp.ones_like(ang)], axis=-1)
    place = np.zeros((ROPE_COLS, 3 * LANES), np.float32)
    for head in range(LANES // HEAD_DIM):
        base = HEAD_DIM * head
        for j in range(half):
            place[j, base + j] = 1.0
            place[j, base + half + j] = 1.0
            place[half + j, LANES + base + j] = -1.0
            place[half + j, 2 * LANES + base + half + j] = 1.0
        place[2 * half, base + ROT_DIM:base + HEAD_DIM] = 1.0
    return cs, jnp.asarray(place, dtype=BF16)


_NAT_PIECES = (
    (0, _A_Q, 512), (512, _A_K, 512), (1024, _B_Q, 512), (1536, _I_Q, 512), (2048, _C_Q, 512),
    (2560, _B_K, 64), (2624, _B_K, 64), (2688, _I_K, 64), (2752, _I_K, 64),
    (2816, _C_KS, 64), (2880, _C_KS, 64), (2944, _C_KS + 64, 64), (3008, _C_KS + 64, 64),
    (3072, _C_KW, 64), (3136, _C_KW, 64), (3200, _C_KW + 64, 64), (3264, _C_KW + 64, 64),
    (3328, _C_KC, 128), (3456, _C_VC, 128))
_GATE_PIECES = ((0, _A_G, 512), (512, _B_G, 512), (1024, _C_G, 512), (1536, _MERGE, 3 * D_MODEL))
_TR_PIECES = ((0, _A_V, 512), (512, _B_V, 64), (576, _C_VS, 128), (704, _C_VW, 128),
              (832, _I_W, 8), (840, _C_BG, 24))


def _relayout_kernel(wt_ref, nat_ref, gates_ref, trt_ref):
    for dst_ref, pieces in ((nat_ref, _NAT_PIECES), (gates_ref, _GATE_PIECES)):
        for d0, s0, n in pieces:
            for r in range(0, n, LANES):
                m = min(LANES, n - r)
                dst_ref[:, d0 + r:d0 + r + m] = wt_ref[s0 + r:s0 + r + m, :].T.astype(BF16)
    groups, cur = [], []
    for d0, s0, n in _TR_PIECES:
        if cur and d0 % ONES_ROWS == 0:
            groups.append(cur)
            cur = []
        cur.append((d0, s0, n))
    groups.append(cur)
    for grp in groups:
        rows = jnp.concatenate([wt_ref[s0:s0 + n, :] for _, s0, n in grp], axis=0)
        trt_ref[grp[0][0]:grp[0][0] + rows.shape[0], :] = rows.astype(BF16)


def _proj_weights(w_in):
    L, D, n_in = w_in.shape
    w_t = jnp.swapaxes(w_in, 1, 2)
    return pl.pallas_call(
        _relayout_kernel,
        grid=(L, D // LANES),
        in_specs=[pl.BlockSpec((None, n_in, LANES), lambda l, r: (l, 0, r))],
        out_specs=(pl.BlockSpec((None, LANES, N_NAT), lambda l, r: (l, r, 0)),
                   pl.BlockSpec((None, LANES, BRANCH_WIDTH * N_BRANCH + N_BRANCH * D_MODEL), lambda l, r: (l, r, 0)),
                   pl.BlockSpec((None, N_TR, LANES), lambda l, r: (l, 0, r))),
        out_shape=(jax.ShapeDtypeStruct((L, D, N_NAT), BF16),
                   jax.ShapeDtypeStruct((L, D, BRANCH_WIDTH * N_BRANCH + N_BRANCH * D_MODEL), BF16),
                   jax.ShapeDtypeStruct((L, N_TR, D), BF16)),
        compiler_params=pltpu.CompilerParams(
            dimension_semantics=("arbitrary", "arbitrary"), vmem_limit_bytes=VMEM_LIMIT),
        name="weight_relayout",
    )(w_t)


def _compress_weights(pe_k, w1_k, w2_k, pe_v, w1_v, w2_v):
    L = pe_k.shape[0]
    half = CMP_LEN // 2
    kdim = half * NSA_GROUPS * HEAD_DIM
    pe = jnp.stack([pe_k, pe_v], axis=1)
    pe = jnp.broadcast_to(pe[:, :, :, None, :], (L, 2, CMP_LEN, NSA_GROUPS, HEAD_DIM))
    w1 = jnp.stack([w1_k, w1_v], axis=1).astype(BF16)
    zero = jnp.zeros_like(w1)
    w1g = jnp.stack([jnp.stack([w1 if h == g else zero for h in range(NSA_GROUPS)], axis=3)
                     for g in range(NSA_GROUPS)], axis=2)
    w1g = w1g.reshape(L, 2 * NSA_GROUPS, CMP_LEN, NSA_GROUPS * HEAD_DIM, CMP_HIDDEN)
    w2 = jnp.stack([w2_k, w2_v], axis=1).astype(BF16)
    return (pe[:, :, :half].reshape(L, 2, 1, kdim), pe[:, :, half:].reshape(L, 2, 1, kdim),
            w1g[:, :, :half].reshape(L, 2 * NSA_GROUPS, kdim, CMP_HIDDEN),
            w1g[:, :, half:].reshape(L, 2 * NSA_GROUPS, kdim, CMP_HIDDEN), w2)


def _overlap_t(S):
    nc = S // CMP_STRIDE
    n_blk = S // SLC_BLOCK
    cstart = np.arange(nc) * CMP_STRIDE
    sstart = np.arange(n_blk) * SLC_BLOCK
    ov = ((cstart[None, :] < sstart[:, None] + SLC_BLOCK) & (cstart[None, :] + CMP_LEN > sstart[:, None]))
    return jnp.asarray(ov.astype(np.float32), dtype=BF16)


def kernel(x, p, positions, norm_g, w_in, diff_lambda, diff_subln_g, cmp_pe_k, cmp_w1_k, cmp_w2_k,
           cmp_pe_v, cmp_w1_v, cmp_w2_v, w_up, w_out, ple_norm_g, w_ple, w_ple_gate, final_norm_g):
    B, S, D = x.shape
    depth = w_in.shape[0]
    k_sel = min(DSA_TOPK_MAX, S // 4)
    n_sel = min(SLC_TOPN_MAX, S // SLC_BLOCK)
    assert S % TM_PROJ == 0 and S % TQ == 0 and k_sel <= TQ and D == D_MODEL
    assert KC == TQ and WINDOW % KC == 0
    cs, place = _rope_inputs(positions)
    ovt = _overlap_t(S)

    wn, wg, wt = _proj_weights(w_in)
    cmp_w = _compress_weights(cmp_pe_k, cmp_w1_k, cmp_w2_k, cmp_pe_v, cmp_w1_v, cmp_w2_v)
    g_in = norm_g.reshape(depth, 1, D)
    g_ple = ple_norm_g.reshape(depth, 1, D)
    g_sub = diff_subln_g.reshape(depth, -1, 1)
    wup, wout, wple, wpg = (w.astype(BF16) for w in (w_up, w_out, w_ple, w_ple_gate))
    p2 = p.reshape(depth, B * S, PLE_DIM)

    for i in range(depth):
        lam_init = 0.8 - 0.6 * math.exp(-0.3 * i)
        (qa, ka, qb, iq, qc, qcr, kb, ik, ks, kw, ck, cv,
         vat, vbt, vst, vwt, iwt, bgt) = _proj_call(x, g_in, wn, wt, cs, place, i)

        kcc, vct = _compress_call(ck, cv, *cmp_w, i)

        oa = _diff_call(qa, ka, vat, diff_lambda, g_sub, lam_init, i)
        ob = _dsa_call(qb, iq, kb, ik, vbt, iwt, k_sel)
        oc = _nsa_call(qc, qcr, kcc, vct, ovt, ks, vst, kw, vwt, bgt, n_sel)

        x = _post_call(
            x.reshape(B * S, D), oa.reshape(B * S, -1), ob.reshape(B * S, -1), oc.reshape(B * S, -1),
            p2, g_in, wg, wup, wout, g_ple, wple, wpg, final_norm_g.reshape(1, D),
            final=(i == depth - 1), layer=i).reshape(B, S, D)
    return x
```

```python
import functools
import math

import numpy as np
import jax
import jax.numpy as jnp
from jax import lax
from jax.experimental import pallas as pl
from jax.experimental.pallas import tpu as pltpu

F32 = jnp.float32
BF16 = jnp.bfloat16

D_MODEL = 1024
HEAD_DIM = 64
ROT_DIM = 16
ROPE_THETA = 500000.0
EPS = 1e-6
PLE_DIM = 256
DA_HEADS = 4
DSA_HEADS = 8
IDX_HEADS = 8
DSA_TOPK_MAX = 256
NSA_GROUPS = 2
NSA_HEADS = 8
CMP_LEN = 32
CMP_STRIDE = 16
CMP_HIDDEN = 128
SLC_BLOCK = 64
SLC_TOPN_MAX = 16
WINDOW = 512
FORCE_SCORE = 1e4
BRANCH_WIDTH = 512
N_BRANCH = 3

IN_SPLITS = (512, 512, 512, 512, 512, 64, 64, 512, 512, 64, 8,
             512, 128, 128, 128, 128, 128, 128, 512, 24, 3 * D_MODEL)
(_A_Q, _A_K, _A_V, _A_G, _B_Q, _B_K, _B_V, _B_G, _I_Q, _I_K, _I_W,
 _C_Q, _C_KC, _C_VC, _C_KS, _C_VS, _C_KW, _C_VW, _C_G, _C_BG, _MERGE) = [
    int(v) for v in np.concatenate([[0], np.cumsum(IN_SPLITS)[:-1]])]

LANES = 128
SUBLANES = 8
TQ = 512
KC = 512
TM_PROJ = 512
TM_POST = 512
NEG_INF = float("-inf")
M_INIT = -1e30
N_NAT = 3584
N_TR = 864
BISECT_PLAIN = 9
ONES_ROWS = 16
HV = HEAD_DIM + ONES_ROWS
HV_A = 2 * HEAD_DIM + ONES_ROWS
LOG2E = math.log2(math.e)
VMEM_LIMIT = 56 * 1024 * 1024


def _dot_nt(a, b):
    return lax.dot_general(a, b, (((1,), (1,)), ((), ())), preferred_element_type=F32)


def _dot(a, b):
    return jnp.dot(a, b, preferred_element_type=F32)


def _rms(x, g):
    return x * lax.rsqrt(jnp.mean(x * x, axis=-1, keepdims=True) + EPS) * g


def _sigmoid(x):
    return 1.0 / (1.0 + jnp.exp(-x))


def _store_half_masked(src_ref, dst_ref, n_pairs):
    lane = lax.broadcasted_iota(jnp.int32, (src_ref.shape[0], LANES), 1)
    for hp in range(n_pairs):
        x = src_ref[:, LANES * hp:LANES * (hp + 1)]
        z = jnp.zeros_like(x)
        dst_ref[2 * hp] = jnp.where(lane < HEAD_DIM, x, z)
        dst_ref[2 * hp + 1] = jnp.where(lane >= HEAD_DIM, x, z)


def _fold_rows(x, op):
    n_acc = 2
    accs = [x[SUBLANES * r:SUBLANES * (r + 1)] for r in range(n_acc)]
    for r in range(n_acc, x.shape[0] // SUBLANES):
        accs[r % n_acc] = op(accs[r % n_acc], x[SUBLANES * r:SUBLANES * (r + 1)])
    return op(accs[0], accs[1])


HALF = KC // 2


def _scores(k, q, dead=None):
    if dead is None:
        return _dot_nt(k, q)
    neg = jnp.full((HALF, HALF), NEG_INF, F32)
    if dead == "bl":
        return jnp.concatenate(
            [_dot_nt(k[:HALF], q), jnp.concatenate([neg, _dot_nt(k[HALF:], q[HALF:])], axis=1)], axis=0)
    return jnp.concatenate(
        [jnp.concatenate([_dot_nt(k[:HALF], q[:HALF]), neg], axis=1), _dot_nt(k[HALF:], q)], axis=0)


def _weighted_values(v_t, p, dead=None):
    if dead is None:
        return _dot(v_t, p)
    if dead == "bl":
        return jnp.concatenate([_dot(v_t[:, :HALF], p[:HALF, :HALF]), _dot(v_t, p[:, HALF:])], axis=1)
    return jnp.concatenate([_dot(v_t, p[:, :HALF]), _dot(v_t[:, HALF:], p[HALF:, HALF:])], axis=1)


class _Stage:
    def __init__(self, j0, j1, nh, hv, prep, score, values, m_ref, al_ref, acc_ref,
                 dead_first=None, dead_last=None, inner_score=None):
        self.j0 = jnp.asarray(j0, jnp.int32)
        self.j1 = jnp.asarray(j1, jnp.int32)
        self.nh, self.hv, self.prep, self.score, self.values = nh, hv, prep, score, values
        self.m_ref, self.al_ref, self.acc_ref = m_ref, al_ref, acc_ref
        self.dead_first, self.dead_last = dead_first, dead_last
        self.inner_score = inner_score or score

    def reset(self):
        self.m_ref[...] = jnp.full(self.m_ref.shape, M_INIT, F32)
        self.acc_ref[...] = jnp.zeros(self.acc_ref.shape, F32)

    def produce(self, sbuf_ref, j, dead=None, inner=False):
        ctx = self.prep(j)
        for h in range(self.nh):
            s = (self.inner_score if inner else self.score)(ctx, h, dead)
            m_old = self.m_ref[h:h + 1, :]
            m_new = jnp.maximum(m_old, jnp.max(s, axis=0, keepdims=True))
            sbuf_ref[h] = s
            self.m_ref[h:h + 1, :] = m_new
            self.al_ref[h:h + 1, :] = jnp.exp2(m_old - m_new)

    def consume(self, sbuf_ref, j, dead=None):
        for h in range(self.nh):
            p = jnp.exp2(sbuf_ref[h] - self.m_ref[h:h + 1, :])
            rows = slice(self.hv * h, self.hv * (h + 1))
            self.acc_ref[rows, :] = (self.al_ref[h:h + 1, :] * self.acc_ref[rows, :]
                                     + _weighted_values(self.values(j, h), p.astype(BF16), dead))

    def loop(self, sbuf_ref, lo, hi):
        def body(j, carry):
            self.consume(sbuf_ref, j - 1)
            self.produce(sbuf_ref, j, inner=True)
            return carry
        lax.fori_loop(lo, hi, body, 0)


def _attn_chain(stages, sbuf_ref, tile):
    for st in stages:
        st.reset()

    @pl.when(tile == 0)
    def _():
        for st in stages:
            st.produce(sbuf_ref, st.j0)
            st.consume(sbuf_ref, st.j0)

    @pl.when(tile > 0)
    def _():
        stages[0].produce(sbuf_ref, stages[0].j0, stages[0].dead_first)
        for n, st in enumerate(stages):
            st.loop(sbuf_ref, st.j0 + 1, st.j1)
            st.consume(sbuf_ref, st.j1 - 1, st.dead_first)
            st.produce(sbuf_ref, st.j1, st.dead_last)
            st.consume(sbuf_ref, st.j1, st.dead_last)
            if n + 1 < len(stages):
                stages[n + 1].produce(sbuf_ref, stages[n + 1].j0, stages[n + 1].dead_first)


def _attn_scratch(nh, hv):
    return [pltpu.VMEM((nh, KC, TQ), F32), pltpu.VMEM((nh, TQ), F32),
            pltpu.VMEM((nh, TQ), F32), pltpu.VMEM((nh * hv, TQ), F32)]


def _layer_spec(a, layer, **kw):
    tail = (0,) * (a.ndim - 1)
    return pl.BlockSpec((None,) + a.shape[1:], lambda *_: (layer,) + tail, **kw)


def _ones_rows(n_cols):
    r = lax.broadcasted_iota(jnp.int32, (ONES_ROWS, n_cols), 0)
    return jnp.where(r == 0, 1.0, 0.0).astype(BF16)


def _proj_kernel(x_ref, g_ref, wn_ref, wt_ref, cs_ref, place_ref,
                 qa_ref, ka_ref, qb_ref, iq_ref, qc_ref, qcr_ref, kb_ref, ik_ref,
                 ks_ref, kw_ref, ck_ref, cv_ref,
                 vat_ref, vbt_ref, vst_ref, vwt_ref, iwt_ref, bgt_ref, zc_ref):
    h = _rms(x_ref[...], g_ref[...]).astype(BF16)
    cs = cs_ref[...]
    hi = cs.astype(BF16)
    rest = cs - hi.astype(F32)
    mid = rest.astype(BF16)
    low = (rest - mid.astype(F32)).astype(BF16)
    spread = _dot(hi, place_ref[...]) + _dot(mid, place_ref[...]) + _dot(low, place_ref[...])
    rc, rs1, rs2 = spread[:, 0:LANES], spread[:, LANES:2 * LANES], spread[:, 2 * LANES:3 * LANES]

    def rope(z):
        return z * rc + pltpu.roll(z, LANES - ROT_DIM // 2, 1) * rs1 + pltpu.roll(z, ROT_DIM // 2, 1) * rs2

    qscale = HEAD_DIM ** -0.5
    qscale2 = qscale * LOG2E
    segs = (
        (0, 512, ((qa_ref, True, qscale2),)),
        (512, 512, ((ka_ref, True, 1.0),)),
        (1024, 512, ((qb_ref, True, qscale2),)),
        (1536, 512, ((iq_ref, True, qscale),)),
        (2048, 512, ((qc_ref, False, qscale), (qcr_ref, True, qscale2))),
        (2560, 128, ((kb_ref, True, 1.0),)),
        (2688, 128, ((ik_ref, True, 1.0),)),
        (2816, 256, ((ks_ref, True, 1.0),)),
        (3072, 256, ((kw_ref, True, 1.0),)),
    )
    zc = _dot(h, wn_ref[:, 3328:3328 + 2 * LANES])
    half_len = CMP_LEN // 2
    for n, c_ref in enumerate((ck_ref, cv_ref)):
        zc_ref[n] = zc[:, LANES * n:LANES * (n + 1)]
        for l in range(half_len):
            c_ref[:, LANES * l:LANES * (l + 1)] = zc_ref[n, pl.ds(l, zc.shape[0] // half_len, stride=half_len), :]
    for c0, width, outs in segs:
        z = _dot(h, wn_ref[:, c0:c0 + width])
        for out_ref, rot, scale in outs:
            for j in range(width // LANES):
                zj = z[:, LANES * j:LANES * (j + 1)]
                if rot:
                    zj = rope(zj)
                if scale != 1.0:
                    zj = zj * scale
                out_ref[:, LANES * j:LANES * (j + 1)] = zj.astype(out_ref.dtype)

    zt = _dot_nt(wt_ref[...], h)
    ones = _ones_rows(zt.shape[1])
    vd = 2 * HEAD_DIM
    for h in range(DA_HEADS):
        vat_ref[HV_A * h:HV_A * h + vd, :] = zt[vd * h:vd * (h + 1)].astype(BF16)
        vat_ref[HV_A * h + vd:HV_A * (h + 1), :] = ones
    vbt_ref[0:HEAD_DIM, :] = zt[512:576].astype(BF16)
    vbt_ref[HEAD_DIM:HV, :] = ones
    for g in range(NSA_GROUPS):
        for ref, base in ((vst_ref, 576), (vwt_ref, 704)):
            ref[HV * g:HV * g + HEAD_DIM, :] = zt[base + HEAD_DIM * g:base + HEAD_DIM * (g + 1)].astype(BF16)
            ref[HV * g + HEAD_DIM:HV * (g + 1), :] = ones
    iwt_ref[...] = zt[832:840] * (IDX_HEADS ** -0.5)
    bgt_ref[...] = zt[840:864]


def _proj_call(x, g, wn, wt, cs, place, layer):
    B, S, D = x.shape
    tm = TM_PROJ
    nat = lambda w: pl.BlockSpec((None, tm, w), lambda b, i: (b, i, 0))
    tr = lambda r: pl.BlockSpec((None, r, tm), lambda b, i: (b, 0, i))
    full = lambda a: _layer_spec(a, layer)
    sds = jax.ShapeDtypeStruct
    half_len = CMP_LEN // 2
    out_shape = (
        sds((B, S, 512), BF16), sds((B, S, 512), BF16), sds((B, S, 512), BF16), sds((B, S, 512), BF16),
        sds((B, S, 512), BF16), sds((B, S, 512), BF16), sds((B, S, 128), BF16), sds((B, S, 128), BF16),
        sds((B, S, 256), BF16), sds((B, S, 256), BF16),
        sds((B, S // half_len, half_len * LANES), F32), sds((B, S // half_len, half_len * LANES), F32),
        sds((B, DA_HEADS * HV_A, S), BF16), sds((B, HV, S), BF16),
        sds((B, NSA_GROUPS * HV, S), BF16), sds((B, NSA_GROUPS * HV, S), BF16),
        sds((B, IDX_HEADS, S), F32), sds((B, 3 * NSA_HEADS, S), F32),
    )
    cmp_rows = pl.BlockSpec((None, tm // half_len, half_len * LANES), lambda b, i: (b, i, 0))
    out_specs = (nat(512), nat(512), nat(512), nat(512), nat(512), nat(512), nat(128), nat(128),
                 nat(256), nat(256), cmp_rows, cmp_rows,
                 tr(DA_HEADS * HV_A), tr(HV), tr(NSA_GROUPS * HV), tr(NSA_GROUPS * HV),
                 tr(IDX_HEADS), tr(3 * NSA_HEADS))
    return pl.pallas_call(
        _proj_kernel,
        grid=(B, S // tm),
        in_specs=[nat(D), full(g), full(wn), full(wt), nat(ROPE_COLS),
                  pl.BlockSpec(place.shape, lambda b, i: (0, 0))],
        out_specs=out_specs,
        out_shape=out_shape,
        scratch_shapes=[pltpu.VMEM((2, tm, LANES), F32)],
        compiler_params=pltpu.CompilerParams(
            dimension_semantics=("arbitrary", "arbitrary"), vmem_limit_bytes=VMEM_LIMIT),
        name="in_proj",
    )(x, g, wn, wt, cs, place)


def _compress_kernel(rk_ref, rv_ref, pea_ref, peb_ref, w1a_ref, w1b_ref, w2_ref, nat_ref, t_ref):
    for kind, r_ref in enumerate((rk_ref, rv_ref)):
        r = r_ref[...]
        xa = (r + pea_ref[kind]).astype(BF16)
        xb = (r + peb_ref[kind]).astype(BF16)
        nrow = r.shape[0]
        for g in range(NSA_GROUPS):
            n = NSA_GROUPS * kind + g
            hid = jax.nn.gelu(_dot(xa, w1a_ref[n]) + pltpu.roll(_dot(xb, w1b_ref[n]), nrow - 1, 0))
            o = _dot(hid.astype(BF16), w2_ref[kind])
            nat_ref[n] = jnp.concatenate([o, o], axis=1).astype(BF16)
            t_ref[n] = o.T.astype(BF16)


def _compress_call(rk, rv, pea, peb, w1a, w1b, w2, layer):
    B, nr, kdim = rk.shape
    full = lambda a: _layer_spec(a, layer)
    rspec = pl.BlockSpec((None, nr, kdim), lambda b: (b, 0, 0))
    return pl.pallas_call(
        _compress_kernel,
        grid=(B,),
        in_specs=[rspec, rspec, full(pea), full(peb), full(w1a), full(w1b), full(w2)],
        out_specs=(
            pl.BlockSpec((None, 2 * NSA_GROUPS, nr, 2 * HEAD_DIM), lambda b: (b, 0, 0, 0)),
            pl.BlockSpec((None, 2 * NSA_GROUPS, HEAD_DIM, nr), lambda b: (b, 0, 0, 0)),
        ),
        out_shape=(jax.ShapeDtypeStruct((B, 2 * NSA_GROUPS, nr, 2 * HEAD_DIM), BF16),
                   jax.ShapeDtypeStruct((B, 2 * NSA_GROUPS, HEAD_DIM, nr), BF16)),
        compiler_params=pltpu.CompilerParams(dimension_semantics=("arbitrary",)),
        name="nsa_compress",
    )(rk, rv, pea, peb, w1a, w1b, w2)


def _diff_kernel(q_ref, k_ref, vt_ref, lam_ref, sg_ref, o_ref,
                 qz_ref, sbuf_ref, m_ref, al_ref, acc_ref, *, lam_init):
    i = pl.program_id(1)
    lp = lam_ref[...]
    lam = (jnp.exp(jnp.sum(lp[0:1] * lp[1:2], axis=1, keepdims=True))
           - jnp.exp(jnp.sum(lp[2:3] * lp[3:4], axis=1, keepdims=True)) + lam_init)
    row_minus_col = (lax.broadcasted_iota(jnp.int32, (KC, TQ), 0)
                     - lax.broadcasted_iota(jnp.int32, (KC, TQ), 1))
    _store_half_masked(q_ref, qz_ref, DA_HEADS)
    vdim = 2 * HEAD_DIM
    n_maps = 2 * DA_HEADS

    def prep(j):
        return pl.multiple_of(j * KC, KC), jnp.where(row_minus_col <= (i - j) * KC, 0.0, NEG_INF)

    def visible_score(ctx, hm, dead):
        h = hm // 2
        return _scores(k_ref[pl.ds(ctx[0], KC), vdim * h:vdim * (h + 1)], qz_ref[hm], dead)

    def score(ctx, hm, dead):
        return visible_score(ctx, hm, dead) + ctx[1]

    def values(j, hm):
        h = hm // 2
        return vt_ref[HV_A * h:HV_A * (h + 1), pl.ds(pl.multiple_of(j * KC, KC), KC)]

    _attn_chain([_Stage(0, i, n_maps, HV_A, prep, score, values, m_ref, al_ref, acc_ref, dead_last="bl",
                        inner_score=visible_score)], sbuf_ref, i)

    for h in range(DA_HEADS):
        r0, r1 = HV_A * 2 * h, HV_A * (2 * h + 1)
        o_t = (acc_ref[r0:r0 + vdim, :] / acc_ref[r0 + vdim:r0 + vdim + 1, :]
               - lam * (acc_ref[r1:r1 + vdim, :] / acc_ref[r1 + vdim:r1 + vdim + 1, :]))
        ms = jnp.mean(o_t * o_t, axis=0, keepdims=True)
        y = o_t * lax.rsqrt(ms + EPS) * sg_ref[...] * (1.0 - lam_init)
        o_ref[:, vdim * h:vdim * (h + 1)] = y.T


def _diff_call(qa, ka, vat, lam_p, sg, lam_init, layer):
    B, S, W = qa.shape
    n_maps = 2 * DA_HEADS
    return pl.pallas_call(
        functools.partial(_diff_kernel, lam_init=lam_init),
        grid=(B, S // TQ),
        in_specs=[
            pl.BlockSpec((None, TQ, W), lambda b, i: (b, i, 0)),
            pl.BlockSpec((None, S, W), lambda b, i: (b, 0, 0)),
            pl.BlockSpec((None, DA_HEADS * HV_A, S), lambda b, i: (b, 0, 0)),
            _layer_spec(lam_p, layer),
            _layer_spec(sg, layer),
        ],
        out_specs=pl.BlockSpec((None, TQ, W), lambda b, i: (b, i, 0)),
        out_shape=jax.ShapeDtypeStruct((B, S, W), F32),
        scratch_shapes=[pltpu.VMEM((n_maps, TQ, LANES), BF16)] + _attn_scratch(n_maps, HV_A),
        compiler_params=pltpu.CompilerParams(
            dimension_semantics=("arbitrary", "arbitrary"), vmem_limit_bytes=VMEM_LIMIT),
        name="diff_attn",
    )(qa, ka, vat, lam_p, sg)


def _dsa_kernel(q_ref, iq_ref, kb_ref, ik_ref, vt_ref, iw_ref, o_ref,
                sc_ref, tj_ref, tri_ref, qz_ref, sbuf_ref, m_ref, al_ref, acc_ref, *, k_sel, seq_len):
    i = pl.program_id(1)
    nch = i + 1
    row = lax.broadcasted_iota(jnp.int32, (KC, TQ), 0)
    col = lax.broadcasted_iota(jnp.int32, (KC, TQ), 1)
    causal = row <= col
    iw = iw_ref[...]
    kf = float(k_sel)
    _store_half_masked(iq_ref, qz_ref, IDX_HEADS // 2)

    def idx_chunk(j, carry, masked):
        mx, mn = carry
        off = pl.multiple_of(j * KC, KC)
        ikc = ik_ref[pl.ds(off, KC), :]
        sc = jnp.zeros((KC, TQ), F32)
        for h in range(IDX_HEADS):
            dots = _scores(ikc, qz_ref[h], "bl" if masked else None)
            sc = sc + iw[h:h + 1, :] * jnp.maximum(dots, 0.0)
        lo_src = sc
        if masked:
            lo_src = jnp.where(causal, sc, jnp.inf)
            sc = jnp.where(causal, sc, NEG_INF)
        sc_ref[pl.ds(off, KC), :] = sc
        return (jnp.maximum(mx, _fold_rows(sc, jnp.maximum)),
                jnp.minimum(mn, _fold_rows(lo_src, jnp.minimum)))

    carry = (jnp.full((SUBLANES, TQ), NEG_INF, F32), jnp.full((SUBLANES, TQ), jnp.inf, F32))
    carry = lax.fori_loop(0, i, functools.partial(idx_chunk, masked=False), carry)
    mx8, mn8 = idx_chunk(i, carry, True)
    row_max = jnp.max(mx8, axis=0, keepdims=True)
    row_min = jnp.min(mn8, axis=0, keepdims=True)

    def probe_pass(th, snap):
        thb = jnp.broadcast_to(th, (SUBLANES, TQ))

        def body(j, c):
            off = pl.multiple_of(j * KC, KC)
            c = list(c)
            for r in range(KC // SUBLANES):
                x = sc_ref[pl.ds(off + SUBLANES * r, SUBLANES), :]
                ge = x >= thb
                c[0] = c[0] + jnp.where(ge, 1.0, 0.0)
                if snap:
                    c[1] = jnp.maximum(c[1], jnp.where(ge, NEG_INF, x))
            return tuple(c)
        init = (jnp.zeros((SUBLANES, TQ), F32),)
        if snap:
            init += (jnp.full((SUBLANES, TQ), NEG_INF, F32),)
        out = lax.fori_loop(0, nch, body, init)
        cnt = jnp.sum(out[0], axis=0, keepdims=True)
        return (cnt, jnp.max(out[1], axis=0, keepdims=True)) if snap else cnt

    n_causal = (i * TQ + 1 + lax.broadcasted_iota(jnp.int32, (1, TQ), 1)).astype(F32)
    keep_all = n_causal <= kf

    def bisect(_, c):
        lo, hi, clo, chi = c
        mid = lo + (hi - lo) * 0.5
        cnt = probe_pass(mid, False)
        ge = cnt >= kf
        return (jnp.where(ge, mid, lo), jnp.where(ge, hi, mid),
                jnp.where(ge, cnt, clo), jnp.where(ge, chi, cnt))

    hi0 = row_max + jnp.abs(row_max) * 1e-6 + 1e-30
    lo, hi, clo, chi = lax.fori_loop(
        0, BISECT_PLAIN, bisect, (row_min, hi0, n_causal, jnp.zeros((1, TQ), F32)))

    def cond(st):
        return jnp.logical_and(st[0] < 256, st[-1] > 0.0)

    def snap_step(st):
        it, lo, hi, clo, chi, hie, known, stuck, done, _ = st
        top = jnp.where(known > 0.0, hie, hi)
        mid = lo + (top - lo) * 0.5
        inside = jnp.logical_and(mid > lo, mid < top)
        near_top = jnp.logical_or(kf - chi <= 2.0, stuck > 0.0)
        use_top = jnp.logical_or(jnp.logical_and(known > 0.0, near_top), jnp.logical_not(inside))
        th = jnp.where(use_top, top, mid)
        cnt, edn = probe_pass(th, True)
        live = done <= 0.0
        ge = jnp.logical_and(live, cnt >= kf)
        lt = jnp.logical_and(live, cnt < kf)
        stuck = jnp.where(jnp.logical_and(ge, cnt == clo), 1.0, 0.0)
        lo = jnp.where(ge, th, lo)
        clo = jnp.where(ge, cnt, clo)
        hi = jnp.where(lt, th, hi)
        chi = jnp.where(lt, cnt, chi)
        hie = jnp.where(lt, edn, hie)
        known = jnp.where(lt, 1.0, known)
        fin = jnp.logical_or(clo == kf, jnp.logical_and(known > 0.0, lo >= hie))
        done = jnp.where(fin, 1.0, done)
        return it + 1, lo, hi, clo, chi, hie, known, stuck, done, jnp.max(1.0 - done)

    zero = jnp.zeros((1, TQ), F32)
    done0 = jnp.where(jnp.logical_or(clo == kf, keep_all), 1.0, 0.0)
    st = lax.while_loop(cond, snap_step, (jnp.int32(0), lo, hi, clo, chi, jnp.full((1, TQ), NEG_INF, F32),
                                          zero, zero, done0, jnp.max(1.0 - done0)))
    thr, clo, chi = st[1], st[3], st[4]
    need = jnp.where(clo == kf, float(seq_len), kf - chi)
    thr = jnp.where(keep_all, NEG_INF, thr)
    need = jnp.where(keep_all, 0.0, need)
    tj_ref[2:3, :] = jnp.zeros((1, TQ), F32)

    _store_half_masked(q_ref, qz_ref, DSA_HEADS // 2)
    half = KC // 2
    tri_ref[...] = jnp.where(lax.broadcasted_iota(jnp.int32, (half, half), 0)
                             >= lax.broadcasted_iota(jnp.int32, (half, half), 1), 1.0, 0.0).astype(BF16)

    def prep(j):
        off = pl.multiple_of(j * KC, KC)
        x = sc_ref[pl.ds(off, KC), :]
        eq = x == thr
        e = jnp.where(eq, 1.0, 0.0).astype(BF16)
        r_top = _dot(tri_ref[...], e[:half]) + tj_ref[2:3, :]
        r_bot = _dot(tri_ref[...], e[half:]) + r_top[half - 1:half, :]
        tj_ref[2:3, :] = r_bot[half - 1:half, :]
        rank = jnp.concatenate([r_top, r_bot], axis=0)
        tie_bias = jnp.where(jnp.logical_and(eq, rank <= need), 0.0, NEG_INF)
        return kb_ref[pl.ds(off, KC), :], jnp.where(x > thr, 0.0, tie_bias)

    def score(ctx, h, dead):
        kc, bias = ctx
        return _scores(kc, qz_ref[h], dead) + bias

    def values(j, h):
        return vt_ref[:, pl.ds(pl.multiple_of(j * KC, KC), KC)]

    _attn_chain([_Stage(0, i, DSA_HEADS, HV, prep, score, values, m_ref, al_ref, acc_ref, dead_last="bl")],
                sbuf_ref, i)
    for hp in range(DSA_HEADS // 2):
        blk = jnp.concatenate(
            [acc_ref[HV * h:HV * h + HEAD_DIM, :] / acc_ref[HV * h + HEAD_DIM:HV * h + HEAD_DIM + 1, :]
             for h in (2 * hp, 2 * hp + 1)], axis=0)
        o_ref[:, LANES * hp:LANES * (hp + 1)] = blk.T


def _dsa_call(qb, iq, kb, ik, vbt, iwt, k_sel):
    B, S, W = qb.shape
    return pl.pallas_call(
        functools.partial(_dsa_kernel, k_sel=k_sel, seq_len=S),
        grid=(B, S // TQ),
        in_specs=[
            pl.BlockSpec((None, TQ, W), lambda b, i: (b, i, 0)),
            pl.BlockSpec((None, TQ, W), lambda b, i: (b, i, 0)),
            pl.BlockSpec((None, S, LANES), lambda b, i: (b, 0, 0)),
            pl.BlockSpec((None, S, LANES), lambda b, i: (b, 0, 0)),
            pl.BlockSpec((None, HV, S), lambda b, i: (b, 0, 0)),
            pl.BlockSpec((None, IDX_HEADS, TQ), lambda b, i: (b, 0, i)),
        ],
        out_specs=pl.BlockSpec((None, TQ, W), lambda b, i: (b, i, 0)),
        out_shape=jax.ShapeDtypeStruct((B, S, W), F32),
        scratch_shapes=[pltpu.VMEM((S, TQ), F32), pltpu.VMEM((SUBLANES, TQ), F32),
                        pltpu.VMEM((KC // 2, KC // 2), BF16),
                        pltpu.VMEM((DSA_HEADS, TQ, LANES), BF16)] + _attn_scratch(DSA_HEADS, HV),
        compiler_params=pltpu.CompilerParams(
            dimension_semantics=("arbitrary", "arbitrary"), vmem_limit_bytes=VMEM_LIMIT),
        name="dsa_attn",
    )(qb, iq, kb, ik, vbt, iwt)


def _rank_tile(rk_ref, m, lanes):
    rows = lambda t: slice(SUBLANES * t, SUBLANES * (t + 1))
    own = rk_ref[0, rows(m), lanes]
    sub = lax.broadcasted_iota(jnp.int32, own.shape, 0)
    own_rows = [jnp.broadcast_to(own[r:r + 1], own.shape) for r in range(SUBLANES)]
    rank = jnp.zeros(own.shape, F32)
    for r, x in enumerate(own_rows):
        ahead = jnp.logical_or(x > own, jnp.logical_and(x == own, sub > r))
        rank = rank + jnp.where(ahead, 1.0, 0.0)
    for t in range(m):
        older = rk_ref[0, rows(t), lanes]
        rank_t = rk_ref[1, rows(t), lanes]
        for r in range(SUBLANES):
            rank = rank + jnp.where(jnp.broadcast_to(older[r:r + 1], own.shape) >= own, 1.0, 0.0)
            rank_t = rank_t + jnp.where(own_rows[r] > older, 1.0, 0.0)
        rk_ref[1, rows(t), lanes] = rank_t
    rk_ref[1, rows(m), lanes] = rank


def _nsa_kernel(q_ref, qr_ref, kc_ref, vct_ref, ovt_ref, ks_ref, vst_ref, kw_ref, vwt_ref, bg_ref,
                o_ref, cmp_ref, sel_ref, rk_ref, qz_ref, qrz_ref, mw_ref, alw_ref, accw_ref,
                sbuf_ref, m_ref, al_ref, acc_ref, *, n_sel):
    i = pl.program_id(1)
    row_minus_col = (lax.broadcasted_iota(jnp.int32, (KC, TQ), 0)
                     - lax.broadcasted_iota(jnp.int32, (KC, TQ), 1))
    tq = i * TQ + lax.broadcasted_iota(jnp.int32, (1, TQ), 1)
    hpg = NSA_HEADS // NSA_GROUPS
    nc = kc_ref.shape[1]
    n_blk = sel_ref.shape[1]
    gw = 2 * HEAD_DIM

    _store_half_masked(q_ref, qz_ref, NSA_HEADS // 2)
    _store_half_masked(qr_ref, qrz_ref, NSA_HEADS // 2)

    def compressed_and_select():
        cmp_valid = CMP_STRIDE * lax.broadcasted_iota(jnp.int32, (nc, TQ), 0) + (CMP_LEN - 1) <= tq
        blk = lax.broadcasted_iota(jnp.int32, (n_blk, TQ), 0)
        forced = jnp.logical_or(blk == jnp.right_shift(tq, int(math.log2(SLC_BLOCK))), blk == 0)
        blk_causal = blk * SLC_BLOCK <= tq
        for h in range(NSA_HEADS):
            s = jnp.where(cmp_valid, _dot_nt(kc_ref[h // hpg], qz_ref[h]), NEG_INF)
            sbuf_ref[h, 0:nc, :] = s
            m_ref[h:h + 1, :] = jnp.maximum(jnp.max(s, axis=0, keepdims=True), M_INIT)
        for g in range(NSA_GROUPS):
            vcg = vct_ref[NSA_GROUPS + g]
            imp = jnp.zeros((n_blk, TQ), F32)
            for hh in range(hpg):
                h = hpg * g + hh
                p = jnp.exp(sbuf_ref[h, 0:nc, :] - m_ref[h:h + 1, :])
                l = jnp.sum(p, axis=0, keepdims=True)
                pc = (p * jnp.where(l > 0.0, 1.0 / l, 0.0)).astype(BF16)
                cmp_ref[HEAD_DIM * h:HEAD_DIM * (h + 1), :] = _dot(vcg, pc)
                imp = imp + _dot(ovt_ref[...], pc)
            imp = jnp.where(forced, FORCE_SCORE, imp)
            imp = jnp.where(blk_causal, imp, NEG_INF)
            rk_ref[0] = imp
            rk_ref[1] = jnp.zeros((n_blk, TQ), F32)
            for m in range(n_blk // SUBLANES):
                @pl.when(m * SUBLANES * SLC_BLOCK < (i + 1) * TQ)
                def _(m=m):
                    for l in range(TQ // LANES):
                        _rank_tile(rk_ref, m, slice(LANES * l, LANES * (l + 1)))
            sel_ref[g] = jnp.where(rk_ref[1] < float(n_sel), 0.0, NEG_INF)

    bpc = KC // SLC_BLOCK

    def slc_prep(j):
        off = pl.multiple_of(j * KC, KC)
        visible = row_minus_col <= (i - j) * KC
        biases = []
        for g in range(NSA_GROUPS):
            bias = jnp.concatenate(
                [jnp.broadcast_to(sel_ref[g, pl.ds(bpc * j + b, 1), :], (SLC_BLOCK, TQ))
                 for b in range(bpc)], axis=0)
            biases.append(jnp.where(visible, bias, NEG_INF))
        return off, biases

    def slc_score(ctx, h, dead):
        off, biases = ctx
        g = h // hpg
        return _scores(ks_ref[pl.ds(off, KC), gw * g:gw * (g + 1)], qrz_ref[h], dead) + biases[g]

    def slc_values(j, h):
        g = h // hpg
        return vst_ref[HV * g:HV * (g + 1), pl.ds(pl.multiple_of(j * KC, KC), KC)]

    def win_prep(j):
        off = pl.multiple_of(j * KC, KC)
        d = (i - j) * KC
        inside = jnp.logical_and(row_minus_col <= d, row_minus_col > d - WINDOW)
        return off, jnp.where(inside, 0.0, NEG_INF)

    def win_score(ctx, h, dead):
        off, bias = ctx
        g = h // hpg
        return _scores(kw_ref[pl.ds(off, KC), gw * g:gw * (g + 1)], qrz_ref[h], dead) + bias

    def win_values(j, h):
        g = h // hpg
        return vwt_ref[HV * g:HV * (g + 1), pl.ds(pl.multiple_of(j * KC, KC), KC)]

    compressed_and_select()
    assert WINDOW == KC
    slc = _Stage(0, i, NSA_HEADS, HV, slc_prep, slc_score, slc_values, m_ref, al_ref, acc_ref,
                 dead_last="bl")
    win = _Stage(jnp.maximum(i - 1, 0), i, NSA_HEADS, HV, win_prep, win_score, win_values,
                 mw_ref, alw_ref, accw_ref, dead_first="tr", dead_last="bl")
    _attn_chain([slc, win], sbuf_ref, i)

    def head_out(ref, h):
        return ref[HV * h:HV * h + HEAD_DIM, :] / ref[HV * h + HEAD_DIM:HV * h + HEAD_DIM + 1, :]

    gb = _sigmoid(bg_ref[...])
    for hp in range(NSA_HEADS // 2):
        parts = []
        for h in (2 * hp, 2 * hp + 1):
            parts.append(gb[3 * h:3 * h + 1, :] * cmp_ref[HEAD_DIM * h:HEAD_DIM * (h + 1), :]
                         + gb[3 * h + 1:3 * h + 2, :] * head_out(acc_ref, h)
                         + gb[3 * h + 2:3 * h + 3, :] * head_out(accw_ref, h))
        o_ref[:, LANES * hp:LANES * (hp + 1)] = jnp.concatenate(parts, axis=0).T


def _nsa_call(qc, qcr, kcc, vct, ovt, ks, vst, kw, vwt, bgt, n_sel):
    B, S, W = qc.shape
    n_blk = S // SLC_BLOCK
    qspec = pl.BlockSpec((None, TQ, W), lambda b, i: (b, i, 0))
    return pl.pallas_call(
        functools.partial(_nsa_kernel, n_sel=n_sel),
        grid=(B, S // TQ),
        in_specs=[
            qspec, qspec,
            pl.BlockSpec((None,) + kcc.shape[1:], lambda b, i: (b, 0, 0, 0)),
            pl.BlockSpec((None,) + vct.shape[1:], lambda b, i: (b, 0, 0, 0)),
            pl.BlockSpec(ovt.shape, lambda b, i: (0, 0)),
            pl.BlockSpec((None, S, 2 * LANES), lambda b, i: (b, 0, 0)),
            pl.BlockSpec((None, NSA_GROUPS * HV, S), lambda b, i: (b, 0, 0)),
            pl.BlockSpec((None, S, 2 * LANES), lambda b, i: (b, 0, 0)),
            pl.BlockSpec((None, NSA_GROUPS * HV, S), lambda b, i: (b, 0, 0)),
            pl.BlockSpec((None, 3 * NSA_HEADS, TQ), lambda b, i: (b, 0, i)),
        ],
        out_specs=qspec,
        out_shape=jax.ShapeDtypeStruct((B, S, W), F32),
        scratch_shapes=[pltpu.VMEM((W, TQ), F32), pltpu.VMEM((NSA_GROUPS, n_blk, TQ), F32),
                        pltpu.VMEM((2, n_blk, TQ), F32), pltpu.VMEM((NSA_HEADS, TQ, LANES), BF16), pltpu.VMEM((NSA_HEADS, TQ, LANES), BF16)]
        + _attn_scratch(NSA_HEADS, HV)[1:] + _attn_scratch(NSA_HEADS, HV),
        compiler_params=pltpu.CompilerParams(
            dimension_semantics=("arbitrary", "arbitrary"), vmem_limit_bytes=VMEM_LIMIT),
        name="nsa_attn",
    )(qc, qcr, kcc, vct, ovt, ks, vst, kw, vwt, bgt)


def _post_kernel(x_ref, oa_ref, ob_ref, oc_ref, p_ref, g_ref, wg_ref, wup_ref, wout_ref,
                 pg_ref, wple_ref, wpg_ref, fg_ref, o_ref, *, final):
    x = x_ref[...]
    h = _rms(x, g_ref[...]).astype(BF16)
    mixed = jnp.zeros(x.shape, F32)
    for n, br_ref in enumerate((oa_ref, ob_ref, oc_ref)):
        gate = _dot(h, wg_ref[:, BRANCH_WIDTH * n:BRANCH_WIDTH * (n + 1)])
        og = br_ref[...] * (gate * _sigmoid(gate))
        up = _dot(og.astype(BF16), wup_ref[n])
        c0 = N_BRANCH * BRANCH_WIDTH + D_MODEL * n
        mixed = mixed + _sigmoid(_dot(h, wg_ref[:, c0:c0 + D_MODEL])) * up
    x1 = x + _dot(mixed.astype(BF16), wout_ref[...])
    emb = _dot(p_ref[...].astype(BF16), wple_ref[...])
    hg = _dot(_rms(x1, pg_ref[...]).astype(BF16), wpg_ref[...])
    x2 = x1 + emb * _sigmoid(hg)
    if final:
        x2 = _rms(x2, fg_ref[...])
    o_ref[...] = x2


def _post_call(x, oa, ob, oc, p, g, wg, wup, wout, pg, wple, wpg, fg, final, layer):
    T, D = x.shape
    tm = TM_POST
    tok = lambda w: pl.BlockSpec((tm, w), lambda i: (i, 0))
    full = lambda a: _layer_spec(a, layer, pipeline_mode=pl.Buffered(1))
    return pl.pallas_call(
        functools.partial(_post_kernel, final=final),
        grid=(T // tm,),
        in_specs=[tok(D), tok(BRANCH_WIDTH), tok(BRANCH_WIDTH), tok(BRANCH_WIDTH),
                  pl.BlockSpec((None, tm, PLE_DIM), lambda i: (layer, i, 0)),
                  full(g), full(wg), full(wup), full(wout), full(pg), full(wple), full(wpg),
                  pl.BlockSpec(fg.shape, lambda i: (0, 0))],
        out_specs=tok(D),
        out_shape=jax.ShapeDtypeStruct((T, D), F32),
        compiler_params=pltpu.CompilerParams(
            dimension_semantics=("arbitrary",), vmem_limit_bytes=VMEM_LIMIT),
        name="post_mix",
    )(x, oa, ob, oc, p, g, wg, wup, wout, pg, wple, wpg, fg)


ROPE_COLS = 3 * (ROT_DIM // 2)


def _rope_inputs(positions):
    half = ROT_DIM // 2
    inv = ROPE_THETA ** (-jnp.arange(0, ROT_DIM, 2, dtype=F32) / ROT_DIM)
    ang = positions.astype(F32)[..., None] * inv
    cs = jnp.concatenate([jnp.cos(ang), jnp.sin(ang), jnp.ones_like(ang)], axis=-1)
    place = np.zeros((ROPE_COLS, 3 * LANES), np.float32)
    for head in range(LANES // HEAD_DIM):
        base = HEAD_DIM * head
        for j in range(half):
            place[j, base + j] = 1.0
            place[j, base + half + j] = 1.0
            place[half + j, LANES + base + j] = -1.0
            place[half + j, 2 * LANES + base + half + j] = 1.0
        place[2 * half, base + ROT_DIM:base + HEAD_DIM] = 1.0
    return cs, jnp.asarray(place, dtype=BF16)


_NAT_PIECES = (
    (0, _A_Q, 512), (512, _A_K, 512), (1024, _B_Q, 512), (1536, _I_Q, 512), (2048, _C_Q, 512),
    (2560, _B_K, 64), (2624, _B_K, 64), (2688, _I_K, 64), (2752, _I_K, 64),
    (2816, _C_KS, 64), (2880, _C_KS, 64), (2944, _C_KS + 64, 64), (3008, _C_KS + 64, 64),
    (3072, _C_KW, 64), (3136, _C_KW, 64), (3200, _C_KW + 64, 64), (3264, _C_KW + 64, 64),
    (3328, _C_KC, 128), (3456, _C_VC, 128))
_GATE_PIECES = ((0, _A_G, 512), (512, _B_G, 512), (1024, _C_G, 512), (1536, _MERGE, 3 * D_MODEL))
_TR_PIECES = ((0, _A_V, 512), (512, _B_V, 64), (576, _C_VS, 128), (704, _C_VW, 128),
              (832, _I_W, 8), (840, _C_BG, 24))


def _relayout_kernel(wt_ref, nat_ref, gates_ref, trt_ref):
    for dst_ref, pieces in ((nat_ref, _NAT_PIECES), (gates_ref, _GATE_PIECES)):
        for d0, s0, n in pieces:
            for r in range(0, n, LANES):
                m = min(LANES, n - r)
                dst_ref[:, d0 + r:d0 + r + m] = wt_ref[s0 + r:s0 + r + m, :].T.astype(BF16)
    groups, cur = [], []
    for d0, s0, n in _TR_PIECES:
        if cur and d0 % ONES_ROWS == 0:
            groups.append(cur)
            cur = []
        cur.append((d0, s0, n))
    groups.append(cur)
    for grp in groups:
        rows = jnp.concatenate([wt_ref[s0:s0 + n, :] for _, s0, n in grp], axis=0)
        trt_ref[grp[0][0]:grp[0][0] + rows.shape[0], :] = rows.astype(BF16)


def _proj_weights(w_in):
    L, D, n_in = w_in.shape
    w_t = jnp.swapaxes(w_in, 1, 2)
    return pl.pallas_call(
        _relayout_kernel,
        grid=(L, D // LANES),
        in_specs=[pl.BlockSpec((None, n_in, LANES), lambda l, r: (l, 0, r))],
        out_specs=(pl.BlockSpec((None, LANES, N_NAT), lambda l, r: (l, r, 0)),
                   pl.BlockSpec((None, LANES, BRANCH_WIDTH * N_BRANCH + N_BRANCH * D_MODEL), lambda l, r: (l, r, 0)),
                   pl.BlockSpec((None, N_TR, LANES), lambda l, r: (l, 0, r))),
        out_shape=(jax.ShapeDtypeStruct((L, D, N_NAT), BF16),
                   jax.ShapeDtypeStruct((L, D, BRANCH_WIDTH * N_BRANCH + N_BRANCH * D_MODEL), BF16),
                   jax.ShapeDtypeStruct((L, N_TR, D), BF16)),
        compiler_params=pltpu.CompilerParams(
            dimension_semantics=("arbitrary", "arbitrary"), vmem_limit_bytes=VMEM_LIMIT),
        name="weight_relayout",
    )(w_t)


def _compress_weights(pe_k, w1_k, w2_k, pe_v, w1_v, w2_v):
    L = pe_k.shape[0]
    half = CMP_LEN // 2
    kdim = half * NSA_GROUPS * HEAD_DIM
    pe = jnp.stack([pe_k, pe_v], axis=1)
    pe = jnp.broadcast_to(pe[:, :, :, None, :], (L, 2, CMP_LEN, NSA_GROUPS, HEAD_DIM))
    w1 = jnp.stack([w1_k, w1_v], axis=1).astype(BF16)
    zero = jnp.zeros_like(w1)
    w1g = jnp.stack([jnp.stack([w1 if h == g else zero for h in range(NSA_GROUPS)], axis=3)
                     for g in range(NSA_GROUPS)], axis=2)
    w1g = w1g.reshape(L, 2 * NSA_GROUPS, CMP_LEN, NSA_GROUPS * HEAD_DIM, CMP_HIDDEN)
    w2 = jnp.stack([w2_k, w2_v], axis=1).astype(BF16)
    return (pe[:, :, :half].reshape(L, 2, 1, kdim), pe[:, :, half:].reshape(L, 2, 1, kdim),
            w1g[:, :, :half].reshape(L, 2 * NSA_GROUPS, kdim, CMP_HIDDEN),
            w1g[:, :, half:].reshape(L, 2 * NSA_GROUPS, kdim, CMP_HIDDEN), w2)


def _overlap_t(S):
    nc = S // CMP_STRIDE
    n_blk = S // SLC_BLOCK
    cstart = np.arange(nc) * CMP_STRIDE
    sstart = np.arange(n_blk) * SLC_BLOCK
    ov = ((cstart[None, :] < sstart[:, None] + SLC_BLOCK) & (cstart[None, :] + CMP_LEN > sstart[:, None]))
    return jnp.asarray(ov.astype(np.float32), dtype=BF16)


def kernel(x, p, positions, norm_g, w_in, diff_lambda, diff_subln_g, cmp_pe_k, cmp_w1_k, cmp_w2_k,
           cmp_pe_v, cmp_w1_v, cmp_w2_v, w_up, w_out, ple_norm_g, w_ple, w_ple_gate, final_norm_g):
    B, S, D = x.shape
    depth = w_in.shape[0]
    k_sel = min(DSA_TOPK_MAX, S // 4)
    n_sel = min(SLC_TOPN_MAX, S // SLC_BLOCK)
    assert S % TM_PROJ == 0 and S % TQ == 0 and k_sel <= TQ and D == D_MODEL
    assert KC == TQ and WINDOW % KC == 0
    cs, place = _rope_inputs(positions)
    ovt = _overlap_t(S)

    wn, wg, wt = _proj_weights(w_in)
    cmp_w = _compress_weights(cmp_pe_k, cmp_w1_k, cmp_w2_k, cmp_pe_v, cmp_w1_v, cmp_w2_v)
    g_in = norm_g.reshape(depth, 1, D)
    g_ple = ple_norm_g.reshape(depth, 1, D)
    g_sub = diff_subln_g.reshape(depth, -1, 1)
    wup, wout, wple, wpg = (w.astype(BF16) for w in (w_up, w_out, w_ple, w_ple_gate))
    p2 = p.reshape(depth, B * S, PLE_DIM)

    for i in range(depth):
        lam_init = 0.8 - 0.6 * math.exp(-0.3 * i)
        (qa, ka, qb, iq, qc, qcr, kb, ik, ks, kw, ck, cv,
         vat, vbt, vst, vwt, iwt, bgt) = _proj_call(x, g_in, wn, wt, cs, place, i)

        kcc, vct = _compress_call(ck, cv, *cmp_w, i)

        oa = _diff_call(qa, ka, vat, diff_lambda, g_sub, lam_init, i)
        ob = _dsa_call(qb, iq, kb, ik, vbt, iwt, k_sel)
        oc = _nsa_call(qc, qcr, kcc, vct, ovt, ks, vst, kw, vwt, bgt, n_sel)

        x = _post_call(
            x.reshape(B * S, D), oa.reshape(B * S, -1), ob.reshape(B * S, -1), oc.reshape(B * S, -1),
            p2, g_in, wg, wup, wout, g_ple, wple, wpg, final_norm_g.reshape(1, D),
            final=(i == depth - 1), layer=i).reshape(B, S, D)
    return x
```

```python
import functools
import math

import numpy as np
import jax
import jax.numpy as jnp
from jax import lax
from jax.experimental import pallas as pl
from jax.experimental.pallas import tpu as pltpu

F32 = jnp.float32
BF16 = jnp.bfloat16

D_MODEL = 1024
HEAD_DIM = 64
ROT_DIM = 16
ROPE_THETA = 500000.0
EPS = 1e-6
PLE_DIM = 256
DA_HEADS = 4
DSA_HEADS = 8
IDX_HEADS = 8
DSA_TOPK_MAX = 256
NSA_GROUPS = 2
NSA_HEADS = 8
CMP_LEN = 32
CMP_STRIDE = 16
CMP_HIDDEN = 128
SLC_BLOCK = 64
SLC_TOPN_MAX = 16
WINDOW = 512
FORCE_SCORE = 1e4
BRANCH_WIDTH = 512
N_BRANCH = 3

IN_SPLITS = (512, 512, 512, 512, 512, 64, 64, 512, 512, 64, 8,
             512, 128, 128, 128, 128, 128, 128, 512, 24, 3 * D_MODEL)
(_A_Q, _A_K, _A_V, _A_G, _B_Q, _B_K, _B_V, _B_G, _I_Q, _I_K, _I_W,
 _C_Q, _C_KC, _C_VC, _C_KS, _C_VS, _C_KW, _C_VW, _C_G, _C_BG, _MERGE) = [
    int(v) for v in np.concatenate([[0], np.cumsum(IN_SPLITS)[:-1]])]

LANES = 128
SUBLANES = 8
TQ = 512
KC = 512
TM_PROJ = 512
TM_POST = 512
NEG_INF = float("-inf")
M_INIT = -1e30
N_NAT = 3584
N_TR = 864
BISECT_PLAIN = 9
ONES_ROWS = 16
HV = HEAD_DIM + ONES_ROWS
HV_A = 2 * HEAD_DIM + ONES_ROWS
LOG2E = math.log2(math.e)
VMEM_LIMIT = 56 * 1024 * 1024


def _dot_nt(a, b):
    return lax.dot_general(a, b, (((1,), (1,)), ((), ())), preferred_element_type=F32)


def _dot(a, b):
    return jnp.dot(a, b, preferred_element_type=F32)


def _rms(x, g):
    return x * lax.rsqrt(jnp.mean(x * x, axis=-1, keepdims=True) + EPS) * g


def _sigmoid(x):
    return 1.0 / (1.0 + jnp.exp(-x))


def _store_half_masked(src_ref, dst_ref, n_pairs):
    lane = lax.broadcasted_iota(jnp.int32, (src_ref.shape[0], LANES), 1)
    for hp in range(n_pairs):
        x = src_ref[:, LANES * hp:LANES * (hp + 1)]
        z = jnp.zeros_like(x)
        dst_ref[2 * hp] = jnp.where(lane < HEAD_DIM, x, z)
        dst_ref[2 * hp + 1] = jnp.where(lane >= HEAD_DIM, x, z)


def _fold_rows(x, op):
    n_acc = 2
    accs = [x[SUBLANES * r:SUBLANES * (r + 1)] for r in range(n_acc)]
    for r in range(n_acc, x.shape[0] // SUBLANES):
        accs[r % n_acc] = op(accs[r % n_acc], x[SUBLANES * r:SUBLANES * (r + 1)])
    return op(accs[0], accs[1])


HALF = KC // 2


def _scores(k, q, dead=None):
    if dead is None:
        return _dot_nt(k, q)
    neg = jnp.full((HALF, HALF), NEG_INF, F32)
    if dead == "bl":
        return jnp.concatenate(
            [_dot_nt(k[:HALF], q), jnp.concatenate([neg, _dot_nt(k[HALF:], q[HALF:])], axis=1)], axis=0)
    return jnp.concatenate(
        [jnp.concatenate([_dot_nt(k[:HALF], q[:HALF]), neg], axis=1), _dot_nt(k[HALF:], q)], axis=0)


def _weighted_values(v_t, p, dead=None):
    if dead is None:
        return _dot(v_t, p)
    if dead == "bl":
        return jnp.concatenate([_dot(v_t[:, :HALF], p[:HALF, :HALF]), _dot(v_t, p[:, HALF:])], axis=1)
    return jnp.concatenate([_dot(v_t, p[:, :HALF]), _dot(v_t[:, HALF:], p[HALF:, HALF:])], axis=1)


class _Stage:
    def __init__(self, j0, j1, nh, hv, prep, score, values, m_ref, al_ref, acc_ref,
                 dead_first=None, dead_last=None):
        self.j0 = jnp.asarray(j0, jnp.int32)
        self.j1 = jnp.asarray(j1, jnp.int32)
        self.nh, self.hv, self.prep, self.score, self.values = nh, hv, prep, score, values
        self.m_ref, self.al_ref, self.acc_ref = m_ref, al_ref, acc_ref
        self.dead_first, self.dead_last = dead_first, dead_last

    def reset(self):
        self.m_ref[...] = jnp.full(self.m_ref.shape, M_INIT, F32)
        self.acc_ref[...] = jnp.zeros(self.acc_ref.shape, F32)

    def produce(self, sbuf_ref, j, dead=None):
        ctx = self.prep(j)
        for h in range(self.nh):
            s = self.score(ctx, h, dead)
            m_old = self.m_ref[h:h + 1, :]
            m_new = jnp.maximum(m_old, jnp.max(s, axis=0, keepdims=True))
            sbuf_ref[h] = s
            self.m_ref[h:h + 1, :] = m_new
            self.al_ref[h:h + 1, :] = jnp.exp2(m_old - m_new)

    def consume(self, sbuf_ref, j, dead=None):
        for h in range(self.nh):
            p = jnp.exp2(sbuf_ref[h] - self.m_ref[h:h + 1, :])
            rows = slice(self.hv * h, self.hv * (h + 1))
            self.acc_ref[rows, :] = (self.al_ref[h:h + 1, :] * self.acc_ref[rows, :]
                                     + _weighted_values(self.values(j, h), p.astype(BF16), dead))

    def loop(self, sbuf_ref, lo, hi):
        def body(j, carry):
            self.consume(sbuf_ref, j - 1)
            self.produce(sbuf_ref, j)
            return carry
        lax.fori_loop(lo, hi, body, 0)


def _attn_chain(stages, sbuf_ref, tile):
    for st in stages:
        st.reset()

    @pl.when(tile == 0)
    def _():
        for st in stages:
            st.produce(sbuf_ref, st.j0)
            st.consume(sbuf_ref, st.j0)

    @pl.when(tile > 0)
    def _():
        stages[0].produce(sbuf_ref, stages[0].j0, stages[0].dead_first)
        for n, st in enumerate(stages):
            st.loop(sbuf_ref, st.j0 + 1, st.j1)
            st.consume(sbuf_ref, st.j1 - 1, st.dead_first)
            st.produce(sbuf_ref, st.j1, st.dead_last)
            st.consume(sbuf_ref, st.j1, st.dead_last)
            if n + 1 < len(stages):
                stages[n + 1].produce(sbuf_ref, stages[n + 1].j0, stages[n + 1].dead_first)


def _attn_scratch(nh, hv):
    return [pltpu.VMEM((nh, KC, TQ), F32), pltpu.VMEM((nh, TQ), F32),
            pltpu.VMEM((nh, TQ), F32), pltpu.VMEM((nh * hv, TQ), F32)]


def _layer_spec(a, layer, **kw):
    tail = (0,) * (a.ndim - 1)
    return pl.BlockSpec((None,) + a.shape[1:], lambda *_: (layer,) + tail, **kw)


def _ones_rows(n_cols):
    r = lax.broadcasted_iota(jnp.int32, (ONES_ROWS, n_cols), 0)
    return jnp.where(r == 0, 1.0, 0.0).astype(BF16)


def _proj_kernel(x_ref, g_ref, wn_ref, wt_ref, cs_ref, place_ref,
                 qa_ref, ka_ref, qb_ref, iq_ref, qc_ref, qcr_ref, kb_ref, ik_ref,
                 ks_ref, kw_ref, ck_ref, cv_ref,
                 vat_ref, vbt_ref, vst_ref, vwt_ref, iwt_ref, bgt_ref, zc_ref):
    h = _rms(x_ref[...], g_ref[...]).astype(BF16)
    cs = cs_ref[...]
    hi = cs.astype(BF16)
    rest = cs - hi.astype(F32)
    mid = rest.astype(BF16)
    low = (rest - mid.astype(F32)).astype(BF16)
    spread = _dot(hi, place_ref[...]) + _dot(mid, place_ref[...]) + _dot(low, place_ref[...])
    rc, rs1, rs2 = spread[:, 0:LANES], spread[:, LANES:2 * LANES], spread[:, 2 * LANES:3 * LANES]

    def rope(z):
        return z * rc + pltpu.roll(z, LANES - ROT_DIM // 2, 1) * rs1 + pltpu.roll(z, ROT_DIM // 2, 1) * rs2

    qscale = HEAD_DIM ** -0.5
    qscale2 = qscale * LOG2E
    segs = (
        (0, 512, ((qa_ref, True, qscale2),)),
        (512, 512, ((ka_ref, True, 1.0),)),
        (1024, 512, ((qb_ref, True, qscale2),)),
        (1536, 512, ((iq_ref, True, qscale),)),
        (2048, 512, ((qc_ref, False, qscale), (qcr_ref, True, qscale2))),
        (2560, 128, ((kb_ref, True, 1.0),)),
        (2688, 128, ((ik_ref, True, 1.0),)),
        (2816, 256, ((ks_ref, True, 1.0),)),
        (3072, 256, ((kw_ref, True, 1.0),)),
    )
    zc = _dot(h, wn_ref[:, 3328:3328 + 2 * LANES])
    half_len = CMP_LEN // 2
    for n, c_ref in enumerate((ck_ref, cv_ref)):
        zc_ref[n] = zc[:, LANES * n:LANES * (n + 1)]
        for l in range(half_len):
            c_ref[:, LANES * l:LANES * (l + 1)] = zc_ref[n, pl.ds(l, zc.shape[0] // half_len, stride=half_len), :]
    for c0, width, outs in segs:
        z = _dot(h, wn_ref[:, c0:c0 + width])
        for out_ref, rot, scale in outs:
            for j in range(width // LANES):
                zj = z[:, LANES * j:LANES * (j + 1)]
                if rot:
                    zj = rope(zj)
                if scale != 1.0:
                    zj = zj * scale
                out_ref[:, LANES * j:LANES * (j + 1)] = zj.astype(out_ref.dtype)

    zt = _dot_nt(wt_ref[...], h)
    ones = _ones_rows(zt.shape[1])
    vd = 2 * HEAD_DIM
    for h in range(DA_HEADS):
        vat_ref[HV_A * h:HV_A * h + vd, :] = zt[vd * h:vd * (h + 1)].astype(BF16)
        vat_ref[HV_A * h + vd:HV_A * (h + 1), :] = ones
    vbt_ref[0:HEAD_DIM, :] = zt[512:576].astype(BF16)
    vbt_ref[HEAD_DIM:HV, :] = ones
    for g in range(NSA_GROUPS):
        for ref, base in ((vst_ref, 576), (vwt_ref, 704)):
            ref[HV * g:HV * g + HEAD_DIM, :] = zt[base + HEAD_DIM * g:base + HEAD_DIM * (g + 1)].astype(BF16)
            ref[HV * g + HEAD_DIM:HV * (g + 1), :] = ones
    iwt_ref[...] = zt[832:840] * (IDX_HEADS ** -0.5)
    bgt_ref[...] = zt[840:864]


def _proj_call(x, g, wn, wt, cs, place, layer):
    B, S, D = x.shape
    tm = TM_PROJ
    nat = lambda w: pl.BlockSpec((None, tm, w), lambda b, i: (b, i, 0))
    tr = lambda r: pl.BlockSpec((None, r, tm), lambda b, i: (b, 0, i))
    full = lambda a: _layer_spec(a, layer)
    sds = jax.ShapeDtypeStruct
    half_len = CMP_LEN // 2
    out_shape = (
        sds((B, S, 512), BF16), sds((B, S, 512), BF16), sds((B, S, 512), BF16), sds((B, S, 512), BF16),
        sds((B, S, 512), BF16), sds((B, S, 512), BF16), sds((B, S, 128), BF16), sds((B, S, 128), BF16),
        sds((B, S, 256), BF16), sds((B, S, 256), BF16),
        sds((B, S // half_len, half_len * LANES), F32), sds((B, S // half_len, half_len * LANES), F32),
        sds((B, DA_HEADS * HV_A, S), BF16), sds((B, HV, S), BF16),
        sds((B, NSA_GROUPS * HV, S), BF16), sds((B, NSA_GROUPS * HV, S), BF16),
        sds((B, IDX_HEADS, S), F32), sds((B, 3 * NSA_HEADS, S), F32),
    )
    cmp_rows = pl.BlockSpec((None, tm // half_len, half_len * LANES), lambda b, i: (b, i, 0))
    out_specs = (nat(512), nat(512), nat(512), nat(512), nat(512), nat(512), nat(128), nat(128),
                 nat(256), nat(256), cmp_rows, cmp_rows,
                 tr(DA_HEADS * HV_A), tr(HV), tr(NSA_GROUPS * HV), tr(NSA_GROUPS * HV),
                 tr(IDX_HEADS), tr(3 * NSA_HEADS))
    return pl.pallas_call(
        _proj_kernel,
        grid=(B, S // tm),
        in_specs=[nat(D), full(g), full(wn), full(wt), nat(ROPE_COLS),
                  pl.BlockSpec(place.shape, lambda b, i: (0, 0))],
        out_specs=out_specs,
        out_shape=out_shape,
        scratch_shapes=[pltpu.VMEM((2, tm, LANES), F32)],
        compiler_params=pltpu.CompilerParams(
            dimension_semantics=("arbitrary", "arbitrary"), vmem_limit_bytes=VMEM_LIMIT),
        name="in_proj",
    )(x, g, wn, wt, cs, place)


def _compress_kernel(rk_ref, rv_ref, pea_ref, peb_ref, w1a_ref, w1b_ref, w2_ref, nat_ref, t_ref):
    for kind, r_ref in enumerate((rk_ref, rv_ref)):
        r = r_ref[...]
        xa = (r + pea_ref[kind]).astype(BF16)
        xb = (r + peb_ref[kind]).astype(BF16)
        nrow = r.shape[0]
        for g in range(NSA_GROUPS):
            n = NSA_GROUPS * kind + g
            hid = jax.nn.gelu(_dot(xa, w1a_ref[n]) + pltpu.roll(_dot(xb, w1b_ref[n]), nrow - 1, 0))
            o = _dot(hid.astype(BF16), w2_ref[kind])
            nat_ref[n] = jnp.concatenate([o, o], axis=1).astype(BF16)
            t_ref[n] = o.T.astype(BF16)


def _compress_call(rk, rv, pea, peb, w1a, w1b, w2, layer):
    B, nr, kdim = rk.shape
    full = lambda a: _layer_spec(a, layer)
    rspec = pl.BlockSpec((None, nr, kdim), lambda b: (b, 0, 0))
    return pl.pallas_call(
        _compress_kernel,
        grid=(B,),
        in_specs=[rspec, rspec, full(pea), full(peb), full(w1a), full(w1b), full(w2)],
        out_specs=(
            pl.BlockSpec((None, 2 * NSA_GROUPS, nr, 2 * HEAD_DIM), lambda b: (b, 0, 0, 0)),
            pl.BlockSpec((None, 2 * NSA_GROUPS, HEAD_DIM, nr), lambda b: (b, 0, 0, 0)),
        ),
        out_shape=(jax.ShapeDtypeStruct((B, 2 * NSA_GROUPS, nr, 2 * HEAD_DIM), BF16),
                   jax.ShapeDtypeStruct((B, 2 * NSA_GROUPS, HEAD_DIM, nr), BF16)),
        compiler_params=pltpu.CompilerParams(dimension_semantics=("arbitrary",)),
        name="nsa_compress",
    )(rk, rv, pea, peb, w1a, w1b, w2)


def _diff_kernel(q_ref, k_ref, vt_ref, lam_ref, sg_ref, o_ref,
                 qz_ref, sbuf_ref, m_ref, al_ref, acc_ref, *, lam_init):
    i = pl.program_id(1)
    lp = lam_ref[...]
    lam = (jnp.exp(jnp.sum(lp[0:1] * lp[1:2], axis=1, keepdims=True))
           - jnp.exp(jnp.sum(lp[2:3] * lp[3:4], axis=1, keepdims=True)) + lam_init)
    row_minus_col = (lax.broadcasted_iota(jnp.int32, (KC, TQ), 0)
                     - lax.broadcasted_iota(jnp.int32, (KC, TQ), 1))
    _store_half_masked(q_ref, qz_ref, DA_HEADS)
    vdim = 2 * HEAD_DIM
    n_maps = 2 * DA_HEADS

    def prep(j):
        return pl.multiple_of(j * KC, KC), jnp.where(row_minus_col <= (i - j) * KC, 0.0, NEG_INF)

    def score(ctx, hm, dead):
        off, bias = ctx
        h = hm // 2
        return _scores(k_ref[pl.ds(off, KC), vdim * h:vdim * (h + 1)], qz_ref[hm], dead) + bias

    def values(j, hm):
        h = hm // 2
        return vt_ref[HV_A * h:HV_A * (h + 1), pl.ds(pl.multiple_of(j * KC, KC), KC)]

    _attn_chain([_Stage(0, i, n_maps, HV_A, prep, score, values, m_ref, al_ref, acc_ref, dead_last="bl")],
                sbuf_ref, i)

    for h in range(DA_HEADS):
        r0, r1 = HV_A * 2 * h, HV_A * (2 * h + 1)
        o_t = (acc_ref[r0:r0 + vdim, :] / acc_ref[r0 + vdim:r0 + vdim + 1, :]
               - lam * (acc_ref[r1:r1 + vdim, :] / acc_ref[r1 + vdim:r1 + vdim + 1, :]))
        ms = jnp.mean(o_t * o_t, axis=0, keepdims=True)
        y = o_t * lax.rsqrt(ms + EPS) * sg_ref[...] * (1.0 - lam_init)
        o_ref[:, vdim * h:vdim * (h + 1)] = y.T


def _diff_call(qa, ka, vat, lam_p, sg, lam_init, layer):
    B, S, W = qa.shape
    n_maps = 2 * DA_HEADS
    return pl.pallas_call(
        functools.partial(_diff_kernel, lam_init=lam_init),
        grid=(B, S // TQ),
        in_specs=[
            pl.BlockSpec((None, TQ, W), lambda b, i: (b, i, 0)),
            pl.BlockSpec((None, S, W), lambda b, i: (b, 0, 0)),
            pl.BlockSpec((None, DA_HEADS * HV_A, S), lambda b, i: (b, 0, 0)),
            _layer_spec(lam_p, layer),
            _layer_spec(sg, layer),
        ],
        out_specs=pl.BlockSpec((None, TQ, W), lambda b, i: (b, i, 0)),
        out_shape=jax.ShapeDtypeStruct((B, S, W), F32),
        scratch_shapes=[pltpu.VMEM((n_maps, TQ, LANES), BF16)] + _attn_scratch(n_maps, HV_A),
        compiler_params=pltpu.CompilerParams(
            dimension_semantics=("arbitrary", "arbitrary"), vmem_limit_bytes=VMEM_LIMIT),
        name="diff_attn",
    )(qa, ka, vat, lam_p, sg)


def _dsa_kernel(q_ref, iq_ref, kb_ref, ik_ref, vt_ref, iw_ref, o_ref,
                sc_ref, tj_ref, tri_ref, qz_ref, sbuf_ref, m_ref, al_ref, acc_ref, *, k_sel, seq_len):
    i = pl.program_id(1)
    nch = i + 1
    row = lax.broadcasted_iota(jnp.int32, (KC, TQ), 0)
    col = lax.broadcasted_iota(jnp.int32, (KC, TQ), 1)
    causal = row <= col
    iw = iw_ref[...]
    kf = float(k_sel)
    _store_half_masked(iq_ref, qz_ref, IDX_HEADS // 2)

    def idx_chunk(j, carry, masked):
        mx, mn = carry
        off = pl.multiple_of(j * KC, KC)
        ikc = ik_ref[pl.ds(off, KC), :]
        sc = jnp.zeros((KC, TQ), F32)
        for h in range(IDX_HEADS):
            dots = _scores(ikc, qz_ref[h], "bl" if masked else None)
            sc = sc + iw[h:h + 1, :] * jnp.maximum(dots, 0.0)
        lo_src = sc
        if masked:
            lo_src = jnp.where(causal, sc, jnp.inf)
            sc = jnp.where(causal, sc, NEG_INF)
        sc_ref[pl.ds(off, KC), :] = sc
        return (jnp.maximum(mx, _fold_rows(sc, jnp.maximum)),
                jnp.minimum(mn, _fold_rows(lo_src, jnp.minimum)))

    carry = (jnp.full((SUBLANES, TQ), NEG_INF, F32), jnp.full((SUBLANES, TQ), jnp.inf, F32))
    carry = lax.fori_loop(0, i, functools.partial(idx_chunk, masked=False), carry)
    mx8, mn8 = idx_chunk(i, carry, True)
    row_max = jnp.max(mx8, axis=0, keepdims=True)
    row_min = jnp.min(mn8, axis=0, keepdims=True)

    def probe_pass(th, snap):
        thb = jnp.broadcast_to(th, (SUBLANES, TQ))

        def body(j, c):
            off = pl.multiple_of(j * KC, KC)
            c = list(c)
            for r in range(KC // SUBLANES):
                x = sc_ref[pl.ds(off + SUBLANES * r, SUBLANES), :]
                ge = x >= thb
                c[0] = c[0] + jnp.where(ge, 1.0, 0.0)
                if snap:
                    c[1] = jnp.maximum(c[1], jnp.where(ge, NEG_INF, x))
            return tuple(c)
        init = (jnp.zeros((SUBLANES, TQ), F32),)
        if snap:
            init += (jnp.full((SUBLANES, TQ), NEG_INF, F32),)
        out = lax.fori_loop(0, nch, body, init)
        cnt = jnp.sum(out[0], axis=0, keepdims=True)
        return (cnt, jnp.max(out[1], axis=0, keepdims=True)) if snap else cnt

    n_causal = (i * TQ + 1 + lax.broadcasted_iota(jnp.int32, (1, TQ), 1)).astype(F32)
    keep_all = n_causal <= kf

    def bisect(_, c):
        lo, hi, clo, chi = c
        mid = lo + (hi - lo) * 0.5
        cnt = probe_pass(mid, False)
        ge = cnt >= kf
        return (jnp.where(ge, mid, lo), jnp.where(ge, hi, mid),
                jnp.where(ge, cnt, clo), jnp.where(ge, chi, cnt))

    hi0 = row_max + jnp.abs(row_max) * 1e-6 + 1e-30
    lo, hi, clo, chi = lax.fori_loop(
        0, BISECT_PLAIN, bisect, (row_min, hi0, n_causal, jnp.zeros((1, TQ), F32)))

    def cond(st):
        return jnp.logical_and(st[0] < 256, st[-1] > 0.0)

    def snap_step(st):
        it, lo, hi, clo, chi, hie, known, stuck, done, _ = st
        top = jnp.where(known > 0.0, hie, hi)
        mid = lo + (top - lo) * 0.5
        inside = jnp.logical_and(mid > lo, mid < top)
        near_top = jnp.logical_or(kf - chi <= 2.0, stuck > 0.0)
        use_top = jnp.logical_or(jnp.logical_and(known > 0.0, near_top), jnp.logical_not(inside))
        th = jnp.where(use_top, top, mid)
        cnt, edn = probe_pass(th, True)
        live = done <= 0.0
        ge = jnp.logical_and(live, cnt >= kf)
        lt = jnp.logical_and(live, cnt < kf)
        stuck = jnp.where(jnp.logical_and(ge, cnt == clo), 1.0, 0.0)
        lo = jnp.where(ge, th, lo)
        clo = jnp.where(ge, cnt, clo)
        hi = jnp.where(lt, th, hi)
        chi = jnp.where(lt, cnt, chi)
        hie = jnp.where(lt, edn, hie)
        known = jnp.where(lt, 1.0, known)
        fin = jnp.logical_or(clo == kf, jnp.logical_and(known > 0.0, lo >= hie))
        done = jnp.where(fin, 1.0, done)
        return it + 1, lo, hi, clo, chi, hie, known, stuck, done, jnp.max(1.0 - done)

    zero = jnp.zeros((1, TQ), F32)
    done0 = jnp.where(jnp.logical_or(clo == kf, keep_all), 1.0, 0.0)
    st = lax.while_loop(cond, snap_step, (jnp.int32(0), lo, hi, clo, chi, jnp.full((1, TQ), NEG_INF, F32),
                                          zero, zero, done0, jnp.max(1.0 - done0)))
    thr, clo, chi = st[1], st[3], st[4]
    need = jnp.where(clo == kf, float(seq_len), kf - chi)
    thr = jnp.where(keep_all, NEG_INF, thr)
    need = jnp.where(keep_all, 0.0, need)
    tj_ref[2:3, :] = jnp.zeros((1, TQ), F32)

    _store_half_masked(q_ref, qz_ref, DSA_HEADS // 2)
    half = KC // 2
    tri_ref[...] = jnp.where(lax.broadcasted_iota(jnp.int32, (half, half), 0)
                             >= lax.broadcasted_iota(jnp.int32, (half, half), 1), 1.0, 0.0).astype(BF16)

    def prep(j):
        off = pl.multiple_of(j * KC, KC)
        x = sc_ref[pl.ds(off, KC), :]
        eq = x == thr
        e = jnp.where(eq, 1.0, 0.0).astype(BF16)
        r_top = _dot(tri_ref[...], e[:half]) + tj_ref[2:3, :]
        r_bot = _dot(tri_ref[...], e[half:]) + r_top[half - 1:half, :]
        tj_ref[2:3, :] = r_bot[half - 1:half, :]
        rank = jnp.concatenate([r_top, r_bot], axis=0)
        tie_bias = jnp.where(jnp.logical_and(eq, rank <= need), 0.0, NEG_INF)
        return kb_ref[pl.ds(off, KC), :], jnp.where(x > thr, 0.0, tie_bias)

    def score(ctx, h, dead):
        kc, bias = ctx
        return _scores(kc, qz_ref[h], dead) + bias

    def values(j, h):
        return vt_ref[:, pl.ds(pl.multiple_of(j * KC, KC), KC)]

    _attn_chain([_Stage(0, i, DSA_HEADS, HV, prep, score, values, m_ref, al_ref, acc_ref, dead_last="bl")],
                sbuf_ref, i)
    for hp in range(DSA_HEADS // 2):
        blk = jnp.concatenate(
            [acc_ref[HV * h:HV * h + HEAD_DIM, :] / acc_ref[HV * h + HEAD_DIM:HV * h + HEAD_DIM + 1, :]
             for h in (2 * hp, 2 * hp + 1)], axis=0)
        o_ref[:, LANES * hp:LANES * (hp + 1)] = blk.T


def _dsa_call(qb, iq, kb, ik, vbt, iwt, k_sel):
    B, S, W = qb.shape
    return pl.pallas_call(
        functools.partial(_dsa_kernel, k_sel=k_sel, seq_len=S),
        grid=(B, S // TQ),
        in_specs=[
            pl.BlockSpec((None, TQ, W), lambda b, i: (b, i, 0)),
            pl.BlockSpec((None, TQ, W), lambda b, i: (b, i, 0)),
            pl.BlockSpec((None, S, LANES), lambda b, i: (b, 0, 0)),
            pl.BlockSpec((None, S, LANES), lambda b, i: (b, 0, 0)),
            pl.BlockSpec((None, HV, S), lambda b, i: (b, 0, 0)),
            pl.BlockSpec((None, IDX_HEADS, TQ), lambda b, i: (b, 0, i)),
        ],
        out_specs=pl.BlockSpec((None, TQ, W), lambda b, i: (b, i, 0)),
        out_shape=jax.ShapeDtypeStruct((B, S, W), F32),
        scratch_shapes=[pltpu.VMEM((S, TQ), F32), pltpu.VMEM((SUBLANES, TQ), F32),
                        pltpu.VMEM((KC // 2, KC // 2), BF16),
                        pltpu.VMEM((DSA_HEADS, TQ, LANES), BF16)] + _attn_scratch(DSA_HEADS, HV),
        compiler_params=pltpu.CompilerParams(
            dimension_semantics=("arbitrary", "arbitrary"), vmem_limit_bytes=VMEM_LIMIT),
        name="dsa_attn",
    )(qb, iq, kb, ik, vbt, iwt)


def _rank_tile(rk_ref, m, lanes):
    rows = lambda t: slice(SUBLANES * t, SUBLANES * (t + 1))
    own = rk_ref[0, rows(m), lanes]
    sub = lax.broadcasted_iota(jnp.int32, own.shape, 0)
    own_rows = [jnp.broadcast_to(own[r:r + 1], own.shape) for r in range(SUBLANES)]
    rank = jnp.zeros(own.shape, F32)
    for r, x in enumerate(own_rows):
        ahead = jnp.logical_or(x > own, jnp.logical_and(x == own, sub > r))
        rank = rank + jnp.where(ahead, 1.0, 0.0)
    for t in range(m):
        older = rk_ref[0, rows(t), lanes]
        rank_t = rk_ref[1, rows(t), lanes]
        for r in range(SUBLANES):
            rank = rank + jnp.where(jnp.broadcast_to(older[r:r + 1], own.shape) >= own, 1.0, 0.0)
            rank_t = rank_t + jnp.where(own_rows[r] > older, 1.0, 0.0)
        rk_ref[1, rows(t), lanes] = rank_t
    rk_ref[1, rows(m), lanes] = rank


def _nsa_kernel(q_ref, qr_ref, kc_ref, vct_ref, ovt_ref, ks_ref, vst_ref, kw_ref, vwt_ref, bg_ref,
                o_ref, cmp_ref, sel_ref, rk_ref, qz_ref, qrz_ref, mw_ref, alw_ref, accw_ref,
                sbuf_ref, m_ref, al_ref, acc_ref, *, n_sel):
    i = pl.program_id(1)
    row_minus_col = (lax.broadcasted_iota(jnp.int32, (KC, TQ), 0)
                     - lax.broadcasted_iota(jnp.int32, (KC, TQ), 1))
    tq = i * TQ + lax.broadcasted_iota(jnp.int32, (1, TQ), 1)
    hpg = NSA_HEADS // NSA_GROUPS
    nc = kc_ref.shape[1]
    n_blk = sel_ref.shape[1]
    gw = 2 * HEAD_DIM

    _store_half_masked(q_ref, qz_ref, NSA_HEADS // 2)
    _store_half_masked(qr_ref, qrz_ref, NSA_HEADS // 2)

    def compressed_and_select(rows):
        cmp_valid = CMP_STRIDE * lax.broadcasted_iota(jnp.int32, (rows, TQ), 0) + (CMP_LEN - 1) <= tq
        blk = lax.broadcasted_iota(jnp.int32, (n_blk, TQ), 0)
        forced = jnp.logical_or(blk == jnp.right_shift(tq, int(math.log2(SLC_BLOCK))), blk == 0)
        blk_causal = blk * SLC_BLOCK <= tq
        for h in range(NSA_HEADS):
            s = jnp.where(cmp_valid, _dot_nt(kc_ref[h // hpg, 0:rows, :], qz_ref[h]), NEG_INF)
            sbuf_ref[h, 0:rows, :] = s
            m_ref[h:h + 1, :] = jnp.maximum(jnp.max(s, axis=0, keepdims=True), M_INIT)
        for g in range(NSA_GROUPS):
            vcg = vct_ref[NSA_GROUPS + g, :, 0:rows]
            imp = jnp.zeros((n_blk, TQ), F32)
            for hh in range(hpg):
                h = hpg * g + hh
                p = jnp.exp(sbuf_ref[h, 0:rows, :] - m_ref[h:h + 1, :])
                l = jnp.sum(p, axis=0, keepdims=True)
                pc = (p * jnp.where(l > 0.0, 1.0 / l, 0.0)).astype(BF16)
                cmp_ref[HEAD_DIM * h:HEAD_DIM * (h + 1), :] = _dot(vcg, pc)
                imp = imp + _dot(ovt_ref[:, 0:rows], pc)
            imp = jnp.where(forced, FORCE_SCORE, imp)
            imp = jnp.where(blk_causal, imp, NEG_INF)
            rk_ref[0] = imp
            rk_ref[1] = jnp.zeros((n_blk, TQ), F32)
            for m in range(n_blk // SUBLANES):
                @pl.when(m * SUBLANES * SLC_BLOCK < (i + 1) * TQ)
                def _(m=m):
                    for l in range(TQ // LANES):
                        _rank_tile(rk_ref, m, slice(LANES * l, LANES * (l + 1)))
            sel_ref[g] = jnp.where(rk_ref[1] < float(n_sel), 0.0, NEG_INF)

    bpc = KC // SLC_BLOCK

    def slc_prep(j):
        off = pl.multiple_of(j * KC, KC)
        visible = row_minus_col <= (i - j) * KC
        biases = []
        for g in range(NSA_GROUPS):
            bias = jnp.concatenate(
                [jnp.broadcast_to(sel_ref[g, pl.ds(bpc * j + b, 1), :], (SLC_BLOCK, TQ))
                 for b in range(bpc)], axis=0)
            biases.append(jnp.where(visible, bias, NEG_INF))
        return off, biases

    def slc_score(ctx, h, dead):
        off, biases = ctx
        g = h // hpg
        return _scores(ks_ref[pl.ds(off, KC), gw * g:gw * (g + 1)], qrz_ref[h], dead) + biases[g]

    def slc_values(j, h):
        g = h // hpg
        return vst_ref[HV * g:HV * (g + 1), pl.ds(pl.multiple_of(j * KC, KC), KC)]

    def win_prep(j):
        off = pl.multiple_of(j * KC, KC)
        d = (i - j) * KC
        inside = jnp.logical_and(row_minus_col <= d, row_minus_col > d - WINDOW)
        return off, jnp.where(inside, 0.0, NEG_INF)

    def win_score(ctx, h, dead):
        off, bias = ctx
        g = h // hpg
        return _scores(kw_ref[pl.ds(off, KC), gw * g:gw * (g + 1)], qrz_ref[h], dead) + bias

    def win_values(j, h):
        g = h // hpg
        return vwt_ref[HV * g:HV * (g + 1), pl.ds(pl.multiple_of(j * KC, KC), KC)]

    n_live = (i + 1) * (TQ // CMP_STRIDE)
    if (nc // 2) % LANES == 0:
        pl.when(n_live <= nc // 2)(functools.partial(compressed_and_select, nc // 2))
        pl.when(n_live > nc // 2)(functools.partial(compressed_and_select, nc))
    else:
        compressed_and_select(nc)
    assert WINDOW == KC
    slc = _Stage(0, i, NSA_HEADS, HV, slc_prep, slc_score, slc_values, m_ref, al_ref, acc_ref,
                 dead_last="bl")
    win = _Stage(jnp.maximum(i - 1, 0), i, NSA_HEADS, HV, win_prep, win_score, win_values,
                 mw_ref, alw_ref, accw_ref, dead_first="tr", dead_last="bl")
    _attn_chain([slc, win], sbuf_ref, i)

    def head_out(ref, h):
        return ref[HV * h:HV * h + HEAD_DIM, :] / ref[HV * h + HEAD_DIM:HV * h + HEAD_DIM + 1, :]

    gb = _sigmoid(bg_ref[...])
    for hp in range(NSA_HEADS // 2):
        parts = []
        for h in (2 * hp, 2 * hp + 1):
            parts.append(gb[3 * h:3 * h + 1, :] * cmp_ref[HEAD_DIM * h:HEAD_DIM * (h + 1), :]
                         + gb[3 * h + 1:3 * h + 2, :] * head_out(acc_ref, h)
                         + gb[3 * h + 2:3 * h + 3, :] * head_out(accw_ref, h))
        o_ref[:, LANES * hp:LANES * (hp + 1)] = jnp.concatenate(parts, axis=0).T


def _nsa_call(qc, qcr, kcc, vct, ovt, ks, vst, kw, vwt, bgt, n_sel):
    B, S, W = qc.shape
    n_blk = S // SLC_BLOCK
    qspec = pl.BlockSpec((None, TQ, W), lambda b, i: (b, i, 0))
    return pl.pallas_call(
        functools.partial(_nsa_kernel, n_sel=n_sel),
        grid=(B, S // TQ),
        in_specs=[
            qspec, qspec,
            pl.BlockSpec((None,) + kcc.shape[1:], lambda b, i: (b, 0, 0, 0)),
            pl.BlockSpec((None,) + vct.shape[1:], lambda b, i: (b, 0, 0, 0)),
            pl.BlockSpec(ovt.shape, lambda b, i: (0, 0)),
            pl.BlockSpec((None, S, 2 * LANES), lambda b, i: (b, 0, 0)),
            pl.BlockSpec((None, NSA_GROUPS * HV, S), lambda b, i: (b, 0, 0)),
            pl.BlockSpec((None, S, 2 * LANES), lambda b, i: (b, 0, 0)),
            pl.BlockSpec((None, NSA_GROUPS * HV, S), lambda b, i: (b, 0, 0)),
            pl.BlockSpec((None, 3 * NSA_HEADS, TQ), lambda b, i: (b, 0, i)),
        ],
        out_specs=qspec,
        out_shape=jax.ShapeDtypeStruct((B, S, W), F32),
        scratch_shapes=[pltpu.VMEM((W, TQ), F32), pltpu.VMEM((NSA_GROUPS, n_blk, TQ), F32),
                        pltpu.VMEM((2, n_blk, TQ), F32), pltpu.VMEM((NSA_HEADS, TQ, LANES), BF16), pltpu.VMEM((NSA_HEADS, TQ, LANES), BF16)]
        + _attn_scratch(NSA_HEADS, HV)[1:] + _attn_scratch(NSA_HEADS, HV),
        compiler_params=pltpu.CompilerParams(
            dimension_semantics=("arbitrary", "arbitrary"), vmem_limit_bytes=VMEM_LIMIT),
        name="nsa_attn",
    )(qc, qcr, kcc, vct, ovt, ks, vst, kw, vwt, bgt)


def _post_kernel(x_ref, oa_ref, ob_ref, oc_ref, p_ref, g_ref, wg_ref, wup_ref, wout_ref,
                 pg_ref, wple_ref, wpg_ref, fg_ref, o_ref, *, final):
    x = x_ref[...]
    h = _rms(x, g_ref[...]).astype(BF16)
    mixed = jnp.zeros(x.shape, F32)
    for n, br_ref in enumerate((oa_ref, ob_ref, oc_ref)):
        gate = _dot(h, wg_ref[:, BRANCH_WIDTH * n:BRANCH_WIDTH * (n + 1)])
        og = br_ref[...] * (gate * _sigmoid(gate))
        up = _dot(og.astype(BF16), wup_ref[n])
        c0 = N_BRANCH * BRANCH_WIDTH + D_MODEL * n
        mixed = mixed + _sigmoid(_dot(h, wg_ref[:, c0:c0 + D_MODEL])) * up
    x1 = x + _dot(mixed.astype(BF16), wout_ref[...])
    emb = _dot(p_ref[...].astype(BF16), wple_ref[...])
    hg = _dot(_rms(x1, pg_ref[...]).astype(BF16), wpg_ref[...])
    x2 = x1 + emb * _sigmoid(hg)
    if final:
        x2 = _rms(x2, fg_ref[...])
    o_ref[...] = x2


def _post_call(x, oa, ob, oc, p, g, wg, wup, wout, pg, wple, wpg, fg, final, layer):
    T, D = x.shape
    tm = TM_POST
    tok = lambda w: pl.BlockSpec((tm, w), lambda i: (i, 0))
    full = lambda a: _layer_spec(a, layer, pipeline_mode=pl.Buffered(1))
    return pl.pallas_call(
        functools.partial(_post_kernel, final=final),
        grid=(T // tm,),
        in_specs=[tok(D), tok(BRANCH_WIDTH), tok(BRANCH_WIDTH), tok(BRANCH_WIDTH),
                  pl.BlockSpec((None, tm, PLE_DIM), lambda i: (layer, i, 0)),
                  full(g), full(wg), full(wup), full(wout), full(pg), full(wple), full(wpg),
                  pl.BlockSpec(fg.shape, lambda i: (0, 0))],
        out_specs=tok(D),
        out_shape=jax.ShapeDtypeStruct((T, D), F32),
        compiler_params=pltpu.CompilerParams(
            dimension_semantics=("arbitrary",), vmem_limit_bytes=VMEM_LIMIT),
        name="post_mix",
    )(x, oa, ob, oc, p, g, wg, wup, wout, pg, wple, wpg, fg)


ROPE_COLS = 3 * (ROT_DIM // 2)


def _rope_inputs(positions):
    half = ROT_DIM // 2
    inv = ROPE_THETA ** (-jnp.arange(0, ROT_DIM, 2, dtype=F32) / ROT_DIM)
    ang = positions.astype(F32)[..., None] * inv
    cs = jnp.concatenate([jnp.cos(ang), jnp.sin(ang), jnp.ones_like(ang)], axis=-1)
    place = np.zeros((ROPE_COLS, 3 * LANES), np.float32)
    for head in range(LANES // HEAD_DIM):
        base = HEAD_DIM * head
        for j in range(half):
            place[j, base + j] = 1.0
            place[j, base + half + j] = 1.0
            place[half + j, LANES + base + j] = -1.0
            place[half + j, 2 * LANES + base + half + j] = 1.0
        place[2 * half, base + ROT_DIM:base + HEAD_DIM] = 1.0
    return cs, jnp.asarray(place, dtype=BF16)


_NAT_PIECES = (
    (0, _A_Q, 512), (512, _A_K, 512), (1024, _B_Q, 512), (1536, _I_Q, 512), (2048, _C_Q, 512),
    (2560, _B_K, 64), (2624, _B_K, 64), (2688, _I_K, 64), (2752, _I_K, 64),
    (2816, _C_KS, 64), (2880, _C_KS, 64), (2944, _C_KS + 64, 64), (3008, _C_KS + 64, 64),
    (3072, _C_KW, 64), (3136, _C_KW, 64), (3200, _C_KW + 64, 64), (3264, _C_KW + 64, 64),
    (3328, _C_KC, 128), (3456, _C_VC, 128))
_GATE_PIECES = ((0, _A_G, 512), (512, _B_G, 512), (1024, _C_G, 512), (1536, _MERGE, 3 * D_MODEL))
_TR_PIECES = ((0, _A_V, 512), (512, _B_V, 64), (576, _C_VS, 128), (704, _C_VW, 128),
              (832, _I_W, 8), (840, _C_BG, 24))


def _relayout_kernel(wt_ref, nat_ref, gates_ref, trt_ref):
    for dst_ref, pieces in ((nat_ref, _NAT_PIECES), (gates_ref, _GATE_PIECES)):
        for d0, s0, n in pieces:
            for r in range(0, n, LANES):
                m = min(LANES, n - r)
                dst_ref[:, d0 + r:d0 + r + m] = wt_ref[s0 + r:s0 + r + m, :].T.astype(BF16)
    groups, cur = [], []
    for d0, s0, n in _TR_PIECES:
        if cur and d0 % ONES_ROWS == 0:
            groups.append(cur)
            cur = []
        cur.append((d0, s0, n))
    groups.append(cur)
    for grp in groups:
        rows = jnp.concatenate([wt_ref[s0:s0 + n, :] for _, s0, n in grp], axis=0)
        trt_ref[grp[0][0]:grp[0][0] + rows.shape[0], :] = rows.astype(BF16)


def _proj_weights(w_in):
    L, D, n_in = w_in.shape
    w_t = jnp.swapaxes(w_in, 1, 2)
    return pl.pallas_call(
        _relayout_kernel,
        grid=(L, D // LANES),
        in_specs=[pl.BlockSpec((None, n_in, LANES), lambda l, r: (l, 0, r))],
        out_specs=(pl.BlockSpec((None, LANES, N_NAT), lambda l, r: (l, r, 0)),
                   pl.BlockSpec((None, LANES, BRANCH_WIDTH * N_BRANCH + N_BRANCH * D_MODEL), lambda l, r: (l, r, 0)),
                   pl.BlockSpec((None, N_TR, LANES), lambda l, r: (l, 0, r))),
        out_shape=(jax.ShapeDtypeStruct((L, D, N_NAT), BF16),
                   jax.ShapeDtypeStruct((L, D, BRANCH_WIDTH * N_BRANCH + N_BRANCH * D_MODEL), BF16),
                   jax.ShapeDtypeStruct((L, N_TR, D), BF16)),
        compiler_params=pltpu.CompilerParams(
            dimension_semantics=("arbitrary", "arbitrary"), vmem_limit_bytes=VMEM_LIMIT),
        name="weight_relayout",
    )(w_t)


def _compress_weights(pe_k, w1_k, w2_k, pe_v, w1_v, w2_v):
    L = pe_k.shape[0]
    half = CMP_LEN // 2
    kdim = half * NSA_GROUPS * HEAD_DIM
    pe = jnp.stack([pe_k, pe_v], axis=1)
    pe = jnp.broadcast_to(pe[:, :, :, None, :], (L, 2, CMP_LEN, NSA_GROUPS, HEAD_DIM))
    w1 = jnp.stack([w1_k, w1_v], axis=1).astype(BF16)
    zero = jnp.zeros_like(w1)
    w1g = jnp.stack([jnp.stack([w1 if h == g else zero for h in range(NSA_GROUPS)], axis=3)
                     for g in range(NSA_GROUPS)], axis=2)
    w1g = w1g.reshape(L, 2 * NSA_GROUPS, CMP_LEN, NSA_GROUPS * HEAD_DIM, CMP_HIDDEN)
    w2 = jnp.stack([w2_k, w2_v], axis=1).astype(BF16)
    return (pe[:, :, :half].reshape(L, 2, 1, kdim), pe[:, :, half:].reshape(L, 2, 1, kdim),
            w1g[:, :, :half].reshape(L, 2 * NSA_GROUPS, kdim, CMP_HIDDEN),
            w1g[:, :, half:].reshape(L, 2 * NSA_GROUPS, kdim, CMP_HIDDEN), w2)


def _overlap_t(S):
    nc = S // CMP_STRIDE
    n_blk = S // SLC_BLOCK
    cstart = np.arange(nc) * CMP_STRIDE
    sstart = np.arange(n_blk) * SLC_BLOCK
    ov = ((cstart[None, :] < sstart[:, None] + SLC_BLOCK) & (cstart[None, :] + CMP_LEN > sstart[:, None]))
    return jnp.asarray(ov.astype(np.float32), dtype=BF16)


def kernel(x, p, positions, norm_g, w_in, diff_lambda, diff_subln_g, cmp_pe_k, cmp_w1_k, cmp_w2_k,
           cmp_pe_v, cmp_w1_v, cmp_w2_v, w_up, w_out, ple_norm_g, w_ple, w_ple_gate, final_norm_g):
    B, S, D = x.shape
    depth = w_in.shape[0]
    k_sel = min(DSA_TOPK_MAX, S // 4)
    n_sel = min(SLC_TOPN_MAX, S // SLC_BLOCK)
    assert S % TM_PROJ == 0 and S % TQ == 0 and k_sel <= TQ and D == D_MODEL
    assert KC == TQ and WINDOW % KC == 0
    cs, place = _rope_inputs(positions)
    ovt = _overlap_t(S)

    wn, wg, wt = _proj_weights(w_in)
    cmp_w = _compress_weights(cmp_pe_k, cmp_w1_k, cmp_w2_k, cmp_pe_v, cmp_w1_v, cmp_w2_v)
    g_in = norm_g.reshape(depth, 1, D)
    g_ple = ple_norm_g.reshape(depth, 1, D)
    g_sub = diff_subln_g.reshape(depth, -1, 1)
    wup, wout, wple, wpg = (w.astype(BF16) for w in (w_up, w_out, w_ple, w_ple_gate))
    p2 = p.reshape(depth, B * S, PLE_DIM)

    for i in range(depth):
        lam_init = 0.8 - 0.6 * math.exp(-0.3 * i)
        (qa, ka, qb, iq, qc, qcr, kb, ik, ks, kw, ck, cv,
         vat, vbt, vst, vwt, iwt, bgt) = _proj_call(x, g_in, wn, wt, cs, place, i)

        kcc, vct = _compress_call(ck, cv, *cmp_w, i)

        oa = _diff_call(qa, ka, vat, diff_lambda, g_sub, lam_init, i)
        ob = _dsa_call(qb, iq, kb, ik, vbt, iwt, k_sel)
        oc = _nsa_call(qc, qcr, kcc, vct, ovt, ks, vst, kw, vwt, bgt, n_sel)

        x = _post_call(
            x.reshape(B * S, D), oa.reshape(B * S, -1), ob.reshape(B * S, -1), oc.reshape(B * S, -1),
            p2, g_in, wg, wup, wout, g_ple, wple, wpg, final_norm_g.reshape(1, D),
            final=(i == depth - 1), layer=i).reshape(B, S, D)
    return x
```

```python
import functools
import math

import numpy as np
import jax
import jax.numpy as jnp
from jax import lax
from jax.experimental import pallas as pl
from jax.experimental.pallas import tpu as pltpu

F32 = jnp.float32
BF16 = jnp.bfloat16

D_MODEL = 1024
HEAD_DIM = 64
ROT_DIM = 16
ROPE_THETA = 500000.0
EPS = 1e-6
PLE_DIM = 256
DA_HEADS = 4
DSA_HEADS = 8
IDX_HEADS = 8
DSA_TOPK_MAX = 256
NSA_GROUPS = 2
NSA_HEADS = 8
CMP_LEN = 32
CMP_STRIDE = 16
CMP_HIDDEN = 128
SLC_BLOCK = 64
SLC_TOPN_MAX = 16
WINDOW = 512
FORCE_SCORE = 1e4
BRANCH_WIDTH = 512
N_BRANCH = 3

IN_SPLITS = (512, 512, 512, 512, 512, 64, 64, 512, 512, 64, 8,
             512, 128, 128, 128, 128, 128, 128, 512, 24, 3 * D_MODEL)
(_A_Q, _A_K, _A_V, _A_G, _B_Q, _B_K, _B_V, _B_G, _I_Q, _I_K, _I_W,
 _C_Q, _C_KC, _C_VC, _C_KS, _C_VS, _C_KW, _C_VW, _C_G, _C_BG, _MERGE) = [
    int(v) for v in np.concatenate([[0], np.cumsum(IN_SPLITS)[:-1]])]

LANES = 128
SUBLANES = 8
TQ = 512
KC = 512
TM_PROJ = 512
TM_POST = 512
NEG_INF = float("-inf")
M_INIT = -1e30
N_NAT = 3584
N_TR = 864
BISECT_PLAIN = 9
ONES_ROWS = 16
HV = HEAD_DIM + ONES_ROWS
HV_A = 2 * HEAD_DIM + ONES_ROWS
LOG2E = math.log2(math.e)
VMEM_LIMIT = 56 * 1024 * 1024


def _dot_nt(a, b):
    return lax.dot_general(a, b, (((1,), (1,)), ((), ())), preferred_element_type=F32)


def _dot(a, b):
    return jnp.dot(a, b, preferred_element_type=F32)


def _rms(x, g):
    return x * lax.rsqrt(jnp.mean(x * x, axis=-1, keepdims=True) + EPS) * g


def _sigmoid(x):
    return 1.0 / (1.0 + jnp.exp(-x))


def _store_half_masked(src_ref, dst_ref, n_pairs):
    lane = lax.broadcasted_iota(jnp.int32, (src_ref.shape[0], LANES), 1)
    for hp in range(n_pairs):
        x = src_ref[:, LANES * hp:LANES * (hp + 1)]
        z = jnp.zeros_like(x)
        dst_ref[2 * hp] = jnp.where(lane < HEAD_DIM, x, z)
        dst_ref[2 * hp + 1] = jnp.where(lane >= HEAD_DIM, x, z)


def _fold_rows(x, op):
    n_acc = 2
    accs = [x[SUBLANES * r:SUBLANES * (r + 1)] for r in range(n_acc)]
    for r in range(n_acc, x.shape[0] // SUBLANES):
        accs[r % n_acc] = op(accs[r % n_acc], x[SUBLANES * r:SUBLANES * (r + 1)])
    return op(accs[0], accs[1])


HALF = KC // 2


def _scores(k, q, dead=None):
    if dead is None:
        return _dot_nt(k, q)
    neg = jnp.full((HALF, HALF), NEG_INF, F32)
    if dead == "bl":
        return jnp.concatenate(
            [_dot_nt(k[:HALF], q), jnp.concatenate([neg, _dot_nt(k[HALF:], q[HALF:])], axis=1)], axis=0)
    return jnp.concatenate(
        [jnp.concatenate([_dot_nt(k[:HALF], q[:HALF]), neg], axis=1), _dot_nt(k[HALF:], q)], axis=0)


def _weighted_values(v_t, p, dead=None):
    if dead is None:
        return _dot(v_t, p)
    if dead == "bl":
        return jnp.concatenate([_dot(v_t[:, :HALF], p[:HALF, :HALF]), _dot(v_t, p[:, HALF:])], axis=1)
    return jnp.concatenate([_dot(v_t, p[:, :HALF]), _dot(v_t[:, HALF:], p[HALF:, HALF:])], axis=1)


class _Stage:
    def __init__(self, j0, j1, nh, hv, prep, score, values, m_ref, al_ref, acc_ref,
                 dead_first=None, dead_last=None):
        self.j0 = jnp.asarray(j0, jnp.int32)
        self.j1 = jnp.asarray(j1, jnp.int32)
        self.nh, self.hv, self.prep, self.score, self.values = nh, hv, prep, score, values
        self.m_ref, self.al_ref, self.acc_ref = m_ref, al_ref, acc_ref
        self.dead_first, self.dead_last = dead_first, dead_last

    def reset(self):
        self.m_ref[...] = jnp.full(self.m_ref.shape, M_INIT, F32)
        self.acc_ref[...] = jnp.zeros(self.acc_ref.shape, F32)

    def produce(self, sbuf_ref, j, dead=None):
        ctx = self.prep(j)
        for h in range(self.nh):
            s = self.score(ctx, h, dead)
            m_old = self.m_ref[h:h + 1, :]
            m_new = jnp.maximum(m_old, jnp.max(s, axis=0, keepdims=True))
            sbuf_ref[h] = s
            self.m_ref[h:h + 1, :] = m_new
            self.al_ref[h:h + 1, :] = jnp.exp2(m_old - m_new)

    def consume(self, sbuf_ref, j, dead=None):
        for h in range(self.nh):
            p = jnp.exp2(sbuf_ref[h] - self.m_ref[h:h + 1, :])
            rows = slice(self.hv * h, self.hv * (h + 1))
            self.acc_ref[rows, :] = (self.al_ref[h:h + 1, :] * self.acc_ref[rows, :]
                                     + _weighted_values(self.values(j, h), p.astype(BF16), dead))

    def loop(self, sbuf_ref, lo, hi):
        def body(j, carry):
            self.consume(sbuf_ref, j - 1)
            self.produce(sbuf_ref, j)
            return carry
        lax.fori_loop(lo, hi, body, 0)


def _attn_chain(stages, sbuf_ref, tile):
    for st in stages:
        st.reset()

    @pl.when(tile == 0)
    def _():
        for st in stages:
            st.produce(sbuf_ref, st.j0)
            st.consume(sbuf_ref, st.j0)

    @pl.when(tile > 0)
    def _():
        stages[0].produce(sbuf_ref, stages[0].j0, stages[0].dead_first)
        for n, st in enumerate(stages):
            st.loop(sbuf_ref, st.j0 + 1, st.j1)
            st.consume(sbuf_ref, st.j1 - 1, st.dead_first)
            st.produce(sbuf_ref, st.j1, st.dead_last)
            st.consume(sbuf_ref, st.j1, st.dead_last)
            if n + 1 < len(stages):
                stages[n + 1].produce(sbuf_ref, stages[n + 1].j0, stages[n + 1].dead_first)


def _attn_scratch(nh, hv):
    return [pltpu.VMEM((nh, KC, TQ), F32), pltpu.VMEM((nh, TQ), F32),
            pltpu.VMEM((nh, TQ), F32), pltpu.VMEM((nh * hv, TQ), F32)]


def _layer_spec(a, layer, **kw):
    tail = (0,) * (a.ndim - 1)
    return pl.BlockSpec((None,) + a.shape[1:], lambda *_: (layer,) + tail, **kw)


def _ones_rows(n_cols):
    r = lax.broadcasted_iota(jnp.int32, (ONES_ROWS, n_cols), 0)
    return jnp.where(r == 0, 1.0, 0.0).astype(BF16)


def _proj_kernel(x_ref, g_ref, wn_ref, wt_ref, cs_ref, place_ref,
                 qa_ref, ka_ref, qb_ref, iq_ref, qc_ref, qcr_ref, kb_ref, ik_ref,
                 ks_ref, kw_ref, ck_ref, cv_ref,
                 vat_ref, vbt_ref, vst_ref, vwt_ref, iwt_ref, bgt_ref, zc_ref):
    h = _rms(x_ref[...], g_ref[...]).astype(BF16)
    cs = cs_ref[...]
    hi = cs.astype(BF16)
    rest = cs - hi.astype(F32)
    mid = rest.astype(BF16)
    low = (rest - mid.astype(F32)).astype(BF16)
    spread = _dot(hi, place_ref[...]) + _dot(mid, place_ref[...]) + _dot(low, place_ref[...])
    rc, rs1, rs2 = spread[:, 0:LANES], spread[:, LANES:2 * LANES], spread[:, 2 * LANES:3 * LANES]

    def rope(z):
        return z * rc + pltpu.roll(z, LANES - ROT_DIM // 2, 1) * rs1 + pltpu.roll(z, ROT_DIM // 2, 1) * rs2

    qscale = HEAD_DIM ** -0.5
    qscale2 = qscale * LOG2E
    segs = (
        (0, 512, ((qa_ref, True, qscale2),)),
        (512, 512, ((ka_ref, True, 1.0),)),
        (1024, 512, ((qb_ref, True, qscale2),)),
        (1536, 512, ((iq_ref, True, qscale),)),
        (2048, 512, ((qc_ref, False, qscale), (qcr_ref, True, qscale2))),
        (2560, 128, ((kb_ref, True, 1.0),)),
        (2688, 128, ((ik_ref, True, 1.0),)),
        (2816, 256, ((ks_ref, True, 1.0),)),
        (3072, 256, ((kw_ref, True, 1.0),)),
    )
    zc = _dot(h, wn_ref[:, 3328:3328 + 2 * LANES])
    half_len = CMP_LEN // 2
    for n, c_ref in enumerate((ck_ref, cv_ref)):
        zc_ref[n] = zc[:, LANES * n:LANES * (n + 1)]
        for l in range(half_len):
            c_ref[:, LANES * l:LANES * (l + 1)] = zc_ref[n, pl.ds(l, zc.shape[0] // half_len, stride=half_len), :]
    for c0, width, outs in segs:
        z = _dot(h, wn_ref[:, c0:c0 + width])
        for out_ref, rot, scale in outs:
            for j in range(width // LANES):
                zj = z[:, LANES * j:LANES * (j + 1)]
                if rot:
                    zj = rope(zj)
                if scale != 1.0:
                    zj = zj * scale
                out_ref[:, LANES * j:LANES * (j + 1)] = zj.astype(out_ref.dtype)

    zt = _dot_nt(wt_ref[...], h)
    ones = _ones_rows(zt.shape[1])
    vd = 2 * HEAD_DIM
    for h in range(DA_HEADS):
        vat_ref[HV_A * h:HV_A * h + vd, :] = zt[vd * h:vd * (h + 1)].astype(BF16)
        vat_ref[HV_A * h + vd:HV_A * (h + 1), :] = ones
    vbt_ref[0:HEAD_DIM, :] = zt[512:576].astype(BF16)
    vbt_ref[HEAD_DIM:HV, :] = ones
    for g in range(NSA_GROUPS):
        for ref, base in ((vst_ref, 576), (vwt_ref, 704)):
            ref[HV * g:HV * g + HEAD_DIM, :] = zt[base + HEAD_DIM * g:base + HEAD_DIM * (g + 1)].astype(BF16)
            ref[HV * g + HEAD_DIM:HV * (g + 1), :] = ones
    iwt_ref[...] = zt[832:840] * (IDX_HEADS ** -0.5)
    bgt_ref[...] = zt[840:864]


def _proj_call(x, g, wn, wt, cs, place, layer):
    B, S, D = x.shape
    tm = TM_PROJ
    nat = lambda w: pl.BlockSpec((None, tm, w), lambda b, i: (b, i, 0))
    tr = lambda r: pl.BlockSpec((None, r, tm), lambda b, i: (b, 0, i))
    full = lambda a: _layer_spec(a, layer)
    sds = jax.ShapeDtypeStruct
    half_len = CMP_LEN // 2
    out_shape = (
        sds((B, S, 512), BF16), sds((B, S, 512), BF16), sds((B, S, 512), BF16), sds((B, S, 512), BF16),
        sds((B, S, 512), BF16), sds((B, S, 512), BF16), sds((B, S, 128), BF16), sds((B, S, 128), BF16),
        sds((B, S, 256), BF16), sds((B, S, 256), BF16),
        sds((B, S // half_len, half_len * LANES), F32), sds((B, S // half_len, half_len * LANES), F32),
        sds((B, DA_HEADS * HV_A, S), BF16), sds((B, HV, S), BF16),
        sds((B, NSA_GROUPS * HV, S), BF16), sds((B, NSA_GROUPS * HV, S), BF16),
        sds((B, IDX_HEADS, S), F32), sds((B, 3 * NSA_HEADS, S), F32),
    )
    cmp_rows = pl.BlockSpec((None, tm // half_len, half_len * LANES), lambda b, i: (b, i, 0))
    out_specs = (nat(512), nat(512), nat(512), nat(512), nat(512), nat(512), nat(128), nat(128),
                 nat(256), nat(256), cmp_rows, cmp_rows,
                 tr(DA_HEADS * HV_A), tr(HV), tr(NSA_GROUPS * HV), tr(NSA_GROUPS * HV),
                 tr(IDX_HEADS), tr(3 * NSA_HEADS))
    return pl.pallas_call(
        _proj_kernel,
        grid=(B, S // tm),
        in_specs=[nat(D), full(g), full(wn), full(wt), nat(ROPE_COLS),
                  pl.BlockSpec(place.shape, lambda b, i: (0, 0))],
        out_specs=out_specs,
        out_shape=out_shape,
        scratch_shapes=[pltpu.VMEM((2, tm, LANES), F32)],
        compiler_params=pltpu.CompilerParams(
            dimension_semantics=("arbitrary", "arbitrary"), vmem_limit_bytes=VMEM_LIMIT),
        name="in_proj",
    )(x, g, wn, wt, cs, place)


def _compress_kernel(rk_ref, rv_ref, pea_ref, peb_ref, w1a_ref, w1b_ref, w2_ref, nat_ref, t_ref):
    for kind, r_ref in enumerate((rk_ref, rv_ref)):
        r = r_ref[...]
        xa = (r + pea_ref[kind]).astype(BF16)
        xb = (r + peb_ref[kind]).astype(BF16)
        nrow = r.shape[0]
        for g in range(NSA_GROUPS):
            n = NSA_GROUPS * kind + g
            hid = jax.nn.gelu(_dot(xa, w1a_ref[n]) + pltpu.roll(_dot(xb, w1b_ref[n]), nrow - 1, 0))
            o = _dot(hid.astype(BF16), w2_ref[kind])
            nat_ref[n] = jnp.concatenate([o, o], axis=1).astype(BF16)
            t_ref[n] = o.T.astype(BF16)


def _compress_call(rk, rv, pea, peb, w1a, w1b, w2, layer):
    B, nr, kdim = rk.shape
    full = lambda a: _layer_spec(a, layer)
    rspec = pl.BlockSpec((None, nr, kdim), lambda b: (b, 0, 0))
    return pl.pallas_call(
        _compress_kernel,
        grid=(B,),
        in_specs=[rspec, rspec, full(pea), full(peb), full(w1a), full(w1b), full(w2)],
        out_specs=(
            pl.BlockSpec((None, 2 * NSA_GROUPS, nr, 2 * HEAD_DIM), lambda b: (b, 0, 0, 0)),
            pl.BlockSpec((None, 2 * NSA_GROUPS, HEAD_DIM, nr), lambda b: (b, 0, 0, 0)),
        ),
        out_shape=(jax.ShapeDtypeStruct((B, 2 * NSA_GROUPS, nr, 2 * HEAD_DIM), BF16),
                   jax.ShapeDtypeStruct((B, 2 * NSA_GROUPS, HEAD_DIM, nr), BF16)),
        compiler_params=pltpu.CompilerParams(dimension_semantics=("arbitrary",)),
        name="nsa_compress",
    )(rk, rv, pea, peb, w1a, w1b, w2)


def _diff_kernel(q_ref, k_ref, vt_ref, lam_ref, sg_ref, o_ref,
                 qz_ref, sbuf_ref, m_ref, al_ref, acc_ref, *, lam_init):
    i = pl.program_id(1)
    lp = lam_ref[...]
    lam = (jnp.exp(jnp.sum(lp[0:1] * lp[1:2], axis=1, keepdims=True))
           - jnp.exp(jnp.sum(lp[2:3] * lp[3:4], axis=1, keepdims=True)) + lam_init)
    row_minus_col = (lax.broadcasted_iota(jnp.int32, (KC, TQ), 0)
                     - lax.broadcasted_iota(jnp.int32, (KC, TQ), 1))
    _store_half_masked(q_ref, qz_ref, DA_HEADS)
    vdim = 2 * HEAD_DIM
    n_maps = 2 * DA_HEADS

    def prep(j):
        return pl.multiple_of(j * KC, KC), jnp.where(row_minus_col <= (i - j) * KC, 0.0, NEG_INF)

    def score(ctx, hm, dead):
        off, bias = ctx
        h = hm // 2
        return _scores(k_ref[pl.ds(off, KC), vdim * h:vdim * (h + 1)], qz_ref[hm], dead) + bias

    def values(j, hm):
        h = hm // 2
        return vt_ref[HV_A * h:HV_A * (h + 1), pl.ds(pl.multiple_of(j * KC, KC), KC)]

    _attn_chain([_Stage(0, i, n_maps, HV_A, prep, score, values, m_ref, al_ref, acc_ref, dead_last="bl")],
                sbuf_ref, i)

    for h in range(DA_HEADS):
        r0, r1 = HV_A * 2 * h, HV_A * (2 * h + 1)
        o_t = (acc_ref[r0:r0 + vdim, :] / acc_ref[r0 + vdim:r0 + vdim + 1, :]
               - lam * (acc_ref[r1:r1 + vdim, :] / acc_ref[r1 + vdim:r1 + vdim + 1, :]))
        ms = jnp.mean(o_t * o_t, axis=0, keepdims=True)
        y = o_t * lax.rsqrt(ms + EPS) * sg_ref[...] * (1.0 - lam_init)
        o_ref[:, vdim * h:vdim * (h + 1)] = y.T


def _diff_call(qa, ka, vat, lam_p, sg, lam_init, layer):
    B, S, W = qa.shape
    n_maps = 2 * DA_HEADS
    return pl.pallas_call(
        functools.partial(_diff_kernel, lam_init=lam_init),
        grid=(B, S // TQ),
        in_specs=[
            pl.BlockSpec((None, TQ, W), lambda b, i: (b, i, 0)),
            pl.BlockSpec((None, S, W), lambda b, i: (b, 0, 0)),
            pl.BlockSpec((None, DA_HEADS * HV_A, S), lambda b, i: (b, 0, 0)),
            _layer_spec(lam_p, layer),
            _layer_spec(sg, layer),
        ],
        out_specs=pl.BlockSpec((None, TQ, W), lambda b, i: (b, i, 0)),
        out_shape=jax.ShapeDtypeStruct((B, S, W), F32),
        scratch_shapes=[pltpu.VMEM((n_maps, TQ, LANES), BF16)] + _attn_scratch(n_maps, HV_A),
        compiler_params=pltpu.CompilerParams(
            dimension_semantics=("arbitrary", "arbitrary"), vmem_limit_bytes=VMEM_LIMIT),
        name="diff_attn",
    )(qa, ka, vat, lam_p, sg)


def _dsa_kernel(q_ref, iq_ref, kb_ref, ik_ref, vt_ref, iw_ref, o_ref,
                sc_ref, tj_ref, tri_ref, qz_ref, sbuf_ref, m_ref, al_ref, acc_ref, *, k_sel, seq_len):
    i = pl.program_id(1)
    nch = i + 1
    row = lax.broadcasted_iota(jnp.int32, (KC, TQ), 0)
    col = lax.broadcasted_iota(jnp.int32, (KC, TQ), 1)
    causal = row <= col
    iw = iw_ref[...]
    kf = float(k_sel)
    _store_half_masked(iq_ref, qz_ref, IDX_HEADS // 2)

    def idx_chunk(j, carry, masked):
        mx, mn = carry
        off = pl.multiple_of(j * KC, KC)
        ikc = ik_ref[pl.ds(off, KC), :]
        sc = jnp.zeros((KC, TQ), F32)
        for h in range(IDX_HEADS):
            dots = _scores(ikc, qz_ref[h], "bl" if masked else None)
            sc = sc + iw[h:h + 1, :] * jnp.maximum(dots, 0.0)
        lo_src = sc
        if masked:
            lo_src = jnp.where(causal, sc, jnp.inf)
            sc = jnp.where(causal, sc, NEG_INF)
        sc_ref[pl.ds(off, KC), :] = sc
        return (jnp.maximum(mx, _fold_rows(sc, jnp.maximum)),
                jnp.minimum(mn, _fold_rows(lo_src, jnp.minimum)))

    carry = (jnp.full((SUBLANES, TQ), NEG_INF, F32), jnp.full((SUBLANES, TQ), jnp.inf, F32))
    carry = lax.fori_loop(0, i, functools.partial(idx_chunk, masked=False), carry)
    mx8, mn8 = idx_chunk(i, carry, True)
    row_max = jnp.max(mx8, axis=0, keepdims=True)
    row_min = jnp.min(mn8, axis=0, keepdims=True)

    def probe_pass(th, snap):
        thb = jnp.broadcast_to(th, (SUBLANES, TQ))

        def body(j, c):
            off = pl.multiple_of(j * KC, KC)
            c = list(c)
            for r in range(KC // SUBLANES):
                x = sc_ref[pl.ds(off + SUBLANES * r, SUBLANES), :]
                ge = x >= thb
                c[0] = c[0] + jnp.where(ge, 1.0, 0.0)
                if snap:
                    c[1] = jnp.maximum(c[1], jnp.where(ge, NEG_INF, x))
            return tuple(c)
        init = (jnp.zeros((SUBLANES, TQ), F32),)
        if snap:
            init += (jnp.full((SUBLANES, TQ), NEG_INF, F32),)
        out = lax.fori_loop(0, nch, body, init)
        cnt = jnp.sum(out[0], axis=0, keepdims=True)
        return (cnt, jnp.max(out[1], axis=0, keepdims=True)) if snap else cnt

    n_causal = (i * TQ + 1 + lax.broadcasted_iota(jnp.int32, (1, TQ), 1)).astype(F32)
    keep_all = n_causal <= kf

    def bisect(_, c):
        lo, hi, clo, chi = c
        mid = lo + (hi - lo) * 0.5
        cnt = probe_pass(mid, False)
        ge = cnt >= kf
        return (jnp.where(ge, mid, lo), jnp.where(ge, hi, mid),
                jnp.where(ge, cnt, clo), jnp.where(ge, chi, cnt))

    hi0 = row_max + jnp.abs(row_max) * 1e-6 + 1e-30
    lo, hi, clo, chi = lax.fori_loop(
        0, BISECT_PLAIN, bisect, (row_min, hi0, n_causal, jnp.zeros((1, TQ), F32)))

    def cond(st):
        return jnp.logical_and(st[0] < 256, st[-1] > 0.0)

    def snap_step(st):
        it, lo, hi, clo, chi, hie, known, stuck, done, _ = st
        top = jnp.where(known > 0.0, hie, hi)
        mid = lo + (top - lo) * 0.5
        inside = jnp.logical_and(mid > lo, mid < top)
        near_top = jnp.logical_or(kf - chi <= 2.0, stuck > 0.0)
        use_top = jnp.logical_or(jnp.logical_and(known > 0.0, near_top), jnp.logical_not(inside))
        th = jnp.where(use_top, top, mid)
        cnt, edn = probe_pass(th, True)
        live = done <= 0.0
        ge = jnp.logical_and(live, cnt >= kf)
        lt = jnp.logical_and(live, cnt < kf)
        stuck = jnp.where(jnp.logical_and(ge, cnt == clo), 1.0, 0.0)
        lo = jnp.where(ge, th, lo)
        clo = jnp.where(ge, cnt, clo)
        hi = jnp.where(lt, th, hi)
        chi = jnp.where(lt, cnt, chi)
        hie = jnp.where(lt, edn, hie)
        known = jnp.where(lt, 1.0, known)
        fin = jnp.logical_or(clo == kf, jnp.logical_and(known > 0.0, lo >= hie))
        done = jnp.where(fin, 1.0, done)
        return it + 1, lo, hi, clo, chi, hie, known, stuck, done, jnp.max(1.0 - done)

    zero = jnp.zeros((1, TQ), F32)
    done0 = jnp.where(jnp.logical_or(clo == kf, keep_all), 1.0, 0.0)
    st = lax.while_loop(cond, snap_step, (jnp.int32(0), lo, hi, clo, chi, jnp.full((1, TQ), NEG_INF, F32),
                                          zero, zero, done0, jnp.max(1.0 - done0)))
    thr, clo, chi = st[1], st[3], st[4]
    need = jnp.where(clo == kf, float(seq_len), kf - chi)
    thr = jnp.where(keep_all, NEG_INF, thr)
    need = jnp.where(keep_all, 0.0, need)
    tj_ref[2:3, :] = jnp.zeros((1, TQ), F32)

    _store_half_masked(q_ref, qz_ref, DSA_HEADS // 2)
    half = KC // 2
    tri_ref[...] = jnp.where(lax.broadcasted_iota(jnp.int32, (half, half), 0)
                             >= lax.broadcasted_iota(jnp.int32, (half, half), 1), 1.0, 0.0).astype(BF16)

    def prep(j, ties):
        off = pl.multiple_of(j * KC, KC)
        x = sc_ref[pl.ds(off, KC), :]
        if not ties:
            return kb_ref[pl.ds(off, KC), :], jnp.where(x >= thr, 0.0, NEG_INF)
        eq = x == thr
        e = jnp.where(eq, 1.0, 0.0).astype(BF16)
        r_top = _dot(tri_ref[...], e[:half]) + tj_ref[2:3, :]
        r_bot = _dot(tri_ref[...], e[half:]) + r_top[half - 1:half, :]
        tj_ref[2:3, :] = r_bot[half - 1:half, :]
        rank = jnp.concatenate([r_top, r_bot], axis=0)
        tie_bias = jnp.where(jnp.logical_and(eq, rank <= need), 0.0, NEG_INF)
        return kb_ref[pl.ds(off, KC), :], jnp.where(x > thr, 0.0, tie_bias)

    def score(ctx, h, dead):
        kc, bias = ctx
        return _scores(kc, qz_ref[h], dead) + bias

    def values(j, h):
        return vt_ref[:, pl.ds(pl.multiple_of(j * KC, KC), KC)]

    def attend(ties):
        _attn_chain([_Stage(0, i, DSA_HEADS, HV, functools.partial(prep, ties=ties), score, values,
                            m_ref, al_ref, acc_ref, dead_last="bl")], sbuf_ref, i)

    any_ties = jnp.max(jnp.where(need < float(seq_len), 1.0, 0.0)) > 0.0
    pl.when(any_ties)(functools.partial(attend, True))
    pl.when(jnp.logical_not(any_ties))(functools.partial(attend, False))
    for hp in range(DSA_HEADS // 2):
        blk = jnp.concatenate(
            [acc_ref[HV * h:HV * h + HEAD_DIM, :] / acc_ref[HV * h + HEAD_DIM:HV * h + HEAD_DIM + 1, :]
             for h in (2 * hp, 2 * hp + 1)], axis=0)
        o_ref[:, LANES * hp:LANES * (hp + 1)] = blk.T


def _dsa_call(qb, iq, kb, ik, vbt, iwt, k_sel):
    B, S, W = qb.shape
    return pl.pallas_call(
        functools.partial(_dsa_kernel, k_sel=k_sel, seq_len=S),
        grid=(B, S // TQ),
        in_specs=[
            pl.BlockSpec((None, TQ, W), lambda b, i: (b, i, 0)),
            pl.BlockSpec((None, TQ, W), lambda b, i: (b, i, 0)),
            pl.BlockSpec((None, S, LANES), lambda b, i: (b, 0, 0)),
            pl.BlockSpec((None, S, LANES), lambda b, i: (b, 0, 0)),
            pl.BlockSpec((None, HV, S), lambda b, i: (b, 0, 0)),
            pl.BlockSpec((None, IDX_HEADS, TQ), lambda b, i: (b, 0, i)),
        ],
        out_specs=pl.BlockSpec((None, TQ, W), lambda b, i: (b, i, 0)),
        out_shape=jax.ShapeDtypeStruct((B, S, W), F32),
        scratch_shapes=[pltpu.VMEM((S, TQ), F32), pltpu.VMEM((SUBLANES, TQ), F32),
                        pltpu.VMEM((KC // 2, KC // 2), BF16),
                        pltpu.VMEM((DSA_HEADS, TQ, LANES), BF16)] + _attn_scratch(DSA_HEADS, HV),
        compiler_params=pltpu.CompilerParams(
            dimension_semantics=("arbitrary", "arbitrary"), vmem_limit_bytes=VMEM_LIMIT),
        name="dsa_attn",
    )(qb, iq, kb, ik, vbt, iwt)


def _rank_tile(rk_ref, m, lanes):
    rows = lambda t: slice(SUBLANES * t, SUBLANES * (t + 1))
    own = rk_ref[0, rows(m), lanes]
    sub = lax.broadcasted_iota(jnp.int32, own.shape, 0)
    own_rows = [jnp.broadcast_to(own[r:r + 1], own.shape) for r in range(SUBLANES)]
    rank = jnp.zeros(own.shape, F32)
    for r, x in enumerate(own_rows):
        ahead = jnp.logical_or(x > own, jnp.logical_and(x == own, sub > r))
        rank = rank + jnp.where(ahead, 1.0, 0.0)
    for t in range(m):
        older = rk_ref[0, rows(t), lanes]
        rank_t = rk_ref[1, rows(t), lanes]
        for r in range(SUBLANES):
            rank = rank + jnp.where(jnp.broadcast_to(older[r:r + 1], own.shape) >= own, 1.0, 0.0)
            rank_t = rank_t + jnp.where(own_rows[r] > older, 1.0, 0.0)
        rk_ref[1, rows(t), lanes] = rank_t
    rk_ref[1, rows(m), lanes] = rank


def _nsa_kernel(q_ref, qr_ref, kc_ref, vct_ref, ovt_ref, ks_ref, vst_ref, kw_ref, vwt_ref, bg_ref,
                o_ref, cmp_ref, sel_ref, rk_ref, qz_ref, qrz_ref, mw_ref, alw_ref, accw_ref,
                sbuf_ref, m_ref, al_ref, acc_ref, *, n_sel):
    i = pl.program_id(1)
    row_minus_col = (lax.broadcasted_iota(jnp.int32, (KC, TQ), 0)
                     - lax.broadcasted_iota(jnp.int32, (KC, TQ), 1))
    tq = i * TQ + lax.broadcasted_iota(jnp.int32, (1, TQ), 1)
    hpg = NSA_HEADS // NSA_GROUPS
    nc = kc_ref.shape[1]
    n_blk = sel_ref.shape[1]
    gw = 2 * HEAD_DIM

    _store_half_masked(q_ref, qz_ref, NSA_HEADS // 2)
    _store_half_masked(qr_ref, qrz_ref, NSA_HEADS // 2)

    def compressed_and_select(rows):
        cmp_valid = CMP_STRIDE * lax.broadcasted_iota(jnp.int32, (rows, TQ), 0) + (CMP_LEN - 1) <= tq
        blk = lax.broadcasted_iota(jnp.int32, (n_blk, TQ), 0)
        forced = jnp.logical_or(blk == jnp.right_shift(tq, int(math.log2(SLC_BLOCK))), blk == 0)
        blk_causal = blk * SLC_BLOCK <= tq
        for h in range(NSA_HEADS):
            s = jnp.where(cmp_valid, _dot_nt(kc_ref[h // hpg, 0:rows, :], qz_ref[h]), NEG_INF)
            sbuf_ref[h, 0:rows, :] = s
            m_ref[h:h + 1, :] = jnp.maximum(jnp.max(s, axis=0, keepdims=True), M_INIT)
        for g in range(NSA_GROUPS):
            vcg = vct_ref[NSA_GROUPS + g, :, 0:rows]
            imp = jnp.zeros((n_blk, TQ), F32)
            for hh in range(hpg):
                h = hpg * g + hh
                p = jnp.exp(sbuf_ref[h, 0:rows, :] - m_ref[h:h + 1, :])
                l = jnp.sum(p, axis=0, keepdims=True)
                pc = (p * jnp.where(l > 0.0, 1.0 / l, 0.0)).astype(BF16)
                cmp_ref[HEAD_DIM * h:HEAD_DIM * (h + 1), :] = _dot(vcg, pc)
                imp = imp + _dot(ovt_ref[:, 0:rows], pc)
            imp = jnp.where(forced, FORCE_SCORE, imp)
            imp = jnp.where(blk_causal, imp, NEG_INF)
            rk_ref[0] = imp
            rk_ref[1] = jnp.zeros((n_blk, TQ), F32)
            for m in range(n_blk // SUBLANES):
                @pl.when(m * SUBLANES * SLC_BLOCK < (i + 1) * TQ)
                def _(m=m):
                    for l in range(TQ // LANES):
                        _rank_tile(rk_ref, m, slice(LANES * l, LANES * (l + 1)))
            sel_ref[g] = jnp.where(rk_ref[1] < float(n_sel), 0.0, NEG_INF)

    bpc = KC // SLC_BLOCK

    def slc_prep(j):
        off = pl.multiple_of(j * KC, KC)
        visible = row_minus_col <= (i - j) * KC
        biases = []
        for g in range(NSA_GROUPS):
            bias = jnp.concatenate(
                [jnp.broadcast_to(sel_ref[g, pl.ds(bpc * j + b, 1), :], (SLC_BLOCK, TQ))
                 for b in range(bpc)], axis=0)
            biases.append(jnp.where(visible, bias, NEG_INF))
        return off, biases

    def slc_score(ctx, h, dead):
        off, biases = ctx
        g = h // hpg
        return _scores(ks_ref[pl.ds(off, KC), gw * g:gw * (g + 1)], qrz_ref[h], dead) + biases[g]

    def slc_values(j, h):
        g = h // hpg
        return vst_ref[HV * g:HV * (g + 1), pl.ds(pl.multiple_of(j * KC, KC), KC)]

    def win_prep(j):
        off = pl.multiple_of(j * KC, KC)
        d = (i - j) * KC
        inside = jnp.logical_and(row_minus_col <= d, row_minus_col > d - WINDOW)
        return off, jnp.where(inside, 0.0, NEG_INF)

    def win_score(ctx, h, dead):
        off, bias = ctx
        g = h // hpg
        return _scores(kw_ref[pl.ds(off, KC), gw * g:gw * (g + 1)], qrz_ref[h], dead) + bias

    def win_values(j, h):
        g = h // hpg
        return vwt_ref[HV * g:HV * (g + 1), pl.ds(pl.multiple_of(j * KC, KC), KC)]

    n_live = (i + 1) * (TQ // CMP_STRIDE)
    if (nc // 2) % LANES == 0:
        pl.when(n_live <= nc // 2)(functools.partial(compressed_and_select, nc // 2))
        pl.when(n_live > nc // 2)(functools.partial(compressed_and_select, nc))
    else:
        compressed_and_select(nc)
    assert WINDOW == KC
    slc = _Stage(0, i, NSA_HEADS, HV, slc_prep, slc_score, slc_values, m_ref, al_ref, acc_ref,
                 dead_last="bl")
    win = _Stage(jnp.maximum(i - 1, 0), i, NSA_HEADS, HV, win_prep, win_score, win_values,
                 mw_ref, alw_ref, accw_ref, dead_first="tr", dead_last="bl")
    _attn_chain([slc, win], sbuf_ref, i)

    def head_out(ref, h):
        return ref[HV * h:HV * h + HEAD_DIM, :] / ref[HV * h + HEAD_DIM:HV * h + HEAD_DIM + 1, :]

    gb = _sigmoid(bg_ref[...])
    for hp in range(NSA_HEADS // 2):
        parts = []
        for h in (2 * hp, 2 * hp + 1):
            parts.append(gb[3 * h:3 * h + 1, :] * cmp_ref[HEAD_DIM * h:HEAD_DIM * (h + 1), :]
                         + gb[3 * h + 1:3 * h + 2, :] * head_out(acc_ref, h)
                         + gb[3 * h + 2:3 * h + 3, :] * head_out(accw_ref, h))
        o_ref[:, LANES * hp:LANES * (hp + 1)] = jnp.concatenate(parts, axis=0).T


def _nsa_call(qc, qcr, kcc, vct, ovt, ks, vst, kw, vwt, bgt, n_sel):
    B, S, W = qc.shape
    n_blk = S // SLC_BLOCK
    qspec = pl.BlockSpec((None, TQ, W), lambda b, i: (b, i, 0))
    return pl.pallas_call(
        functools.partial(_nsa_kernel, n_sel=n_sel),
        grid=(B, S // TQ),
        in_specs=[
            qspec, qspec,
            pl.BlockSpec((None,) + kcc.shape[1:], lambda b, i: (b, 0, 0, 0)),
            pl.BlockSpec((None,) + vct.shape[1:], lambda b, i: (b, 0, 0, 0)),
            pl.BlockSpec(ovt.shape, lambda b, i: (0, 0)),
            pl.BlockSpec((None, S, 2 * LANES), lambda b, i: (b, 0, 0)),
            pl.BlockSpec((None, NSA_GROUPS * HV, S), lambda b, i: (b, 0, 0)),
            pl.BlockSpec((None, S, 2 * LANES), lambda b, i: (b, 0, 0)),
            pl.BlockSpec((None, NSA_GROUPS * HV, S), lambda b, i: (b, 0, 0)),
            pl.BlockSpec((None, 3 * NSA_HEADS, TQ), lambda b, i: (b, 0, i)),
        ],
        out_specs=qspec,
        out_shape=jax.ShapeDtypeStruct((B, S, W), F32),
        scratch_shapes=[pltpu.VMEM((W, TQ), F32), pltpu.VMEM((NSA_GROUPS, n_blk, TQ), F32),
                        pltpu.VMEM((2, n_blk, TQ), F32), pltpu.VMEM((NSA_HEADS, TQ, LANES), BF16), pltpu.VMEM((NSA_HEADS, TQ, LANES), BF16)]
        + _attn_scratch(NSA_HEADS, HV)[1:] + _attn_scratch(NSA_HEADS, HV),
        compiler_params=pltpu.CompilerParams(
            dimension_semantics=("arbitrary", "arbitrary"), vmem_limit_bytes=VMEM_LIMIT),
        name="nsa_attn",
    )(qc, qcr, kcc, vct, ovt, ks, vst, kw, vwt, bgt)


def _post_kernel(x_ref, oa_ref, ob_ref, oc_ref, p_ref, g_ref, wg_ref, wup_ref, wout_ref,
                 pg_ref, wple_ref, wpg_ref, fg_ref, o_ref, *, final):
    x = x_ref[...]
    h = _rms(x, g_ref[...]).astype(BF16)
    mixed = jnp.zeros(x.shape, F32)
    for n, br_ref in enumerate((oa_ref, ob_ref, oc_ref)):
        gate = _dot(h, wg_ref[:, BRANCH_WIDTH * n:BRANCH_WIDTH * (n + 1)])
        og = br_ref[...] * (gate * _sigmoid(gate))
        up = _dot(og.astype(BF16), wup_ref[n])
        c0 = N_BRANCH * BRANCH_WIDTH + D_MODEL * n
        mixed = mixed + _sigmoid(_dot(h, wg_ref[:, c0:c0 + D_MODEL])) * up
    x1 = x + _dot(mixed.astype(BF16), wout_ref[...])
    emb = _dot(p_ref[...].astype(BF16), wple_ref[...])
    hg = _dot(_rms(x1, pg_ref[...]).astype(BF16), wpg_ref[...])
    x2 = x1 + emb * _sigmoid(hg)
    if final:
        x2 = _rms(x2, fg_ref[...])
    o_ref[...] = x2


def _post_call(x, oa, ob, oc, p, g, wg, wup, wout, pg, wple, wpg, fg, final, layer):
    T, D = x.shape
    tm = TM_POST
    tok = lambda w: pl.BlockSpec((tm, w), lambda i: (i, 0))
    full = lambda a: _layer_spec(a, layer, pipeline_mode=pl.Buffered(1))
    return pl.pallas_call(
        functools.partial(_post_kernel, final=final),
        grid=(T // tm,),
        in_specs=[tok(D), tok(BRANCH_WIDTH), tok(BRANCH_WIDTH), tok(BRANCH_WIDTH),
                  pl.BlockSpec((None, tm, PLE_DIM), lambda i: (layer, i, 0)),
                  full(g), full(wg), full(wup), full(wout), full(pg), full(wple), full(wpg),
                  pl.BlockSpec(fg.shape, lambda i: (0, 0))],
        out_specs=tok(D),
        out_shape=jax.ShapeDtypeStruct((T, D), F32),
        compiler_params=pltpu.CompilerParams(
            dimension_semantics=("arbitrary",), vmem_limit_bytes=VMEM_LIMIT),
        name="post_mix",
    )(x, oa, ob, oc, p, g, wg, wup, wout, pg, wple, wpg, fg)


ROPE_COLS = 3 * (ROT_DIM // 2)


def _rope_inputs(positions):
    half = ROT_DIM // 2
    inv = ROPE_THETA ** (-jnp.arange(0, ROT_DIM, 2, dtype=F32) / ROT_DIM)
    ang = positions.astype(F32)[..., None] * inv
    cs = jnp.concatenate([jnp.cos(ang), jnp.sin(ang), jnp.ones_like(ang)], axis=-1)
    place = np.zeros((ROPE_COLS, 3 * LANES), np.float32)
    for head in range(LANES // HEAD_DIM):
        base = HEAD_DIM * head
        for j in range(half):
            place[j, base + j] = 1.0
            place[j, base + half + j] = 1.0
            place[half + j, LANES + base + j] = -1.0
            place[half + j, 2 * LANES + base + half + j] = 1.0
        place[2 * half, base + ROT_DIM:base + HEAD_DIM] = 1.0
    return cs, jnp.asarray(place, dtype=BF16)


_NAT_PIECES = (
    (0, _A_Q, 512), (512, _A_K, 512), (1024, _B_Q, 512), (1536, _I_Q, 512), (2048, _C_Q, 512),
    (2560, _B_K, 64), (2624, _B_K, 64), (2688, _I_K, 64), (2752, _I_K, 64),
    (2816, _C_KS, 64), (2880, _C_KS, 64), (2944, _C_KS + 64, 64), (3008, _C_KS + 64, 64),
    (3072, _C_KW, 64), (3136, _C_KW, 64), (3200, _C_KW + 64, 64), (3264, _C_KW + 64, 64),
    (3328, _C_KC, 128), (3456, _C_VC, 128))
_GATE_PIECES = ((0, _A_G, 512), (512, _B_G, 512), (1024, _C_G, 512), (1536, _MERGE, 3 * D_MODEL))
_TR_PIECES = ((0, _A_V, 512), (512, _B_V, 64), (576, _C_VS, 128), (704, _C_VW, 128),
              (832, _I_W, 8), (840, _C_BG, 24))


def _relayout_kernel(wt_ref, nat_ref, gates_ref, trt_ref):
    for dst_ref, pieces in ((nat_ref, _NAT_PIECES), (gates_ref, _GATE_PIECES)):
        for d0, s0, n in pieces:
            for r in range(0, n, LANES):
                m = min(LANES, n - r)
                dst_ref[:, d0 + r:d0 + r + m] = wt_ref[s0 + r:s0 + r + m, :].T.astype(BF16)
    groups, cur = [], []
    for d0, s0, n in _TR_PIECES:
        if cur and d0 % ONES_ROWS == 0:
            groups.append(cur)
            cur = []
        cur.append((d0, s0, n))
    groups.append(cur)
    for grp in groups:
        rows = jnp.concatenate([wt_ref[s0:s0 + n, :] for _, s0, n in grp], axis=0)
        trt_ref[grp[0][0]:grp[0][0] + rows.shape[0], :] = rows.astype(BF16)


def _proj_weights(w_in):
    L, D, n_in = w_in.shape
    w_t = jnp.swapaxes(w_in, 1, 2)
    return pl.pallas_call(
        _relayout_kernel,
        grid=(L, D // LANES),
        in_specs=[pl.BlockSpec((None, n_in, LANES), lambda l, r: (l, 0, r))],
        out_specs=(pl.BlockSpec((None, LANES, N_NAT), lambda l, r: (l, r, 0)),
                   pl.BlockSpec((None, LANES, BRANCH_WIDTH * N_BRANCH + N_BRANCH * D_MODEL), lambda l, r: (l, r, 0)),
                   pl.BlockSpec((None, N_TR, LANES), lambda l, r: (l, 0, r))),
        out_shape=(jax.ShapeDtypeStruct((L, D, N_NAT), BF16),
                   jax.ShapeDtypeStruct((L, D, BRANCH_WIDTH * N_BRANCH + N_BRANCH * D_MODEL), BF16),
                   jax.ShapeDtypeStruct((L, N_TR, D), BF16)),
        compiler_params=pltpu.CompilerParams(
            dimension_semantics=("arbitrary", "arbitrary"), vmem_limit_bytes=VMEM_LIMIT),
        name="weight_relayout",
    )(w_t)


def _compress_weights(pe_k, w1_k, w2_k, pe_v, w1_v, w2_v):
    L = pe_k.shape[0]
    half = CMP_LEN // 2
    kdim = half * NSA_GROUPS * HEAD_DIM
    pe = jnp.stack([pe_k, pe_v], axis=1)
    pe = jnp.broadcast_to(pe[:, :, :, None, :], (L, 2, CMP_LEN, NSA_GROUPS, HEAD_DIM))
    w1 = jnp.stack([w1_k, w1_v], axis=1).astype(BF16)
    zero = jnp.zeros_like(w1)
    w1g = jnp.stack([jnp.stack([w1 if h == g else zero for h in range(NSA_GROUPS)], axis=3)
                     for g in range(NSA_GROUPS)], axis=2)
    w1g = w1g.reshape(L, 2 * NSA_GROUPS, CMP_LEN, NSA_GROUPS * HEAD_DIM, CMP_HIDDEN)
    w2 = jnp.stack([w2_k, w2_v], axis=1).astype(BF16)
    return (pe[:, :, :half].reshape(L, 2, 1, kdim), pe[:, :, half:].reshape(L, 2, 1, kdim),
            w1g[:, :, :half].reshape(L, 2 * NSA_GROUPS, kdim, CMP_HIDDEN),
            w1g[:, :, half:].reshape(L, 2 * NSA_GROUPS, kdim, CMP_HIDDEN), w2)


def _overlap_t(S):
    nc = S // CMP_STRIDE
    n_blk = S // SLC_BLOCK
    cstart = np.arange(nc) * CMP_STRIDE
    sstart = np.arange(n_blk) * SLC_BLOCK
    ov = ((cstart[None, :] < sstart[:, None] + SLC_BLOCK) & (cstart[None, :] + CMP_LEN > sstart[:, None]))
    return jnp.asarray(ov.astype(np.float32), dtype=BF16)


def kernel(x, p, positions, norm_g, w_in, diff_lambda, diff_subln_g, cmp_pe_k, cmp_w1_k, cmp_w2_k,
           cmp_pe_v, cmp_w1_v, cmp_w2_v, w_up, w_out, ple_norm_g, w_ple, w_ple_gate, final_norm_g):
    B, S, D = x.shape
    depth = w_in.shape[0]
    k_sel = min(DSA_TOPK_MAX, S // 4)
    n_sel = min(SLC_TOPN_MAX, S // SLC_BLOCK)
    assert S % TM_PROJ == 0 and S % TQ == 0 and k_sel <= TQ and D == D_MODEL
    assert KC == TQ and WINDOW % KC == 0
    cs, place = _rope_inputs(positions)
    ovt = _overlap_t(S)

    wn, wg, wt = _proj_weights(w_in)
    cmp_w = _compress_weights(cmp_pe_k, cmp_w1_k, cmp_w2_k, cmp_pe_v, cmp_w1_v, cmp_w2_v)
    g_in = norm_g.reshape(depth, 1, D)
    g_ple = ple_norm_g.reshape(depth, 1, D)
    g_sub = diff_subln_g.reshape(depth, -1, 1)
    wup, wout, wple, wpg = (w.astype(BF16) for w in (w_up, w_out, w_ple, w_ple_gate))
    p2 = p.reshape(depth, B * S, PLE_DIM)

    for i in range(depth):
        lam_init = 0.8 - 0.6 * math.exp(-0.3 * i)
        (qa, ka, qb, iq, qc, qcr, kb, ik, ks, kw, ck, cv,
         vat, vbt, vst, vwt, iwt, bgt) = _proj_call(x, g_in, wn, wt, cs, place, i)

        kcc, vct = _compress_call(ck, cv, *cmp_w, i)

        oa = _diff_call(qa, ka, vat, diff_lambda, g_sub, lam_init, i)
        ob = _dsa_call(qb, iq, kb, ik, vbt, iwt, k_sel)
        oc = _nsa_call(qc, qcr, kcc, vct, ovt, ks, vst, kw, vwt, bgt, n_sel)

        x = _post_call(
            x.reshape(B * S, D), oa.reshape(B * S, -1), ob.reshape(B * S, -1), oc.reshape(B * S, -1),
            p2, g_in, wg, wup, wout, g_ple, wple, wpg, final_norm_g.reshape(1, D),
            final=(i == depth - 1), layer=i).reshape(B, S, D)
    return x
```
